```python
import math
import jax, jax.numpy as jnp
from jax import lax
import numpy as np


D_MODEL = 1024
BATCH = 8
SEQ = 8192
DEPTH = 1

SSM_WIDTH = D_MODEL // 2
SSM_GROUP = 16
SSM_GROUPS = SSM_WIDTH // SSM_GROUP
SSM_STATE = 64
N_HEADS = 16
HEAD_DIM = 64
N_KV = 4
GQA = N_HEADS // N_KV
ATTN_WIDTH = N_HEADS * HEAD_DIM
KV_WIDTH = N_KV * HEAD_DIM
CMP_LEN = 32
CMP_STRIDE = 16
CMP_HIDDEN = 256
SEL_LEN = 64
SEL_TOP = 16
WINDOW = 512
Q_BLOCK = 64
FORCE_BONUS = 1.0e4
REL_BUCKETS = 32
REL_MAX_DIST = 128
N_EXPERTS = 256
TOP_K = 8
D_EXPERT = 256
ROUTED_SCALE = 2.5
EXPERT_BLOCK = 128
DN_ALPHA = (2 * DEPTH) ** 0.25
DN_BETA = (8 * DEPTH) ** -0.25
LN_EPS = 1e-5

SPLIT_SIZES = (SSM_WIDTH, ATTN_WIDTH, KV_WIDTH, KV_WIDTH, KV_WIDTH, KV_WIDTH, KV_WIDTH, KV_WIDTH, 3 * N_HEADS, 2 * D_MODEL)
IN_WIDTH = sum(SPLIT_SIZES)
VALUE_PIECES = (3, 5, 7)

kernel_name = 'hybrid_s5_nsa_moe_block'


def _split_points():
    return [int(v) for v in np.cumsum(SPLIT_SIZES)[:-1]]


def _norm(x):
    xf = x.astype(jnp.float32)
    xc = xf - jnp.mean(xf, axis=-1, keepdims=True)
    var = jnp.mean(xc * xc, axis=-1, keepdims=True)
    return xc * lax.rsqrt(var + LN_EPS)


def _layer_norm(x, g, b):
    return (_norm(x) * g.astype(jnp.float32) + b.astype(jnp.float32)).astype(x.dtype)


def _ada_modulate(x, shift, scale):
    return (_norm(x) * (1.0 + scale.astype(jnp.float32)) + shift.astype(jnp.float32)).astype(x.dtype)


def _rel_bucket(dist):
    dist = jnp.maximum(dist, 0)
    exact = REL_BUCKETS // 2
    log_ratio = jnp.log(jnp.maximum(dist, 1).astype(jnp.float32) / exact) / math.log(REL_MAX_DIST / exact)
    large = jnp.minimum(exact + (log_ratio * (REL_BUCKETS - exact)).astype(jnp.int32), REL_BUCKETS - 1)
    return jnp.where(dist < exact, dist, large)


def _masked_softmax(s, mask):
    s = jnp.where(mask, s.astype(jnp.float32), -jnp.inf)
    m = jnp.max(s, axis=-1, keepdims=True)
    m = jnp.where(jnp.isfinite(m), m, 0.0)
    e = jnp.exp(s - m)
    return e / jnp.maximum(jnp.sum(e, axis=-1, keepdims=True), 1e-30)


def _cplx_combine(e1, e2):
    ar1, ai1, br1, bi1 = e1
    ar2, ai2, br2, bi2 = e2
    return (ar1 * ar2 - ai1 * ai2, ar1 * ai2 + ai1 * ar2,
            ar2 * br1 - ai2 * bi1 + br2, ar2 * bi1 + ai2 * br1 + bi2)


def _s5_branch(u, lam_re, lam_im, log_step, b_re, b_im, c_re, c_im, d_skip, glu_w):
    bsz, seq, _ = u.shape
    f32 = jnp.float32
    lr, li = lam_re.astype(f32), lam_im.astype(f32)
    dt = jnp.exp(log_step.astype(f32))[:, None]
    mag = jnp.exp(lr * dt)
    ar, ai = mag * jnp.cos(li * dt), mag * jnp.sin(li * dt)
    den = lr * lr + li * li
    kr = ((ar - 1.0) * lr + ai * li) / den
    ki = (ai * lr - (ar - 1.0) * li) / den
    br, bi = b_re.astype(f32), b_im.astype(f32)
    bbr = kr[..., None] * br - ki[..., None] * bi
    bbi = kr[..., None] * bi + ki[..., None] * br
    cr, ci = c_re.astype(f32), c_im.astype(f32)
    dsk = d_skip.astype(f32)

    def one_sequence(us):
        uf = us.astype(f32)
        ug = uf.reshape(seq, SSM_GROUPS, SSM_GROUP)
        bur = jnp.einsum('lgc,gpc->lgp', ug, bbr)
        bui = jnp.einsum('lgc,gpc->lgp', ug, bbi)
        a_r = jnp.broadcast_to(ar, bur.shape)
        a_i = jnp.broadcast_to(ai, bur.shape)
        _, _, xr, xi = lax.associative_scan(_cplx_combine, (a_r, a_i, bur, bui), axis=0)
        y = jnp.einsum('lgp,gcp->lgc', xr, cr) - jnp.einsum('lgp,gcp->lgc', xi, ci)
        return y.reshape(seq, SSM_WIDTH) + dsk * uf

    y = lax.map(one_sequence, u)
    z = jax.nn.gelu(y).astype(u.dtype)
    val, gate = jnp.split(jnp.einsum('blc,cn->bln', z, glu_w), 2, axis=-1)
    return val * jax.nn.sigmoid(gate)


def _compress(kv, pos, w1, w2):
    bsz, seq = kv.shape[:2]
    n_cmp = (seq - CMP_LEN) // CMP_STRIDE + 1
    idx = jnp.arange(n_cmp)[:, None] * CMP_STRIDE + jnp.arange(CMP_LEN)[None, :]
    blocks = kv[:, idx] + pos[None, None, :, None, :]
    flat = jnp.transpose(blocks, (0, 1, 3, 2, 4)).reshape(bsz, n_cmp, N_KV, CMP_LEN * HEAD_DIM)
    return jax.nn.gelu(flat @ w1) @ w2


def _nsa_branch(q, kc, vc, k_slc, v_slc, k_win, v_win, gates, rel_bias):
    bsz, seq = q.shape[:2]
    n_cmp = kc.shape[1]
    n_blk = seq // SEL_LEN
    n_sel = min(SEL_TOP, n_blk)
    table = rel_bias.T.reshape(N_KV, GQA, REL_BUCKETS)
    c_start = jnp.arange(n_cmp) * CMP_STRIDE
    c_end = c_start + CMP_LEN - 1
    blk = jnp.arange(n_blk)
    overlap = ((c_start[:, None] < (blk[None, :] + 1) * SEL_LEN) &
               (c_end[:, None] >= blk[None, :] * SEL_LEN)).astype(jnp.float32)
    ks_b = k_slc.reshape(bsz, n_blk, SEL_LEN, N_KV, HEAD_DIM).transpose(0, 3, 1, 2, 4)
    vs_b = v_slc.reshape(bsz, n_blk, SEL_LEN, N_KV, HEAD_DIM).transpose(0, 3, 1, 2, 4)
    kw_p = jnp.pad(k_win, ((0, 0), (WINDOW, 0), (0, 0), (0, 0)))
    vw_p = jnp.pad(v_win, ((0, 0), (WINDOW, 0), (0, 0), (0, 0)))
    qg = q.reshape(bsz, seq, N_KV, GQA, HEAD_DIM)
    gg = gates.reshape(bsz, seq, N_KV, GQA, 3)
    b_ix = jnp.arange(bsz)[:, None, None, None]
    g_ix = jnp.arange(N_KV)[None, None, :, None]
    g5 = jnp.arange(N_KV)[None, None, :, None, None]
    r5 = jnp.arange(GQA)[None, None, None, :, None]

    def query_block(qi):
        s0 = qi * Q_BLOCK
        t = s0 + jnp.arange(Q_BLOCK)
        qb = lax.dynamic_slice_in_dim(qg, s0, Q_BLOCK, axis=1)
        gb = lax.dynamic_slice_in_dim(gg, s0, Q_BLOCK, axis=1)
        d_cmp = t[:, None] - c_end[None, :]
        bias = table[:, :, _rel_bucket(d_cmp)].transpose(2, 0, 1, 3)
        s = jnp.einsum('bqgrd,bngd->bqgrn', qb, kc) + bias
        p_cmp = _masked_softmax(s, (d_cmp >= 0)[:, None, None, :])
        o_cmp = jnp.einsum('bqgrn,bngd->bqgrd', p_cmp.astype(vc.dtype), vc)
        imp = jnp.einsum('bqgrn,nj->bqgj', p_cmp, overlap)
        cur = (t // SEL_LEN)[:, None]
        forced = (blk[None, :] == 0) | (blk[None, :] == cur) | (blk[None, :] == cur - 1)
        score = jnp.where((blk[None, :] <= cur)[None, :, None, :],
                          imp + jnp.where(forced, FORCE_BONUS, 0.0)[None, :, None, :], -jnp.inf)
        _, sel = lax.top_k(score, n_sel)
        kg = ks_b[b_ix, g_ix, sel].reshape(bsz, Q_BLOCK, N_KV, n_sel * SEL_LEN, HEAD_DIM)
        vg = vs_b[b_ix, g_ix, sel].reshape(bsz, Q_BLOCK, N_KV, n_sel * SEL_LEN, HEAD_DIM)
        kpos = (sel[..., None] * SEL_LEN + jnp.arange(SEL_LEN)).reshape(bsz, Q_BLOCK, N_KV, n_sel * SEL_LEN)
        d_sel = t[None, :, None, None] - kpos
        bias = table[g5, r5, _rel_bucket(d_sel)[:, :, :, None, :]]
        s = jnp.einsum('bqgrd,bqgkd->bqgrk', qb, kg) + bias
        p = _masked_softmax(s, (d_sel >= 0)[:, :, :, None, :])
        o_slc = jnp.einsum('bqgrk,bqgkd->bqgrd', p.astype(vg.dtype), vg)
        kw = lax.dynamic_slice_in_dim(kw_p, s0, WINDOW + Q_BLOCK, axis=1)
        vw = lax.dynamic_slice_in_dim(vw_p, s0, WINDOW + Q_BLOCK, axis=1)
        wpos = s0 - WINDOW + jnp.arange(WINDOW + Q_BLOCK)
        d_win = t[:, None] - wpos[None, :]
        bias = table[:, :, _rel_bucket(d_win)].transpose(2, 0, 1, 3)
        s = jnp.einsum('bqgrd,bkgd->bqgrk', qb, kw) + bias
        win_mask = (d_win >= 0) & (d_win < WINDOW) & (wpos[None, :] >= 0)
        p = _masked_softmax(s, win_mask[:, None, None, :])
        o_win = jnp.einsum('bqgrk,bkgd->bqgrd', p.astype(vw.dtype), vw)
        return gb[..., 0:1] * o_cmp + gb[..., 1:2] * o_slc + gb[..., 2:3] * o_win

    out = lax.map(query_block, jnp.arange(seq // Q_BLOCK))
    return jnp.moveaxis(out, 0, 1).reshape(bsz, seq, ATTN_WIDTH)


def _token_mixer(h, w_in, lam_re, lam_im, log_step, b_re, b_im, c_re, c_im, d_skip, glu_w,
                 cmp_pos, cmp_w1, cmp_w2, rel_bias, w_out):
    bsz, seq, _ = h.shape
    proj = jnp.einsum('bld,de->ble', h, w_in)
    u, q, kcr, vcr, ksr, vsr, kwr, vwr, nsa_g, merge_g = jnp.split(proj, _split_points(), axis=-1)

    def heads(a, n):
        return a.reshape(bsz, seq, n, HEAD_DIM)

    q = heads(q, N_HEADS) * (HEAD_DIM ** -0.5)
    kc = _compress(heads(kcr, N_KV), cmp_pos[0], cmp_w1[0], cmp_w2[0])
    vc = _compress(heads(vcr, N_KV), cmp_pos[1], cmp_w1[1], cmp_w2[1])
    y_nsa = _nsa_branch(q, kc, vc, heads(ksr, N_KV), heads(vsr, N_KV), heads(kwr, N_KV), heads(vwr, N_KV),
                        jax.nn.sigmoid(nsa_g.reshape(bsz, seq, N_HEADS, 3)), rel_bias)
    y_ssm = _s5_branch(u, lam_re, lam_im, log_step, b_re, b_im, c_re, c_im, d_skip, glu_w)
    mg = jax.nn.sigmoid(merge_g.reshape(bsz, seq, 2, D_MODEL))
    merged = mg[:, :, 0] * y_ssm + mg[:, :, 1] * y_nsa
    return jnp.einsum('bld,de->ble', merged, w_out)


def _moe(h, router_w, router_bias, w_gate, w_up, w_down, sh_gate, sh_up, sh_down):
    bsz, seq, dm = h.shape
    n_tok = bsz * seq
    hf = h.reshape(n_tok, dm)
    scores = jax.nn.sigmoid(jnp.einsum('td,de->te', hf, router_w).astype(jnp.float32))
    _, top_e = lax.top_k(scores + router_bias.astype(jnp.float32), TOP_K)
    top_s = jnp.take_along_axis(scores, top_e, axis=-1)
    top_w = top_s / jnp.sum(top_s, axis=-1, keepdims=True) * ROUTED_SCALE
    n_asg = n_tok * TOP_K
    e_flat = top_e.reshape(n_asg)
    tok_flat = jnp.repeat(jnp.arange(n_tok, dtype=jnp.int32), TOP_K)
    order = jnp.argsort(e_flat)
    e_sorted = e_flat[order]
    counts = jnp.bincount(e_flat, length=N_EXPERTS)
    starts = jnp.cumsum(counts) - counts
    padded = (counts + EXPERT_BLOCK - 1) // EXPERT_BLOCK * EXPERT_BLOCK
    pad_end = jnp.cumsum(padded)
    pad_start = pad_end - padded
    dest = pad_start[e_sorted] + jnp.arange(n_asg) - starts[e_sorted]
    n_rows = n_asg + N_EXPERTS * EXPERT_BLOCK
    n_blocks = n_rows // EXPERT_BLOCK
    row_tok = jnp.full((n_rows,), n_tok, jnp.int32).at[dest].set(tok_flat[order])
    row_w = jnp.zeros((n_rows,), jnp.float32).at[dest].set(top_w.reshape(n_asg)[order])
    blk_exp = jnp.minimum(jnp.searchsorted(pad_end, jnp.arange(n_blocks) * EXPERT_BLOCK, side='right'), N_EXPERTS - 1)
    h_pad = jnp.concatenate([hf, jnp.zeros((1, dm), hf.dtype)], axis=0)

    def expert_block(acc, blk):
        tok, wt, e = blk
        xb = h_pad[tok]
        yb = (jax.nn.silu(xb @ w_gate[e]) * (xb @ w_up[e])) @ w_down[e]
        return acc.at[tok].add(yb * wt.astype(yb.dtype)[:, None]), None

    routed, _ = lax.scan(expert_block, jnp.zeros_like(h_pad),
                         (row_tok.reshape(n_blocks, EXPERT_BLOCK), row_w.reshape(n_blocks, EXPERT_BLOCK), blk_exp))
    shared = (jax.nn.silu(hf @ sh_gate) * (hf @ sh_up)) @ sh_down
    return (routed[:n_tok] + shared).reshape(bsz, seq, dm)


def setup_inputs(seed: int = 0) -> dict:
    key = jax.random.key(seed)
    ks = jax.random.split(key, 32)
    f32 = jnp.float32

    def nrm(k, shape, scale):
        return jax.random.normal(k, shape, f32) * scale

    nl, dm = DEPTH, D_MODEL
    offs = np.concatenate([[0], np.cumsum(SPLIT_SIZES)])
    col_scale = np.ones((IN_WIDTH,), np.float32)
    for i in VALUE_PIECES:
        col_scale[offs[i]:offs[i + 1]] = DN_BETA
    n_idx = jnp.arange(SSM_STATE, dtype=f32)
    return {
        'x': nrm(ks[0], (BATCH, SEQ, dm), 1.0),
        'c': nrm(ks[1], (BATCH, dm), 1.0),
        'ada_w': nrm(ks[2], (nl, dm, 6 * dm), dm ** -0.5),
        'ada_b': nrm(ks[3], (nl, 6 * dm), 0.02),
        'w_in': nrm(ks[4], (nl, dm, IN_WIDTH), dm ** -0.5) * jnp.asarray(col_scale),
        'ssm_lambda_re': -0.5 + nrm(ks[5], (nl, SSM_GROUPS, SSM_STATE), 0.01),
        'ssm_lambda_im': math.pi * n_idx + nrm(ks[6], (nl, SSM_GROUPS, SSM_STATE), 0.01),
        'ssm_log_step': jax.random.uniform(ks[7], (nl, SSM_GROUPS), f32, math.log(1e-3), math.log(1e-1)),
        'ssm_b_re': nrm(ks[8], (nl, SSM_GROUPS, SSM_STATE, SSM_GROUP), (2 * SSM_GROUP) ** -0.5),
        'ssm_b_im': nrm(ks[9], (nl, SSM_GROUPS, SSM_STATE, SSM_GROUP), (2 * SSM_GROUP) ** -0.5),
        'ssm_c_re': nrm(ks[10], (nl, SSM_GROUPS, SSM_GROUP, SSM_STATE), SSM_STATE ** -0.5),
        'ssm_c_im': nrm(ks[11], (nl, SSM_GROUPS, SSM_GROUP, SSM_STATE), SSM_STATE ** -0.5),
        'ssm_d': nrm(ks[12], (nl, SSM_WIDTH), 1.0),
        'ssm_glu_w': nrm(ks[13], (nl, SSM_WIDTH, 2 * dm), SSM_WIDTH ** -0.5),
        'cmp_pos': nrm(ks[14], (nl, 2, CMP_LEN, HEAD_DIM), 0.1),
        'cmp_w1': nrm(ks[15], (nl, 2, CMP_LEN * HEAD_DIM, CMP_HIDDEN), (CMP_LEN * HEAD_DIM) ** -0.5),
        'cmp_w2': nrm(ks[16], (nl, 2, CMP_HIDDEN, HEAD_DIM), CMP_HIDDEN ** -0.5),
        'rel_bias': nrm(ks[17], (REL_BUCKETS, N_HEADS), 0.5),
        'w_out': nrm(ks[18], (nl, dm, dm), dm ** -0.5 * DN_BETA),
        'ln1_g': 1.0 + nrm(ks[19], (nl, dm), 0.01),
        'ln1_b': nrm(ks[20], (nl, dm), 0.01),
        'router_w': nrm(ks[21], (nl, dm, N_EXPERTS), dm ** -0.5),
        'router_bias': nrm(ks[22], (nl, N_EXPERTS), 0.01),
        'exp_w_gate': nrm(ks[23], (nl, N_EXPERTS, dm, D_EXPERT), dm ** -0.5),
        'exp_w_up': nrm(ks[24], (nl, N_EXPERTS, dm, D_EXPERT), dm ** -0.5),
        'exp_w_down': nrm(ks[25], (nl, N_EXPERTS, D_EXPERT, dm), D_EXPERT ** -0.5 * DN_BETA),
        'sh_w_gate': nrm(ks[26], (nl, dm, D_EXPERT), dm ** -0.5),
        'sh_w_up': nrm(ks[27], (nl, dm, D_EXPERT), dm ** -0.5),
        'sh_w_down': nrm(ks[28], (nl, D_EXPERT, dm), D_EXPERT ** -0.5 * DN_BETA),
        'ln2_g': 1.0 + nrm(ks[29], (nl, dm), 0.01),
        'ln2_b': nrm(ks[30], (nl, dm), 0.01),
    }


def reference(x, c, ada_w, ada_b, w_in, ssm_lambda_re, ssm_lambda_im, ssm_log_step, ssm_b_re, ssm_b_im,
              ssm_c_re, ssm_c_im, ssm_d, ssm_glu_w, cmp_pos, cmp_w1, cmp_w2, rel_bias, w_out, ln1_g, ln1_b,
              router_w, router_bias, exp_w_gate, exp_w_up, exp_w_down, sh_w_gate, sh_w_up, sh_w_down,
              ln2_g, ln2_b):
    cond = jax.nn.silu(c)
    for l in range(DEPTH):
        mod = jnp.einsum('bd,de->be', cond, ada_w[l]) + ada_b[l]
        sh1, sc1, g1, sh2, sc2, g2 = jnp.split(mod[:, None, :], 6, axis=-1)
        h = _ada_modulate(x, sh1, sc1)
        y = _token_mixer(h, w_in[l], ssm_lambda_re[l], ssm_lambda_im[l], ssm_log_step[l], ssm_b_re[l],
                         ssm_b_im[l], ssm_c_re[l], ssm_c_im[l], ssm_d[l], ssm_glu_w[l], cmp_pos[l],
                         cmp_w1[l], cmp_w2[l], rel_bias, w_out[l])
        x = _layer_norm(DN_ALPHA * x + g1 * y, ln1_g[l], ln1_b[l])
        h = _ada_modulate(x, sh2, sc2)
        y = _moe(h, router_w[l], router_bias[l], exp_w_gate[l], exp_w_up[l], exp_w_down[l],
                 sh_w_gate[l], sh_w_up[l], sh_w_down[l])
        x = _layer_norm(DN_ALPHA * x + g2 * y, ln2_g[l], ln2_b[l])
    return x
```

```python
import functools
import math

import numpy as np
import jax
import jax.numpy as jnp
from jax import lax
from jax.experimental import pallas as pl
from jax.experimental.pallas import tpu as pltpu

F32 = jnp.float32
BF16 = jnp.bfloat16

SSM_GROUP = 16
SSM_STATE = 64
N_HEADS = 16
HEAD_DIM = 64
N_KV = 4
GQA = N_HEADS // N_KV
CMP_LEN = 32
CMP_STRIDE = 16
SEL_LEN = 64
SEL_TOP = 16
WINDOW = 512
Q_BLOCK = 64
FORCE_BONUS = 1.0e4
REL_BUCKETS = 32
REL_MAX_DIST = 128
TOP_K = 8
ROUTED_SCALE = 2.5
LN_EPS = 1e-5

LANES = 128
SUBLANES = 8
VMEM_LIMIT_BYTES = 56 * 1024 * 1024
ROW_TILE = 512
ROUTER_TILE = 256
EXPERT_TILE = 256
SCAN_CHUNK = 128
SCAN_LANES = 512
MASK_BIG = 32768.0
NEG = -1.0e30
WIN_BLOCKS = WINDOW // SEL_LEN
ROWS = GQA * Q_BLOCK
PAD_CMP = 8
BAND_ROWS = 24


def _cparams(sem):
    return pltpu.CompilerParams(dimension_semantics=sem, vmem_limit_bytes=VMEM_LIMIT_BYTES)


def _norm_rows(x):
    mu = jnp.mean(x, axis=-1, keepdims=True)
    xc = x - mu
    var = jnp.mean(xc * xc, axis=-1, keepdims=True)
    return xc * lax.rsqrt(var + LN_EPS)


def _mod_kernel(c_ref, w_ref, b_ref, o_ref):
    cond = jax.nn.silu(c_ref[...])
    o_ref[...] = jnp.dot(cond.astype(BF16), w_ref[...].astype(BF16), preferred_element_type=F32) + b_ref[...]


def _modulation(c, ada_w, ada_b):
    bsz, dm = c.shape
    n = ada_w.shape[1]
    tn = dm
    return pl.pallas_call(
        _mod_kernel,
        grid=(n // tn,),
        in_specs=[pl.BlockSpec((bsz, dm), lambda j: (0, 0)),
                  pl.BlockSpec((dm, tn), lambda j: (0, j)),
                  pl.BlockSpec((1, tn), lambda j: (0, j))],
        out_specs=pl.BlockSpec((bsz, tn), lambda j: (0, j)),
        out_shape=jax.ShapeDtypeStruct((bsz, n), F32),
        compiler_params=_cparams(("arbitrary",)),
        name="ada_modulation",
    )(c, ada_w, ada_b.reshape(1, n))


def _inproj_kernel(x_ref, sh_ref, sc_ref, wu_ref, wa_ref, wg_ref, wm_ref, u_ref, a_ref, g_ref, m_ref):
    h = _norm_rows(x_ref[0]) * (1.0 + sc_ref[0]) + sh_ref[0]
    hb = h.astype(BF16)
    u_ref[...] = jnp.dot(hb, wu_ref[...], preferred_element_type=F32)
    a_ref[0] = jnp.dot(hb, wa_ref[...], preferred_element_type=F32).astype(BF16)
    g_ref[0] = jax.nn.sigmoid(jnp.dot(hb, wg_ref[...], preferred_element_type=F32))
    m_ref[0] = jax.nn.sigmoid(jnp.dot(hb, wm_ref[...], preferred_element_type=F32)).astype(BF16)


def _input_projection(x, sh1, sc1, wu, wa, wg, wm):
    bsz, seq, dm = x.shape
    tm = min(ROW_TILE, seq)
    nu, na, ng, nm = wu.shape[1], wa.shape[1], wg.shape[1], wm.shape[1]
    full = lambda n: pl.BlockSpec((dm, n), lambda b, i: (0, 0))
    vec = pl.BlockSpec((1, 1, dm), lambda b, i: (b, 0, 0))
    return pl.pallas_call(
        _inproj_kernel,
        grid=(bsz, seq // tm),
        in_specs=[pl.BlockSpec((1, tm, dm), lambda b, i: (b, i, 0)), vec, vec,
                  full(nu), full(na), full(ng), full(nm)],
        out_specs=[pl.BlockSpec((tm, nu), lambda b, i: (i, b)),
                   pl.BlockSpec((1, tm, na), lambda b, i: (b, i, 0)),
                   pl.BlockSpec((1, tm, ng), lambda b, i: (b, i, 0)),
                   pl.BlockSpec((1, tm, nm), lambda b, i: (b, i, 0))],
        out_shape=[jax.ShapeDtypeStruct((seq, bsz * nu), F32),
                   jax.ShapeDtypeStruct((bsz, seq, na), BF16),
                   jax.ShapeDtypeStruct((bsz, seq, ng), F32),
                   jax.ShapeDtypeStruct((bsz, seq, nm), BF16)],
        compiler_params=_cparams(("arbitrary", "arbitrary")),
        name="adaln_input_projection",
    )(x, sh1, sc1, wu, wa, wg, wm)


def _compress_kernel(x_ref, w1_ref, pb_ref, w2_ref, o_ref, *, n_cmp):
    hid = w2_ref.shape[0]
    p = jnp.dot(x_ref[0, 0], w1_ref[...], preferred_element_type=F32)
    nrow = p.shape[0]
    nxt = pltpu.roll(p[:, hid:], nrow - 1, 0)
    hidv = jax.nn.gelu(p[:, :hid] + nxt + pb_ref[...])
    out = jnp.dot(hidv.astype(BF16), w2_ref[...], preferred_element_type=F32)
    rows = lax.broadcasted_iota(jnp.int32, out.shape, 0)
    o_ref[0, 0] = jnp.where(rows < n_cmp, out, 0.0)


def _compress(xc, w1cat, posb, w2):
    bsz, nkv, nch, kdim = xc.shape
    hid2 = w1cat.shape[1]
    return pl.pallas_call(
        functools.partial(_compress_kernel, n_cmp=nch - 1),
        grid=(bsz, nkv),
        in_specs=[pl.BlockSpec((1, 1, nch, kdim), lambda b, g: (b, g, 0, 0)),
                  pl.BlockSpec((kdim, hid2), lambda b, g: (0, 0)),
                  pl.BlockSpec((1, hid2 // 2), lambda b, g: (0, 0)),
                  pl.BlockSpec((hid2 // 2, HEAD_DIM), lambda b, g: (0, 0))],
        out_specs=pl.BlockSpec((1, 1, nch, HEAD_DIM), lambda b, g: (b, g, 0, 0)),
        out_shape=jax.ShapeDtypeStruct((bsz, nkv, nch, HEAD_DIM), F32),
        compiler_params=_cparams(("arbitrary", "arbitrary")),
        name="kv_compress",
    )(xc, w1cat, posb, w2)


def _nsa_kernel(qt_ref, kaug_ref, vst_ref, kw_ref, vwt_ref, kcp_ref, vcpt_ref, ovt_ref, nb_ref, cb_ref,
                g_ref, o_ref, s_ref, *, n_blk, n_cmp, n_sel):
    qi = pl.program_id(2)
    qt = qt_ref[0, 0, 0]

    s_ref[...] = jnp.dot(kcp_ref[0, 0], qt, preferred_element_type=F32)
    band = pl.multiple_of(SUBLANES * (qi // 2), SUBLANES)
    s_ref[pl.ds(band, BAND_ROWS), :] += cb_ref[0, qi % 2]
    s = s_ref[...]
    rho = lax.broadcasted_iota(jnp.int32, s.shape, 0)
    tok = lax.broadcasted_iota(jnp.int32, s.shape, 1) & (Q_BLOCK - 1)
    first_visible = CMP_STRIDE * (rho - PAD_CMP) + (CMP_LEN - 1) - Q_BLOCK * qi
    vis = (rho >= PAD_CMP) & (rho < PAD_CMP + n_cmp) & (first_visible <= tok)
    s = jnp.where(vis, s, -jnp.inf)
    mx = jnp.max(s, axis=0, keepdims=True)
    mx = jnp.where(mx == -jnp.inf, 0.0, mx)
    e = jnp.exp(s - mx)
    p_cmp = (e / jnp.maximum(jnp.sum(e, axis=0, keepdims=True), 1e-30)).astype(BF16)
    o_cmp = jnp.dot(vcpt_ref[0, 0], p_cmp, preferred_element_type=F32)

    imp4 = jnp.dot(ovt_ref[...], p_cmp, preferred_element_type=F32)
    imp2 = imp4[:, :LANES] + imp4[:, LANES:]
    imp = imp2 + pltpu.roll(imp2, Q_BLOCK, 1)
    blk = lax.broadcasted_iota(jnp.int32, imp.shape, 0)
    blkf = blk.astype(F32)
    forced = (blk == 0) | (blk == qi) | (blk == qi - 1)
    score = jnp.where(blk <= qi, imp + jnp.where(forced, FORCE_BONUS, 0.0), -jnp.inf)
    notsel = jnp.ones(imp.shape, F32)
    for _ in range(n_sel):
        best = jnp.max(score, axis=0, keepdims=True)
        first = jnp.min(jnp.where(score == best, blkf, float(n_blk)), axis=0, keepdims=True)
        pick = blkf == first
        notsel = jnp.where(pick, 0.0, notsel)
        score = jnp.where(pick, -jnp.inf, score)
    notsel = jnp.concatenate([notsel, notsel], axis=1).astype(BF16)
    qaug = jnp.concatenate([qt, notsel], axis=0)

    def attend(k_ref, vt_ref, qmat, lo, hi, tile_of):
        def body(jp, carry):
            m, l, acc = carry
            off = pl.multiple_of(jp * (2 * SEL_LEN), 2 * SEL_LEN)
            sc = jnp.dot(k_ref[0, 0, pl.ds(off, 2 * SEL_LEN), :], qmat, preferred_element_type=F32)
            j0 = 2 * jp
            sc = sc + jnp.concatenate([nb_ref[0, tile_of(j0)], nb_ref[0, tile_of(j0 + 1)]], axis=0)
            m_new = jnp.maximum(m, jnp.max(sc, axis=0, keepdims=True))
            alpha = jnp.exp(m - m_new)
            p = jnp.exp(sc - m_new)
            l = alpha * l + jnp.sum(p, axis=0, keepdims=True)
            pv = jnp.dot(vt_ref[0, 0, :, pl.ds(off, 2 * SEL_LEN)], p.astype(BF16), preferred_element_type=F32)
            return m_new, l, alpha * acc + pv

        init = (jnp.full((1, ROWS), NEG, F32), jnp.zeros((1, ROWS), F32), jnp.zeros((HEAD_DIM, ROWS), F32))
        _, l, acc = lax.fori_loop(lo, hi, body, init)
        return acc / l

    def sel_tile(j):
        d = qi - j
        return jnp.where(d < 0, 4, jnp.minimum(d, 3))

    def win_tile(j):
        d = qi - j
        return jnp.where((d < 0) | (d > WIN_BLOCKS), 4, jnp.where(d == WIN_BLOCKS, 5, jnp.minimum(d, 3)))

    o_slc = attend(kaug_ref, vst_ref, qaug, 0, qi // 2 + 1, sel_tile)
    o_win = attend(kw_ref, vwt_ref, qt, jnp.maximum(qi - WIN_BLOCKS, 0) // 2, qi // 2 + 1, win_tile)
    g = g_ref[0, 0, 0]
    o_ref[0, 0, 0] = (g[0:1] * o_cmp + g[1:2] * o_slc + g[2:3] * o_win).astype(BF16)


def _nsa(qt, kaug, vst, kw, vwt, kcp, vcpt, ovt, nb, cb, gt):
    bsz, nkv, nq, dh, rows = qt.shape
    seq = kw.shape[2]
    n_blk = seq // SEL_LEN
    ncp = kcp.shape[2]
    n_cmp = (seq - CMP_LEN) // CMP_STRIDE + 1
    per_bg = lambda shape: pl.BlockSpec((1, 1) + shape, lambda b, g, i: (b, g, 0, 0))
    per_tile = lambda shape: pl.BlockSpec((1, 1, 1) + shape, lambda b, g, i: (b, g, i, 0, 0))
    kern = functools.partial(_nsa_kernel, n_blk=n_blk, n_cmp=n_cmp, n_sel=min(SEL_TOP, n_blk))
    return pl.pallas_call(
        kern,
        grid=(bsz, nkv, nq),
        in_specs=[per_tile((dh, rows)),
                  per_bg((seq, kaug.shape[3])), per_bg((dh, seq)), per_bg((seq, dh)), per_bg((dh, seq)),
                  per_bg((ncp, dh)), per_bg((dh, ncp)),
                  pl.BlockSpec(ovt.shape, lambda b, g, i: (0, 0)),
                  pl.BlockSpec((1,) + nb.shape[1:], lambda b, g, i: (g, 0, 0, 0)),
                  pl.BlockSpec((1,) + cb.shape[1:], lambda b, g, i: (g, 0, 0, 0)),
                  per_tile((SUBLANES, rows))],
        out_specs=per_tile((dh, rows)),
        out_shape=jax.ShapeDtypeStruct((bsz, nkv, nq, dh, rows), BF16),
        scratch_shapes=[pltpu.VMEM((ncp, rows), F32)],
        compiler_params=_cparams(("arbitrary", "arbitrary", "arbitrary")),
        name="nsa_attention",
    )(qt, kaug, vst, kw, vwt, kcp, vcpt, ovt, nb, cb, gt)


def _s5_kernel(u_ref, wb_ref, a_ref, wc_ref, d_ref, z_ref, xs_ref, st_ref, *, bsz, n_state):
    @pl.when(pl.program_id(0) == 0)
    def _():
        st_ref[...] = jnp.zeros_like(st_ref)

    u = u_ref[...]
    xs_ref[...] = jnp.dot(u.astype(BF16), wb_ref[...], preferred_element_type=F32)
    steps = u.shape[0] // bsz
    for c0 in range(0, n_state, SCAN_LANES):
        re = pl.ds(c0, SCAN_LANES)
        im = pl.ds(n_state + c0, SCAN_LANES)
        ar = jnp.broadcast_to(a_ref[0:1, re], (bsz, SCAN_LANES))
        ai = jnp.broadcast_to(a_ref[0:1, im], (bsz, SCAN_LANES))

        def step(t, carry):
            xr, xi = carry
            rows = pl.ds(pl.multiple_of(t * bsz, bsz), bsz)
            nr = ar * xr - ai * xi + xs_ref[rows, re]
            ni = ar * xi + ai * xr + xs_ref[rows, im]
            xs_ref[rows, re] = nr
            xs_ref[rows, im] = ni
            return nr, ni

        xr, xi = lax.fori_loop(0, steps, step, (st_ref[:, re], st_ref[:, im]), unroll=8)
        st_ref[:, re] = xr
        st_ref[:, im] = xi
    y = jnp.dot(xs_ref[...].astype(BF16), wc_ref[...], preferred_element_type=F32) + d_ref[...] * u
    z_ref[...] = jax.nn.gelu(y).astype(BF16)


def _s5(u2, wb, a, wc, dsk, bsz):
    rows, width = u2.shape
    seq = rows // bsz
    chunk = min(SCAN_CHUNK, seq)
    n_state2 = wb.shape[1]
    return pl.pallas_call(
        functools.partial(_s5_kernel, bsz=bsz, n_state=n_state2 // 2),
        grid=(seq // chunk,),
        in_specs=[pl.BlockSpec((chunk * bsz, width), lambda i: (i, 0)),
                  pl.BlockSpec(wb.shape, lambda i: (0, 0)),
                  pl.BlockSpec(a.shape, lambda i: (0, 0)),
                  pl.BlockSpec(wc.shape, lambda i: (0, 0)),
                  pl.BlockSpec(dsk.shape, lambda i: (0, 0))],
        out_specs=pl.BlockSpec((chunk * bsz, width), lambda i: (i, 0)),
        out_shape=jax.ShapeDtypeStruct((rows, width), BF16),
        scratch_shapes=[pltpu.VMEM((chunk * bsz, n_state2), F32), pltpu.VMEM((bsz, n_state2), F32)],
        compiler_params=_cparams(("arbitrary",)),
        name="s5_scan",
    )(u2, wb, a, wc, dsk)


def _mixout_kernel(z_ref, yn_ref, mg_ref, x_ref, g1_ref, sh2_ref, sc2_ref, g2_ref, lg_ref, lb_ref,
                   glu_ref, wo_ref, sg_ref, su_ref, sd_ref, base_ref, h_ref, *, alpha):
    dm = x_ref.shape[2]
    glu = jnp.dot(z_ref[...], glu_ref[...], preferred_element_type=F32)
    y_ssm = glu[:, :dm] * jax.nn.sigmoid(glu[:, dm:])
    mg = mg_ref[0].astype(F32)
    merged = mg[:, :dm] * y_ssm + mg[:, dm:] * yn_ref[0].astype(F32)
    y = jnp.dot(merged.astype(BF16), wo_ref[...], preferred_element_type=F32)
    x1 = _norm_rows(alpha * x_ref[0] + g1_ref[0] * y) * lg_ref[...] + lb_ref[...]
    h = (_norm_rows(x1) * (1.0 + sc2_ref[0]) + sh2_ref[0]).astype(BF16)
    h_ref[0] = h
    hs = jax.nn.silu(jnp.dot(h, sg_ref[...], preferred_element_type=F32)) * jnp.dot(h, su_ref[...], preferred_element_type=F32)
    shared = jnp.dot(hs.astype(BF16), sd_ref[...], preferred_element_type=F32)
    base_ref[0] = alpha * x1 + g2_ref[0] * shared


def _mix_out(z2d, y_nsa, mg, x, g1, sh2, sc2, g2, ln_g, ln_b, glu_w, w_out, sg, su, sd, alpha):
    bsz, seq, dm = x.shape
    tm = min(ROW_TILE, seq)
    width = z2d.shape[1] // bsz
    vec = pl.BlockSpec((1, 1, dm), lambda b, i: (b, 0, 0))
    row = pl.BlockSpec((1, dm), lambda b, i: (0, 0))
    full = lambda w: pl.BlockSpec(w.shape, lambda b, i: (0, 0))
    tile = lambda n: pl.BlockSpec((1, tm, n), lambda b, i: (b, i, 0))
    return pl.pallas_call(
        functools.partial(_mixout_kernel, alpha=alpha),
        grid=(bsz, seq // tm),
        in_specs=[pl.BlockSpec((tm, width), lambda b, i: (i, b)), tile(dm), tile(2 * dm), tile(dm),
                  vec, vec, vec, vec, row, row, full(glu_w), full(w_out), full(sg), full(su), full(sd)],
        out_specs=[tile(dm), tile(dm)],
        out_shape=[jax.ShapeDtypeStruct((bsz, seq, dm), F32), jax.ShapeDtypeStruct((bsz, seq, dm), BF16)],
        compiler_params=_cparams(("arbitrary", "arbitrary")),
        name="merge_outproj_ln_shared",
    )(z2d, y_nsa, mg, x, g1, sh2, sc2, g2, ln_g, ln_b, glu_w, w_out, sg, su, sd)


def _router_kernel(h_ref, rwt_ref, rb_ref, tri_ref, e_ref, w_ref, p_ref, cnt_ref):
    @pl.when(pl.program_id(0) == 0)
    def _():
        cnt_ref[...] = jnp.zeros_like(cnt_ref)

    logits = lax.dot_general(rwt_ref[...], h_ref[...], (((1,), (1,)), ((), ())), preferred_element_type=F32)
    scores = jax.nn.sigmoid(logits)
    cur = scores + rb_ref[...]
    n_exp = scores.shape[0]
    eid = lax.broadcasted_iota(jnp.int32, scores.shape, 0).astype(F32)
    chosen = jnp.zeros(scores.shape, F32)
    ids, vals = [], []
    for _ in range(TOP_K):
        best = jnp.max(cur, axis=0, keepdims=True)
        first = jnp.min(jnp.where(cur == best, eid, float(n_exp)), axis=0, keepdims=True)
        pick = eid == first
        ids.append(first)
        vals.append(jnp.sum(jnp.where(pick, scores, 0.0), axis=0, keepdims=True))
        chosen = jnp.where(pick, 1.0, chosen)
        cur = jnp.where(pick, -jnp.inf, cur)
    top_s = jnp.concatenate(vals, axis=0)
    w_ref[...] = top_s / jnp.sum(top_s, axis=0, keepdims=True) * ROUTED_SCALE
    top_e = jnp.concatenate(ids, axis=0)
    e_ref[...] = top_e.astype(jnp.int32)
    before = jnp.dot(chosen.astype(BF16), tri_ref[...], preferred_element_type=F32) + cnt_ref[...]
    ranks = [jnp.sum(jnp.where(eid == ids[k], before, 0.0), axis=0, keepdims=True) for k in range(TOP_K)]
    p_ref[...] = jnp.concatenate(ranks, axis=0).astype(jnp.int32)
    cnt_ref[...] += jnp.sum(chosen, axis=1, keepdims=True)


def _router(h2, rwt, rb, tri):
    n_tok, dm = h2.shape
    n_exp = rwt.shape[0]
    tm = tri.shape[0]
    kt = pl.BlockSpec((TOP_K, tm), lambda i: (0, i))
    return pl.pallas_call(
        _router_kernel,
        grid=(n_tok // tm,),
        in_specs=[pl.BlockSpec((tm, dm), lambda i: (i, 0)),
                  pl.BlockSpec((n_exp, dm), lambda i: (0, 0)),
                  pl.BlockSpec((n_exp, 1), lambda i: (0, 0)),
                  pl.BlockSpec((tm, tm), lambda i: (0, 0))],
        out_specs=[kt, kt, kt, pl.BlockSpec((n_exp, 1), lambda i: (0, 0))],
        out_shape=[jax.ShapeDtypeStruct((TOP_K, n_tok), jnp.int32),
                   jax.ShapeDtypeStruct((TOP_K, n_tok), F32),
                   jax.ShapeDtypeStruct((TOP_K, n_tok), jnp.int32),
                   jax.ShapeDtypeStruct((n_exp, 1), F32)],
        compiler_params=_cparams(("arbitrary",)),
        name="router_topk_rank",
    )(h2, rwt, rb, tri)


def _expert_kernel(te_ref, nu_ref, x_ref, wg_ref, wu_ref, wd_ref, y_ref):
    i = pl.program_id(0)

    @pl.when(i < nu_ref[0])
    def _():
        x = x_ref[...]
        hmid = jax.nn.silu(jnp.dot(x, wg_ref[0], preferred_element_type=F32)) * jnp.dot(x, wu_ref[0], preferred_element_type=F32)
        y_ref[...] = jnp.dot(hmid.astype(BF16), wd_ref[0], preferred_element_type=F32).astype(BF16)

    @pl.when(i >= nu_ref[0])
    def _():
        y_ref[...] = jnp.zeros_like(y_ref)


def _experts(tile_exp, n_used, xs, wg, wu, wd):
    n_rows, dm = xs.shape
    de = wg.shape[2]
    grid_spec = pltpu.PrefetchScalarGridSpec(
        num_scalar_prefetch=2,
        grid=(n_rows // EXPERT_TILE,),
        in_specs=[pl.BlockSpec((EXPERT_TILE, dm), lambda i, te, nu: (i, 0)),
                  pl.BlockSpec((1, dm, de), lambda i, te, nu: (te[i], 0, 0)),
                  pl.BlockSpec((1, dm, de), lambda i, te, nu: (te[i], 0, 0)),
                  pl.BlockSpec((1, de, dm), lambda i, te, nu: (te[i], 0, 0))],
        out_specs=pl.BlockSpec((EXPERT_TILE, dm), lambda i, te, nu: (i, 0)),
    )
    return pl.pallas_call(
        _expert_kernel,
        grid_spec=grid_spec,
        out_shape=jax.ShapeDtypeStruct((n_rows, dm), BF16),
        compiler_params=_cparams(("arbitrary",)),
        name="expert_mlp",
    )(tile_exp, n_used, xs, wg, wu, wd)


def _final_kernel(base_ref, r_ref, g2_ref, lg_ref, lb_ref, o_ref):
    o_ref[0] = _norm_rows(base_ref[0] + g2_ref[0] * r_ref[0]) * lg_ref[...] + lb_ref[...]


def _final(base, routed, g2, ln_g, ln_b):
    bsz, seq, dm = base.shape
    tm = min(ROW_TILE, seq)
    tile = pl.BlockSpec((1, tm, dm), lambda b, i: (b, i, 0))
    return pl.pallas_call(
        _final_kernel,
        grid=(bsz, seq // tm),
        in_specs=[tile, tile, pl.BlockSpec((1, 1, dm), lambda b, i: (b, 0, 0)),
                  pl.BlockSpec((1, dm), lambda b, i: (0, 0)), pl.BlockSpec((1, dm), lambda b, i: (0, 0))],
        out_specs=tile,
        out_shape=jax.ShapeDtypeStruct((bsz, seq, dm), F32),
        compiler_params=_cparams(("arbitrary", "arbitrary")),
        name="final_layernorm",
    )(base, routed, g2, ln_g, ln_b)


def _rel_bucket(dist):
    dist = jnp.maximum(dist, 0)
    exact = REL_BUCKETS // 2
    log_ratio = jnp.log(jnp.maximum(dist, 1).astype(F32) / exact) / math.log(REL_MAX_DIST / exact)
    large = jnp.minimum(exact + (log_ratio * (REL_BUCKETS - exact)).astype(jnp.int32), REL_BUCKETS - 1)
    return jnp.where(dist < exact, dist, large)


def _bias_tiles(rel_bias):
    n_d = 4 * SEL_LEN
    vec = rel_bias[_rel_bucket(jnp.arange(n_d))].T
    far = rel_bias[REL_BUCKETS - 1]
    vec = (vec - far[:, None]).reshape(N_KV, GQA, n_d)
    tok = np.arange(Q_BLOCK)[None, :]
    key = np.arange(SEL_LEN)[:, None]

    def toeplitz(d):
        vals = vec[:, :, np.clip(d, 0, n_d - 1)]
        vals = jnp.where(jnp.asarray(d >= 0), vals, NEG)
        return jnp.transpose(vals, (0, 2, 1, 3)).reshape(N_KV, d.shape[0], ROWS)

    near = [toeplitz(delta + tok - key) for delta in (0, SEL_LEN, 2 * SEL_LEN)]
    zero = jnp.zeros((N_KV, SEL_LEN, ROWS), F32)
    edge = np.where(tok < key, 0.0, NEG).astype(np.float32)
    edge = jnp.broadcast_to(jnp.asarray(np.tile(edge, (1, GQA)))[None], (N_KV, SEL_LEN, ROWS))
    nb = jnp.stack(near + [zero, jnp.full_like(zero, NEG), edge], axis=1)
    w = np.arange(BAND_ROWS)[:, None]
    bands = []
    for ph in (0, 1):
        d = tok - CMP_STRIDE * (w - PAD_CMP - (Q_BLOCK // CMP_STRIDE) * ph) - (CMP_LEN - 1)
        vals = vec[:, :, np.clip(d, 0, n_d - 1)]
        vals = jnp.where(jnp.asarray(d >= 0), vals, 0.0)
        bands.append(jnp.transpose(vals, (0, 2, 1, 3)).reshape(N_KV, BAND_ROWS, ROWS))
    cb = jnp.stack(bands, axis=1)
    return nb.astype(F32), cb.astype(F32)


def _padded_cmp_rows(seq):
    return -(-(seq // CMP_STRIDE + 2 * PAD_CMP) // LANES) * LANES


def _overlap_t(seq):
    n_cmp = (seq - CMP_LEN) // CMP_STRIDE + 1
    n_blk = seq // SEL_LEN
    ncp = _padded_cmp_rows(seq)
    c_start = np.arange(n_cmp) * CMP_STRIDE
    c_end = c_start + CMP_LEN - 1
    blk = np.arange(n_blk)
    ov = ((c_start[:, None] < (blk[None, :] + 1) * SEL_LEN) & (c_end[:, None] >= blk[None, :] * SEL_LEN))
    out = np.zeros((n_blk, ncp), np.float32)
    out[:, PAD_CMP:PAD_CMP + n_cmp] = ov.T
    return jnp.asarray(out, BF16)


def _s5_params(lam_re, lam_im, log_step, b_re, b_im, c_re, c_im):
    lr, li = lam_re.astype(F32), lam_im.astype(F32)
    dt = jnp.exp(log_step.astype(F32))[:, None]
    mag = jnp.exp(lr * dt)
    ar, ai = mag * jnp.cos(li * dt), mag * jnp.sin(li * dt)
    den = lr * lr + li * li
    kr = ((ar - 1.0) * lr + ai * li) / den
    ki = (ai * lr - (ar - 1.0) * li) / den
    br, bi = b_re.astype(F32), b_im.astype(F32)
    bbr = kr[..., None] * br - ki[..., None] * bi
    bbi = kr[..., None] * bi + ki[..., None] * br
    n_g = lr.shape[0]
    eye = jnp.eye(n_g, dtype=F32)

    def drive(bb):
        return jnp.einsum('gpc,gh->gchp', bb, eye).reshape(n_g * SSM_GROUP, n_g * SSM_STATE)

    def readout(c):
        return jnp.einsum('gcp,gh->gphc', c, eye).reshape(n_g * SSM_STATE, n_g * SSM_GROUP)

    wb = jnp.concatenate([drive(bbr), drive(bbi)], axis=1).astype(BF16)
    wc = jnp.concatenate([readout(c_re.astype(F32)), -readout(c_im.astype(F32))], axis=0).astype(BF16)
    a = jnp.concatenate([ar.reshape(1, -1), ai.reshape(1, -1)], axis=1)
    return wb, jnp.broadcast_to(a, (SUBLANES, a.shape[1])), wc


def _layer(x, mod, w_in, lam_re, lam_im, log_step, b_re, b_im, c_re, c_im, d_skip, glu_w, cmp_pos, cmp_w1,
           cmp_w2, rel_bias, w_out, ln1_g, ln1_b, router_w, router_bias, e_gate, e_up, e_down, sg, su, sd,
           ln2_g, ln2_b, alpha):
    bsz, seq, dm = x.shape
    n_tok = bsz * seq
    sh1, sc1, g1, sh2, sc2, g2 = [m[:, None, :] for m in jnp.split(mod, 6, axis=-1)]
    ssm_w = dm // 2
    attn_w = N_HEADS * HEAD_DIM
    kv_w = N_KV * HEAD_DIM
    n_gate = 3 * N_HEADS
    offs = np.cumsum([0, ssm_w, attn_w] + [kv_w] * 6 + [n_gate, 2 * dm])

    wu = w_in[:, offs[0]:offs[1]].astype(BF16)
    wq = w_in[:, offs[1]:offs[2]] * (HEAD_DIM ** -0.5)
    wa = jnp.concatenate([wq, w_in[:, offs[2]:offs[8]]], axis=1).astype(BF16)
    wg = jnp.pad(w_in[:, offs[8]:offs[9]], ((0, 0), (0, LANES - n_gate))).astype(BF16)
    wm = w_in[:, offs[9]:offs[10]].astype(BF16)
    u2d, act, gates, mg = _input_projection(x, sh1, sc1, wu, wa, wg, wm)

    nq = seq // Q_BLOCK
    n_blk = seq // SEL_LEN
    nch = seq // CMP_STRIDE

    def piece(i):
        return act[:, :, attn_w + i * kv_w: attn_w + (i + 1) * kv_w]

    def compress(raw, pos, w1, w2):
        xc = raw.reshape(bsz, nch, CMP_STRIDE, N_KV, HEAD_DIM).transpose(0, 3, 1, 2, 4).reshape(bsz, N_KV, nch, CMP_STRIDE * HEAD_DIM)
        half = CMP_STRIDE * HEAD_DIM
        w1cat = jnp.concatenate([w1[:half], w1[half:]], axis=1).astype(BF16)
        posb = jnp.dot(pos.reshape(1, -1), w1, precision=lax.Precision.HIGHEST)
        return _compress(xc, w1cat, posb, w2.astype(BF16))

    kc = compress(piece(0), cmp_pos[0], cmp_w1[0], cmp_w2[0])
    vc = compress(piece(1), cmp_pos[1], cmp_w1[1], cmp_w2[1])
    pad = ((0, 0), (0, 0), (PAD_CMP, _padded_cmp_rows(seq) - nch - PAD_CMP), (0, 0))
    kcp = jnp.pad(kc, pad).astype(BF16)
    vcpt = jnp.swapaxes(jnp.pad(vc, pad), 2, 3).astype(BF16)

    def heads(a):
        return a.reshape(bsz, seq, N_KV, HEAD_DIM).transpose(0, 2, 1, 3)

    q = act[:, :, :attn_w].reshape(bsz, nq, Q_BLOCK, N_KV, GQA, HEAD_DIM)
    qt = q.transpose(0, 3, 1, 5, 4, 2).reshape(bsz, N_KV, nq, HEAD_DIM, ROWS)
    onehot = (np.arange(seq)[:, None] // SEL_LEN == np.arange(n_blk)[None, :]).astype(np.float32) * -MASK_BIG
    kaug = jnp.concatenate([heads(piece(2)), jnp.broadcast_to(jnp.asarray(onehot, BF16), (bsz, N_KV, seq, n_blk))], axis=3)
    vst = jnp.swapaxes(heads(piece(3)), 2, 3)
    kw = heads(piece(4))
    vwt = jnp.swapaxes(heads(piece(5)), 2, 3)
    gt = gates[:, :, :n_gate].reshape(bsz, nq, Q_BLOCK, N_KV, GQA, 3).transpose(0, 3, 1, 5, 4, 2).reshape(bsz, N_KV, nq, 3, ROWS)
    gt = jnp.pad(gt, ((0, 0), (0, 0), (0, 0), (0, SUBLANES - 3), (0, 0)))
    nb, cb = _bias_tiles(rel_bias)
    ot = _nsa(qt, kaug, vst, kw, vwt, kcp, vcpt, _overlap_t(seq), nb, cb, gt)
    y_nsa = ot.reshape(bsz, N_KV, nq, HEAD_DIM, GQA, Q_BLOCK).transpose(0, 2, 5, 1, 4, 3).reshape(bsz, seq, attn_w)

    wb, a, wc = _s5_params(lam_re, lam_im, log_step, b_re, b_im, c_re, c_im)
    z2 = _s5(u2d.reshape(seq * bsz, ssm_w), wb, a, wc, d_skip.reshape(1, ssm_w).astype(F32), bsz)
    z2d = z2.reshape(seq, bsz * ssm_w)

    base, h2 = _mix_out(z2d, y_nsa, mg, x, g1, sh2, sc2, g2, ln1_g.reshape(1, dm), ln1_b.reshape(1, dm),
                        glu_w.astype(BF16), w_out.astype(BF16), sg.astype(BF16), su.astype(BF16), sd.astype(BF16), alpha)
    h2 = h2.reshape(n_tok, dm)

    n_exp = router_w.shape[1]
    rt = min(ROUTER_TILE, n_tok)
    tri = jnp.asarray(np.triu(np.ones((rt, rt), np.float32), 1), BF16)
    top_e, top_w, rank, counts = _router(h2, router_w.T.astype(BF16), router_bias.reshape(n_exp, 1).astype(F32), tri)
    counts = counts[:, 0].astype(jnp.int32)
    padded = (counts + EXPERT_TILE - 1) // EXPERT_TILE * EXPERT_TILE
    pad_end = jnp.cumsum(padded)
    pad_start = pad_end - padded
    dest = pad_start[top_e] + rank
    n_rows = n_tok * TOP_K + n_exp * EXPERT_TILE
    n_tiles = n_rows // EXPERT_TILE
    tile_exp = jnp.minimum(jnp.searchsorted(pad_end, jnp.arange(n_tiles) * EXPERT_TILE, side='right'), n_exp - 1).astype(jnp.int32)
    n_used = (pad_end[-1] // EXPERT_TILE).astype(jnp.int32).reshape(1)
    tok = jnp.broadcast_to(jnp.arange(n_tok, dtype=jnp.int32)[None, :], dest.shape)
    row_tok = jnp.full((n_rows,), n_tok, jnp.int32).at[dest.reshape(-1)].set(tok.reshape(-1))
    h_pad = jnp.concatenate([h2, jnp.zeros((1, dm), h2.dtype)], axis=0)
    xs = h_pad[row_tok]
    ys = _experts(tile_exp, n_used, xs, e_gate.astype(BF16), e_up.astype(BF16), e_down.astype(BF16))
    routed = jnp.einsum('ktd,kt->td', ys[dest].astype(F32), top_w)
    return _final(base, routed.reshape(bsz, seq, dm), g2, ln2_g.reshape(1, dm), ln2_b.reshape(1, dm))


def kernel(x, c, ada_w, ada_b, w_in, ssm_lambda_re, ssm_lambda_im, ssm_log_step, ssm_b_re, ssm_b_im, ssm_c_re, ssm_c_im, ssm_d, ssm_glu_w, cmp_pos, cmp_w1, cmp_w2, rel_bias, w_out, ln1_g, ln1_b, router_w, router_bias, exp_w_gate, exp_w_up, exp_w_down, sh_w_gate, sh_w_up, sh_w_down, ln2_g, ln2_b):
    depth = ada_w.shape[0]
    alpha = (2 * depth) ** 0.25
    for l in range(depth):
        mod = _modulation(c, ada_w[l], ada_b[l])
        x = _layer(x, mod, w_in[l], ssm_lambda_re[l], ssm_lambda_im[l], ssm_log_step[l], ssm_b_re[l], ssm_b_im[l],
                   ssm_c_re[l], ssm_c_im[l], ssm_d[l], ssm_glu_w[l], cmp_pos[l], cmp_w1[l], cmp_w2[l], rel_bias,
                   w_out[l], ln1_g[l], ln1_b[l], router_w[l], router_bias[l], exp_w_gate[l], exp_w_up[l],
                   exp_w_down[l], sh_w_gate[l], sh_w_up[l], sh_w_down[l], ln2_g[l], ln2_b[l], alpha)
    return x
```

```python
import functools
import math

import numpy as np
import jax
import jax.numpy as jnp
from jax import lax
from jax.experimental import pallas as pl
from jax.experimental.pallas import tpu as pltpu

F32 = jnp.float32
BF16 = jnp.bfloat16

SSM_GROUP = 16
SSM_STATE = 64
N_HEADS = 16
HEAD_DIM = 64
N_KV = 4
GQA = N_HEADS // N_KV
CMP_LEN = 32
CMP_STRIDE = 16
SEL_LEN = 64
SEL_TOP = 16
WINDOW = 512
Q_BLOCK = 64
FORCE_BONUS = 1.0e4
REL_BUCKETS = 32
REL_MAX_DIST = 128
TOP_K = 8
ROUTED_SCALE = 2.5
LN_EPS = 1e-5

LANES = 128
SUBLANES = 8
VMEM_LIMIT_BYTES = 56 * 1024 * 1024
ROW_TILE = 512
ROUTER_TILE = 256
EXPERT_TILE = 256
SCAN_CHUNK = 128
SCAN_LANES = 512
MASK_BIG = 32768.0
NEG = -1.0e30
WIN_BLOCKS = WINDOW // SEL_LEN
ROWS = GQA * Q_BLOCK
NSA_TILES = 4
COLS = NSA_TILES * ROWS
CHUNK_BLOCKS = 8
CHUNK_KEYS = CHUNK_BLOCKS * SEL_LEN
PAD_CMP = 8
BAND_ROWS = 24


def _cparams(sem):
    return pltpu.CompilerParams(dimension_semantics=sem, vmem_limit_bytes=VMEM_LIMIT_BYTES)


def _norm_rows(x):
    mu = jnp.mean(x, axis=-1, keepdims=True)
    xc = x - mu
    var = jnp.mean(xc * xc, axis=-1, keepdims=True)
    return xc * lax.rsqrt(var + LN_EPS)


def _mod_kernel(c_ref, w_ref, b_ref, o_ref):
    cond = jax.nn.silu(c_ref[...])
    o_ref[...] = jnp.dot(cond.astype(BF16), w_ref[...].astype(BF16), preferred_element_type=F32) + b_ref[...]


def _modulation(c, ada_w, ada_b):
    bsz, dm = c.shape
    n = ada_w.shape[1]
    tn = dm
    return pl.pallas_call(
        _mod_kernel,
        grid=(n // tn,),
        in_specs=[pl.BlockSpec((bsz, dm), lambda j: (0, 0)),
                  pl.BlockSpec((dm, tn), lambda j: (0, j)),
                  pl.BlockSpec((1, tn), lambda j: (0, j))],
        out_specs=pl.BlockSpec((bsz, tn), lambda j: (0, j)),
        out_shape=jax.ShapeDtypeStruct((bsz, n), F32),
        compiler_params=_cparams(("arbitrary",)),
        name="ada_modulation",
    )(c, ada_w, ada_b.reshape(1, n))


def _inproj_kernel(x_ref, sh_ref, sc_ref, wu_ref, wa_ref, wg_ref, wm_ref, u_ref, a_ref, g_ref, m_ref):
    h = _norm_rows(x_ref[0]) * (1.0 + sc_ref[0]) + sh_ref[0]
    hb = h.astype(BF16)
    u_ref[...] = jnp.dot(hb, wu_ref[...], preferred_element_type=F32)
    a_ref[0] = jnp.dot(hb, wa_ref[...], preferred_element_type=F32).astype(BF16)
    g_ref[0] = jax.nn.sigmoid(jnp.dot(hb, wg_ref[...], preferred_element_type=F32))
    m_ref[0] = jax.nn.sigmoid(jnp.dot(hb, wm_ref[...], preferred_element_type=F32)).astype(BF16)


def _input_projection(x, sh1, sc1, wu, wa, wg, wm):
    bsz, seq, dm = x.shape
    tm = min(ROW_TILE, seq)
    nu, na, ng, nm = wu.shape[1], wa.shape[1], wg.shape[1], wm.shape[1]
    full = lambda n: pl.BlockSpec((dm, n), lambda b, i: (0, 0))
    vec = pl.BlockSpec((1, 1, dm), lambda b, i: (b, 0, 0))
    return pl.pallas_call(
        _inproj_kernel,
        grid=(bsz, seq // tm),
        in_specs=[pl.BlockSpec((1, tm, dm), lambda b, i: (b, i, 0)), vec, vec,
                  full(nu), full(na), full(ng), full(nm)],
        out_specs=[pl.BlockSpec((tm, nu), lambda b, i: (i, b)),
                   pl.BlockSpec((1, tm, na), lambda b, i: (b, i, 0)),
                   pl.BlockSpec((1, tm, ng), lambda b, i: (b, i, 0)),
                   pl.BlockSpec((1, tm, nm), lambda b, i: (b, i, 0))],
        out_shape=[jax.ShapeDtypeStruct((seq, bsz * nu), F32),
                   jax.ShapeDtypeStruct((bsz, seq, na), BF16),
                   jax.ShapeDtypeStruct((bsz, seq, ng), F32),
                   jax.ShapeDtypeStruct((bsz, seq, nm), BF16)],
        compiler_params=_cparams(("arbitrary", "arbitrary")),
        name="adaln_input_projection",
    )(x, sh1, sc1, wu, wa, wg, wm)


def _compress_kernel(x_ref, w1_ref, pb_ref, w2_ref, o_ref, *, n_cmp):
    hid = w2_ref.shape[0]
    p = jnp.dot(x_ref[0, 0], w1_ref[...], preferred_element_type=F32)
    nrow = p.shape[0]
    nxt = pltpu.roll(p[:, hid:], nrow - 1, 0)
    hidv = jax.nn.gelu(p[:, :hid] + nxt + pb_ref[...])
    out = jnp.dot(hidv.astype(BF16), w2_ref[...], preferred_element_type=F32)
    rows = lax.broadcasted_iota(jnp.int32, out.shape, 0)
    o_ref[0, 0] = jnp.where(rows < n_cmp, out, 0.0)


def _compress(xc, w1cat, posb, w2):
    bsz, nkv, nch, kdim = xc.shape
    hid2 = w1cat.shape[1]
    return pl.pallas_call(
        functools.partial(_compress_kernel, n_cmp=nch - 1),
        grid=(bsz, nkv),
        in_specs=[pl.BlockSpec((1, 1, nch, kdim), lambda b, g: (b, g, 0, 0)),
                  pl.BlockSpec((kdim, hid2), lambda b, g: (0, 0)),
                  pl.BlockSpec((1, hid2 // 2), lambda b, g: (0, 0)),
                  pl.BlockSpec((hid2 // 2, HEAD_DIM), lambda b, g: (0, 0))],
        out_specs=pl.BlockSpec((1, 1, nch, HEAD_DIM), lambda b, g: (b, g, 0, 0)),
        out_shape=jax.ShapeDtypeStruct((bsz, nkv, nch, HEAD_DIM), F32),
        compiler_params=_cparams(("arbitrary", "arbitrary")),
        name="kv_compress",
    )(xc, w1cat, posb, w2)


def _nsa_kernel(qt_ref, kaug_ref, vst_ref, kw_ref, vwt_ref, kcp_ref, vcpt_ref, ovt_ref, nb_ref, cb_ref,
                g_ref, o_ref, s_ref, c_ref, *, n_blk, n_cmp, n_sel):
    q0 = pl.program_id(2) * NSA_TILES
    qt = qt_ref[0, 0, 0]
    col_tile = lambda n: slice(n * ROWS, (n + 1) * ROWS)

    s_ref[...] = jnp.dot(kcp_ref[0, 0], qt, preferred_element_type=F32)
    for n in range(NSA_TILES):
        band = pl.multiple_of(SUBLANES * ((q0 + n) // 2), SUBLANES)
        s_ref[pl.ds(band, BAND_ROWS), col_tile(n)] += cb_ref[0, n % 2]
    s = s_ref[...]
    rho = lax.broadcasted_iota(jnp.int32, s.shape, 0)
    col = lax.broadcasted_iota(jnp.int32, s.shape, 1)
    tok = col & (Q_BLOCK - 1)
    qblk = q0 + jnp.right_shift(col, ROWS.bit_length() - 1)
    first_visible = CMP_STRIDE * (rho - PAD_CMP) + (CMP_LEN - 1) - Q_BLOCK * qblk
    vis = (rho >= PAD_CMP) & (rho < PAD_CMP + n_cmp) & (first_visible <= tok)
    s = jnp.where(vis, s, -jnp.inf)
    mx = jnp.max(s, axis=0, keepdims=True)
    mx = jnp.where(mx == -jnp.inf, 0.0, mx)
    e = jnp.exp(s - mx)
    p_cmp = (e / jnp.maximum(jnp.sum(e, axis=0, keepdims=True), 1e-30)).astype(BF16)
    o_cmp = jnp.dot(vcpt_ref[0, 0], p_cmp, preferred_element_type=F32)

    imp4 = jnp.dot(ovt_ref[...], p_cmp, preferred_element_type=F32)
    parts = []
    for n in range(NSA_TILES):
        two = imp4[:, n * ROWS:n * ROWS + LANES] + imp4[:, n * ROWS + LANES:(n + 1) * ROWS]
        parts.append(two + pltpu.roll(two, Q_BLOCK, 1))
    imp = jnp.concatenate(parts, axis=1)
    blk = lax.broadcasted_iota(jnp.int32, imp.shape, 0)
    cur = q0 + jnp.right_shift(lax.broadcasted_iota(jnp.int32, imp.shape, 1), LANES.bit_length() - 1)
    blkf = blk.astype(F32)
    forced = (blk == 0) | (blk == cur) | (blk == cur - 1)
    score = jnp.where(blk <= cur, imp + jnp.where(forced, FORCE_BONUS, 0.0), -jnp.inf)
    notsel = jnp.ones(imp.shape, F32)
    for _ in range(n_sel):
        best = jnp.max(score, axis=0, keepdims=True)
        first = jnp.min(jnp.where(score == best, blkf, float(n_blk)), axis=0, keepdims=True)
        pick = blkf == first
        notsel = jnp.where(pick, 0.0, notsel)
        score = jnp.where(pick, -jnp.inf, score)
    notsel = notsel.astype(BF16)
    halves = [notsel[:, n * LANES:(n + 1) * LANES] for n in range(NSA_TILES)]
    qaug = jnp.concatenate([qt, jnp.concatenate([h for h in halves for _ in (0, 1)], axis=1)], axis=0)

    def update(carry, sc, vt):
        m, l, acc = carry
        m_new = jnp.maximum(m, jnp.max(sc, axis=0, keepdims=True))
        alpha = jnp.exp(m - m_new)
        p = jnp.exp(sc - m_new)
        l = alpha * l + jnp.sum(p, axis=0, keepdims=True)
        return m_new, l, alpha * acc + jnp.dot(vt, p.astype(BF16), preferred_element_type=F32)

    def far_chunk(c, carry):
        keys = pl.ds(pl.multiple_of(c * CHUNK_KEYS, CHUNK_KEYS), CHUNK_KEYS)
        sc = jnp.dot(kaug_ref[0, 0, keys, :], qaug, preferred_element_type=F32)
        return update(carry, sc, vst_ref[0, 0, :, keys])

    def near_chunk(k_ref, vt_ref, qmat, blk0, tile_of, carry):
        keys = pl.ds(pl.multiple_of(blk0 * SEL_LEN, NSA_TILES * SEL_LEN), CHUNK_KEYS)
        c_ref[...] = jnp.dot(k_ref[0, 0, keys, :], qmat, preferred_element_type=F32)
        for o in range(CHUNK_BLOCKS):
            for n in range(NSA_TILES):
                c_ref[o * SEL_LEN:(o + 1) * SEL_LEN, col_tile(n)] += nb_ref[0, tile_of(blk0 + o, q0 + n)]
        return update(carry, c_ref[...], vt_ref[0, 0, :, keys])

    def sel_tile(j, qi):
        d = qi - j
        return jnp.where(d < 0, 4, jnp.minimum(d, 3))

    def win_tile(j, qi):
        d = qi - j
        return jnp.where((d < 0) | (d > WIN_BLOCKS), 4, jnp.where(d == WIN_BLOCKS, 5, jnp.minimum(d, 3)))

    init = (jnp.full((1, COLS), NEG, F32), jnp.zeros((1, COLS), F32), jnp.zeros((HEAD_DIM, COLS), F32))
    n_far = jnp.maximum(q0 - 2, 0) // CHUNK_BLOCKS
    carry = lax.fori_loop(0, n_far, far_chunk, init)
    _, l, acc = lax.fori_loop(
        n_far, (q0 + NSA_TILES - 1) // CHUNK_BLOCKS + 1,
        lambda c, cr: near_chunk(kaug_ref, vst_ref, qaug, c * CHUNK_BLOCKS, sel_tile, cr), carry)
    o_slc = acc / l
    win0 = jnp.maximum(q0 - (WIN_BLOCKS + NSA_TILES), 0)
    _, l, acc = lax.fori_loop(
        0, 2, lambda c, cr: near_chunk(kw_ref, vwt_ref, qt, win0 + c * CHUNK_BLOCKS, win_tile, cr), init)
    o_win = acc / l
    g = g_ref[0, 0, 0]
    o_ref[0, 0, 0] = (g[0:1] * o_cmp + g[1:2] * o_slc + g[2:3] * o_win).astype(BF16)


def _nsa(qt, kaug, vst, kw, vwt, kcp, vcpt, ovt, nb, cb, gt):
    bsz, nkv, nsteps, dh, cols = qt.shape
    seq = kw.shape[2]
    n_blk = seq // SEL_LEN
    ncp = kcp.shape[2]
    n_cmp = (seq - CMP_LEN) // CMP_STRIDE + 1
    assert n_blk % CHUNK_BLOCKS == 0 and n_blk >= 2 * CHUNK_BLOCKS and WIN_BLOCKS + 2 * NSA_TILES == 2 * CHUNK_BLOCKS
    per_bg = lambda shape: pl.BlockSpec((1, 1) + shape, lambda b, g, i: (b, g, 0, 0))
    per_step = lambda shape: pl.BlockSpec((1, 1, 1) + shape, lambda b, g, i: (b, g, i, 0, 0))
    kern = functools.partial(_nsa_kernel, n_blk=n_blk, n_cmp=n_cmp, n_sel=min(SEL_TOP, n_blk))
    return pl.pallas_call(
        kern,
        grid=(bsz, nkv, nsteps),
        in_specs=[per_step((dh, cols)),
                  per_bg((seq, kaug.shape[3])), per_bg((dh, seq)), per_bg((seq, dh)), per_bg((dh, seq)),
                  per_bg((ncp, dh)), per_bg((dh, ncp)),
                  pl.BlockSpec(ovt.shape, lambda b, g, i: (0, 0)),
                  pl.BlockSpec((1,) + nb.shape[1:], lambda b, g, i: (g, 0, 0, 0)),
                  pl.BlockSpec((1,) + cb.shape[1:], lambda b, g, i: (g, 0, 0, 0)),
                  per_step((SUBLANES, cols))],
        out_specs=per_step((dh, cols)),
        out_shape=jax.ShapeDtypeStruct((bsz, nkv, nsteps, dh, cols), BF16),
        scratch_shapes=[pltpu.VMEM((ncp, cols), F32), pltpu.VMEM((CHUNK_KEYS, cols), F32)],
        compiler_params=_cparams(("arbitrary", "arbitrary", "arbitrary")),
        name="nsa_attention",
    )(qt, kaug, vst, kw, vwt, kcp, vcpt, ovt, nb, cb, gt)


def _s5_kernel(u_ref, wb_ref, a_ref, wc_ref, d_ref, z_ref, xs_ref, st_ref, *, bsz, n_state):
    @pl.when(pl.program_id(0) == 0)
    def _():
        st_ref[...] = jnp.zeros_like(st_ref)

    u = u_ref[...]
    xs_ref[...] = jnp.dot(u.astype(BF16), wb_ref[...], preferred_element_type=F32)
    steps = u.shape[0] // bsz
    for c0 in range(0, n_state, SCAN_LANES):
        re = pl.ds(c0, SCAN_LANES)
        im = pl.ds(n_state + c0, SCAN_LANES)
        ar = jnp.broadcast_to(a_ref[0:1, re], (bsz, SCAN_LANES))
        ai = jnp.broadcast_to(a_ref[0:1, im], (bsz, SCAN_LANES))

        def step(t, carry):
            xr, xi = carry
            rows = pl.ds(pl.multiple_of(t * bsz, bsz), bsz)
            nr = ar * xr - ai * xi + xs_ref[rows, re]
            ni = ar * xi + ai * xr + xs_ref[rows, im]
            xs_ref[rows, re] = nr
            xs_ref[rows, im] = ni
            return nr, ni

        xr, xi = lax.fori_loop(0, steps, step, (st_ref[:, re], st_ref[:, im]), unroll=8)
        st_ref[:, re] = xr
        st_ref[:, im] = xi
    y = jnp.dot(xs_ref[...].astype(BF16), wc_ref[...], preferred_element_type=F32) + d_ref[...] * u
    z_ref[...] = jax.nn.gelu(y).astype(BF16)


def _s5(u2, wb, a, wc, dsk, bsz):
    rows, width = u2.shape
    seq = rows // bsz
    chunk = min(SCAN_CHUNK, seq)
    n_state2 = wb.shape[1]
    return pl.pallas_call(
        functools.partial(_s5_kernel, bsz=bsz, n_state=n_state2 // 2),
        grid=(seq // chunk,),
        in_specs=[pl.BlockSpec((chunk * bsz, width), lambda i: (i, 0)),
                  pl.BlockSpec(wb.shape, lambda i: (0, 0)),
                  pl.BlockSpec(a.shape, lambda i: (0, 0)),
                  pl.BlockSpec(wc.shape, lambda i: (0, 0)),
                  pl.BlockSpec(dsk.shape, lambda i: (0, 0))],
        out_specs=pl.BlockSpec((chunk * bsz, width), lambda i: (i, 0)),
        out_shape=jax.ShapeDtypeStruct((rows, width), BF16),
        scratch_shapes=[pltpu.VMEM((chunk * bsz, n_state2), F32), pltpu.VMEM((bsz, n_state2), F32)],
        compiler_params=_cparams(("arbitrary",)),
        name="s5_scan",
    )(u2, wb, a, wc, dsk)


def _mixout_kernel(z_ref, yn_ref, mg_ref, x_ref, g1_ref, sh2_ref, sc2_ref, g2_ref, lg_ref, lb_ref,
                   glu_ref, wo_ref, sg_ref, su_ref, sd_ref, base_ref, h_ref, *, alpha):
    dm = x_ref.shape[2]
    glu = jnp.dot(z_ref[...], glu_ref[...], preferred_element_type=F32)
    y_ssm = glu[:, :dm] * jax.nn.sigmoid(glu[:, dm:])
    mg = mg_ref[0].astype(F32)
    merged = mg[:, :dm] * y_ssm + mg[:, dm:] * yn_ref[0].astype(F32)
    y = jnp.dot(merged.astype(BF16), wo_ref[...], preferred_element_type=F32)
    x1 = _norm_rows(alpha * x_ref[0] + g1_ref[0] * y) * lg_ref[...] + lb_ref[...]
    h = (_norm_rows(x1) * (1.0 + sc2_ref[0]) + sh2_ref[0]).astype(BF16)
    h_ref[0] = h
    hs = jax.nn.silu(jnp.dot(h, sg_ref[...], preferred_element_type=F32)) * jnp.dot(h, su_ref[...], preferred_element_type=F32)
    shared = jnp.dot(hs.astype(BF16), sd_ref[...], preferred_element_type=F32)
    base_ref[0] = alpha * x1 + g2_ref[0] * shared


def _mix_out(z2d, y_nsa, mg, x, g1, sh2, sc2, g2, ln_g, ln_b, glu_w, w_out, sg, su, sd, alpha):
    bsz, seq, dm = x.shape
    tm = min(ROW_TILE, seq)
    width = z2d.shape[1] // bsz
    vec = pl.BlockSpec((1, 1, dm), lambda b, i: (b, 0, 0))
    row = pl.BlockSpec((1, dm), lambda b, i: (0, 0))
    full = lambda w: pl.BlockSpec(w.shape, lambda b, i: (0, 0))
    tile = lambda n: pl.BlockSpec((1, tm, n), lambda b, i: (b, i, 0))
    return pl.pallas_call(
        functools.partial(_mixout_kernel, alpha=alpha),
        grid=(bsz, seq // tm),
        in_specs=[pl.BlockSpec((tm, width), lambda b, i: (i, b)), tile(dm), tile(2 * dm), tile(dm),
                  vec, vec, vec, vec, row, row, full(glu_w), full(w_out), full(sg), full(su), full(sd)],
        out_specs=[tile(dm), tile(dm)],
        out_shape=[jax.ShapeDtypeStruct((bsz, seq, dm), F32), jax.ShapeDtypeStruct((bsz, seq, dm), BF16)],
        compiler_params=_cparams(("arbitrary", "arbitrary")),
        name="merge_outproj_ln_shared",
    )(z2d, y_nsa, mg, x, g1, sh2, sc2, g2, ln_g, ln_b, glu_w, w_out, sg, su, sd)


def _router_kernel(h_ref, rwt_ref, rb_ref, tri_ref, e_ref, w_ref, p_ref, cnt_ref):
    @pl.when(pl.program_id(0) == 0)
    def _():
        cnt_ref[...] = jnp.zeros_like(cnt_ref)

    logits = lax.dot_general(rwt_ref[...], h_ref[...], (((1,), (1,)), ((), ())), preferred_element_type=F32)
    scores = jax.nn.sigmoid(logits)
    cur = scores + rb_ref[...]
    n_exp = scores.shape[0]
    eid = lax.broadcasted_iota(jnp.int32, scores.shape, 0).astype(F32)
    chosen = jnp.zeros(scores.shape, F32)
    ids, vals = [], []
    for _ in range(TOP_K):
        best = jnp.max(cur, axis=0, keepdims=True)
        first = jnp.min(jnp.where(cur == best, eid, float(n_exp)), axis=0, keepdims=True)
        pick = eid == first
        ids.append(first)
        vals.append(jnp.sum(jnp.where(pick, scores, 0.0), axis=0, keepdims=True))
        chosen = jnp.where(pick, 1.0, chosen)
        cur = jnp.where(pick, -jnp.inf, cur)
    top_s = jnp.concatenate(vals, axis=0)
    w_ref[...] = top_s / jnp.sum(top_s, axis=0, keepdims=True) * ROUTED_SCALE
    top_e = jnp.concatenate(ids, axis=0)
    e_ref[...] = top_e.astype(jnp.int32)
    before = jnp.dot(chosen.astype(BF16), tri_ref[...], preferred_element_type=F32) + cnt_ref[...]
    ranks = [jnp.sum(jnp.where(eid == ids[k], before, 0.0), axis=0, keepdims=True) for k in range(TOP_K)]
    p_ref[...] = jnp.concatenate(ranks, axis=0).astype(jnp.int32)
    cnt_ref[...] += jnp.sum(chosen, axis=1, keepdims=True)


def _router(h2, rwt, rb, tri):
    n_tok, dm = h2.shape
    n_exp = rwt.shape[0]
    tm = tri.shape[0]
    kt = pl.BlockSpec((TOP_K, tm), lambda i: (0, i))
    return pl.pallas_call(
        _router_kernel,
        grid=(n_tok // tm,),
        in_specs=[pl.BlockSpec((tm, dm), lambda i: (i, 0)),
                  pl.BlockSpec((n_exp, dm), lambda i: (0, 0)),
                  pl.BlockSpec((n_exp, 1), lambda i: (0, 0)),
                  pl.BlockSpec((tm, tm), lambda i: (0, 0))],
        out_specs=[kt, kt, kt, pl.BlockSpec((n_exp, 1), lambda i: (0, 0))],
        out_shape=[jax.ShapeDtypeStruct((TOP_K, n_tok), jnp.int32),
                   jax.ShapeDtypeStruct((TOP_K, n_tok), F32),
                   jax.ShapeDtypeStruct((TOP_K, n_tok), jnp.int32),
                   jax.ShapeDtypeStruct((n_exp, 1), F32)],
        compiler_params=_cparams(("arbitrary",)),
        name="router_topk_rank",
    )(h2, rwt, rb, tri)


def _expert_kernel(te_ref, nu_ref, x_ref, wg_ref, wu_ref, wd_ref, y_ref):
    i = pl.program_id(0)

    @pl.when(i < nu_ref[0])
    def _():
        x = x_ref[...]
        hmid = jax.nn.silu(jnp.dot(x, wg_ref[0], preferred_element_type=F32)) * jnp.dot(x, wu_ref[0], preferred_element_type=F32)
        y_ref[...] = jnp.dot(hmid.astype(BF16), wd_ref[0], preferred_element_type=F32).astype(BF16)

    @pl.when(i >= nu_ref[0])
    def _():
        y_ref[...] = jnp.zeros_like(y_ref)


def _experts(tile_exp, n_used, xs, wg, wu, wd):
    n_rows, dm = xs.shape
    de = wg.shape[2]
    grid_spec = pltpu.PrefetchScalarGridSpec(
        num_scalar_prefetch=2,
        grid=(n_rows // EXPERT_TILE,),
        in_specs=[pl.BlockSpec((EXPERT_TILE, dm), lambda i, te, nu: (i, 0)),
                  pl.BlockSpec((1, dm, de), lambda i, te, nu: (te[i], 0, 0)),
                  pl.BlockSpec((1, dm, de), lambda i, te, nu: (te[i], 0, 0)),
                  pl.BlockSpec((1, de, dm), lambda i, te, nu: (te[i], 0, 0))],
        out_specs=pl.BlockSpec((EXPERT_TILE, dm), lambda i, te, nu: (i, 0)),
    )
    return pl.pallas_call(
        _expert_kernel,
        grid_spec=grid_spec,
        out_shape=jax.ShapeDtypeStruct((n_rows, dm), BF16),
        compiler_params=_cparams(("arbitrary",)),
        name="expert_mlp",
    )(tile_exp, n_used, xs, wg, wu, wd)


def _final_kernel(base_ref, r_ref, g2_ref, lg_ref, lb_ref, o_ref):
    o_ref[0] = _norm_rows(base_ref[0] + g2_ref[0] * r_ref[0]) * lg_ref[...] + lb_ref[...]


def _final(base, routed, g2, ln_g, ln_b):
    bsz, seq, dm = base.shape
    tm = min(ROW_TILE, seq)
    tile = pl.BlockSpec((1, tm, dm), lambda b, i: (b, i, 0))
    return pl.pallas_call(
        _final_kernel,
        grid=(bsz, seq // tm),
        in_specs=[tile, tile, pl.BlockSpec((1, 1, dm), lambda b, i: (b, 0, 0)),
                  pl.BlockSpec((1, dm), lambda b, i: (0, 0)), pl.BlockSpec((1, dm), lambda b, i: (0, 0))],
        out_specs=tile,
        out_shape=jax.ShapeDtypeStruct((bsz, seq, dm), F32),
        compiler_params=_cparams(("arbitrary", "arbitrary")),
        name="final_layernorm",
    )(base, routed, g2, ln_g, ln_b)


def _rel_bucket(dist):
    dist = jnp.maximum(dist, 0)
    exact = REL_BUCKETS // 2
    log_ratio = jnp.log(jnp.maximum(dist, 1).astype(F32) / exact) / math.log(REL_MAX_DIST / exact)
    large = jnp.minimum(exact + (log_ratio * (REL_BUCKETS - exact)).astype(jnp.int32), REL_BUCKETS - 1)
    return jnp.where(dist < exact, dist, large)


def _bias_tiles(rel_bias):
    n_d = 4 * SEL_LEN
    vec = rel_bias[_rel_bucket(jnp.arange(n_d))].T
    far = rel_bias[REL_BUCKETS - 1]
    vec = (vec - far[:, None]).reshape(N_KV, GQA, n_d)
    tok = np.arange(Q_BLOCK)[None, :]
    key = np.arange(SEL_LEN)[:, None]

    def toeplitz(d):
        vals = vec[:, :, np.clip(d, 0, n_d - 1)]
        vals = jnp.where(jnp.asarray(d >= 0), vals, NEG)
        return jnp.transpose(vals, (0, 2, 1, 3)).reshape(N_KV, d.shape[0], ROWS)

    near = [toeplitz(delta + tok - key) for delta in (0, SEL_LEN, 2 * SEL_LEN)]
    zero = jnp.zeros((N_KV, SEL_LEN, ROWS), F32)
    edge = np.where(tok < key, 0.0, NEG).astype(np.float32)
    edge = jnp.broadcast_to(jnp.asarray(np.tile(edge, (1, GQA)))[None], (N_KV, SEL_LEN, ROWS))
    nb = jnp.stack(near + [zero, jnp.full_like(zero, NEG), edge], axis=1)
    w = np.arange(BAND_ROWS)[:, None]
    bands = []
    for ph in (0, 1):
        d = tok - CMP_STRIDE * (w - PAD_CMP - (Q_BLOCK // CMP_STRIDE) * ph) - (CMP_LEN - 1)
        vals = vec[:, :, np.clip(d, 0, n_d - 1)]
        vals = jnp.where(jnp.asarray(d >= 0), vals, 0.0)
        bands.append(jnp.transpose(vals, (0, 2, 1, 3)).reshape(N_KV, BAND_ROWS, ROWS))
    cb = jnp.stack(bands, axis=1)
    return nb.astype(F32), cb.astype(F32)


def _padded_cmp_rows(seq):
    return -(-(seq // CMP_STRIDE + 2 * PAD_CMP) // LANES) * LANES


def _overlap_t(seq):
    n_cmp = (seq - CMP_LEN) // CMP_STRIDE + 1
    n_blk = seq // SEL_LEN
    ncp = _padded_cmp_rows(seq)
    c_start = np.arange(n_cmp) * CMP_STRIDE
    c_end = c_start + CMP_LEN - 1
    blk = np.arange(n_blk)
    ov = ((c_start[:, None] < (blk[None, :] + 1) * SEL_LEN) & (c_end[:, None] >= blk[None, :] * SEL_LEN))
    out = np.zeros((n_blk, ncp), np.float32)
    out[:, PAD_CMP:PAD_CMP + n_cmp] = ov.T
    return jnp.asarray(out, BF16)


def _s5_params(lam_re, lam_im, log_step, b_re, b_im, c_re, c_im):
    lr, li = lam_re.astype(F32), lam_im.astype(F32)
    dt = jnp.exp(log_step.astype(F32))[:, None]
    mag = jnp.exp(lr * dt)
    ar, ai = mag * jnp.cos(li * dt), mag * jnp.sin(li * dt)
    den = lr * lr + li * li
    kr = ((ar - 1.0) * lr + ai * li) / den
    ki = (ai * lr - (ar - 1.0) * li) / den
    br, bi = b_re.astype(F32), b_im.astype(F32)
    bbr = kr[..., None] * br - ki[..., None] * bi
    bbi = kr[..., None] * bi + ki[..., None] * br
    n_g = lr.shape[0]
    eye = jnp.eye(n_g, dtype=F32)

    def drive(bb):
        return jnp.einsum('gpc,gh->gchp', bb, eye).reshape(n_g * SSM_GROUP, n_g * SSM_STATE)

    def readout(c):
        return jnp.einsum('gcp,gh->gphc', c, eye).reshape(n_g * SSM_STATE, n_g * SSM_GROUP)

    wb = jnp.concatenate([drive(bbr), drive(bbi)], axis=1).astype(BF16)
    wc = jnp.concatenate([readout(c_re.astype(F32)), -readout(c_im.astype(F32))], axis=0).astype(BF16)
    a = jnp.concatenate([ar.reshape(1, -1), ai.reshape(1, -1)], axis=1)
    return wb, jnp.broadcast_to(a, (SUBLANES, a.shape[1])), wc


def _layer(x, mod, w_in, lam_re, lam_im, log_step, b_re, b_im, c_re, c_im, d_skip, glu_w, cmp_pos, cmp_w1,
           cmp_w2, rel_bias, w_out, ln1_g, ln1_b, router_w, router_bias, e_gate, e_up, e_down, sg, su, sd,
           ln2_g, ln2_b, alpha):
    bsz, seq, dm = x.shape
    n_tok = bsz * seq
    sh1, sc1, g1, sh2, sc2, g2 = [m[:, None, :] for m in jnp.split(mod, 6, axis=-1)]
    ssm_w = dm // 2
    attn_w = N_HEADS * HEAD_DIM
    kv_w = N_KV * HEAD_DIM
    n_gate = 3 * N_HEADS
    offs = np.cumsum([0, ssm_w, attn_w] + [kv_w] * 6 + [n_gate, 2 * dm])

    wu = w_in[:, offs[0]:offs[1]].astype(BF16)
    wq = w_in[:, offs[1]:offs[2]] * (HEAD_DIM ** -0.5)
    wa = jnp.concatenate([wq, w_in[:, offs[2]:offs[8]]], axis=1).astype(BF16)
    wg = jnp.pad(w_in[:, offs[8]:offs[9]], ((0, 0), (0, LANES - n_gate))).astype(BF16)
    wm = w_in[:, offs[9]:offs[10]].astype(BF16)
    u2d, act, gates, mg = _input_projection(x, sh1, sc1, wu, wa, wg, wm)

    nq = seq // Q_BLOCK
    n_blk = seq // SEL_LEN
    nch = seq // CMP_STRIDE

    def piece(i):
        return act[:, :, attn_w + i * kv_w: attn_w + (i + 1) * kv_w]

    def compress(raw, pos, w1, w2):
        xc = raw.reshape(bsz, nch, CMP_STRIDE, N_KV, HEAD_DIM).transpose(0, 3, 1, 2, 4).reshape(bsz, N_KV, nch, CMP_STRIDE * HEAD_DIM)
        half = CMP_STRIDE * HEAD_DIM
        w1cat = jnp.concatenate([w1[:half], w1[half:]], axis=1).astype(BF16)
        posb = jnp.dot(pos.reshape(1, -1), w1, precision=lax.Precision.HIGHEST)
        return _compress(xc, w1cat, posb, w2.astype(BF16))

    kc = compress(piece(0), cmp_pos[0], cmp_w1[0], cmp_w2[0])
    vc = compress(piece(1), cmp_pos[1], cmp_w1[1], cmp_w2[1])
    pad = ((0, 0), (0, 0), (PAD_CMP, _padded_cmp_rows(seq) - nch - PAD_CMP), (0, 0))
    kcp = jnp.pad(kc, pad).astype(BF16)
    vcpt = jnp.swapaxes(jnp.pad(vc, pad), 2, 3).astype(BF16)

    def heads(a):
        return a.reshape(bsz, seq, N_KV, HEAD_DIM).transpose(0, 2, 1, 3)

    nst = nq // NSA_TILES
    q = act[:, :, :attn_w].reshape(bsz, nst, NSA_TILES, Q_BLOCK, N_KV, GQA, HEAD_DIM)
    qt = q.transpose(0, 4, 1, 6, 2, 5, 3).reshape(bsz, N_KV, nst, HEAD_DIM, COLS)
    onehot = (np.arange(seq)[:, None] // SEL_LEN == np.arange(n_blk)[None, :]).astype(np.float32) * -MASK_BIG
    kaug = jnp.concatenate([heads(piece(2)), jnp.broadcast_to(jnp.asarray(onehot, BF16), (bsz, N_KV, seq, n_blk))], axis=3)
    vst = jnp.swapaxes(heads(piece(3)), 2, 3)
    kw = heads(piece(4))
    vwt = jnp.swapaxes(heads(piece(5)), 2, 3)
    gt = gates[:, :, :n_gate].reshape(bsz, nst, NSA_TILES, Q_BLOCK, N_KV, GQA, 3)
    gt = gt.transpose(0, 4, 1, 6, 2, 5, 3).reshape(bsz, N_KV, nst, 3, COLS)
    gt = jnp.pad(gt, ((0, 0), (0, 0), (0, 0), (0, SUBLANES - 3), (0, 0)))
    nb, cb = _bias_tiles(rel_bias)
    ot = _nsa(qt, kaug, vst, kw, vwt, kcp, vcpt, _overlap_t(seq), nb, cb, gt)
    ot = ot.reshape(bsz, N_KV, nst, HEAD_DIM, NSA_TILES, GQA, Q_BLOCK)
    y_nsa = ot.transpose(0, 2, 4, 6, 1, 5, 3).reshape(bsz, seq, attn_w)

    wb, a, wc = _s5_params(lam_re, lam_im, log_step, b_re, b_im, c_re, c_im)
    z2 = _s5(u2d.reshape(seq * bsz, ssm_w), wb, a, wc, d_skip.reshape(1, ssm_w).astype(F32), bsz)
    z2d = z2.reshape(seq, bsz * ssm_w)

    base, h2 = _mix_out(z2d, y_nsa, mg, x, g1, sh2, sc2, g2, ln1_g.reshape(1, dm), ln1_b.reshape(1, dm),
                        glu_w.astype(BF16), w_out.astype(BF16), sg.astype(BF16), su.astype(BF16), sd.astype(BF16), alpha)
    h2 = h2.reshape(n_tok, dm)

    n_exp = router_w.shape[1]
    rt = min(ROUTER_TILE, n_tok)
    tri = jnp.asarray(np.triu(np.ones((rt, rt), np.float32), 1), BF16)
    top_e, top_w, rank, counts = _router(h2, router_w.T.astype(BF16), router_bias.reshape(n_exp, 1).astype(F32), tri)
    counts = counts[:, 0].astype(jnp.int32)
    padded = (counts + EXPERT_TILE - 1) // EXPERT_TILE * EXPERT_TILE
    pad_end = jnp.cumsum(padded)
    pad_start = pad_end - padded
    dest = pad_start[top_e] + rank
    n_rows = n_tok * TOP_K + n_exp * EXPERT_TILE
    n_tiles = n_rows // EXPERT_TILE
    tile_exp = jnp.minimum(jnp.searchsorted(pad_end, jnp.arange(n_tiles) * EXPERT_TILE, side='right'), n_exp - 1).astype(jnp.int32)
    n_used = (pad_end[-1] // EXPERT_TILE).astype(jnp.int32).reshape(1)
    tok = jnp.broadcast_to(jnp.arange(n_tok, dtype=jnp.int32)[None, :], dest.shape)
    row_tok = jnp.full((n_rows,), n_tok, jnp.int32).at[dest.reshape(-1)].set(tok.reshape(-1))
    h_pad = jnp.concatenate([h2, jnp.zeros((1, dm), h2.dtype)], axis=0)
    xs = h_pad[row_tok]
    ys = _experts(tile_exp, n_used, xs, e_gate.astype(BF16), e_up.astype(BF16), e_down.astype(BF16))
    routed = jnp.einsum('ktd,kt->td', ys[dest].astype(F32), top_w)
    return _final(base, routed.reshape(bsz, seq, dm), g2, ln2_g.reshape(1, dm), ln2_b.reshape(1, dm))


def kernel(x, c, ada_w, ada_b, w_in, ssm_lambda_re, ssm_lambda_im, ssm_log_step, ssm_b_re, ssm_b_im, ssm_c_re, ssm_c_im, ssm_d, ssm_glu_w, cmp_pos, cmp_w1, cmp_w2, rel_bias, w_out, ln1_g, ln1_b, router_w, router_bias, exp_w_gate, exp_w_up, exp_w_down, sh_w_gate, sh_w_up, sh_w_down, ln2_g, ln2_b):
    depth = ada_w.shape[0]
    alpha = (2 * depth) ** 0.25
    for l in range(depth):
        mod = _modulation(c, ada_w[l], ada_b[l])
        x = _layer(x, mod, w_in[l], ssm_lambda_re[l], ssm_lambda_im[l], ssm_log_step[l], ssm_b_re[l], ssm_b_im[l],
                   ssm_c_re[l], ssm_c_im[l], ssm_d[l], ssm_glu_w[l], cmp_pos[l], cmp_w1[l], cmp_w2[l], rel_bias,
                   w_out[l], ln1_g[l], ln1_b[l], router_w[l], router_bias[l], exp_w_gate[l], exp_w_up[l],
                   exp_w_down[l], sh_w_gate[l], sh_w_up[l], sh_w_down[l], ln2_g[l], ln2_b[l], alpha)
    return x
```

```python
import functools
import math

import numpy as np
import jax
import jax.numpy as jnp
from jax import lax
from jax.experimental import pallas as pl
from jax.experimental.pallas import tpu as pltpu

F32 = jnp.float32
BF16 = jnp.bfloat16

SSM_GROUP = 16
SSM_STATE = 64
N_HEADS = 16
HEAD_DIM = 64
N_KV = 4
GQA = N_HEADS // N_KV
CMP_LEN = 32
CMP_STRIDE = 16
SEL_LEN = 64
SEL_TOP = 16
WINDOW = 512
Q_BLOCK = 64
FORCE_BONUS = 1.0e4
REL_BUCKETS = 32
REL_MAX_DIST = 128
TOP_K = 8
ROUTED_SCALE = 2.5
LN_EPS = 1e-5

LANES = 128
SUBLANES = 8
VMEM_LIMIT_BYTES = 56 * 1024 * 1024
ROW_TILE = 512
ROUTER_TILE = 256
EXPERT_TILE = 256
MOVE_TILE = 256
SCAN_CHUNK = 128
SCAN_LANES = 512
MASK_BIG = 32768.0
NEG = -1.0e30
WIN_BLOCKS = WINDOW // SEL_LEN
ROWS = GQA * Q_BLOCK
NSA_TILES = 4
COLS = NSA_TILES * ROWS
CHUNK_BLOCKS = 8
CHUNK_KEYS = CHUNK_BLOCKS * SEL_LEN
PAD_CMP = 8
BAND_ROWS = 24


def _cparams(sem):
    return pltpu.CompilerParams(dimension_semantics=sem, vmem_limit_bytes=VMEM_LIMIT_BYTES)


def _pack_rows(x):
    n = x.shape[1] // 2
    xb = x.astype(jnp.bfloat16).astype(F32)
    lo = lax.shift_right_logical(lax.bitcast_convert_type(xb[:, :n], jnp.uint32), jnp.uint32(16))
    return lax.bitcast_convert_type(xb[:, n:], jnp.uint32) | lo


def _unpack_rows(w):
    lo = lax.bitcast_convert_type(lax.shift_left(w, jnp.uint32(16)), F32)
    hi = lax.bitcast_convert_type(w & jnp.uint32(0xFFFF0000), F32)
    return lo, hi


def _norm_rows(x):
    mu = jnp.mean(x, axis=-1, keepdims=True)
    xc = x - mu
    var = jnp.mean(xc * xc, axis=-1, keepdims=True)
    return xc * lax.rsqrt(var + LN_EPS)


def _mod_kernel(c_ref, w_ref, b_ref, o_ref):
    cond = jax.nn.silu(c_ref[...])
    o_ref[...] = jnp.dot(cond.astype(BF16), w_ref[...].astype(BF16), preferred_element_type=F32) + b_ref[...]


def _modulation(c, ada_w, ada_b):
    bsz, dm = c.shape
    n = ada_w.shape[1]
    tn = dm
    return pl.pallas_call(
        _mod_kernel,
        grid=(n // tn,),
        in_specs=[pl.BlockSpec((bsz, dm), lambda j: (0, 0)),
                  pl.BlockSpec((dm, tn), lambda j: (0, j)),
                  pl.BlockSpec((1, tn), lambda j: (0, j))],
        out_specs=pl.BlockSpec((bsz, tn), lambda j: (0, j)),
        out_shape=jax.ShapeDtypeStruct((bsz, n), F32),
        compiler_params=_cparams(("arbitrary",)),
        name="ada_modulation",
    )(c, ada_w, ada_b.reshape(1, n))


def _inproj_kernel(x_ref, sh_ref, sc_ref, wu_ref, wa_ref, wg_ref, wm_ref, u_ref, a_ref, g_ref, m_ref):
    h = _norm_rows(x_ref[0]) * (1.0 + sc_ref[0]) + sh_ref[0]
    hb = h.astype(BF16)
    u_ref[...] = jnp.dot(hb, wu_ref[...], preferred_element_type=F32)
    a_ref[0] = jnp.dot(hb, wa_ref[...], preferred_element_type=F32).astype(BF16)
    g_ref[0] = jax.nn.sigmoid(jnp.dot(hb, wg_ref[...], preferred_element_type=F32))
    m_ref[0] = jax.nn.sigmoid(jnp.dot(hb, wm_ref[...], preferred_element_type=F32)).astype(BF16)


def _input_projection(x, sh1, sc1, wu, wa, wg, wm):
    bsz, seq, dm = x.shape
    tm = min(ROW_TILE, seq)
    nu, na, ng, nm = wu.shape[1], wa.shape[1], wg.shape[1], wm.shape[1]
    full = lambda n: pl.BlockSpec((dm, n), lambda b, i: (0, 0))
    vec = pl.BlockSpec((1, 1, dm), lambda b, i: (b, 0, 0))
    return pl.pallas_call(
        _inproj_kernel,
        grid=(bsz, seq // tm),
        in_specs=[pl.BlockSpec((1, tm, dm), lambda b, i: (b, i, 0)), vec, vec,
                  full(nu), full(na), full(ng), full(nm)],
        out_specs=[pl.BlockSpec((tm, nu), lambda b, i: (i, b)),
                   pl.BlockSpec((1, tm, na), lambda b, i: (b, i, 0)),
                   pl.BlockSpec((1, tm, ng), lambda b, i: (b, i, 0)),
                   pl.BlockSpec((1, tm, nm), lambda b, i: (b, i, 0))],
        out_shape=[jax.ShapeDtypeStruct((seq, bsz * nu), F32),
                   jax.ShapeDtypeStruct((bsz, seq, na), BF16),
                   jax.ShapeDtypeStruct((bsz, seq, ng), F32),
                   jax.ShapeDtypeStruct((bsz, seq, nm), BF16)],
        compiler_params=_cparams(("arbitrary", "arbitrary")),
        name="adaln_input_projection",
    )(x, sh1, sc1, wu, wa, wg, wm)


def _compress_kernel(x_ref, w1_ref, pb_ref, w2_ref, o_ref, *, n_cmp):
    hid = w2_ref.shape[0]
    p = jnp.dot(x_ref[0, 0], w1_ref[...], preferred_element_type=F32)
    nrow = p.shape[0]
    nxt = pltpu.roll(p[:, hid:], nrow - 1, 0)
    hidv = jax.nn.gelu(p[:, :hid] + nxt + pb_ref[...])
    out = jnp.dot(hidv.astype(BF16), w2_ref[...], preferred_element_type=F32)
    rows = lax.broadcasted_iota(jnp.int32, out.shape, 0)
    o_ref[0, 0] = jnp.where(rows < n_cmp, out, 0.0)


def _compress(xc, w1cat, posb, w2):
    bsz, nkv, nch, kdim = xc.shape
    hid2 = w1cat.shape[1]
    return pl.pallas_call(
        functools.partial(_compress_kernel, n_cmp=nch - 1),
        grid=(bsz, nkv),
        in_specs=[pl.BlockSpec((1, 1, nch, kdim), lambda b, g: (b, g, 0, 0)),
                  pl.BlockSpec((kdim, hid2), lambda b, g: (0, 0)),
                  pl.BlockSpec((1, hid2 // 2), lambda b, g: (0, 0)),
                  pl.BlockSpec((hid2 // 2, HEAD_DIM), lambda b, g: (0, 0))],
        out_specs=pl.BlockSpec((1, 1, nch, HEAD_DIM), lambda b, g: (b, g, 0, 0)),
        out_shape=jax.ShapeDtypeStruct((bsz, nkv, nch, HEAD_DIM), F32),
        compiler_params=_cparams(("arbitrary", "arbitrary")),
        name="kv_compress",
    )(xc, w1cat, posb, w2)


def _nsa_kernel(qt_ref, kaug_ref, vst_ref, kw_ref, vwt_ref, kcp_ref, vcpt_ref, ovt_ref, nb_ref, cb_ref,
                g_ref, o_ref, s_ref, c_ref, *, n_blk, n_cmp, n_sel):
    q0 = pl.program_id(2) * NSA_TILES
    qt = qt_ref[0, 0, 0]
    col_tile = lambda n: slice(n * ROWS, (n + 1) * ROWS)

    s_ref[...] = jnp.dot(kcp_ref[0, 0], qt, preferred_element_type=F32)
    for n in range(NSA_TILES):
        band = pl.multiple_of(SUBLANES * ((q0 + n) // 2), SUBLANES)
        s_ref[pl.ds(band, BAND_ROWS), col_tile(n)] += cb_ref[0, n % 2]
    s = s_ref[...]
    rho = lax.broadcasted_iota(jnp.int32, s.shape, 0)
    col = lax.broadcasted_iota(jnp.int32, s.shape, 1)
    tok = col & (Q_BLOCK - 1)
    qblk = q0 + jnp.right_shift(col, ROWS.bit_length() - 1)
    first_visible = CMP_STRIDE * (rho - PAD_CMP) + (CMP_LEN - 1) - Q_BLOCK * qblk
    vis = (rho >= PAD_CMP) & (rho < PAD_CMP + n_cmp) & (first_visible <= tok)
    s = jnp.where(vis, s, -jnp.inf)
    mx = jnp.max(s, axis=0, keepdims=True)
    mx = jnp.where(mx == -jnp.inf, 0.0, mx)
    e = jnp.exp(s - mx)
    p_cmp = (e / jnp.maximum(jnp.sum(e, axis=0, keepdims=True), 1e-30)).astype(BF16)
    o_cmp = jnp.dot(vcpt_ref[0, 0], p_cmp, preferred_element_type=F32)

    imp4 = jnp.dot(ovt_ref[...], p_cmp, preferred_element_type=F32)
    parts = []
    for n in range(NSA_TILES):
        two = imp4[:, n * ROWS:n * ROWS + LANES] + imp4[:, n * ROWS + LANES:(n + 1) * ROWS]
        parts.append(two + pltpu.roll(two, Q_BLOCK, 1))
    imp = jnp.concatenate(parts, axis=1)
    blk = lax.broadcasted_iota(jnp.int32, imp.shape, 0)
    cur = q0 + jnp.right_shift(lax.broadcasted_iota(jnp.int32, imp.shape, 1), LANES.bit_length() - 1)
    blkf = blk.astype(F32)
    forced = (blk == 0) | (blk == cur) | (blk == cur - 1)
    score = jnp.where(blk <= cur, imp + jnp.where(forced, FORCE_BONUS, 0.0), -jnp.inf)
    notsel = jnp.ones(imp.shape, F32)
    for _ in range(n_sel):
        best = jnp.max(score, axis=0, keepdims=True)
        first = jnp.min(jnp.where(score == best, blkf, float(n_blk)), axis=0, keepdims=True)
        pick = blkf == first
        notsel = jnp.where(pick, 0.0, notsel)
        score = jnp.where(pick, -jnp.inf, score)
    notsel = notsel.astype(BF16)
    halves = [notsel[:, n * LANES:(n + 1) * LANES] for n in range(NSA_TILES)]
    qaug = jnp.concatenate([qt, jnp.concatenate([h for h in halves for _ in (0, 1)], axis=1)], axis=0)

    def update(carry, sc, vt):
        m, l, acc = carry
        m_new = jnp.maximum(m, jnp.max(sc, axis=0, keepdims=True))
        alpha = jnp.exp(m - m_new)
        p = jnp.exp(sc - m_new)
        l = alpha * l + jnp.sum(p, axis=0, keepdims=True)
        return m_new, l, alpha * acc + jnp.dot(vt, p.astype(BF16), preferred_element_type=F32)

    def far_chunk(c, carry):
        keys = pl.ds(pl.multiple_of(c * CHUNK_KEYS, CHUNK_KEYS), CHUNK_KEYS)
        sc = jnp.dot(kaug_ref[0, 0, keys, :], qaug, preferred_element_type=F32)
        return update(carry, sc, vst_ref[0, 0, :, keys])

    def near_chunk(k_ref, vt_ref, qmat, blk0, tile_of, carry):
        keys = pl.ds(pl.multiple_of(blk0 * SEL_LEN, NSA_TILES * SEL_LEN), CHUNK_KEYS)
        c_ref[...] = jnp.dot(k_ref[0, 0, keys, :], qmat, preferred_element_type=F32)
        for o in range(CHUNK_BLOCKS):
            for n in range(NSA_TILES):
                c_ref[o * SEL_LEN:(o + 1) * SEL_LEN, col_tile(n)] += nb_ref[0, tile_of(blk0 + o, q0 + n)]
        return update(carry, c_ref[...], vt_ref[0, 0, :, keys])

    def sel_tile(j, qi):
        d = qi - j
        return jnp.where(d < 0, 4, jnp.minimum(d, 3))

    def win_tile(j, qi):
        d = qi - j
        return jnp.where((d < 0) | (d > WIN_BLOCKS), 4, jnp.where(d == WIN_BLOCKS, 5, jnp.minimum(d, 3)))

    init = (jnp.full((1, COLS), NEG, F32), jnp.zeros((1, COLS), F32), jnp.zeros((HEAD_DIM, COLS), F32))
    n_far = jnp.maximum(q0 - 2, 0) // CHUNK_BLOCKS
    carry = lax.fori_loop(0, n_far, far_chunk, init)
    _, l, acc = lax.fori_loop(
        n_far, (q0 + NSA_TILES - 1) // CHUNK_BLOCKS + 1,
        lambda c, cr: near_chunk(kaug_ref, vst_ref, qaug, c * CHUNK_BLOCKS, sel_tile, cr), carry)
    o_slc = acc / l
    win0 = jnp.maximum(q0 - (WIN_BLOCKS + NSA_TILES), 0)
    _, l, acc = lax.fori_loop(
        0, 2, lambda c, cr: near_chunk(kw_ref, vwt_ref, qt, win0 + c * CHUNK_BLOCKS, win_tile, cr), init)
    o_win = acc / l
    g = g_ref[0, 0, 0]
    o_ref[0, 0, 0] = (g[0:1] * o_cmp + g[1:2] * o_slc + g[2:3] * o_win).astype(BF16)


def _nsa(qt, kaug, vst, kw, vwt, kcp, vcpt, ovt, nb, cb, gt):
    bsz, nkv, nsteps, dh, cols = qt.shape
    seq = kw.shape[2]
    n_blk = seq // SEL_LEN
    ncp = kcp.shape[2]
    n_cmp = (seq - CMP_LEN) // CMP_STRIDE + 1
    assert n_blk % CHUNK_BLOCKS == 0 and n_blk >= 2 * CHUNK_BLOCKS and WIN_BLOCKS + 2 * NSA_TILES == 2 * CHUNK_BLOCKS
    per_bg = lambda shape: pl.BlockSpec((1, 1) + shape, lambda b, g, i: (b, g, 0, 0))
    per_step = lambda shape: pl.BlockSpec((1, 1, 1) + shape, lambda b, g, i: (b, g, i, 0, 0))
    kern = functools.partial(_nsa_kernel, n_blk=n_blk, n_cmp=n_cmp, n_sel=min(SEL_TOP, n_blk))
    return pl.pallas_call(
        kern,
        grid=(bsz, nkv, nsteps),
        in_specs=[per_step((dh, cols)),
                  per_bg((seq, kaug.shape[3])), per_bg((dh, seq)), per_bg((seq, dh)), per_bg((dh, seq)),
                  per_bg((ncp, dh)), per_bg((dh, ncp)),
                  pl.BlockSpec(ovt.shape, lambda b, g, i: (0, 0)),
                  pl.BlockSpec((1,) + nb.shape[1:], lambda b, g, i: (g, 0, 0, 0)),
                  pl.BlockSpec((1,) + cb.shape[1:], lambda b, g, i: (g, 0, 0, 0)),
                  per_step((SUBLANES, cols))],
        out_specs=per_step((dh, cols)),
        out_shape=jax.ShapeDtypeStruct((bsz, nkv, nsteps, dh, cols), BF16),
        scratch_shapes=[pltpu.VMEM((ncp, cols), F32), pltpu.VMEM((CHUNK_KEYS, cols), F32)],
        compiler_params=_cparams(("arbitrary", "arbitrary", "arbitrary")),
        name="nsa_attention",
    )(qt, kaug, vst, kw, vwt, kcp, vcpt, ovt, nb, cb, gt)


def _s5_kernel(u_ref, wb_ref, a_ref, wc_ref, d_ref, z_ref, xs_ref, st_ref, *, bsz, n_state):
    @pl.when(pl.program_id(0) == 0)
    def _():
        st_ref[...] = jnp.zeros_like(st_ref)

    u = u_ref[...]
    xs_ref[...] = jnp.dot(u.astype(BF16), wb_ref[...], preferred_element_type=F32)
    steps = u.shape[0] // bsz
    for c0 in range(0, n_state, SCAN_LANES):
        re = pl.ds(c0, SCAN_LANES)
        im = pl.ds(n_state + c0, SCAN_LANES)
        ar = jnp.broadcast_to(a_ref[0:1, re], (bsz, SCAN_LANES))
        ai = jnp.broadcast_to(a_ref[0:1, im], (bsz, SCAN_LANES))

        def step(t, carry):
            xr, xi = carry
            rows = pl.ds(pl.multiple_of(t * bsz, bsz), bsz)
            nr = ar * xr - ai * xi + xs_ref[rows, re]
            ni = ar * xi + ai * xr + xs_ref[rows, im]
            xs_ref[rows, re] = nr
            xs_ref[rows, im] = ni
            return nr, ni

        xr, xi = lax.fori_loop(0, steps, step, (st_ref[:, re], st_ref[:, im]), unroll=8)
        st_ref[:, re] = xr
        st_ref[:, im] = xi
    y = jnp.dot(xs_ref[...].astype(BF16), wc_ref[...], preferred_element_type=F32) + d_ref[...] * u
    z_ref[...] = jax.nn.gelu(y).astype(BF16)


def _s5(u2, wb, a, wc, dsk, bsz):
    rows, width = u2.shape
    seq = rows // bsz
    chunk = min(SCAN_CHUNK, seq)
    n_state2 = wb.shape[1]
    return pl.pallas_call(
        functools.partial(_s5_kernel, bsz=bsz, n_state=n_state2 // 2),
        grid=(seq // chunk,),
        in_specs=[pl.BlockSpec((chunk * bsz, width), lambda i: (i, 0)),
                  pl.BlockSpec(wb.shape, lambda i: (0, 0)),
                  pl.BlockSpec(a.shape, lambda i: (0, 0)),
                  pl.BlockSpec(wc.shape, lambda i: (0, 0)),
                  pl.BlockSpec(dsk.shape, lambda i: (0, 0))],
        out_specs=pl.BlockSpec((chunk * bsz, width), lambda i: (i, 0)),
        out_shape=jax.ShapeDtypeStruct((rows, width), BF16),
        scratch_shapes=[pltpu.VMEM((chunk * bsz, n_state2), F32), pltpu.VMEM((bsz, n_state2), F32)],
        compiler_params=_cparams(("arbitrary",)),
        name="s5_scan",
    )(u2, wb, a, wc, dsk)


def _mixout_kernel(z_ref, yn_ref, mg_ref, x_ref, g1_ref, sh2_ref, sc2_ref, g2_ref, lg_ref, lb_ref,
                   glu_ref, wo_ref, sg_ref, su_ref, sd_ref, base_ref, h_ref, *, alpha):
    dm = x_ref.shape[2]
    glu = jnp.dot(z_ref[...], glu_ref[...], preferred_element_type=F32)
    y_ssm = glu[:, :dm] * jax.nn.sigmoid(glu[:, dm:])
    mg = mg_ref[0].astype(F32)
    merged = mg[:, :dm] * y_ssm + mg[:, dm:] * yn_ref[0].astype(F32)
    y = jnp.dot(merged.astype(BF16), wo_ref[...], preferred_element_type=F32)
    x1 = _norm_rows(alpha * x_ref[0] + g1_ref[0] * y) * lg_ref[...] + lb_ref[...]
    hf = _norm_rows(x1) * (1.0 + sc2_ref[0]) + sh2_ref[0]
    h_ref[0] = _pack_rows(hf)
    h = hf.astype(BF16)
    hs =jax.nn.silu(jnp.dot(h, sg_ref[...], preferred_element_type=F32)) * jnp.dot(h, su_ref[...], preferred_element_type=F32)
    shared = jnp.dot(hs.astype(BF16), sd_ref[...], preferred_element_type=F32)
    base_ref[0] = alpha * x1 + g2_ref[0] * shared


def _mix_out(z2d, y_nsa, mg, x, g1, sh2, sc2, g2, ln_g, ln_b, glu_w, w_out, sg, su, sd, alpha):
    bsz, seq, dm = x.shape
    tm = min(ROW_TILE, seq)
    width = z2d.shape[1] // bsz
    vec = pl.BlockSpec((1, 1, dm), lambda b, i: (b, 0, 0))
    row = pl.BlockSpec((1, dm), lambda b, i: (0, 0))
    full = lambda w: pl.BlockSpec(w.shape, lambda b, i: (0, 0))
    tile = lambda n: pl.BlockSpec((1, tm, n), lambda b, i: (b, i, 0))
    return pl.pallas_call(
        functools.partial(_mixout_kernel, alpha=alpha),
        grid=(bsz, seq // tm),
        in_specs=[pl.BlockSpec((tm, width), lambda b, i: (i, b)), tile(dm), tile(2 * dm), tile(dm),
                  vec, vec, vec, vec, row, row, full(glu_w), full(w_out), full(sg), full(su), full(sd)],
        out_specs=[tile(dm), tile(dm // 2)],
        out_shape=[jax.ShapeDtypeStruct((bsz, seq, dm), F32), jax.ShapeDtypeStruct((bsz, seq, dm // 2), jnp.uint32)],
        compiler_params=_cparams(("arbitrary", "arbitrary")),
        name="merge_outproj_ln_shared",
    )(z2d, y_nsa, mg, x, g1, sh2, sc2, g2, ln_g, ln_b, glu_w, w_out, sg, su, sd)


def _router_kernel(h_ref, rwt_ref, rb_ref, tri_ref, e_ref, w_ref, p_ref, cnt_ref):
    @pl.when(pl.program_id(0) == 0)
    def _():
        cnt_ref[...] = jnp.zeros_like(cnt_ref)

    h = jnp.concatenate(_unpack_rows(h_ref[...]), axis=1).astype(BF16)
    logits = lax.dot_general(rwt_ref[...], h, (((1,), (1,)), ((), ())), preferred_element_type=F32)
    scores = jax.nn.sigmoid(logits)
    cur = scores + rb_ref[...]
    n_exp = scores.shape[0]
    eid = lax.broadcasted_iota(jnp.int32, scores.shape, 0).astype(F32)
    chosen = jnp.zeros(scores.shape, F32)
    ids, vals = [], []
    for _ in range(TOP_K):
        best = jnp.max(cur, axis=0, keepdims=True)
        first = jnp.min(jnp.where(cur == best, eid, float(n_exp)), axis=0, keepdims=True)
        pick = eid == first
        ids.append(first)
        vals.append(jnp.sum(jnp.where(pick, scores, 0.0), axis=0, keepdims=True))
        chosen = jnp.where(pick, 1.0, chosen)
        cur = jnp.where(pick, -jnp.inf, cur)
    top_s = jnp.concatenate(vals, axis=0)
    w_ref[...] = top_s / jnp.sum(top_s, axis=0, keepdims=True) * ROUTED_SCALE
    top_e = jnp.concatenate(ids, axis=0)
    e_ref[...] = top_e.astype(jnp.int32)
    before = jnp.dot(chosen.astype(BF16), tri_ref[...], preferred_element_type=F32) + cnt_ref[...]
    ranks = [jnp.sum(jnp.where(eid == ids[k], before, 0.0), axis=0, keepdims=True) for k in range(TOP_K)]
    p_ref[...] = jnp.concatenate(ranks, axis=0).astype(jnp.int32)
    cnt_ref[...] += jnp.sum(chosen, axis=1, keepdims=True)


def _router(h2, rwt, rb, tri):
    n_tok, words = h2.shape
    n_exp, dm = rwt.shape
    tm = tri.shape[0]
    kt = pl.BlockSpec((TOP_K, tm), lambda i: (0, i))
    return pl.pallas_call(
        _router_kernel,
        grid=(n_tok // tm,),
        in_specs=[pl.BlockSpec((tm, words), lambda i: (i, 0)),
                  pl.BlockSpec((n_exp, dm), lambda i: (0, 0)),
                  pl.BlockSpec((n_exp, 1), lambda i: (0, 0)),
                  pl.BlockSpec((tm, tm), lambda i: (0, 0))],
        out_specs=[kt, kt, kt, pl.BlockSpec((n_exp, 1), lambda i: (0, 0))],
        out_shape=[jax.ShapeDtypeStruct((TOP_K, n_tok), jnp.int32),
                   jax.ShapeDtypeStruct((TOP_K, n_tok), F32),
                   jax.ShapeDtypeStruct((TOP_K, n_tok), jnp.int32),
                   jax.ShapeDtypeStruct((n_exp, 1), F32)],
        compiler_params=_cparams(("arbitrary",)),
        name="router_topk_rank",
    )(h2, rwt, rb, tri)


def _row_copy(src_ref, src_row, dst_ref, dst_row, sem):
    return pltpu.make_async_copy(src_ref.at[pl.ds(src_row, 1), :], dst_ref.at[pl.ds(dst_row, 1), :], sem)


def _dispatch_kernel(dest_ref, h_ref, init_ref, xs_ref, sem):
    del init_ref
    tm = h_ref.shape[0]

    def issue(t, carry):
        for k in range(TOP_K):
            _row_copy(h_ref, t, xs_ref, dest_ref[0, k, t], sem).start()
        return carry

    lax.fori_loop(0, tm, issue, 0)
    pltpu.make_async_copy(xs_ref.at[pl.ds(0, TOP_K * tm), :], xs_ref.at[pl.ds(0, TOP_K * tm), :], sem).wait()


def _dispatch(dest3, h2, n_rows):
    n_tok, words = h2.shape
    tm = dest3.shape[2]
    return pl.pallas_call(
        _dispatch_kernel,
        grid=(n_tok // tm,),
        in_specs=[pl.BlockSpec((1, TOP_K, tm), lambda i: (i, 0, 0), memory_space=pltpu.SMEM),
                  pl.BlockSpec((tm, words), lambda i: (i, 0)),
                  pl.BlockSpec(memory_space=pl.ANY)],
        out_specs=pl.BlockSpec(memory_space=pl.ANY),
        out_shape=jax.ShapeDtypeStruct((n_rows, words), jnp.uint32),
        scratch_shapes=[pltpu.SemaphoreType.DMA(())],
        input_output_aliases={2: 0},
        compiler_params=_cparams(("arbitrary",)),
        name="moe_dispatch",
    )(dest3, h2, jnp.zeros((n_rows, words), jnp.uint32))


def _expert_kernel(te_ref, nu_ref, x_ref, wg_ref, wu_ref, wd_ref, y_ref):
    i = pl.program_id(0)

    @pl.when(i < nu_ref[0])
    def _():
        lo, hi = _unpack_rows(x_ref[...])
        lo, hi = lo.astype(BF16), hi.astype(BF16)
        half = lo.shape[1]

        def proj(w_ref):
            return (jnp.dot(lo, w_ref[0, :half, :], preferred_element_type=F32)
                    + jnp.dot(hi, w_ref[0, half:, :], preferred_element_type=F32))

        hmid = jax.nn.silu(proj(wg_ref)) * proj(wu_ref)
        y_ref[...] = _pack_rows(jnp.dot(hmid.astype(BF16), wd_ref[0], preferred_element_type=F32))

    @pl.when(i >= nu_ref[0])
    def _():
        y_ref[...] = jnp.zeros_like(y_ref)


def _experts(tile_exp, n_used, xs, wg, wu, wd):
    n_rows, words = xs.shape
    dm, de = wg.shape[1], wg.shape[2]
    grid_spec = pltpu.PrefetchScalarGridSpec(
        num_scalar_prefetch=2,
        grid=(n_rows // EXPERT_TILE,),
        in_specs=[pl.BlockSpec((EXPERT_TILE, words), lambda i, te, nu: (i, 0)),
                  pl.BlockSpec((1, dm, de), lambda i, te, nu: (te[i], 0, 0)),
                  pl.BlockSpec((1, dm, de), lambda i, te, nu: (te[i], 0, 0)),
                  pl.BlockSpec((1, de, dm), lambda i, te, nu: (te[i], 0, 0))],
        out_specs=pl.BlockSpec((EXPERT_TILE, words), lambda i, te, nu: (i, 0)),
    )
    return pl.pallas_call(
        _expert_kernel,
        grid_spec=grid_spec,
        out_shape=jax.ShapeDtypeStruct((n_rows, words), jnp.uint32),
        compiler_params=_cparams(("arbitrary",)),
        name="expert_mlp",
    )(tile_exp, n_used, xs, wg, wu, wd)


def _final_kernel(dest_ref, w_ref, base_ref, g2_ref, lg_ref, lb_ref, ys_ref, o_ref, buf_ref, sem):
    tm = base_ref.shape[1]

    def issue(t, carry):
        for k in range(TOP_K):
            _row_copy(ys_ref, dest_ref[0, k, t], buf_ref.at[k], t, sem).start()
        return carry

    lax.fori_loop(0, tm, issue, 0)
    pltpu.make_async_copy(buf_ref, buf_ref, sem).wait()
    w = w_ref[...]
    lo_sum, hi_sum = jnp.zeros((tm, buf_ref.shape[2]), F32), jnp.zeros((tm, buf_ref.shape[2]), F32)
    for k in range(TOP_K):
        lo, hi = _unpack_rows(buf_ref[k])
        lo_sum += w[:, k:k + 1] * lo
        hi_sum += w[:, k:k + 1] * hi
    routed = jnp.concatenate([lo_sum, hi_sum], axis=1)
    o_ref[0] = _norm_rows(base_ref[0] + g2_ref[0] * routed) * lg_ref[...] + lb_ref[...]


def _final(dest3, w_tok, base, g2, ln_g, ln_b, ys):
    bsz, seq, dm = base.shape
    tm = dest3.shape[2]
    nt = seq // tm
    tile = pl.BlockSpec((1, tm, dm), lambda b, i: (b, i, 0))
    return pl.pallas_call(
        _final_kernel,
        grid=(bsz, nt),
        in_specs=[pl.BlockSpec((1, TOP_K, tm), lambda b, i: (b * nt + i, 0, 0), memory_space=pltpu.SMEM),
                  pl.BlockSpec((tm, TOP_K), lambda b, i: (b * nt + i, 0)),
                  tile, pl.BlockSpec((1, 1, dm), lambda b, i: (b, 0, 0)),
                  pl.BlockSpec((1, dm), lambda b, i: (0, 0)), pl.BlockSpec((1, dm), lambda b, i: (0, 0)),
                  pl.BlockSpec(memory_space=pl.ANY)],
        out_specs=tile,
        out_shape=jax.ShapeDtypeStruct((bsz, seq, dm), F32),
        scratch_shapes=[pltpu.VMEM((TOP_K, tm, ys.shape[1]), jnp.uint32), pltpu.SemaphoreType.DMA(())],
        compiler_params=_cparams(("arbitrary", "arbitrary")),
        name="combine_final_layernorm",
    )(dest3, w_tok, base, g2, ln_g, ln_b, ys)


def _rel_bucket(dist):
    dist = jnp.maximum(dist, 0)
    exact = REL_BUCKETS // 2
    log_ratio = jnp.log(jnp.maximum(dist, 1).astype(F32) / exact) / math.log(REL_MAX_DIST / exact)
    large = jnp.minimum(exact + (log_ratio * (REL_BUCKETS - exact)).astype(jnp.int32), REL_BUCKETS - 1)
    return jnp.where(dist < exact, dist, large)


def _bias_tiles(rel_bias):
    n_d = 4 * SEL_LEN
    vec = rel_bias[_rel_bucket(jnp.arange(n_d))].T
    far = rel_bias[REL_BUCKETS - 1]
    vec = (vec - far[:, None]).reshape(N_KV, GQA, n_d)
    tok = np.arange(Q_BLOCK)[None, :]
    key = np.arange(SEL_LEN)[:, None]

    def toeplitz(d):
        vals = vec[:, :, np.clip(d, 0, n_d - 1)]
        vals = jnp.where(jnp.asarray(d >= 0), vals, NEG)
        return jnp.transpose(vals, (0, 2, 1, 3)).reshape(N_KV, d.shape[0], ROWS)

    near = [toeplitz(delta + tok - key) for delta in (0, SEL_LEN, 2 * SEL_LEN)]
    zero = jnp.zeros((N_KV, SEL_LEN, ROWS), F32)
    edge = np.where(tok < key, 0.0, NEG).astype(np.float32)
    edge = jnp.broadcast_to(jnp.asarray(np.tile(edge, (1, GQA)))[None], (N_KV, SEL_LEN, ROWS))
    nb = jnp.stack(near + [zero, jnp.full_like(zero, NEG), edge], axis=1)
    w = np.arange(BAND_ROWS)[:, None]
    bands = []
    for ph in (0, 1):
        d = tok - CMP_STRIDE * (w - PAD_CMP - (Q_BLOCK // CMP_STRIDE) * ph) - (CMP_LEN - 1)
        vals = vec[:, :, np.clip(d, 0, n_d - 1)]
        vals = jnp.where(jnp.asarray(d >= 0), vals, 0.0)
        bands.append(jnp.transpose(vals, (0, 2, 1, 3)).reshape(N_KV, BAND_ROWS, ROWS))
    cb = jnp.stack(bands, axis=1)
    return nb.astype(F32), cb.astype(F32)


def _padded_cmp_rows(seq):
    return -(-(seq // CMP_STRIDE + 2 * PAD_CMP) // LANES) * LANES


def _overlap_t(seq):
    n_cmp = (seq - CMP_LEN) // CMP_STRIDE + 1
    n_blk = seq // SEL_LEN
    ncp = _padded_cmp_rows(seq)
    c_start = np.arange(n_cmp) * CMP_STRIDE
    c_end = c_start + CMP_LEN - 1
    blk = np.arange(n_blk)
    ov = ((c_start[:, None] < (blk[None, :] + 1) * SEL_LEN) & (c_end[:, None] >= blk[None, :] * SEL_LEN))
    out = np.zeros((n_blk, ncp), np.float32)
    out[:, PAD_CMP:PAD_CMP + n_cmp] = ov.T
    return jnp.asarray(out, BF16)


def _s5_params(lam_re, lam_im, log_step, b_re, b_im, c_re, c_im):
    lr, li = lam_re.astype(F32), lam_im.astype(F32)
    dt = jnp.exp(log_step.astype(F32))[:, None]
    mag = jnp.exp(lr * dt)
    ar, ai = mag * jnp.cos(li * dt), mag * jnp.sin(li * dt)
    den = lr * lr + li * li
    kr = ((ar - 1.0) * lr + ai * li) / den
    ki = (ai * lr - (ar - 1.0) * li) / den
    br, bi = b_re.astype(F32), b_im.astype(F32)
    bbr = kr[..., None] * br - ki[..., None] * bi
    bbi = kr[..., None] * bi + ki[..., None] * br
    n_g = lr.shape[0]
    eye = jnp.eye(n_g, dtype=F32)

    def drive(bb):
        return jnp.einsum('gpc,gh->gchp', bb, eye).reshape(n_g * SSM_GROUP, n_g * SSM_STATE)

    def readout(c):
        return jnp.einsum('gcp,gh->gphc', c, eye).reshape(n_g * SSM_STATE, n_g * SSM_GROUP)

    wb = jnp.concatenate([drive(bbr), drive(bbi)], axis=1).astype(BF16)
    wc = jnp.concatenate([readout(c_re.astype(F32)), -readout(c_im.astype(F32))], axis=0).astype(BF16)
    a = jnp.concatenate([ar.reshape(1, -1), ai.reshape(1, -1)], axis=1)
    return wb, jnp.broadcast_to(a, (SUBLANES, a.shape[1])), wc


def _layer(x, mod, w_in, lam_re, lam_im, log_step, b_re, b_im, c_re, c_im, d_skip, glu_w, cmp_pos, cmp_w1,
           cmp_w2, rel_bias, w_out, ln1_g, ln1_b, router_w, router_bias, e_gate, e_up, e_down, sg, su, sd,
           ln2_g, ln2_b, alpha):
    bsz, seq, dm = x.shape
    n_tok = bsz * seq
    sh1, sc1, g1, sh2, sc2, g2 = [m[:, None, :] for m in jnp.split(mod, 6, axis=-1)]
    ssm_w = dm // 2
    attn_w = N_HEADS * HEAD_DIM
    kv_w = N_KV * HEAD_DIM
    n_gate = 3 * N_HEADS
    offs = np.cumsum([0, ssm_w, attn_w] + [kv_w] * 6 + [n_gate, 2 * dm])

    wu = w_in[:, offs[0]:offs[1]].astype(BF16)
    wq = w_in[:, offs[1]:offs[2]] * (HEAD_DIM ** -0.5)
    wa = jnp.concatenate([wq, w_in[:, offs[2]:offs[8]]], axis=1).astype(BF16)
    wg = jnp.pad(w_in[:, offs[8]:offs[9]], ((0, 0), (0, LANES - n_gate))).astype(BF16)
    wm = w_in[:, offs[9]:offs[10]].astype(BF16)
    u2d, act, gates, mg = _input_projection(x, sh1, sc1, wu, wa, wg, wm)

    nq = seq // Q_BLOCK
    n_blk = seq // SEL_LEN
    nch = seq // CMP_STRIDE

    def piece(i):
        return act[:, :, attn_w + i * kv_w: attn_w + (i + 1) * kv_w]

    def compress(raw, pos, w1, w2):
        xc = raw.reshape(bsz, nch, CMP_STRIDE, N_KV, HEAD_DIM).transpose(0, 3, 1, 2, 4).reshape(bsz, N_KV, nch, CMP_STRIDE * HEAD_DIM)
        half = CMP_STRIDE * HEAD_DIM
        w1cat = jnp.concatenate([w1[:half], w1[half:]], axis=1).astype(BF16)
        posb = jnp.dot(pos.reshape(1, -1), w1, precision=lax.Precision.HIGHEST)
        return _compress(xc, w1cat, posb, w2.astype(BF16))

    kc = compress(piece(0), cmp_pos[0], cmp_w1[0], cmp_w2[0])
    vc = compress(piece(1), cmp_pos[1], cmp_w1[1], cmp_w2[1])
    pad = ((0, 0), (0, 0), (PAD_CMP, _padded_cmp_rows(seq) - nch - PAD_CMP), (0, 0))
    kcp = jnp.pad(kc, pad).astype(BF16)
    vcpt = jnp.swapaxes(jnp.pad(vc, pad), 2, 3).astype(BF16)

    def heads(a):
        return a.reshape(bsz, seq, N_KV, HEAD_DIM).transpose(0, 2, 1, 3)

    nst = nq // NSA_TILES
    q = act[:, :, :attn_w].reshape(bsz, nst, NSA_TILES, Q_BLOCK, N_KV, GQA, HEAD_DIM)
    qt = q.transpose(0, 4, 1, 6, 2, 5, 3).reshape(bsz, N_KV, nst, HEAD_DIM, COLS)
    onehot = (np.arange(seq)[:, None] // SEL_LEN == np.arange(n_blk)[None, :]).astype(np.float32) * -MASK_BIG
    kaug = jnp.concatenate([heads(piece(2)), jnp.broadcast_to(jnp.asarray(onehot, BF16), (bsz, N_KV, seq, n_blk))], axis=3)
    vst = jnp.swapaxes(heads(piece(3)), 2, 3)
    kw = heads(piece(4))
    vwt = jnp.swapaxes(heads(piece(5)), 2, 3)
    gt = gates[:, :, :n_gate].reshape(bsz, nst, NSA_TILES, Q_BLOCK, N_KV, GQA, 3)
    gt = gt.transpose(0, 4, 1, 6, 2, 5, 3).reshape(bsz, N_KV, nst, 3, COLS)
    gt = jnp.pad(gt, ((0, 0), (0, 0), (0, 0), (0, SUBLANES - 3), (0, 0)))
    nb, cb = _bias_tiles(rel_bias)
    ot = _nsa(qt, kaug, vst, kw, vwt, kcp, vcpt, _overlap_t(seq), nb, cb, gt)
    ot = ot.reshape(bsz, N_KV, nst, HEAD_DIM, NSA_TILES, GQA, Q_BLOCK)
    y_nsa = ot.transpose(0, 2, 4, 6, 1, 5, 3).reshape(bsz, seq, attn_w)

    wb, a, wc = _s5_params(lam_re, lam_im, log_step, b_re, b_im, c_re, c_im)
    z2 = _s5(u2d.reshape(seq * bsz, ssm_w), wb, a, wc, d_skip.reshape(1, ssm_w).astype(F32), bsz)
    z2d = z2.reshape(seq, bsz * ssm_w)

    base, h2 = _mix_out(z2d, y_nsa, mg, x, g1, sh2, sc2, g2, ln1_g.reshape(1, dm), ln1_b.reshape(1, dm),
                        glu_w.astype(BF16), w_out.astype(BF16), sg.astype(BF16), su.astype(BF16), sd.astype(BF16), alpha)
    h2 = h2.reshape(n_tok, dm // 2)

    n_exp = router_w.shape[1]
    rt = min(ROUTER_TILE, n_tok)
    tri = jnp.asarray(np.triu(np.ones((rt, rt), np.float32), 1), BF16)
    top_e, top_w, rank, counts = _router(h2, router_w.T.astype(BF16), router_bias.reshape(n_exp, 1).astype(F32), tri)
    counts = counts[:, 0].astype(jnp.int32)
    padded = (counts + EXPERT_TILE - 1) // EXPERT_TILE * EXPERT_TILE
    pad_end = jnp.cumsum(padded)
    pad_start = pad_end - padded
    dest = pad_start[top_e] + rank
    n_rows = n_tok * TOP_K + n_exp * EXPERT_TILE
    n_tiles = n_rows // EXPERT_TILE
    tile_exp = jnp.minimum(jnp.searchsorted(pad_end, jnp.arange(n_tiles) * EXPERT_TILE, side='right'), n_exp - 1).astype(jnp.int32)
    n_used = (pad_end[-1] // EXPERT_TILE).astype(jnp.int32).reshape(1)
    mt = min(MOVE_TILE, seq)
    dest3 = dest.reshape(TOP_K, n_tok // mt, mt).transpose(1, 0, 2)
    xs = _dispatch(dest3, h2, n_rows)
    ys = _experts(tile_exp, n_used, xs, e_gate.astype(BF16), e_up.astype(BF16), e_down.astype(BF16))
    return _final(dest3, top_w.T, base, g2, ln2_g.reshape(1, dm), ln2_b.reshape(1, dm), ys)


def kernel(x, c, ada_w, ada_b, w_in, ssm_lambda_re, ssm_lambda_im, ssm_log_step, ssm_b_re, ssm_b_im, ssm_c_re, ssm_c_im, ssm_d, ssm_glu_w, cmp_pos, cmp_w1, cmp_w2, rel_bias, w_out, ln1_g, ln1_b, router_w, router_bias, exp_w_gate, exp_w_up, exp_w_down, sh_w_gate, sh_w_up, sh_w_down, ln2_g, ln2_b):
    depth = ada_w.shape[0]
    alpha = (2 * depth) ** 0.25
    for l in range(depth):
        mod = _modulation(c, ada_w[l], ada_b[l])
        x = _layer(x, mod, w_in[l], ssm_lambda_re[l], ssm_lambda_im[l], ssm_log_step[l], ssm_b_re[l], ssm_b_im[l],
                   ssm_c_re[l], ssm_c_im[l], ssm_d[l], ssm_glu_w[l], cmp_pos[l], cmp_w1[l], cmp_w2[l], rel_bias,
                   w_out[l], ln1_g[l], ln1_b[l], router_w[l], router_bias[l], exp_w_gate[l], exp_w_up[l],
                   exp_w_down[l], sh_w_gate[l], sh_w_up[l], sh_w_down[l], ln2_g[l], ln2_b[l], alpha)
    return x
```

```python
import functools
import math

import numpy as np
import jax
import jax.numpy as jnp
from jax import lax
from jax.experimental import pallas as pl
from jax.experimental.pallas import tpu as pltpu

F32 = jnp.float32
BF16 = jnp.bfloat16

SSM_GROUP = 16
SSM_STATE = 64
N_HEADS = 16
HEAD_DIM = 64
N_KV = 4
GQA = N_HEADS // N_KV
CMP_LEN = 32
CMP_STRIDE = 16
SEL_LEN = 64
SEL_TOP = 16
WINDOW = 512
Q_BLOCK = 64
FORCE_BONUS = 1.0e4
REL_BUCKETS = 32
REL_MAX_DIST = 128
TOP_K = 8
ROUTED_SCALE = 2.5
LN_EPS = 1e-5

LANES = 128
SUBLANES = 8
VMEM_LIMIT_BYTES = 56 * 1024 * 1024
ROW_TILE = 512
ROUTER_TILE = 256
EXPERT_TILE = 256
MOVE_TILE = 256
DEST_TILE = 2048
SCAN_CHUNK = 128
SCAN_LANES = 512
MASK_BIG = 32768.0
NEG = -1.0e30
LOG2E = math.log2(math.e)
WIN_BLOCKS = WINDOW // SEL_LEN
ROWS = GQA * Q_BLOCK
NSA_TILES = 4
COLS = NSA_TILES * ROWS
CHUNK_BLOCKS = 8
CHUNK_KEYS = CHUNK_BLOCKS * SEL_LEN
PAD_CMP = 8
BAND_ROWS = 24


def _cparams(sem):
    return pltpu.CompilerParams(dimension_semantics=sem, vmem_limit_bytes=VMEM_LIMIT_BYTES)


def _pack_rows(x):
    n = x.shape[1] // 2
    xb = x.astype(jnp.bfloat16).astype(F32)
    lo = lax.shift_right_logical(lax.bitcast_convert_type(xb[:, :n], jnp.uint32), jnp.uint32(16))
    return lax.bitcast_convert_type(xb[:, n:], jnp.uint32) | lo


def _unpack_rows(w):
    lo = lax.bitcast_convert_type(lax.shift_left(w, jnp.uint32(16)), F32)
    hi = lax.bitcast_convert_type(w & jnp.uint32(0xFFFF0000), F32)
    return lo, hi


def _norm_rows(x):
    mu = jnp.mean(x, axis=-1, keepdims=True)
    xc = x - mu
    var = jnp.mean(xc * xc, axis=-1, keepdims=True)
    return xc * lax.rsqrt(var + LN_EPS)


def _mod_kernel(c_ref, w_ref, b_ref, o_ref):
    cond = jax.nn.silu(c_ref[...])
    o_ref[...] = jnp.dot(cond.astype(BF16), w_ref[...].astype(BF16), preferred_element_type=F32) + b_ref[...]


def _modulation(c, ada_w, ada_b):
    bsz, dm = c.shape
    n = ada_w.shape[1]
    tn = dm
    return pl.pallas_call(
        _mod_kernel,
        grid=(n // tn,),
        in_specs=[pl.BlockSpec((bsz, dm), lambda j: (0, 0)),
                  pl.BlockSpec((dm, tn), lambda j: (0, j)),
                  pl.BlockSpec((1, tn), lambda j: (0, j))],
        out_specs=pl.BlockSpec((bsz, tn), lambda j: (0, j)),
        out_shape=jax.ShapeDtypeStruct((bsz, n), F32),
        compiler_params=_cparams(("arbitrary",)),
        name="ada_modulation",
    )(c, ada_w, ada_b.reshape(1, n))


def _inproj_kernel(x_ref, sh_ref, sc_ref, wu_ref, wa_ref, wg_ref, wm_ref, u_ref, a_ref, g_ref, m_ref):
    h = _norm_rows(x_ref[0]) * (1.0 + sc_ref[0]) + sh_ref[0]
    hb = h.astype(BF16)
    u_ref[...] = jnp.dot(hb, wu_ref[...], preferred_element_type=F32)
    a_ref[0] = jnp.dot(hb, wa_ref[...], preferred_element_type=F32).astype(BF16)
    g_ref[0] = jax.nn.sigmoid(jnp.dot(hb, wg_ref[...], preferred_element_type=F32))
    m_ref[0] = jax.nn.sigmoid(jnp.dot(hb, wm_ref[...], preferred_element_type=F32)).astype(BF16)


def _input_projection(x, sh1, sc1, wu, wa, wg, wm):
    bsz, seq, dm = x.shape
    tm = min(ROW_TILE, seq)
    nu, na, ng, nm = wu.shape[1], wa.shape[1], wg.shape[1], wm.shape[1]
    full = lambda n: pl.BlockSpec((dm, n), lambda b, i: (0, 0))
    vec = pl.BlockSpec((1, 1, dm), lambda b, i: (b, 0, 0))
    return pl.pallas_call(
        _inproj_kernel,
        grid=(bsz, seq // tm),
        in_specs=[pl.BlockSpec((1, tm, dm), lambda b, i: (b, i, 0)), vec, vec,
                  full(nu), full(na), full(ng), full(nm)],
        out_specs=[pl.BlockSpec((tm, nu), lambda b, i: (i, b)),
                   pl.BlockSpec((1, tm, na), lambda b, i: (b, i, 0)),
                   pl.BlockSpec((1, tm, ng), lambda b, i: (b, i, 0)),
                   pl.BlockSpec((1, tm, nm), lambda b, i: (b, i, 0))],
        out_shape=[jax.ShapeDtypeStruct((seq, bsz * nu), F32),
                   jax.ShapeDtypeStruct((bsz, seq, na), BF16),
                   jax.ShapeDtypeStruct((bsz, seq, ng), F32),
                   jax.ShapeDtypeStruct((bsz, seq, nm), BF16)],
        compiler_params=_cparams(("arbitrary", "arbitrary")),
        name="adaln_input_projection",
    )(x, sh1, sc1, wu, wa, wg, wm)


def _compress_kernel(x_ref, w1_ref, pb_ref, w2_ref, o_ref, *, n_cmp):
    hid = w2_ref.shape[0]
    p = jnp.dot(x_ref[0, 0], w1_ref[...], preferred_element_type=F32)
    nrow = p.shape[0]
    nxt = pltpu.roll(p[:, hid:], nrow - 1, 0)
    hidv = jax.nn.gelu(p[:, :hid] + nxt + pb_ref[...])
    out = jnp.dot(hidv.astype(BF16), w2_ref[...], preferred_element_type=F32)
    rows = lax.broadcasted_iota(jnp.int32, out.shape, 0)
    o_ref[0, 0] = jnp.where(rows < n_cmp, out, 0.0)


def _compress(xc, w1cat, posb, w2):
    bsz, nkv, nch, kdim = xc.shape
    hid2 = w1cat.shape[1]
    return pl.pallas_call(
        functools.partial(_compress_kernel, n_cmp=nch - 1),
        grid=(bsz, nkv),
        in_specs=[pl.BlockSpec((1, 1, nch, kdim), lambda b, g: (b, g, 0, 0)),
                  pl.BlockSpec((kdim, hid2), lambda b, g: (0, 0)),
                  pl.BlockSpec((1, hid2 // 2), lambda b, g: (0, 0)),
                  pl.BlockSpec((hid2 // 2, HEAD_DIM), lambda b, g: (0, 0))],
        out_specs=pl.BlockSpec((1, 1, nch, HEAD_DIM), lambda b, g: (b, g, 0, 0)),
        out_shape=jax.ShapeDtypeStruct((bsz, nkv, nch, HEAD_DIM), F32),
        compiler_params=_cparams(("arbitrary", "arbitrary")),
        name="kv_compress",
    )(xc, w1cat, posb, w2)


def _nsa_kernel(qt_ref, kaug_ref, vst_ref, kw_ref, vwt_ref, kcp_ref, vcpt_ref, ovt_ref, nb_ref, cb_ref,
                g_ref, o_ref, s_ref, c_ref, fa_ref, fb_ref, *, n_blk, n_cmp, n_sel):
    q0 = pl.program_id(2) * NSA_TILES
    qt = qt_ref[0, 0, 0]
    col_tile = lambda n: slice(n * ROWS, (n + 1) * ROWS)

    s_ref[...] = jnp.dot(kcp_ref[0, 0], qt, preferred_element_type=F32)
    for n in range(NSA_TILES):
        band = pl.multiple_of(SUBLANES * ((q0 + n) // 2), SUBLANES)
        s_ref[pl.ds(band, BAND_ROWS), col_tile(n)] += cb_ref[0, n % 2]
    s = s_ref[...]
    rho = lax.broadcasted_iota(jnp.int32, s.shape, 0)
    col = lax.broadcasted_iota(jnp.int32, s.shape, 1)
    tok = col & (Q_BLOCK - 1)
    qblk = q0 + jnp.right_shift(col, ROWS.bit_length() - 1)
    first_visible = CMP_STRIDE * (rho - PAD_CMP) + (CMP_LEN - 1) - Q_BLOCK * qblk
    vis = (rho >= PAD_CMP) & (rho < PAD_CMP + n_cmp) & (first_visible <= tok)
    s = jnp.where(vis, s, -jnp.inf)
    mx = jnp.max(s, axis=0, keepdims=True)
    mx = jnp.where(mx == -jnp.inf, 0.0, mx)
    e = jnp.exp2(s - mx)
    p_cmp = (e / jnp.maximum(jnp.sum(e, axis=0, keepdims=True), 1e-30)).astype(BF16)
    o_cmp = jnp.dot(vcpt_ref[0, 0], p_cmp, preferred_element_type=F32)

    imp4 = jnp.dot(ovt_ref[...], p_cmp, preferred_element_type=F32)
    sums = []
    for n in range(NSA_TILES):
        two = imp4[:, n * ROWS:n * ROWS + LANES] + imp4[:, n * ROWS + LANES:(n + 1) * ROWS]
        sums.append(two + pltpu.roll(two, Q_BLOCK, 1))
    low = lax.broadcasted_iota(jnp.int32, sums[0].shape, 1) < Q_BLOCK
    imp = jnp.concatenate([jnp.where(low, sums[n], sums[n + 1]) for n in range(0, NSA_TILES, 2)], axis=1)
    blk = lax.broadcasted_iota(jnp.int32, imp.shape, 0)
    cur = q0 + jnp.right_shift(lax.broadcasted_iota(jnp.int32, imp.shape, 1), Q_BLOCK.bit_length() - 1)
    blkf = blk.astype(F32)
    forced = (blk == 0) | (blk == cur) | (blk == cur - 1)
    score = jnp.where(blk <= cur, imp + jnp.where(forced, FORCE_BONUS, 0.0), -jnp.inf)
    notsel = jnp.ones(imp.shape, F32)
    for _ in range(n_sel):
        best = jnp.max(score, axis=0, keepdims=True)
        first = jnp.min(jnp.where(score == best, blkf, float(n_blk)), axis=0, keepdims=True)
        pick = blkf == first
        notsel = jnp.where(pick, 0.0, notsel)
        score = jnp.where(pick, -jnp.inf, score)
    halves = []
    for n in range(0, NSA_TILES, 2):
        pair = notsel[:, (n // 2) * LANES:(n // 2 + 1) * LANES]
        swapped = pltpu.roll(pair, Q_BLOCK, 1)
        halves += [jnp.where(low, pair, swapped), jnp.where(low, swapped, pair)]
    notsel = jnp.concatenate([h for h in halves for _ in (0, 1)], axis=1).astype(BF16)
    qaug = jnp.concatenate([qt, notsel], axis=0)

    def update(carry, sc, vt):
        m, acc = carry
        m_new = jnp.maximum(m, jnp.max(sc, axis=0, keepdims=True))
        p = jnp.exp2((sc - m_new).astype(BF16))
        return m_new, jnp.exp2(m - m_new) * acc + jnp.dot(vt, p, preferred_element_type=F32)

    def chunk_keys(c):
        return pl.ds(pl.multiple_of(c * CHUNK_KEYS, CHUNK_KEYS), CHUNK_KEYS)

    def far_scores(c):
        return jnp.dot(kaug_ref[0, 0, chunk_keys(c), :], qaug, preferred_element_type=F32)

    def far_pair(i, carry):
        fb_ref[...] = far_scores(2 * i + 1)
        carry = update(carry, fa_ref[...], vst_ref[0, 0, :, chunk_keys(2 * i)])
        fa_ref[...] = far_scores(2 * i + 2)
        return update(carry, fb_ref[...], vst_ref[0, 0, :, chunk_keys(2 * i + 1)])

    def far_last(_, carry, n_far):
        return update(carry, fa_ref[...], vst_ref[0, 0, :, chunk_keys(n_far - 1)])

    def near_chunk(k_ref, vt_ref, qmat, blk0, tile_of, carry):
        keys = pl.ds(pl.multiple_of(blk0 * SEL_LEN, NSA_TILES * SEL_LEN), CHUNK_KEYS)
        c_ref[...] = jnp.dot(k_ref[0, 0, keys, :], qmat, preferred_element_type=F32)
        for o in range(CHUNK_BLOCKS):
            for n in range(NSA_TILES):
                c_ref[o * SEL_LEN:(o + 1) * SEL_LEN, col_tile(n)] += nb_ref[0, tile_of(blk0 + o, q0 + n)]
        return update(carry, c_ref[...], vt_ref[0, 0, :, keys])

    def sel_tile(j, qi):
        d = qi - j
        return jnp.where(d < 0, 4, jnp.minimum(d, 3))

    def win_tile(j, qi):
        d = qi - j
        return jnp.where((d < 0) | (d > WIN_BLOCKS), 4, jnp.where(d == WIN_BLOCKS, 5, jnp.minimum(d, 3)))

    init = (jnp.full((1, COLS), NEG, F32), jnp.zeros((vst_ref.shape[2], COLS), F32))
    n_far = jnp.maximum(q0 - 2, 0) // CHUNK_BLOCKS
    fa_ref[...] = far_scores(0)
    carry = lax.fori_loop(0, n_far // 2, far_pair, init)
    carry = lax.fori_loop(0, n_far % 2, functools.partial(far_last, n_far=n_far), carry)
    _, acc = lax.fori_loop(
        n_far, (q0 + NSA_TILES - 1) // CHUNK_BLOCKS + 1,
        lambda c, cr: near_chunk(kaug_ref, vst_ref, qaug, c * CHUNK_BLOCKS, sel_tile, cr), carry)
    o_slc = acc[:HEAD_DIM] / acc[HEAD_DIM:HEAD_DIM + 1]
    win0 = jnp.maximum(q0 - (WIN_BLOCKS + NSA_TILES), 0)
    _, acc = lax.fori_loop(
        0, 2, lambda c, cr: near_chunk(kw_ref, vwt_ref, qt, win0 + c * CHUNK_BLOCKS, win_tile, cr), init)
    o_win = acc[:HEAD_DIM] / acc[HEAD_DIM:HEAD_DIM + 1]
    g = g_ref[0, 0, 0]
    o_ref[0, 0, 0] = (g[0:1] * o_cmp + g[1:2] * o_slc + g[2:3] * o_win).astype(BF16)


def _nsa(qt, kaug, vst, kw, vwt, kcp, vcpt, ovt, nb, cb, gt):
    bsz, nkv, nsteps, dh, cols = qt.shape
    seq = kw.shape[2]
    n_blk = seq // SEL_LEN
    ncp = kcp.shape[2]
    n_cmp = (seq - CMP_LEN) // CMP_STRIDE + 1
    assert n_blk % CHUNK_BLOCKS == 0 and n_blk >= 2 * CHUNK_BLOCKS and WIN_BLOCKS + 2 * NSA_TILES == 2 * CHUNK_BLOCKS
    per_bg = lambda shape: pl.BlockSpec((1, 1) + shape, lambda b, g, i: (b, g, 0, 0))
    per_step = lambda shape: pl.BlockSpec((1, 1, 1) + shape, lambda b, g, i: (b, g, i, 0, 0))
    kern = functools.partial(_nsa_kernel, n_blk=n_blk, n_cmp=n_cmp, n_sel=min(SEL_TOP, n_blk))
    return pl.pallas_call(
        kern,
        grid=(bsz, nkv, nsteps),
        in_specs=[per_step((dh, cols)),
                  per_bg((seq, kaug.shape[3])), per_bg((vst.shape[2], seq)), per_bg((seq, dh)), per_bg((vwt.shape[2], seq)),
                  per_bg((ncp, dh)), per_bg((dh, ncp)),
                  pl.BlockSpec(ovt.shape, lambda b, g, i: (0, 0)),
                  pl.BlockSpec((1,) + nb.shape[1:], lambda b, g, i: (g, 0, 0, 0)),
                  pl.BlockSpec((1,) + cb.shape[1:], lambda b, g, i: (g, 0, 0, 0)),
                  per_step((SUBLANES, cols))],
        out_specs=per_step((dh, cols)),
        out_shape=jax.ShapeDtypeStruct((bsz, nkv, nsteps, dh, cols), BF16),
        scratch_shapes=[pltpu.VMEM((ncp, cols), F32)] + [pltpu.VMEM((CHUNK_KEYS, cols), F32)] * 3,
        compiler_params=_cparams(("arbitrary", "arbitrary", "arbitrary")),
        name="nsa_attention",
    )(qt, kaug, vst, kw, vwt, kcp, vcpt, ovt, nb, cb, gt)


def _s5_kernel(u_ref, wb_ref, a_ref, wc_ref, d_ref, z_ref, xs_ref, st_ref, *, bsz, n_state):
    @pl.when(pl.program_id(0) == 0)
    def _():
        st_ref[...] = jnp.zeros_like(st_ref)

    u = u_ref[...]
    xs_ref[...] = jnp.dot(u.astype(BF16), wb_ref[...], preferred_element_type=F32)
    steps = u.shape[0] // bsz
    for c0 in range(0, n_state, SCAN_LANES):
        re = pl.ds(c0, SCAN_LANES)
        im = pl.ds(n_state + c0, SCAN_LANES)
        ar = jnp.broadcast_to(a_ref[0:1, re], (bsz, SCAN_LANES))
        ai = jnp.broadcast_to(a_ref[0:1, im], (bsz, SCAN_LANES))

        def step(t, carry):
            xr, xi = carry
            rows = pl.ds(pl.multiple_of(t * bsz, bsz), bsz)
            nr = ar * xr - ai * xi + xs_ref[rows, re]
            ni = ar * xi + ai * xr + xs_ref[rows, im]
            xs_ref[rows, re] = nr
            xs_ref[rows, im] = ni
            return nr, ni

        xr, xi = lax.fori_loop(0, steps, step, (st_ref[:, re], st_ref[:, im]), unroll=8)
        st_ref[:, re] = xr
        st_ref[:, im] = xi
    y = jnp.dot(xs_ref[...].astype(BF16), wc_ref[...], preferred_element_type=F32) + d_ref[...] * u
    z_ref[...] = jax.nn.gelu(y).astype(BF16)


def _s5(u2, wb, a, wc, dsk, bsz):
    rows, width = u2.shape
    seq = rows // bsz
    chunk = min(SCAN_CHUNK, seq)
    n_state2 = wb.shape[1]
    return pl.pallas_call(
        functools.partial(_s5_kernel, bsz=bsz, n_state=n_state2 // 2),
        grid=(seq // chunk,),
        in_specs=[pl.BlockSpec((chunk * bsz, width), lambda i: (i, 0)),
                  pl.BlockSpec(wb.shape, lambda i: (0, 0)),
                  pl.BlockSpec(a.shape, lambda i: (0, 0)),
                  pl.BlockSpec(wc.shape, lambda i: (0, 0)),
                  pl.BlockSpec(dsk.shape, lambda i: (0, 0))],
        out_specs=pl.BlockSpec((chunk * bsz, width), lambda i: (i, 0)),
        out_shape=jax.ShapeDtypeStruct((rows, width), BF16),
        scratch_shapes=[pltpu.VMEM((chunk * bsz, n_state2), F32), pltpu.VMEM((bsz, n_state2), F32)],
        compiler_params=_cparams(("arbitrary",)),
        name="s5_scan",
    )(u2, wb, a, wc, dsk)


def _mixout_kernel(z_ref, yn_ref, mg_ref, x_ref, g1_ref, sh2_ref, sc2_ref, g2_ref, lg_ref, lb_ref,
                   glu_ref, wo_ref, sg_ref, su_ref, sd_ref, base_ref, h_ref, *, alpha):
    dm = x_ref.shape[2]
    glu = jnp.dot(z_ref[...], glu_ref[...], preferred_element_type=F32)
    y_ssm = glu[:, :dm] * jax.nn.sigmoid(glu[:, dm:])
    mg = mg_ref[0].astype(F32)
    merged = mg[:, :dm] * y_ssm + mg[:, dm:] * yn_ref[0].astype(F32)
    y = jnp.dot(merged.astype(BF16), wo_ref[...], preferred_element_type=F32)
    x1 = _norm_rows(alpha * x_ref[0] + g1_ref[0] * y) * lg_ref[...] + lb_ref[...]
    hf = _norm_rows(x1) * (1.0 + sc2_ref[0]) + sh2_ref[0]
    h_ref[0] = _pack_rows(hf)
    h = hf.astype(BF16)
    hs =jax.nn.silu(jnp.dot(h, sg_ref[...], preferred_element_type=F32)) * jnp.dot(h, su_ref[...], preferred_element_type=F32)
    shared = jnp.dot(hs.astype(BF16), sd_ref[...], preferred_element_type=F32)
    base_ref[0] = alpha * x1 + g2_ref[0] * shared


def _mix_out(z2d, y_nsa, mg, x, g1, sh2, sc2, g2, ln_g, ln_b, glu_w, w_out, sg, su, sd, alpha):
    bsz, seq, dm = x.shape
    tm = min(ROW_TILE, seq)
    width = z2d.shape[1] // bsz
    vec = pl.BlockSpec((1, 1, dm), lambda b, i: (b, 0, 0))
    row = pl.BlockSpec((1, dm), lambda b, i: (0, 0))
    full = lambda w: pl.BlockSpec(w.shape, lambda b, i: (0, 0))
    tile = lambda n: pl.BlockSpec((1, tm, n), lambda b, i: (b, i, 0))
    return pl.pallas_call(
        functools.partial(_mixout_kernel, alpha=alpha),
        grid=(bsz, seq // tm),
        in_specs=[pl.BlockSpec((tm, width), lambda b, i: (i, b)), tile(dm), tile(2 * dm), tile(dm),
                  vec, vec, vec, vec, row, row, full(glu_w), full(w_out), full(sg), full(su), full(sd)],
        out_specs=[tile(dm), tile(dm // 2)],
        out_shape=[jax.ShapeDtypeStruct((bsz, seq, dm), F32), jax.ShapeDtypeStruct((bsz, seq, dm // 2), jnp.uint32)],
        compiler_params=_cparams(("arbitrary", "arbitrary")),
        name="merge_outproj_ln_shared",
    )(z2d, y_nsa, mg, x, g1, sh2, sc2, g2, ln_g, ln_b, glu_w, w_out, sg, su, sd)


def _router_kernel(h_ref, rwt_ref, rb_ref, tri_ref, e_ref, w_ref, p_ref, cnt_ref):
    @pl.when(pl.program_id(0) == 0)
    def _():
        cnt_ref[...] = jnp.zeros_like(cnt_ref)

    h = jnp.concatenate(_unpack_rows(h_ref[...]), axis=1).astype(BF16)
    logits = lax.dot_general(rwt_ref[...], h, (((1,), (1,)), ((), ())), preferred_element_type=F32)
    scores = jax.nn.sigmoid(logits)
    cur = scores + rb_ref[...]
    n_exp = scores.shape[0]
    eid = lax.broadcasted_iota(jnp.int32, scores.shape, 0).astype(F32)
    chosen = jnp.zeros(scores.shape, F32)
    ids, vals = [], []
    for _ in range(TOP_K):
        best = jnp.max(cur, axis=0, keepdims=True)
        first = jnp.min(jnp.where(cur == best, eid, float(n_exp)), axis=0, keepdims=True)
        pick = eid == first
        ids.append(first)
        vals.append(jnp.sum(jnp.where(pick, scores, 0.0), axis=0, keepdims=True))
        chosen = jnp.where(pick, 1.0, chosen)
        cur = jnp.where(pick, -jnp.inf, cur)
    top_s = jnp.concatenate(vals, axis=0)
    w_ref[...] = top_s / jnp.sum(top_s, axis=0, keepdims=True) * ROUTED_SCALE
    top_e = jnp.concatenate(ids, axis=0)
    e_ref[...] = top_e.astype(jnp.int32)
    before = jnp.dot(chosen.astype(BF16), tri_ref[...], preferred_element_type=F32) + cnt_ref[...]
    ranks = [jnp.sum(jnp.where(eid == ids[k], before, 0.0), axis=0, keepdims=True) for k in range(TOP_K)]
    p_ref[...] = jnp.concatenate(ranks, axis=0).astype(jnp.int32)
    cnt_ref[...] += jnp.sum(chosen, axis=1, keepdims=True)


def _router(h2, rwt, rb, tri):
    n_tok, words = h2.shape
    n_exp, dm = rwt.shape
    tm = tri.shape[0]
    kt = pl.BlockSpec((TOP_K, tm), lambda i: (0, i))
    return pl.pallas_call(
        _router_kernel,
        grid=(n_tok // tm,),
        in_specs=[pl.BlockSpec((tm, words), lambda i: (i, 0)),
                  pl.BlockSpec((n_exp, dm), lambda i: (0, 0)),
                  pl.BlockSpec((n_exp, 1), lambda i: (0, 0)),
                  pl.BlockSpec((tm, tm), lambda i: (0, 0))],
        out_specs=[kt, kt, kt, pl.BlockSpec((n_exp, 1), lambda i: (0, 0))],
        out_shape=[jax.ShapeDtypeStruct((TOP_K, n_tok), jnp.int32),
                   jax.ShapeDtypeStruct((TOP_K, n_tok), F32),
                   jax.ShapeDtypeStruct((TOP_K, n_tok), jnp.int32),
                   jax.ShapeDtypeStruct((n_exp, 1), F32)],
        compiler_params=_cparams(("arbitrary",)),
        name="router_topk_rank",
    )(h2, rwt, rb, tri)


def _dest_kernel(start_ref, e_ref, r_ref, o_ref):
    e = e_ref[...]
    start = lax.fori_loop(0, start_ref.shape[0], lambda j, acc: jnp.where(e == j, start_ref[j], acc),
                          jnp.zeros(e.shape, jnp.int32))
    o_ref[...] = start + r_ref[...]


def _dest_rows(pad_start, top_e, rank):
    n_tok = top_e.shape[1]
    tm = min(DEST_TILE, n_tok)
    blk = pl.BlockSpec((TOP_K, tm), lambda i, ps: (0, i))
    return pl.pallas_call(
        _dest_kernel,
        grid_spec=pltpu.PrefetchScalarGridSpec(num_scalar_prefetch=1, grid=(n_tok // tm,), in_specs=[blk, blk], out_specs=blk),
        out_shape=jax.ShapeDtypeStruct(top_e.shape, jnp.int32),
        compiler_params=_cparams(("arbitrary",)),
        name="moe_dest_rows",
    )(pad_start, top_e, rank)


def _row_copy(src_ref, src_row, dst_ref, dst_row, sem):
    return pltpu.make_async_copy(src_ref.at[pl.ds(src_row, 1), :], dst_ref.at[pl.ds(dst_row, 1), :], sem)


def _dispatch_kernel(dest_ref, h_ref, init_ref, xs_ref, sem):
    del init_ref
    tm = h_ref.shape[0]

    def issue(t, carry):
        for k in range(TOP_K):
            _row_copy(h_ref, t, xs_ref, dest_ref[0, k, t], sem).start()
        return carry

    lax.fori_loop(0, tm, issue, 0)
    pltpu.make_async_copy(xs_ref.at[pl.ds(0, TOP_K * tm), :], xs_ref.at[pl.ds(0, TOP_K * tm), :], sem).wait()


def _dispatch(dest3, h2, n_rows):
    n_tok, words = h2.shape
    tm = dest3.shape[2]
    return pl.pallas_call(
        _dispatch_kernel,
        grid=(n_tok // tm,),
        in_specs=[pl.BlockSpec((1, TOP_K, tm), lambda i: (i, 0, 0), memory_space=pltpu.SMEM),
                  pl.BlockSpec((tm, words), lambda i: (i, 0)),
                  pl.BlockSpec(memory_space=pl.ANY)],
        out_specs=pl.BlockSpec(memory_space=pl.ANY),
        out_shape=jax.ShapeDtypeStruct((n_rows, words), jnp.uint32),
        scratch_shapes=[pltpu.SemaphoreType.DMA(())],
        input_output_aliases={2: 0},
        compiler_params=_cparams(("arbitrary",)),
        name="moe_dispatch",
    )(dest3, h2, jnp.zeros((n_rows, words), jnp.uint32))


def _expert_kernel(te_ref, nu_ref, x_ref, wg_ref, wu_ref, wd_ref, y_ref):
    i = pl.program_id(0)

    @pl.when(i < nu_ref[0])
    def _():
        lo, hi = _unpack_rows(x_ref[...])
        lo, hi = lo.astype(BF16), hi.astype(BF16)
        half = lo.shape[1]

        def proj(w_ref):
            return (jnp.dot(lo, w_ref[0, :half, :], preferred_element_type=F32)
                    + jnp.dot(hi, w_ref[0, half:, :], preferred_element_type=F32))

        hmid = jax.nn.silu(proj(wg_ref)) * proj(wu_ref)
        y_ref[...] = _pack_rows(jnp.dot(hmid.astype(BF16), wd_ref[0], preferred_element_type=F32))

    @pl.when(i >= nu_ref[0])
    def _():
        y_ref[...] = jnp.zeros_like(y_ref)


def _experts(tile_exp, n_used, xs, wg, wu, wd):
    n_rows, words = xs.shape
    dm, de = wg.shape[1], wg.shape[2]
    grid_spec = pltpu.PrefetchScalarGridSpec(
        num_scalar_prefetch=2,
        grid=(n_rows // EXPERT_TILE,),
        in_specs=[pl.BlockSpec((EXPERT_TILE, words), lambda i, te, nu: (i, 0)),
                  pl.BlockSpec((1, dm, de), lambda i, te, nu: (te[i], 0, 0)),
                  pl.BlockSpec((1, dm, de), lambda i, te, nu: (te[i], 0, 0)),
                  pl.BlockSpec((1, de, dm), lambda i, te, nu: (te[i], 0, 0))],
        out_specs=pl.BlockSpec((EXPERT_TILE, words), lambda i, te, nu: (i, 0)),
    )
    return pl.pallas_call(
        _expert_kernel,
        grid_spec=grid_spec,
        out_shape=jax.ShapeDtypeStruct((n_rows, words), jnp.uint32),
        compiler_params=_cparams(("arbitrary",)),
        name="expert_mlp",
    )(tile_exp, n_used, xs, wg, wu, wd)


def _final_kernel(dest_ref, w_ref, base_ref, g2_ref, lg_ref, lb_ref, ys_ref, o_ref, buf_ref, sem):
    tm = base_ref.shape[1]

    def issue(t, carry):
        for k in range(TOP_K):
            _row_copy(ys_ref, dest_ref[0, k, t], buf_ref.at[k], t, sem).start()
        return carry

    lax.fori_loop(0, tm, issue, 0)
    pltpu.make_async_copy(buf_ref, buf_ref, sem).wait()
    w = w_ref[...]
    lo_sum, hi_sum = jnp.zeros((tm, buf_ref.shape[2]), F32), jnp.zeros((tm, buf_ref.shape[2]), F32)
    for k in range(TOP_K):
        lo, hi = _unpack_rows(buf_ref[k])
        lo_sum += w[:, k:k + 1] * lo
        hi_sum += w[:, k:k + 1] * hi
    routed = jnp.concatenate([lo_sum, hi_sum], axis=1)
    o_ref[0] = _norm_rows(base_ref[0] + g2_ref[0] * routed) * lg_ref[...] + lb_ref[...]


def _final(dest3, w_tok, base, g2, ln_g, ln_b, ys):
    bsz, seq, dm = base.shape
    tm = dest3.shape[2]
    nt = seq // tm
    tile = pl.BlockSpec((1, tm, dm), lambda b, i: (b, i, 0))
    return pl.pallas_call(
        _final_kernel,
        grid=(bsz, nt),
        in_specs=[pl.BlockSpec((1, TOP_K, tm), lambda b, i: (b * nt + i, 0, 0), memory_space=pltpu.SMEM),
                  pl.BlockSpec((tm, TOP_K), lambda b, i: (b * nt + i, 0)),
                  tile, pl.BlockSpec((1, 1, dm), lambda b, i: (b, 0, 0)),
                  pl.BlockSpec((1, dm), lambda b, i: (0, 0)), pl.BlockSpec((1, dm), lambda b, i: (0, 0)),
                  pl.BlockSpec(memory_space=pl.ANY)],
        out_specs=tile,
        out_shape=jax.ShapeDtypeStruct((bsz, seq, dm), F32),
        scratch_shapes=[pltpu.VMEM((TOP_K, tm, ys.shape[1]), jnp.uint32), pltpu.SemaphoreType.DMA(())],
        compiler_params=_cparams(("arbitrary", "arbitrary")),
        name="combine_final_layernorm",
    )(dest3, w_tok, base, g2, ln_g, ln_b, ys)


def _rel_bucket(dist):
    dist = jnp.maximum(dist, 0)
    exact = REL_BUCKETS // 2
    log_ratio = jnp.log(jnp.maximum(dist, 1).astype(F32) / exact) / math.log(REL_MAX_DIST / exact)
    large = jnp.minimum(exact + (log_ratio * (REL_BUCKETS - exact)).astype(jnp.int32), REL_BUCKETS - 1)
    return jnp.where(dist < exact, dist, large)


def _bias_tiles(rel_bias):
    n_d = 4 * SEL_LEN
    vec = rel_bias[_rel_bucket(jnp.arange(n_d))].T
    far = rel_bias[REL_BUCKETS - 1]
    vec = ((vec - far[:, None]) * LOG2E).reshape(N_KV, GQA, n_d)
    tok = np.arange(Q_BLOCK)[None, :]
    key = np.arange(SEL_LEN)[:, None]

    def toeplitz(d):
        vals = vec[:, :, np.clip(d, 0, n_d - 1)]
        vals = jnp.where(jnp.asarray(d >= 0), vals, NEG)
        return jnp.transpose(vals, (0, 2, 1, 3)).reshape(N_KV, d.shape[0], ROWS)

    near = [toeplitz(delta + tok - key) for delta in (0, SEL_LEN, 2 * SEL_LEN)]
    zero = jnp.zeros((N_KV, SEL_LEN, ROWS), F32)
    edge = np.where(tok < key, 0.0, NEG).astype(np.float32)
    edge = jnp.broadcast_to(jnp.asarray(np.tile(edge, (1, GQA)))[None], (N_KV, SEL_LEN, ROWS))
    nb = jnp.stack(near + [zero, jnp.full_like(zero, NEG), edge], axis=1)
    w = np.arange(BAND_ROWS)[:, None]
    bands = []
    for ph in (0, 1):
        d = tok - CMP_STRIDE * (w - PAD_CMP - (Q_BLOCK // CMP_STRIDE) * ph) - (CMP_LEN - 1)
        vals = vec[:, :, np.clip(d, 0, n_d - 1)]
        vals = jnp.where(jnp.asarray(d >= 0), vals, 0.0)
        bands.append(jnp.transpose(vals, (0, 2, 1, 3)).reshape(N_KV, BAND_ROWS, ROWS))
    cb = jnp.stack(bands, axis=1)
    return nb.astype(F32), cb.astype(F32)


def _padded_cmp_rows(seq):
    return -(-(seq // CMP_STRIDE + 2 * PAD_CMP) // LANES) * LANES


def _overlap_t(seq):
    n_cmp = (seq - CMP_LEN) // CMP_STRIDE + 1
    n_blk = seq // SEL_LEN
    ncp = _padded_cmp_rows(seq)
    c_start = np.arange(n_cmp) * CMP_STRIDE
    c_end = c_start + CMP_LEN - 1
    blk = np.arange(n_blk)
    ov = ((c_start[:, None] < (blk[None, :] + 1) * SEL_LEN) & (c_end[:, None] >= blk[None, :] * SEL_LEN))
    out = np.zeros((n_blk, ncp), np.float32)
    out[:, PAD_CMP:PAD_CMP + n_cmp] = ov.T
    return jnp.asarray(out, BF16)


def _s5_params(lam_re, lam_im, log_step, b_re, b_im, c_re, c_im):
    lr, li = lam_re.astype(F32), lam_im.astype(F32)
    dt = jnp.exp(log_step.astype(F32))[:, None]
    mag = jnp.exp(lr * dt)
    ar, ai = mag * jnp.cos(li * dt), mag * jnp.sin(li * dt)
    den = lr * lr + li * li
    kr = ((ar - 1.0) * lr + ai * li) / den
    ki = (ai * lr - (ar - 1.0) * li) / den
    br, bi = b_re.astype(F32), b_im.astype(F32)
    bbr = kr[..., None] * br - ki[..., None] * bi
    bbi = kr[..., None] * bi + ki[..., None] * br
    n_g = lr.shape[0]
    eye = jnp.eye(n_g, dtype=F32)

    def drive(bb):
        return jnp.einsum('gpc,gh->gchp', bb, eye).reshape(n_g * SSM_GROUP, n_g * SSM_STATE)

    def readout(c):
        return jnp.einsum('gcp,gh->gphc', c, eye).reshape(n_g * SSM_STATE, n_g * SSM_GROUP)

    wb = jnp.concatenate([drive(bbr), drive(bbi)], axis=1).astype(BF16)
    wc = jnp.concatenate([readout(c_re.astype(F32)), -readout(c_im.astype(F32))], axis=0).astype(BF16)
    a = jnp.concatenate([ar.reshape(1, -1), ai.reshape(1, -1)], axis=1)
    return wb, jnp.broadcast_to(a, (SUBLANES, a.shape[1])), wc


def _layer(x, mod, w_in, lam_re, lam_im, log_step, b_re, b_im, c_re, c_im, d_skip, glu_w, cmp_pos, cmp_w1,
           cmp_w2, rel_bias, w_out, ln1_g, ln1_b, router_w, router_bias, e_gate, e_up, e_down, sg, su, sd,
           ln2_g, ln2_b, alpha):
    bsz, seq, dm = x.shape
    n_tok = bsz * seq
    sh1, sc1, g1, sh2, sc2, g2 = [m[:, None, :] for m in jnp.split(mod, 6, axis=-1)]
    ssm_w = dm // 2
    attn_w = N_HEADS * HEAD_DIM
    kv_w = N_KV * HEAD_DIM
    n_gate = 3 * N_HEADS
    offs = np.cumsum([0, ssm_w, attn_w] + [kv_w] * 6 + [n_gate, 2 * dm])

    wu = w_in[:, offs[0]:offs[1]].astype(BF16)
    wq = w_in[:, offs[1]:offs[2]] * (HEAD_DIM ** -0.5 * LOG2E)
    wa = jnp.concatenate([wq, w_in[:, offs[2]:offs[8]]], axis=1).astype(BF16)
    wg = jnp.pad(w_in[:, offs[8]:offs[9]], ((0, 0), (0, LANES - n_gate))).astype(BF16)
    wm = w_in[:, offs[9]:offs[10]].astype(BF16)
    u2d, act, gates, mg = _input_projection(x, sh1, sc1, wu, wa, wg, wm)

    nq = seq // Q_BLOCK
    n_blk = seq // SEL_LEN
    nch = seq // CMP_STRIDE

    def piece(i):
        return act[:, :, attn_w + i * kv_w: attn_w + (i + 1) * kv_w]

    def compress(raw, pos, w1, w2):
        xc = raw.reshape(bsz, nch, CMP_STRIDE, N_KV, HEAD_DIM).transpose(0, 3, 1, 2, 4).reshape(bsz, N_KV, nch, CMP_STRIDE * HEAD_DIM)
        half = CMP_STRIDE * HEAD_DIM
        w1cat = jnp.concatenate([w1[:half], w1[half:]], axis=1).astype(BF16)
        posb = jnp.dot(pos.reshape(1, -1), w1, precision=lax.Precision.HIGHEST)
        return _compress(xc, w1cat, posb, w2.astype(BF16))

    kc = compress(piece(0), cmp_pos[0], cmp_w1[0], cmp_w2[0])
    vc = compress(piece(1), cmp_pos[1], cmp_w1[1], cmp_w2[1])
    pad = ((0, 0), (0, 0), (PAD_CMP, _padded_cmp_rows(seq) - nch - PAD_CMP), (0, 0))
    kcp = jnp.pad(kc, pad).astype(BF16)
    vcpt = jnp.swapaxes(jnp.pad(vc, pad), 2, 3).astype(BF16)

    def heads(a):
        return a.reshape(bsz, seq, N_KV, HEAD_DIM).transpose(0, 2, 1, 3)

    nst = nq // NSA_TILES
    q = act[:, :, :attn_w].reshape(bsz, nst, NSA_TILES, Q_BLOCK, N_KV, GQA, HEAD_DIM)
    qt = q.transpose(0, 4, 1, 6, 2, 5, 3).reshape(bsz, N_KV, nst, HEAD_DIM, COLS)
    onehot = (np.arange(seq)[:, None] // SEL_LEN == np.arange(n_blk)[None, :]).astype(np.float32) * -MASK_BIG
    kaug = jnp.concatenate([heads(piece(2)), jnp.broadcast_to(jnp.asarray(onehot, BF16), (bsz, N_KV, seq, n_blk))], axis=3)
    ones_rows = np.zeros((1, 1, 2 * SUBLANES, seq), np.float32)
    ones_rows[:, :, 0] = 1.0

    def values_t(a):
        return jnp.concatenate([jnp.swapaxes(heads(a), 2, 3),
                                jnp.broadcast_to(jnp.asarray(ones_rows, BF16), (bsz, N_KV, 2 * SUBLANES, seq))], axis=2)

    vst = values_t(piece(3))
    kw = heads(piece(4))
    vwt = values_t(piece(5))
    gt = gates[:, :, :n_gate].reshape(bsz, nst, NSA_TILES, Q_BLOCK, N_KV, GQA, 3)
    gt = gt.transpose(0, 4, 1, 6, 2, 5, 3).reshape(bsz, N_KV, nst, 3, COLS)
    gt = jnp.pad(gt, ((0, 0), (0, 0), (0, 0), (0, SUBLANES - 3), (0, 0)))
    nb, cb = _bias_tiles(rel_bias)
    ot = _nsa(qt, kaug, vst, kw, vwt, kcp, vcpt, _overlap_t(seq), nb, cb, gt)
    ot = ot.reshape(bsz, N_KV, nst, HEAD_DIM, NSA_TILES, GQA, Q_BLOCK)
    y_nsa = ot.transpose(0, 2, 4, 6, 1, 5, 3).reshape(bsz, seq, attn_w)

    wb, a, wc = _s5_params(lam_re, lam_im, log_step, b_re, b_im, c_re, c_im)
    z2 = _s5(u2d.reshape(seq * bsz, ssm_w), wb, a, wc, d_skip.reshape(1, ssm_w).astype(F32), bsz)
    z2d = z2.reshape(seq, bsz * ssm_w)

    base, h2 = _mix_out(z2d, y_nsa, mg, x, g1, sh2, sc2, g2, ln1_g.reshape(1, dm), ln1_b.reshape(1, dm),
                        glu_w.astype(BF16), w_out.astype(BF16), sg.astype(BF16), su.astype(BF16), sd.astype(BF16), alpha)
    h2 = h2.reshape(n_tok, dm // 2)

    n_exp = router_w.shape[1]
    rt = min(ROUTER_TILE, n_tok)
    tri = jnp.asarray(np.triu(np.ones((rt, rt), np.float32), 1), BF16)
    top_e, top_w, rank, counts = _router(h2, router_w.T.astype(BF16), router_bias.reshape(n_exp, 1).astype(F32), tri)
    counts = counts[:, 0].astype(jnp.int32)
    padded = (counts + EXPERT_TILE - 1) // EXPERT_TILE * EXPERT_TILE
    pad_end = jnp.cumsum(padded)
    pad_start = pad_end - padded
    dest = _dest_rows(pad_start.astype(jnp.int32), top_e, rank)
    n_rows = n_tok * TOP_K + n_exp * EXPERT_TILE
    n_tiles = n_rows // EXPERT_TILE
    tile_exp = jnp.minimum(jnp.searchsorted(pad_end, jnp.arange(n_tiles) * EXPERT_TILE, side='right'), n_exp - 1).astype(jnp.int32)
    n_used = (pad_end[-1] // EXPERT_TILE).astype(jnp.int32).reshape(1)
    mt = min(MOVE_TILE, seq)
    dest3 = dest.reshape(TOP_K, n_tok // mt, mt).transpose(1, 0, 2)
    xs = _dispatch(dest3, h2, n_rows)
    ys = _experts(tile_exp, n_used, xs, e_gate.astype(BF16), e_up.astype(BF16), e_down.astype(BF16))
    return _final(dest3, top_w.T, base, g2, ln2_g.reshape(1, dm), ln2_b.reshape(1, dm), ys)


def kernel(x, c, ada_w, ada_b, w_in, ssm_lambda_re, ssm_lambda_im, ssm_log_step, ssm_b_re, ssm_b_im, ssm_c_re, ssm_c_im, ssm_d, ssm_glu_w, cmp_pos, cmp_w1, cmp_w2, rel_bias, w_out, ln1_g, ln1_b, router_w, router_bias, exp_w_gate, exp_w_up, exp_w_down, sh_w_gate, sh_w_up, sh_w_down, ln2_g, ln2_b):
    depth = ada_w.shape[0]
    alpha = (2 * depth) ** 0.25
    for l in range(depth):
        mod = _modulation(c, ada_w[l], ada_b[l])
        x = _layer(x, mod, w_in[l], ssm_lambda_re[l], ssm_lambda_im[l], ssm_log_step[l], ssm_b_re[l], ssm_b_im[l],
                   ssm_c_re[l], ssm_c_im[l], ssm_d[l], ssm_glu_w[l], cmp_pos[l], cmp_w1[l], cmp_w2[l], rel_bias,
                   w_out[l], ln1_g[l], ln1_b[l], router_w[l], router_bias[l], exp_w_gate[l], exp_w_up[l],
                   exp_w_down[l], sh_w_gate[l], sh_w_up[l], sh_w_down[l], ln2_g[l], ln2_b[l], alpha)
    return x
```

```python
import functools
import math

import numpy as np
import jax
import jax.numpy as jnp
from jax import lax
from jax.experimental import pallas as pl
from jax.experimental.pallas import tpu as pltpu

F32 = jnp.float32
BF16 = jnp.bfloat16

SSM_GROUP = 16
SSM_STATE = 64
N_HEADS = 16
HEAD_DIM = 64
N_KV = 4
GQA = N_HEADS // N_KV
CMP_LEN = 32
CMP_STRIDE = 16
SEL_LEN = 64
SEL_TOP = 16
WINDOW = 512
Q_BLOCK = 64
FORCE_BONUS = 1.0e4
REL_BUCKETS = 32
REL_MAX_DIST = 128
TOP_K = 8
ROUTED_SCALE = 2.5
LN_EPS = 1e-5

LANES = 128
SUBLANES = 8
VMEM_LIMIT_BYTES = 56 * 1024 * 1024
ROW_TILE = 512
ROUTER_TILE = 256
EXPERT_TILE = 256
MOVE_TILE = 256
DEST_TILE = 2048
SCAN_CHUNK = 128
SCAN_LANES = 512
MASK_BIG = 32768.0
NEG = -1.0e30
LOG2E = math.log2(math.e)
WIN_BLOCKS = WINDOW // SEL_LEN
ROWS = GQA * Q_BLOCK
NSA_TILES = 4
COLS = NSA_TILES * ROWS
CHUNK_BLOCKS = 8
CHUNK_KEYS = CHUNK_BLOCKS * SEL_LEN
PAD_CMP = 8
BAND_ROWS = 24


def _cparams(sem):
    return pltpu.CompilerParams(dimension_semantics=sem, vmem_limit_bytes=VMEM_LIMIT_BYTES)


def _pack_rows(x):
    n = x.shape[1] // 2
    xb = x.astype(jnp.bfloat16).astype(F32)
    lo = lax.shift_right_logical(lax.bitcast_convert_type(xb[:, :n], jnp.uint32), jnp.uint32(16))
    return lax.bitcast_convert_type(xb[:, n:], jnp.uint32) | lo


def _unpack_rows(w):
    lo = lax.bitcast_convert_type(lax.shift_left(w, jnp.uint32(16)), F32)
    hi = lax.bitcast_convert_type(w & jnp.uint32(0xFFFF0000), F32)
    return lo, hi


def _norm_rows(x):
    mu = jnp.mean(x, axis=-1, keepdims=True)
    xc = x - mu
    var = jnp.mean(xc * xc, axis=-1, keepdims=True)
    return xc * lax.rsqrt(var + LN_EPS)


def _mod_kernel(c_ref, w_ref, b_ref, o_ref):
    cond = jax.nn.silu(c_ref[...])
    o_ref[...] = jnp.dot(cond.astype(BF16), w_ref[...].astype(BF16), preferred_element_type=F32) + b_ref[...]


def _modulation(c, ada_w, ada_b):
    bsz, dm = c.shape
    n = ada_w.shape[1]
    tn = dm
    return pl.pallas_call(
        _mod_kernel,
        grid=(n // tn,),
        in_specs=[pl.BlockSpec((bsz, dm), lambda j: (0, 0)),
                  pl.BlockSpec((dm, tn), lambda j: (0, j)),
                  pl.BlockSpec((1, tn), lambda j: (0, j))],
        out_specs=pl.BlockSpec((bsz, tn), lambda j: (0, j)),
        out_shape=jax.ShapeDtypeStruct((bsz, n), F32),
        compiler_params=_cparams(("arbitrary",)),
        name="ada_modulation",
    )(c, ada_w, ada_b.reshape(1, n))


def _inproj_kernel(x_ref, sh_ref, sc_ref, wu_ref, wa_ref, wg_ref, wm_ref, wks_ref, wkw_ref, wvs_ref, wvw_ref,
                   u_ref, a_ref, g_ref, m_ref, ks_ref, kw_ref, vs_ref, vw_ref, *, aug_w):
    h = _norm_rows(x_ref[0]) * (1.0 + sc_ref[0]) + sh_ref[0]
    hb = h.astype(BF16)
    tm = hb.shape[0]
    u_ref[...] = jnp.dot(hb, wu_ref[...], preferred_element_type=F32)
    a_ref[0] = jnp.dot(hb, wa_ref[...], preferred_element_type=F32).astype(BF16)
    g_ref[0] = jax.nn.sigmoid(jnp.dot(hb, wg_ref[...], preferred_element_type=F32))
    m_ref[0] = jax.nn.sigmoid(jnp.dot(hb, wm_ref[...], preferred_element_type=F32)).astype(BF16)
    ks = jnp.dot(hb, wks_ref[...], preferred_element_type=F32)
    assert aug_w & (aug_w - 1) == 0
    col = (lax.broadcasted_iota(jnp.int32, ks.shape, 1) & (aug_w - 1)) - 2 * HEAD_DIM
    blk = jnp.right_shift(pl.program_id(1) * tm + lax.broadcasted_iota(jnp.int32, ks.shape, 0), SEL_LEN.bit_length() - 1)
    ks_ref[0] = jnp.where(col == blk, -MASK_BIG, ks).astype(BF16)
    kw_ref[0] = jnp.dot(hb, wkw_ref[...], preferred_element_type=F32).astype(BF16)
    rows = lax.broadcasted_iota(jnp.int32, (2 * SUBLANES, tm), 0)
    ones_rows = jnp.where(rows == 0, 1.0, 0.0).astype(BF16)
    for w_ref, v_ref in ((wvs_ref, vs_ref), (wvw_ref, vw_ref)):
        vt = lax.dot_general(w_ref[...], hb, (((1,), (1,)), ((), ())), preferred_element_type=F32).astype(BF16)
        for g in range(N_KV):
            v_ref[0, g] = jnp.concatenate([vt[g * HEAD_DIM:(g + 1) * HEAD_DIM], ones_rows], axis=0)


def _input_projection(x, sh1, sc1, wu, wa, wg, wm, wks, wkw, wvs_t, wvw_t):
    bsz, seq, dm = x.shape
    tm = min(ROW_TILE, seq)
    nu, na, ng, nm = wu.shape[1], wa.shape[1], wg.shape[1], wm.shape[1]
    full = lambda w: pl.BlockSpec(w.shape, lambda b, i: (0, 0))
    vec = pl.BlockSpec((1, 1, dm), lambda b, i: (b, 0, 0))
    rows_out = lambda n: pl.BlockSpec((1, tm, n), lambda b, i: (b, i, 0))
    vt_rows = HEAD_DIM + 2 * SUBLANES
    vt_out = pl.BlockSpec((1, N_KV, vt_rows, tm), lambda b, i: (b, 0, 0, i))
    return pl.pallas_call(
        functools.partial(_inproj_kernel, aug_w=wks.shape[1] // N_KV),
        grid=(bsz, seq // tm),
        in_specs=[pl.BlockSpec((1, tm, dm), lambda b, i: (b, i, 0)), vec, vec,
                  full(wu), full(wa), full(wg), full(wm), full(wks), full(wkw), full(wvs_t), full(wvw_t)],
        out_specs=[pl.BlockSpec((tm, nu), lambda b, i: (i, b)), rows_out(na), rows_out(ng), rows_out(nm),
                   rows_out(wks.shape[1]), rows_out(wkw.shape[1]), vt_out, vt_out],
        out_shape=[jax.ShapeDtypeStruct((seq, bsz * nu), F32),
                   jax.ShapeDtypeStruct((bsz, seq, na), BF16),
                   jax.ShapeDtypeStruct((bsz, seq, ng), F32),
                   jax.ShapeDtypeStruct((bsz, seq, nm), BF16),
                   jax.ShapeDtypeStruct((bsz, seq, wks.shape[1]), BF16),
                   jax.ShapeDtypeStruct((bsz, seq, wkw.shape[1]), BF16),
                   jax.ShapeDtypeStruct((bsz, N_KV, vt_rows, seq), BF16),
                   jax.ShapeDtypeStruct((bsz, N_KV, vt_rows, seq), BF16)],
        compiler_params=_cparams(("arbitrary", "arbitrary")),
        name="adaln_input_projection",
    )(x, sh1, sc1, wu, wa, wg, wm, wks, wkw, wvs_t, wvw_t)


def _compress_kernel(x_ref, w1_ref, pb_ref, w2_ref, o_ref, *, n_cmp):
    hid = w2_ref.shape[0]
    p = jnp.dot(x_ref[0, 0], w1_ref[...], preferred_element_type=F32)
    nrow = p.shape[0]
    nxt = pltpu.roll(p[:, hid:], nrow - 1, 0)
    hidv = jax.nn.gelu(p[:, :hid] + nxt + pb_ref[...])
    out = jnp.dot(hidv.astype(BF16), w2_ref[...], preferred_element_type=F32)
    rows = lax.broadcasted_iota(jnp.int32, out.shape, 0)
    o_ref[0, 0] = jnp.where(rows < n_cmp, out, 0.0)


def _compress(xc, w1cat, posb, w2):
    bsz, nkv, nch, kdim = xc.shape
    hid2 = w1cat.shape[1]
    return pl.pallas_call(
        functools.partial(_compress_kernel, n_cmp=nch - 1),
        grid=(bsz, nkv),
        in_specs=[pl.BlockSpec((1, 1, nch, kdim), lambda b, g: (b, g, 0, 0)),
                  pl.BlockSpec((kdim, hid2), lambda b, g: (0, 0)),
                  pl.BlockSpec((1, hid2 // 2), lambda b, g: (0, 0)),
                  pl.BlockSpec((hid2 // 2, HEAD_DIM), lambda b, g: (0, 0))],
        out_specs=pl.BlockSpec((1, 1, nch, HEAD_DIM), lambda b, g: (b, g, 0, 0)),
        out_shape=jax.ShapeDtypeStruct((bsz, nkv, nch, HEAD_DIM), F32),
        compiler_params=_cparams(("arbitrary", "arbitrary")),
        name="kv_compress",
    )(xc, w1cat, posb, w2)


def _nsa_kernel(qt_ref, kaug_ref, vst_ref, kw_ref, vwt_ref, kcp_ref, vcpt_ref, ovt_ref, nb_ref, cb_ref,
                g_ref, o_ref, s_ref, c_ref, fa_ref, fb_ref, *, n_blk, n_cmp, n_sel):
    q0 = pl.program_id(2) * NSA_TILES
    qt = qt_ref[0, 0, 0]
    col_tile = lambda n: slice(n * ROWS, (n + 1) * ROWS)

    s_ref[...] = jnp.dot(kcp_ref[0, 0], qt, preferred_element_type=F32)
    for n in range(NSA_TILES):
        band = pl.multiple_of(SUBLANES * ((q0 + n) // 2), SUBLANES)
        s_ref[pl.ds(band, BAND_ROWS), col_tile(n)] += cb_ref[0, n % 2]
    s = s_ref[...]
    rho = lax.broadcasted_iota(jnp.int32, s.shape, 0)
    col = lax.broadcasted_iota(jnp.int32, s.shape, 1)
    tok = col & (Q_BLOCK - 1)
    qblk = q0 + jnp.right_shift(col, ROWS.bit_length() - 1)
    first_visible = CMP_STRIDE * (rho - PAD_CMP) + (CMP_LEN - 1) - Q_BLOCK * qblk
    vis = (rho >= PAD_CMP) & (rho < PAD_CMP + n_cmp) & (first_visible <= tok)
    s = jnp.where(vis, s, -jnp.inf)
    mx = jnp.max(s, axis=0, keepdims=True)
    mx = jnp.where(mx == -jnp.inf, 0.0, mx)
    e = jnp.exp2(s - mx)
    p_cmp = (e / jnp.maximum(jnp.sum(e, axis=0, keepdims=True), 1e-30)).astype(BF16)
    both = jnp.dot(jnp.concatenate([vcpt_ref[0, 0], ovt_ref[...]], axis=0), p_cmp, preferred_element_type=F32)
    o_cmp = both[:HEAD_DIM]

    imp4 = both[HEAD_DIM:]
    sums = []
    for n in range(NSA_TILES):
        two = imp4[:, n * ROWS:n * ROWS + LANES] + imp4[:, n * ROWS + LANES:(n + 1) * ROWS]
        sums.append(two + pltpu.roll(two, Q_BLOCK, 1))
    low = lax.broadcasted_iota(jnp.int32, sums[0].shape, 1) < Q_BLOCK
    imp = jnp.concatenate([jnp.where(low, sums[n], sums[n + 1]) for n in range(0, NSA_TILES, 2)], axis=1)
    blk = lax.broadcasted_iota(jnp.int32, imp.shape, 0)
    cur = q0 + jnp.right_shift(lax.broadcasted_iota(jnp.int32, imp.shape, 1), Q_BLOCK.bit_length() - 1)
    blkf = blk.astype(F32)
    forced = (blk == 0) | (blk == cur) | (blk == cur - 1)
    score = jnp.where(blk <= cur, imp + jnp.where(forced, FORCE_BONUS, 0.0), -jnp.inf)
    notsel = jnp.ones(imp.shape, F32)
    for _ in range(n_sel):
        best = jnp.max(score, axis=0, keepdims=True)
        first = jnp.min(jnp.where(score == best, blkf, float(n_blk)), axis=0, keepdims=True)
        pick = blkf == first
        notsel = jnp.where(pick, 0.0, notsel)
        score = jnp.where(pick, -jnp.inf, score)
    halves = []
    for n in range(0, NSA_TILES, 2):
        pair = notsel[:, (n // 2) * LANES:(n // 2 + 1) * LANES]
        swapped = pltpu.roll(pair, Q_BLOCK, 1)
        halves += [jnp.where(low, pair, swapped), jnp.where(low, swapped, pair)]
    notsel = jnp.concatenate([h for h in halves for _ in (0, 1)], axis=1).astype(BF16)
    qwin = jnp.concatenate([qt, jnp.zeros_like(qt)], axis=0)
    tail = kaug_ref.shape[2] - 2 * HEAD_DIM - n_blk
    qaug = jnp.concatenate([qwin, notsel] + ([jnp.zeros((tail, COLS), notsel.dtype)] if tail else []), axis=0)

    def update(carry, sc, vt, top=None):
        m, acc = carry
        m_new = jnp.maximum(m, jnp.max(sc, axis=0, keepdims=True) if top is None else top)
        p = jnp.exp2((sc - m_new).astype(BF16))
        return m_new, jnp.exp2(m - m_new) * acc + jnp.dot(vt, p, preferred_element_type=F32)

    def chunk_keys(c):
        return pl.ds(pl.multiple_of(c * CHUNK_KEYS, CHUNK_KEYS), CHUNK_KEYS)

    def far_scores(c, buf_ref):
        sc = jnp.dot(kaug_ref[0, chunk_keys(c), :], qaug, preferred_element_type=F32)
        buf_ref[...] = sc
        return jnp.max(sc, axis=0, keepdims=True)

    def far_pair(i, carry):
        m, acc, top_a = carry
        top_b = far_scores(2 * i + 1, fb_ref)
        m, acc = update((m, acc), fa_ref[...], vst_ref[0, 0, :, chunk_keys(2 * i)], top_a)
        top_a = far_scores(2 * i + 2, fa_ref)
        m, acc = update((m, acc), fb_ref[...], vst_ref[0, 0, :, chunk_keys(2 * i + 1)], top_b)
        return m, acc, top_a

    def far_last(_, carry, n_far):
        m, acc, top_a = carry
        return update((m, acc), fa_ref[...], vst_ref[0, 0, :, chunk_keys(n_far - 1)], top_a) + (top_a,)

    def near_chunk(k_ref, vt_ref, qmat, blk0, tile_of, carry):
        keys = pl.ds(pl.multiple_of(blk0 * SEL_LEN, NSA_TILES * SEL_LEN), CHUNK_KEYS)
        c_ref[...] = jnp.dot(k_ref[0, keys, :], qmat, preferred_element_type=F32)
        for o in range(CHUNK_BLOCKS):
            for n in range(NSA_TILES):
                c_ref[o * SEL_LEN:(o + 1) * SEL_LEN, col_tile(n)] += nb_ref[0, tile_of(blk0 + o, q0 + n)]
        return update(carry, c_ref[...], vt_ref[0, 0, :, keys])

    def sel_tile(j, qi):
        d = qi - j
        return jnp.where(d < 0, 4, jnp.minimum(d, 3))

    def win_tile(j, qi):
        d = qi - j
        return jnp.where((d < 0) | (d > WIN_BLOCKS), 4, jnp.where(d == WIN_BLOCKS, 5, jnp.minimum(d, 3)))

    init = (jnp.full((1, COLS), NEG, F32), jnp.zeros((vst_ref.shape[2], COLS), F32))
    n_far = jnp.maximum(q0 - 2, 0) // CHUNK_BLOCKS
    carry = lax.fori_loop(0, n_far // 2, far_pair, init + (far_scores(0, fa_ref),))
    carry = lax.fori_loop(0, n_far % 2, functools.partial(far_last, n_far=n_far), carry)
    _, acc = lax.fori_loop(
        n_far, (q0 + NSA_TILES - 1) // CHUNK_BLOCKS + 1,
        lambda c, cr: near_chunk(kaug_ref, vst_ref, qaug, c * CHUNK_BLOCKS, sel_tile, cr), carry[:2])
    o_slc = acc[:HEAD_DIM] / acc[HEAD_DIM:HEAD_DIM + 1]
    win0 = jnp.maximum(q0 - (WIN_BLOCKS + NSA_TILES), 0)
    _, acc = lax.fori_loop(
        0, 2, lambda c, cr: near_chunk(kw_ref, vwt_ref, qwin, win0 + c * CHUNK_BLOCKS, win_tile, cr), init)
    o_win = acc[:HEAD_DIM] / acc[HEAD_DIM:HEAD_DIM + 1]
    g = g_ref[0, 0, 0]
    o_ref[0, 0, 0] = (g[0:1] * o_cmp + g[1:2] * o_slc + g[2:3] * o_win).astype(BF16)


def _nsa(qt, kaug, vst, kw, vwt, kcp, vcpt, ovt, nb, cb, gt):
    bsz, nkv, nsteps, dh, cols = qt.shape
    seq = kw.shape[1]
    n_blk = seq // SEL_LEN
    ncp = kcp.shape[2]
    n_cmp = (seq - CMP_LEN) // CMP_STRIDE + 1
    assert n_blk % CHUNK_BLOCKS == 0 and n_blk >= 2 * CHUNK_BLOCKS and WIN_BLOCKS + 2 * NSA_TILES == 2 * CHUNK_BLOCKS
    per_bg = lambda shape: pl.BlockSpec((1, 1) + shape, lambda b, g, i: (b, g, 0, 0))
    per_step = lambda shape: pl.BlockSpec((1, 1, 1) + shape, lambda b, g, i: (b, g, i, 0, 0))
    head_cols = lambda a: pl.BlockSpec((1, seq, a.shape[2] // nkv), lambda b, g, i: (b, 0, g))
    kern = functools.partial(_nsa_kernel, n_blk=n_blk, n_cmp=n_cmp, n_sel=min(SEL_TOP, n_blk))
    return pl.pallas_call(
        kern,
        grid=(bsz, nkv, nsteps),
        in_specs=[per_step((dh, cols)),
                  head_cols(kaug), per_bg((vst.shape[2], seq)), head_cols(kw), per_bg((vwt.shape[2], seq)),
                  per_bg((ncp, dh)), per_bg((dh, ncp)),
                  pl.BlockSpec(ovt.shape, lambda b, g, i: (0, 0)),
                  pl.BlockSpec((1,) + nb.shape[1:], lambda b, g, i: (g, 0, 0, 0)),
                  pl.BlockSpec((1,) + cb.shape[1:], lambda b, g, i: (g, 0, 0, 0)),
                  per_step((SUBLANES, cols))],
        out_specs=per_step((dh, cols)),
        out_shape=jax.ShapeDtypeStruct((bsz, nkv, nsteps, dh, cols), BF16),
        scratch_shapes=[pltpu.VMEM((ncp, cols), F32)] + [pltpu.VMEM((CHUNK_KEYS, cols), F32)] * 3,
        compiler_params=_cparams(("arbitrary", "arbitrary", "arbitrary")),
        name="nsa_attention",
    )(qt, kaug, vst, kw, vwt, kcp, vcpt, ovt, nb, cb, gt)


def _s5_kernel(u_ref, wb_ref, a_ref, wc_ref, d_ref, z_ref, xs_ref, st_ref, *, bsz, n_state):
    @pl.when(pl.program_id(0) == 0)
    def _():
        st_ref[...] = jnp.zeros_like(st_ref)

    u = u_ref[...]
    xs_ref[...] = jnp.dot(u.astype(BF16), wb_ref[...], preferred_element_type=F32)
    steps = u.shape[0] // bsz
    for c0 in range(0, n_state, SCAN_LANES):
        re = pl.ds(c0, SCAN_LANES)
        im = pl.ds(n_state + c0, SCAN_LANES)
        ar = jnp.broadcast_to(a_ref[0:1, re], (bsz, SCAN_LANES))
        ai = jnp.broadcast_to(a_ref[0:1, im], (bsz, SCAN_LANES))

        def step(t, carry):
            xr, xi = carry
            rows = pl.ds(pl.multiple_of(t * bsz, bsz), bsz)
            nr = ar * xr - ai * xi + xs_ref[rows, re]
            ni = ar * xi + ai * xr + xs_ref[rows, im]
            xs_ref[rows, re] = nr
            xs_ref[rows, im] = ni
            return nr, ni

        xr, xi = lax.fori_loop(0, steps, step, (st_ref[:, re], st_ref[:, im]), unroll=8)
        st_ref[:, re] = xr
        st_ref[:, im] = xi
    y = jnp.dot(xs_ref[...].astype(BF16), wc_ref[...], preferred_element_type=F32) + d_ref[...] * u
    z_ref[...] = jax.nn.gelu(y).astype(BF16)


def _s5(u2, wb, a, wc, dsk, bsz):
    rows, width = u2.shape
    seq = rows // bsz
    chunk = min(SCAN_CHUNK, seq)
    n_state2 = wb.shape[1]
    return pl.pallas_call(
        functools.partial(_s5_kernel, bsz=bsz, n_state=n_state2 // 2),
        grid=(seq // chunk,),
        in_specs=[pl.BlockSpec((chunk * bsz, width), lambda i: (i, 0)),
                  pl.BlockSpec(wb.shape, lambda i: (0, 0)),
                  pl.BlockSpec(a.shape, lambda i: (0, 0)),
                  pl.BlockSpec(wc.shape, lambda i: (0, 0)),
                  pl.BlockSpec(dsk.shape, lambda i: (0, 0))],
        out_specs=pl.BlockSpec((chunk * bsz, width), lambda i: (i, 0)),
        out_shape=jax.ShapeDtypeStruct((rows, width), BF16),
        scratch_shapes=[pltpu.VMEM((chunk * bsz, n_state2), F32), pltpu.VMEM((bsz, n_state2), F32)],
        compiler_params=_cparams(("arbitrary",)),
        name="s5_scan",
    )(u2, wb, a, wc, dsk)


def _mixout_kernel(z_ref, yn_ref, mg_ref, x_ref, g1_ref, sh2_ref, sc2_ref, g2_ref, lg_ref, lb_ref,
                   glu_ref, wo_ref, sg_ref, su_ref, sd_ref, base_ref, h_ref, *, alpha):
    dm = x_ref.shape[2]
    glu = jnp.dot(z_ref[...], glu_ref[...], preferred_element_type=F32)
    y_ssm = glu[:, :dm] * jax.nn.sigmoid(glu[:, dm:])
    mg = mg_ref[0].astype(F32)
    merged = mg[:, :dm] * y_ssm + mg[:, dm:] * yn_ref[0].astype(F32)
    y = jnp.dot(merged.astype(BF16), wo_ref[...], preferred_element_type=F32)
    x1 = _norm_rows(alpha * x_ref[0] + g1_ref[0] * y) * lg_ref[...] + lb_ref[...]
    hf = _norm_rows(x1) * (1.0 + sc2_ref[0]) + sh2_ref[0]
    h_ref[0] = _pack_rows(hf)
    h = hf.astype(BF16)
    hs =jax.nn.silu(jnp.dot(h, sg_ref[...], preferred_element_type=F32)) * jnp.dot(h, su_ref[...], preferred_element_type=F32)
    shared = jnp.dot(hs.astype(BF16), sd_ref[...], preferred_element_type=F32)
    base_ref[0] = alpha * x1 + g2_ref[0] * shared


def _mix_out(z2d, y_nsa, mg, x, g1, sh2, sc2, g2, ln_g, ln_b, glu_w, w_out, sg, su, sd, alpha):
    bsz, seq, dm = x.shape
    tm = min(ROW_TILE, seq)
    width = z2d.shape[1] // bsz
    vec = pl.BlockSpec((1, 1, dm), lambda b, i: (b, 0, 0))
    row = pl.BlockSpec((1, dm), lambda b, i: (0, 0))
    full = lambda w: pl.BlockSpec(w.shape, lambda b, i: (0, 0))
    tile = lambda n: pl.BlockSpec((1, tm, n), lambda b, i: (b, i, 0))
    return pl.pallas_call(
        functools.partial(_mixout_kernel, alpha=alpha),
        grid=(bsz, seq // tm),
        in_specs=[pl.BlockSpec((tm, width), lambda b, i: (i, b)), tile(dm), tile(2 * dm), tile(dm),
                  vec, vec, vec, vec, row, row, full(glu_w), full(w_out), full(sg), full(su), full(sd)],
        out_specs=[tile(dm), tile(dm // 2)],
        out_shape=[jax.ShapeDtypeStruct((bsz, seq, dm), F32), jax.ShapeDtypeStruct((bsz, seq, dm // 2), jnp.uint32)],
        compiler_params=_cparams(("arbitrary", "arbitrary")),
        name="merge_outproj_ln_shared",
    )(z2d, y_nsa, mg, x, g1, sh2, sc2, g2, ln_g, ln_b, glu_w, w_out, sg, su, sd)


def _router_kernel(h_ref, rwt_ref, rb_ref, tri_ref, e_ref, w_ref, p_ref, cnt_ref):
    @pl.when(pl.program_id(0) == 0)
    def _():
        cnt_ref[...] = jnp.zeros_like(cnt_ref)

    h = jnp.concatenate(_unpack_rows(h_ref[...]), axis=1).astype(BF16)
    logits = lax.dot_general(rwt_ref[...], h, (((1,), (1,)), ((), ())), preferred_element_type=F32)
    scores = jax.nn.sigmoid(logits)
    cur = scores + rb_ref[...]
    n_exp = scores.shape[0]
    eid = lax.broadcasted_iota(jnp.int32, scores.shape, 0).astype(F32)
    chosen = jnp.zeros(scores.shape, F32)
    ids, vals = [], []
    for _ in range(TOP_K):
        best = jnp.max(cur, axis=0, keepdims=True)
        first = jnp.min(jnp.where(cur == best, eid, float(n_exp)), axis=0, keepdims=True)
        pick = eid == first
        ids.append(first)
        vals.append(jnp.sum(jnp.where(pick, scores, 0.0), axis=0, keepdims=True))
        chosen = jnp.where(pick, 1.0, chosen)
        cur = jnp.where(pick, -jnp.inf, cur)
    top_s = jnp.concatenate(vals, axis=0)
    w_ref[...] = top_s / jnp.sum(top_s, axis=0, keepdims=True) * ROUTED_SCALE
    top_e = jnp.concatenate(ids, axis=0)
    e_ref[...] = top_e.astype(jnp.int32)
    before = jnp.dot(chosen.astype(BF16), tri_ref[...], preferred_element_type=F32) + cnt_ref[...]
    ranks = [jnp.sum(jnp.where(eid == ids[k], before, 0.0), axis=0, keepdims=True) for k in range(TOP_K)]
    p_ref[...] = jnp.concatenate(ranks, axis=0).astype(jnp.int32)
    cnt_ref[...] += jnp.sum(chosen, axis=1, keepdims=True)


def _router(h2, rwt, rb, tri):
    n_tok, words = h2.shape
    n_exp, dm = rwt.shape
    tm = tri.shape[0]
    kt = pl.BlockSpec((TOP_K, tm), lambda i: (0, i))
    return pl.pallas_call(
        _router_kernel,
        grid=(n_tok // tm,),
        in_specs=[pl.BlockSpec((tm, words), lambda i: (i, 0)),
                  pl.BlockSpec((n_exp, dm), lambda i: (0, 0)),
                  pl.BlockSpec((n_exp, 1), lambda i: (0, 0)),
                  pl.BlockSpec((tm, tm), lambda i: (0, 0))],
        out_specs=[kt, kt, kt, pl.BlockSpec((n_exp, 1), lambda i: (0, 0))],
        out_shape=[jax.ShapeDtypeStruct((TOP_K, n_tok), jnp.int32),
                   jax.ShapeDtypeStruct((TOP_K, n_tok), F32),
                   jax.ShapeDtypeStruct((TOP_K, n_tok), jnp.int32),
                   jax.ShapeDtypeStruct((n_exp, 1), F32)],
        compiler_params=_cparams(("arbitrary",)),
        name="router_topk_rank",
    )(h2, rwt, rb, tri)


def _dest_kernel(start_ref, e_ref, r_ref, o_ref):
    e = e_ref[...]
    start = lax.fori_loop(0, start_ref.shape[0], lambda j, acc: jnp.where(e == j, start_ref[j], acc),
                          jnp.zeros(e.shape, jnp.int32))
    o_ref[...] = start + r_ref[...]


def _dest_rows(pad_start, top_e, rank):
    n_tok = top_e.shape[1]
    tm = min(DEST_TILE, n_tok)
    blk = pl.BlockSpec((TOP_K, tm), lambda i, ps: (0, i))
    return pl.pallas_call(
        _dest_kernel,
        grid_spec=pltpu.PrefetchScalarGridSpec(num_scalar_prefetch=1, grid=(n_tok // tm,), in_specs=[blk, blk], out_specs=blk),
        out_shape=jax.ShapeDtypeStruct(top_e.shape, jnp.int32),
        compiler_params=_cparams(("arbitrary",)),
        name="moe_dest_rows",
    )(pad_start, top_e, rank)


def _row_copy(src_ref, src_row, dst_ref, dst_row, sem):
    return pltpu.make_async_copy(src_ref.at[pl.ds(src_row, 1), :], dst_ref.at[pl.ds(dst_row, 1), :], sem)


def _dispatch_kernel(dest_ref, h_ref, init_ref, xs_ref, sem):
    del init_ref
    tm = h_ref.shape[0]

    def issue(t, carry):
        for k in range(TOP_K):
            _row_copy(h_ref, t, xs_ref, dest_ref[0, k, t], sem).start()
        return carry

    lax.fori_loop(0, tm, issue, 0)
    pltpu.make_async_copy(xs_ref.at[pl.ds(0, TOP_K * tm), :], xs_ref.at[pl.ds(0, TOP_K * tm), :], sem).wait()


def _dispatch(dest3, h2, n_rows):
    n_tok, words = h2.shape
    tm = dest3.shape[2]
    return pl.pallas_call(
        _dispatch_kernel,
        grid=(n_tok // tm,),
        in_specs=[pl.BlockSpec((1, TOP_K, tm), lambda i: (i, 0, 0), memory_space=pltpu.SMEM),
                  pl.BlockSpec((tm, words), lambda i: (i, 0)),
                  pl.BlockSpec(memory_space=pl.ANY)],
        out_specs=pl.BlockSpec(memory_space=pl.ANY),
        out_shape=jax.ShapeDtypeStruct((n_rows, words), jnp.uint32),
        scratch_shapes=[pltpu.SemaphoreType.DMA(())],
        input_output_aliases={2: 0},
        compiler_params=_cparams(("arbitrary",)),
        name="moe_dispatch",
    )(dest3, h2, jnp.zeros((n_rows, words), jnp.uint32))


def _expert_kernel(te_ref, nu_ref, x_ref, wg_ref, wu_ref, wd_ref, y_ref):
    i = pl.program_id(0)

    @pl.when(i < nu_ref[0])
    def _():
        lo, hi = _unpack_rows(x_ref[...])
        lo, hi = lo.astype(BF16), hi.astype(BF16)
        half = lo.shape[1]

        def proj(w_ref):
            return (jnp.dot(lo, w_ref[0, :half, :].astype(BF16), preferred_element_type=F32)
                    + jnp.dot(hi, w_ref[0, half:, :].astype(BF16), preferred_element_type=F32))

        hmid = jax.nn.silu(proj(wg_ref)) * proj(wu_ref)
        y_ref[...] = _pack_rows(jnp.dot(hmid.astype(BF16), wd_ref[0].astype(BF16), preferred_element_type=F32))

    @pl.when(i >= nu_ref[0])
    def _():
        y_ref[...] = jnp.zeros_like(y_ref)


def _experts(tile_exp, n_used, xs, wg, wu, wd):
    n_rows, words = xs.shape
    dm, de = wg.shape[1], wg.shape[2]
    grid_spec = pltpu.PrefetchScalarGridSpec(
        num_scalar_prefetch=2,
        grid=(n_rows // EXPERT_TILE,),
        in_specs=[pl.BlockSpec((EXPERT_TILE, words), lambda i, te, nu: (i, 0)),
                  pl.BlockSpec((1, dm, de), lambda i, te, nu: (te[i], 0, 0)),
                  pl.BlockSpec((1, dm, de), lambda i, te, nu: (te[i], 0, 0)),
                  pl.BlockSpec((1, de, dm), lambda i, te, nu: (te[i], 0, 0))],
        out_specs=pl.BlockSpec((EXPERT_TILE, words), lambda i, te, nu: (i, 0)),
    )
    return pl.pallas_call(
        _expert_kernel,
        grid_spec=grid_spec,
        out_shape=jax.ShapeDtypeStruct((n_rows, words), jnp.uint32),
        compiler_params=_cparams(("arbitrary",)),
        name="expert_mlp",
    )(tile_exp, n_used, xs, wg, wu, wd)


def _final_kernel(dest_ref, w_ref, base_ref, g2_ref, lg_ref, lb_ref, ys_ref, o_ref, buf_ref, sem):
    tm = base_ref.shape[1]

    def issue(t, carry):
        for k in range(TOP_K):
            _row_copy(ys_ref, dest_ref[0, k, t], buf_ref.at[k], t, sem).start()
        return carry

    lax.fori_loop(0, tm, issue, 0)
    pltpu.make_async_copy(buf_ref, buf_ref, sem).wait()
    w = w_ref[...]
    lo_sum, hi_sum = jnp.zeros((tm, buf_ref.shape[2]), F32), jnp.zeros((tm, buf_ref.shape[2]), F32)
    for k in range(TOP_K):
        lo, hi = _unpack_rows(buf_ref[k])
        lo_sum += w[:, k:k + 1] * lo
        hi_sum += w[:, k:k + 1] * hi
    routed = jnp.concatenate([lo_sum, hi_sum], axis=1)
    o_ref[0] = _norm_rows(base_ref[0] + g2_ref[0] * routed) * lg_ref[...] + lb_ref[...]


def _final(dest3, w_tok, base, g2, ln_g, ln_b, ys):
    bsz, seq, dm = base.shape
    tm = dest3.shape[2]
    nt = seq // tm
    tile = pl.BlockSpec((1, tm, dm), lambda b, i: (b, i, 0))
    return pl.pallas_call(
        _final_kernel,
        grid=(bsz, nt),
        in_specs=[pl.BlockSpec((1, TOP_K, tm), lambda b, i: (b * nt + i, 0, 0), memory_space=pltpu.SMEM),
                  pl.BlockSpec((tm, TOP_K), lambda b, i: (b * nt + i, 0)),
                  tile, pl.BlockSpec((1, 1, dm), lambda b, i: (b, 0, 0)),
                  pl.BlockSpec((1, dm), lambda b, i: (0, 0)), pl.BlockSpec((1, dm), lambda b, i: (0, 0)),
                  pl.BlockSpec(memory_space=pl.ANY)],
        out_specs=tile,
        out_shape=jax.ShapeDtypeStruct((bsz, seq, dm), F32),
        scratch_shapes=[pltpu.VMEM((TOP_K, tm, ys.shape[1]), jnp.uint32), pltpu.SemaphoreType.DMA(())],
        compiler_params=_cparams(("arbitrary", "arbitrary")),
        name="combine_final_layernorm",
    )(dest3, w_tok, base, g2, ln_g, ln_b, ys)


def _rel_bucket(dist):
    dist = jnp.maximum(dist, 0)
    exact = REL_BUCKETS // 2
    log_ratio = jnp.log(jnp.maximum(dist, 1).astype(F32) / exact) / math.log(REL_MAX_DIST / exact)
    large = jnp.minimum(exact + (log_ratio * (REL_BUCKETS - exact)).astype(jnp.int32), REL_BUCKETS - 1)
    return jnp.where(dist < exact, dist, large)


def _bias_tiles(rel_bias):
    n_d = 4 * SEL_LEN
    vec = rel_bias[_rel_bucket(jnp.arange(n_d))].T
    far = rel_bias[REL_BUCKETS - 1]
    vec = ((vec - far[:, None]) * LOG2E).reshape(N_KV, GQA, n_d)
    tok = np.arange(Q_BLOCK)[None, :]
    key = np.arange(SEL_LEN)[:, None]

    def toeplitz(d):
        vals = vec[:, :, np.clip(d, 0, n_d - 1)]
        vals = jnp.where(jnp.asarray(d >= 0), vals, NEG)
        return jnp.transpose(vals, (0, 2, 1, 3)).reshape(N_KV, d.shape[0], ROWS)

    near = [toeplitz(delta + tok - key) for delta in (0, SEL_LEN, 2 * SEL_LEN)]
    zero = jnp.zeros((N_KV, SEL_LEN, ROWS), F32)
    edge = np.where(tok < key, 0.0, NEG).astype(np.float32)
    edge = jnp.broadcast_to(jnp.asarray(np.tile(edge, (1, GQA)))[None], (N_KV, SEL_LEN, ROWS))
    nb = jnp.stack(near + [zero, jnp.full_like(zero, NEG), edge], axis=1)
    w = np.arange(BAND_ROWS)[:, None]
    bands = []
    for ph in (0, 1):
        d = tok - CMP_STRIDE * (w - PAD_CMP - (Q_BLOCK // CMP_STRIDE) * ph) - (CMP_LEN - 1)
        vals = vec[:, :, np.clip(d, 0, n_d - 1)]
        vals = jnp.where(jnp.asarray(d >= 0), vals, 0.0)
        bands.append(jnp.transpose(vals, (0, 2, 1, 3)).reshape(N_KV, BAND_ROWS, ROWS))
    cb = jnp.stack(bands, axis=1)
    return nb.astype(F32), cb.astype(F32)


def _padded_cmp_rows(seq):
    return -(-(seq // CMP_STRIDE + 2 * PAD_CMP) // LANES) * LANES


def _overlap_t(seq):
    n_cmp = (seq - CMP_LEN) // CMP_STRIDE + 1
    n_blk = seq // SEL_LEN
    ncp = _padded_cmp_rows(seq)
    c_start = np.arange(n_cmp) * CMP_STRIDE
    c_end = c_start + CMP_LEN - 1
    blk = np.arange(n_blk)
    ov = ((c_start[:, None] < (blk[None, :] + 1) * SEL_LEN) & (c_end[:, None] >= blk[None, :] * SEL_LEN))
    out = np.zeros((n_blk, ncp), np.float32)
    out[:, PAD_CMP:PAD_CMP + n_cmp] = ov.T
    return jnp.asarray(out, BF16)


def _s5_params(lam_re, lam_im, log_step, b_re, b_im, c_re, c_im):
    lr, li = lam_re.astype(F32), lam_im.astype(F32)
    dt = jnp.exp(log_step.astype(F32))[:, None]
    mag = jnp.exp(lr * dt)
    ar, ai = mag * jnp.cos(li * dt), mag * jnp.sin(li * dt)
    den = lr * lr + li * li
    kr = ((ar - 1.0) * lr + ai * li) / den
    ki = (ai * lr - (ar - 1.0) * li) / den
    br, bi = b_re.astype(F32), b_im.astype(F32)
    bbr = kr[..., None] * br - ki[..., None] * bi
    bbi = kr[..., None] * bi + ki[..., None] * br
    n_g = lr.shape[0]
    eye = jnp.eye(n_g, dtype=F32)

    def drive(bb):
        return jnp.einsum('gpc,gh->gchp', bb, eye).reshape(n_g * SSM_GROUP, n_g * SSM_STATE)

    def readout(c):
        return jnp.einsum('gcp,gh->gphc', c, eye).reshape(n_g * SSM_STATE, n_g * SSM_GROUP)

    wb = jnp.concatenate([drive(bbr), drive(bbi)], axis=1).astype(BF16)
    wc = jnp.concatenate([readout(c_re.astype(F32)), -readout(c_im.astype(F32))], axis=0).astype(BF16)
    a = jnp.concatenate([ar.reshape(1, -1), ai.reshape(1, -1)], axis=1)
    return wb, jnp.broadcast_to(a, (SUBLANES, a.shape[1])), wc


def _layer(x, mod, w_in, lam_re, lam_im, log_step, b_re, b_im, c_re, c_im, d_skip, glu_w, cmp_pos, cmp_w1,
           cmp_w2, rel_bias, w_out, ln1_g, ln1_b, router_w, router_bias, e_gate, e_up, e_down, sg, su, sd,
           ln2_g, ln2_b, alpha):
    bsz, seq, dm = x.shape
    n_tok = bsz * seq
    sh1, sc1, g1, sh2, sc2, g2 = [m[:, None, :] for m in jnp.split(mod, 6, axis=-1)]
    ssm_w = dm // 2
    attn_w = N_HEADS * HEAD_DIM
    kv_w = N_KV * HEAD_DIM
    n_gate = 3 * N_HEADS
    offs = np.cumsum([0, ssm_w, attn_w] + [kv_w] * 6 + [n_gate, 2 * dm])

    wu = w_in[:, offs[0]:offs[1]].astype(BF16)
    wq = w_in[:, offs[1]:offs[2]] * (HEAD_DIM ** -0.5 * LOG2E)
    wa = jnp.concatenate([wq, w_in[:, offs[2]:offs[4]]], axis=1).astype(BF16)
    wg = jnp.pad(w_in[:, offs[8]:offs[9]], ((0, 0), (0, LANES - n_gate))).astype(BF16)
    wm = w_in[:, offs[9]:offs[10]].astype(BF16)
    nq = seq // Q_BLOCK
    n_blk = seq // SEL_LEN
    nch = seq // CMP_STRIDE

    def head_padded(w, width):
        w = w.reshape(dm, N_KV, HEAD_DIM)
        return jnp.pad(w, ((0, 0), (0, 0), (0, width - HEAD_DIM))).reshape(dm, N_KV * width).astype(BF16)

    aug_w = 2 * HEAD_DIM + -(-n_blk // LANES) * LANES
    u2d, act, gates, mg, kaug, kw, vst, vwt = _input_projection(
        x, sh1, sc1, wu, wa, wg, wm, head_padded(w_in[:, offs[4]:offs[5]], aug_w),
        head_padded(w_in[:, offs[6]:offs[7]], 2 * HEAD_DIM),
        w_in[:, offs[5]:offs[6]].T.astype(BF16), w_in[:, offs[7]:offs[8]].T.astype(BF16))

    def piece(i):
        return act[:, :, attn_w + i * kv_w: attn_w + (i + 1) * kv_w]

    def compress(raw, pos, w1, w2):
        xc = raw.reshape(bsz, nch, CMP_STRIDE, N_KV, HEAD_DIM).transpose(0, 3, 1, 2, 4).reshape(bsz, N_KV, nch, CMP_STRIDE * HEAD_DIM)
        half = CMP_STRIDE * HEAD_DIM
        w1cat = jnp.concatenate([w1[:half], w1[half:]], axis=1).astype(BF16)
        posb = jnp.dot(pos.reshape(1, -1), w1, precision=lax.Precision.HIGHEST)
        return _compress(xc, w1cat, posb, w2.astype(BF16))

    kc = compress(piece(0), cmp_pos[0], cmp_w1[0], cmp_w2[0])
    vc = compress(piece(1), cmp_pos[1], cmp_w1[1], cmp_w2[1])
    pad = ((0, 0), (0, 0), (PAD_CMP, _padded_cmp_rows(seq) - nch - PAD_CMP), (0, 0))
    kcp = jnp.pad(kc, pad).astype(BF16)
    vcpt = jnp.swapaxes(jnp.pad(vc, pad), 2, 3).astype(BF16)

    nst = nq // NSA_TILES
    q = act[:, :, :attn_w].reshape(bsz, nst, NSA_TILES, Q_BLOCK, N_KV, GQA, HEAD_DIM)
    qt = q.transpose(0, 4, 1, 6, 2, 5, 3).reshape(bsz, N_KV, nst, HEAD_DIM, COLS)
    gt =gates[:, :, :n_gate].reshape(bsz, nst, NSA_TILES, Q_BLOCK, N_KV, GQA, 3)
    gt = gt.transpose(0, 4, 1, 6, 2, 5, 3).reshape(bsz, N_KV, nst, 3, COLS)
    gt = jnp.pad(gt, ((0, 0), (0, 0), (0, 0), (0, SUBLANES - 3), (0, 0)))
    nb, cb = _bias_tiles(rel_bias)
    ot = _nsa(qt, kaug, vst, kw, vwt, kcp, vcpt, _overlap_t(seq), nb, cb, gt)
    ot = ot.reshape(bsz, N_KV, nst, HEAD_DIM, NSA_TILES, GQA, Q_BLOCK)
    y_nsa = ot.transpose(0, 2, 4, 6, 1, 5, 3).reshape(bsz, seq, attn_w)

    wb, a, wc = _s5_params(lam_re, lam_im, log_step, b_re, b_im, c_re, c_im)
    z2 = _s5(u2d.reshape(seq * bsz, ssm_w), wb, a, wc, d_skip.reshape(1, ssm_w).astype(F32), bsz)
    z2d = z2.reshape(seq, bsz * ssm_w)

    base, h2 = _mix_out(z2d, y_nsa, mg, x, g1, sh2, sc2, g2, ln1_g.reshape(1, dm), ln1_b.reshape(1, dm),
                        glu_w.astype(BF16), w_out.astype(BF16), sg.astype(BF16), su.astype(BF16), sd.astype(BF16), alpha)
    h2 = h2.reshape(n_tok, dm // 2)

    n_exp = router_w.shape[1]
    rt = min(ROUTER_TILE, n_tok)
    tri = jnp.asarray(np.triu(np.ones((rt, rt), np.float32), 1), BF16)
    top_e, top_w, rank, counts = _router(h2, router_w.T.astype(BF16), router_bias.reshape(n_exp, 1).astype(F32), tri)
    counts = counts[:, 0].astype(jnp.int32)
    padded = (counts + EXPERT_TILE - 1) // EXPERT_TILE * EXPERT_TILE
    pad_end = jnp.cumsum(padded)
    pad_start = pad_end - padded
    dest = _dest_rows(pad_start.astype(jnp.int32), top_e, rank)
    n_rows = n_tok * TOP_K + n_exp * EXPERT_TILE
    n_tiles = n_rows // EXPERT_TILE
    tile_exp = jnp.minimum(jnp.searchsorted(pad_end, jnp.arange(n_tiles) * EXPERT_TILE, side='right'), n_exp - 1).astype(jnp.int32)
    n_used = (pad_end[-1] // EXPERT_TILE).astype(jnp.int32).reshape(1)
    mt = min(MOVE_TILE, seq)
    dest3 = dest.reshape(TOP_K, n_tok // mt, mt).transpose(1, 0, 2)
    xs = _dispatch(dest3, h2, n_rows)
    ys = _experts(tile_exp, n_used, xs, e_gate, e_up, e_down)
    return _final(dest3, top_w.T, base, g2, ln2_g.reshape(1, dm), ln2_b.reshape(1, dm), ys)


def kernel(x, c, ada_w, ada_b, w_in, ssm_lambda_re, ssm_lambda_im, ssm_log_step, ssm_b_re, ssm_b_im, ssm_c_re, ssm_c_im, ssm_d, ssm_glu_w, cmp_pos, cmp_w1, cmp_w2, rel_bias, w_out, ln1_g, ln1_b, router_w, router_bias, exp_w_gate, exp_w_up, exp_w_down, sh_w_gate, sh_w_up, sh_w_down, ln2_g, ln2_b):
    depth = ada_w.shape[0]
    alpha = (2 * depth) ** 0.25
    for l in range(depth):
        mod = _modulation(c, ada_w[l], ada_b[l])
        x = _layer(x, mod, w_in[l], ssm_lambda_re[l], ssm_lambda_im[l], ssm_log_step[l], ssm_b_re[l], ssm_b_im[l],
                   ssm_c_re[l], ssm_c_im[l], ssm_d[l], ssm_glu_w[l], cmp_pos[l], cmp_w1[l], cmp_w2[l], rel_bias,
                   w_out[l], ln1_g[l], ln1_b[l], router_w[l], router_bias[l], exp_w_gate[l], exp_w_up[l],
                   exp_w_down[l], sh_w_gate[l], sh_w_up[l], sh_w_down[l], ln2_g[l], ln2_b[l], alpha)
    return x
```

```python
import functools
import math

import numpy as np
import jax
import jax.numpy as jnp
from jax import lax
from jax.experimental import pallas as pl
from jax.experimental.pallas import tpu as pltpu

F32 = jnp.float32
BF16 = jnp.bfloat16

SSM_GROUP = 16
SSM_STATE = 64
N_HEADS = 16
HEAD_DIM = 64
N_KV = 4
GQA = N_HEADS // N_KV
CMP_LEN = 32
CMP_STRIDE = 16
SEL_LEN = 64
SEL_TOP = 16
WINDOW = 512
Q_BLOCK = 64
FORCE_BONUS = 1.0e4
REL_BUCKETS = 32
REL_MAX_DIST = 128
TOP_K = 8
ROUTED_SCALE = 2.5
LN_EPS = 1e-5

LANES = 128
SUBLANES = 8
VMEM_LIMIT_BYTES = 56 * 1024 * 1024
ROW_TILE = 512
ROUTER_TILE = 256
EXPERT_TILE = 256
MOVE_TILE = 256
DEST_TILE = 2048
SCAN_CHUNK = 128
SCAN_LANES = 512
MASK_BIG = 32768.0
NEG = -1.0e30
LOG2E = math.log2(math.e)
WIN_BLOCKS = WINDOW // SEL_LEN
ROWS = GQA * Q_BLOCK
NSA_TILES = 4
COLS = NSA_TILES * ROWS
CHUNK_BLOCKS = 8
CHUNK_KEYS = CHUNK_BLOCKS * SEL_LEN
FAR_BLOCKS = NSA_TILES
FAR_KEYS = FAR_BLOCKS * SEL_LEN
PAD_CMP = 8
BAND_ROWS = 24


def _cparams(sem):
    return pltpu.CompilerParams(dimension_semantics=sem, vmem_limit_bytes=VMEM_LIMIT_BYTES)


def _pack_rows(x):
    n = x.shape[1] // 2
    xb = x.astype(jnp.bfloat16).astype(F32)
    lo = lax.shift_right_logical(lax.bitcast_convert_type(xb[:, :n], jnp.uint32), jnp.uint32(16))
    return lax.bitcast_convert_type(xb[:, n:], jnp.uint32) | lo


def _unpack_rows(w):
    lo = lax.bitcast_convert_type(lax.shift_left(w, jnp.uint32(16)), F32)
    hi = lax.bitcast_convert_type(w & jnp.uint32(0xFFFF0000), F32)
    return lo, hi


def _norm_rows(x):
    mu = jnp.mean(x, axis=-1, keepdims=True)
    xc = x - mu
    var = jnp.mean(xc * xc, axis=-1, keepdims=True)
    return xc * lax.rsqrt(var + LN_EPS)


def _mod_kernel(c_ref, w_ref, b_ref, o_ref):
    cond = jax.nn.silu(c_ref[...])
    o_ref[...] = jnp.dot(cond.astype(BF16), w_ref[...].astype(BF16), preferred_element_type=F32) + b_ref[...]


def _modulation(c, ada_w, ada_b):
    bsz, dm = c.shape
    n = ada_w.shape[1]
    tn = dm
    return pl.pallas_call(
        _mod_kernel,
        grid=(n // tn,),
        in_specs=[pl.BlockSpec((bsz, dm), lambda j: (0, 0)),
                  pl.BlockSpec((dm, tn), lambda j: (0, j)),
                  pl.BlockSpec((1, tn), lambda j: (0, j))],
        out_specs=pl.BlockSpec((bsz, tn), lambda j: (0, j)),
        out_shape=jax.ShapeDtypeStruct((bsz, n), F32),
        compiler_params=_cparams(("arbitrary",)),
        name="ada_modulation",
    )(c, ada_w, ada_b.reshape(1, n))


def _inproj_kernel(x_ref, sh_ref, sc_ref, wu_ref, wa_ref, wg_ref, wm_ref, wks_ref, wkw_ref, wvs_ref, wvw_ref,
                   u_ref, a_ref, g_ref, m_ref, ks_ref, kw_ref, vs_ref, vw_ref, *, aug_w):
    h = _norm_rows(x_ref[0]) * (1.0 + sc_ref[0]) + sh_ref[0]
    hb = h.astype(BF16)
    tm = hb.shape[0]
    u_ref[...] = jnp.dot(hb, wu_ref[...], preferred_element_type=F32)
    a_ref[0] = jnp.dot(hb, wa_ref[...], preferred_element_type=F32).astype(BF16)
    g_ref[0] = jax.nn.sigmoid(jnp.dot(hb, wg_ref[...], preferred_element_type=F32))
    m_ref[0] = jax.nn.sigmoid(jnp.dot(hb, wm_ref[...], preferred_element_type=F32)).astype(BF16)
    ks = jnp.dot(hb, wks_ref[...], preferred_element_type=F32)
    assert aug_w & (aug_w - 1) == 0
    col = (lax.broadcasted_iota(jnp.int32, ks.shape, 1) & (aug_w - 1)) - 2 * HEAD_DIM
    blk = jnp.right_shift(pl.program_id(1) * tm + lax.broadcasted_iota(jnp.int32, ks.shape, 0), SEL_LEN.bit_length() - 1)
    ks_ref[0] = jnp.where(col == blk, -MASK_BIG, ks).astype(BF16)
    kw_ref[0] = jnp.dot(hb, wkw_ref[...], preferred_element_type=F32).astype(BF16)
    rows = lax.broadcasted_iota(jnp.int32, (2 * SUBLANES, tm), 0)
    ones_rows = jnp.where(rows == 0, 1.0, 0.0).astype(BF16)
    for w_ref, v_ref in ((wvs_ref, vs_ref), (wvw_ref, vw_ref)):
        vt = lax.dot_general(w_ref[...], hb, (((1,), (1,)), ((), ())), preferred_element_type=F32).astype(BF16)
        for g in range(N_KV):
            v_ref[0, g] = jnp.concatenate([vt[g * HEAD_DIM:(g + 1) * HEAD_DIM], ones_rows], axis=0)


def _input_projection(x, sh1, sc1, wu, wa, wg, wm, wks, wkw, wvs_t, wvw_t):
    bsz, seq, dm = x.shape
    tm = min(ROW_TILE, seq)
    nu, na, ng, nm = wu.shape[1], wa.shape[1], wg.shape[1], wm.shape[1]
    full = lambda w: pl.BlockSpec(w.shape, lambda b, i: (0, 0))
    vec = pl.BlockSpec((1, 1, dm), lambda b, i: (b, 0, 0))
    rows_out = lambda n: pl.BlockSpec((1, tm, n), lambda b, i: (b, i, 0))
    vt_rows = HEAD_DIM + 2 * SUBLANES
    vt_out = pl.BlockSpec((1, N_KV, vt_rows, tm), lambda b, i: (b, 0, 0, i))
    return pl.pallas_call(
        functools.partial(_inproj_kernel, aug_w=wks.shape[1] // N_KV),
        grid=(bsz, seq // tm),
        in_specs=[pl.BlockSpec((1, tm, dm), lambda b, i: (b, i, 0)), vec, vec,
                  full(wu), full(wa), full(wg), full(wm), full(wks), full(wkw), full(wvs_t), full(wvw_t)],
        out_specs=[pl.BlockSpec((tm, nu), lambda b, i: (i, b)), rows_out(na), rows_out(ng), rows_out(nm),
                   rows_out(wks.shape[1]), rows_out(wkw.shape[1]), vt_out, vt_out],
        out_shape=[jax.ShapeDtypeStruct((seq, bsz * nu), F32),
                   jax.ShapeDtypeStruct((bsz, seq, na), BF16),
                   jax.ShapeDtypeStruct((bsz, seq, ng), F32),
                   jax.ShapeDtypeStruct((bsz, seq, nm), BF16),
                   jax.ShapeDtypeStruct((bsz, seq, wks.shape[1]), BF16),
                   jax.ShapeDtypeStruct((bsz, seq, wkw.shape[1]), BF16),
                   jax.ShapeDtypeStruct((bsz, N_KV, vt_rows, seq), BF16),
                   jax.ShapeDtypeStruct((bsz, N_KV, vt_rows, seq), BF16)],
        compiler_params=_cparams(("arbitrary", "arbitrary")),
        name="adaln_input_projection",
    )(x, sh1, sc1, wu, wa, wg, wm, wks, wkw, wvs_t, wvw_t)


def _compress_kernel(x_ref, w1_ref, pb_ref, w2_ref, o_ref, *, n_cmp):
    hid = w2_ref.shape[0]
    p = jnp.dot(x_ref[0, 0], w1_ref[...], preferred_element_type=F32)
    nrow = p.shape[0]
    nxt = pltpu.roll(p[:, hid:], nrow - 1, 0)
    hidv = jax.nn.gelu(p[:, :hid] + nxt + pb_ref[...])
    out = jnp.dot(hidv.astype(BF16), w2_ref[...], preferred_element_type=F32)
    rows = lax.broadcasted_iota(jnp.int32, out.shape, 0)
    o_ref[0, 0] = jnp.where(rows < n_cmp, out, 0.0)


def _compress(xc, w1cat, posb, w2):
    bsz, nkv, nch, kdim = xc.shape
    hid2 = w1cat.shape[1]
    return pl.pallas_call(
        functools.partial(_compress_kernel, n_cmp=nch - 1),
        grid=(bsz, nkv),
        in_specs=[pl.BlockSpec((1, 1, nch, kdim), lambda b, g: (b, g, 0, 0)),
                  pl.BlockSpec((kdim, hid2), lambda b, g: (0, 0)),
                  pl.BlockSpec((1, hid2 // 2), lambda b, g: (0, 0)),
                  pl.BlockSpec((hid2 // 2, HEAD_DIM), lambda b, g: (0, 0))],
        out_specs=pl.BlockSpec((1, 1, nch, HEAD_DIM), lambda b, g: (b, g, 0, 0)),
        out_shape=jax.ShapeDtypeStruct((bsz, nkv, nch, HEAD_DIM), F32),
        compiler_params=_cparams(("arbitrary", "arbitrary")),
        name="kv_compress",
    )(xc, w1cat, posb, w2)


def _nsa_kernel(qt_ref, kaug_ref, vst_ref, kw_ref, vwt_ref, kcp_ref, vcpt_ref, ovt_ref, nb_ref, cb_ref,
                g_ref, o_ref, s_ref, c_ref, w_ref, fa_ref, fb_ref, *, n_blk, n_cmp, n_sel):
    q0 = pl.program_id(2) * NSA_TILES
    qt = qt_ref[0, 0, 0]
    col_tile = lambda n: slice(n * ROWS, (n + 1) * ROWS)

    s_ref[...] = jnp.dot(kcp_ref[0, 0], qt, preferred_element_type=F32)
    for n in range(NSA_TILES):
        band = pl.multiple_of(SUBLANES * ((q0 + n) // 2), SUBLANES)
        s_ref[pl.ds(band, BAND_ROWS), col_tile(n)] += cb_ref[0, n % 2]
    s = s_ref[...]
    rho = lax.broadcasted_iota(jnp.int32, s.shape, 0)
    col = lax.broadcasted_iota(jnp.int32, s.shape, 1)
    tok = col & (Q_BLOCK - 1)
    qblk = q0 + jnp.right_shift(col, ROWS.bit_length() - 1)
    first_visible = CMP_STRIDE * (rho - PAD_CMP) + (CMP_LEN - 1) - Q_BLOCK * qblk
    vis = (rho >= PAD_CMP) & (rho < PAD_CMP + n_cmp) & (first_visible <= tok)
    s = jnp.where(vis, s, -jnp.inf)
    mx = jnp.max(s, axis=0, keepdims=True)
    mx = jnp.where(mx == -jnp.inf, 0.0, mx)
    e = jnp.exp2(s - mx)
    p_cmp = (e / jnp.maximum(jnp.sum(e, axis=0, keepdims=True), 1e-30)).astype(BF16)
    both = jnp.dot(jnp.concatenate([vcpt_ref[0, 0], ovt_ref[...]], axis=0), p_cmp, preferred_element_type=F32)
    o_cmp = both[:HEAD_DIM]

    imp4 = both[HEAD_DIM:]
    sums = []
    for n in range(NSA_TILES):
        two = imp4[:, n * ROWS:n * ROWS + LANES] + imp4[:, n * ROWS + LANES:(n + 1) * ROWS]
        sums.append(two + pltpu.roll(two, Q_BLOCK, 1))
    low = lax.broadcasted_iota(jnp.int32, sums[0].shape, 1) < Q_BLOCK
    imp = jnp.concatenate([jnp.where(low, sums[n], sums[n + 1]) for n in range(0, NSA_TILES, 2)], axis=1)
    blk = lax.broadcasted_iota(jnp.int32, imp.shape, 0)
    cur = q0 + jnp.right_shift(lax.broadcasted_iota(jnp.int32, imp.shape, 1), Q_BLOCK.bit_length() - 1)
    blkf = blk.astype(F32)
    forced = (blk == 0) | (blk == cur) | (blk == cur - 1)
    score = jnp.where(blk <= cur, imp + jnp.where(forced, FORCE_BONUS, 0.0), -jnp.inf)
    notsel = jnp.ones(imp.shape, F32)
    for _ in range(n_sel):
        best = jnp.max(score, axis=0, keepdims=True)
        first = jnp.min(jnp.where(score == best, blkf, float(n_blk)), axis=0, keepdims=True)
        pick = blkf == first
        notsel = jnp.where(pick, 0.0, notsel)
        score = jnp.where(pick, -jnp.inf, score)
    halves = []
    for n in range(0, NSA_TILES, 2):
        pair = notsel[:, (n // 2) * LANES:(n // 2 + 1) * LANES]
        swapped = pltpu.roll(pair, Q_BLOCK, 1)
        halves += [jnp.where(low, pair, swapped), jnp.where(low, swapped, pair)]
    notsel = jnp.concatenate([h for h in halves for _ in (0, 1)], axis=1).astype(BF16)
    qwin = jnp.concatenate([qt, jnp.zeros_like(qt)], axis=0)
    tail = kaug_ref.shape[2] - 2 * HEAD_DIM - n_blk
    qaug = jnp.concatenate([qwin, notsel] + ([jnp.zeros((tail, COLS), notsel.dtype)] if tail else []), axis=0)

    def update(carry, sc, vt, top=None):
        m, acc = carry
        m_new = jnp.maximum(m, jnp.max(sc, axis=0, keepdims=True) if top is None else top)
        p = jnp.exp2((sc - m_new).astype(BF16))
        return m_new, jnp.exp2(m - m_new) * acc + jnp.dot(vt, p, preferred_element_type=F32)

    def chunk_keys(c):
        return pl.ds(pl.multiple_of(c * FAR_KEYS, FAR_KEYS), FAR_KEYS)

    def far_scores(c, buf_ref):
        sc = jnp.dot(kaug_ref[0, chunk_keys(c), :], qaug, preferred_element_type=F32)
        buf_ref[...] = sc
        return jnp.max(sc, axis=0, keepdims=True)

    def far_pair(i, carry):
        m, acc, top_a = carry
        top_b = far_scores(2 * i + 1, fb_ref)
        m, acc = update((m, acc), fa_ref[...], vst_ref[0, 0, :, chunk_keys(2 * i)], top_a)
        top_a = far_scores(2 * i + 2, fa_ref)
        m, acc = update((m, acc), fb_ref[...], vst_ref[0, 0, :, chunk_keys(2 * i + 1)], top_b)
        return m, acc, top_a

    def far_last(_, carry, n_far):
        m, acc, top_a = carry
        return update((m, acc), fa_ref[...], vst_ref[0, 0, :, chunk_keys(n_far - 1)], top_a) + (top_a,)

    def near_chunk(k_ref, vt_ref, qmat, blk0, tile_of, carry, buf_ref):
        n_keys = buf_ref.shape[0]
        keys = pl.ds(pl.multiple_of(blk0 * SEL_LEN, NSA_TILES * SEL_LEN), n_keys)
        buf_ref[...] = jnp.dot(k_ref[0, keys, :], qmat, preferred_element_type=F32)
        for o in range(n_keys // SEL_LEN):
            for n in range(NSA_TILES):
                buf_ref[o * SEL_LEN:(o + 1) * SEL_LEN, col_tile(n)] += nb_ref[0, tile_of(blk0 + o, q0 + n)]
        return update(carry, buf_ref[...], vt_ref[0, 0, :, keys])

    def sel_tile(j, qi):
        d = qi - j
        return jnp.where(d < 0, 4, jnp.minimum(d, 3))

    def win_tile(j, qi):
        d = qi - j
        return jnp.where((d < 0) | (d > WIN_BLOCKS), 4, jnp.where(d == WIN_BLOCKS, 5, jnp.minimum(d, 3)))

    init = (jnp.full((1, COLS), NEG, F32), jnp.zeros((vst_ref.shape[2], COLS), F32))
    n_far = jnp.maximum(q0 - 2, 0) // FAR_BLOCKS
    carry = lax.fori_loop(0, n_far // 2, far_pair, init + (far_scores(0, fa_ref),))
    carry = lax.fori_loop(0, n_far % 2, functools.partial(far_last, n_far=n_far), carry)
    _, acc = near_chunk(kaug_ref, vst_ref, qaug, n_far * FAR_BLOCKS, sel_tile, carry[:2], c_ref)
    o_slc = acc[:HEAD_DIM] / acc[HEAD_DIM:HEAD_DIM + 1]
    _, acc = near_chunk(kw_ref, vwt_ref, qwin, jnp.maximum(q0 - WIN_BLOCKS, 0), win_tile, init, w_ref)
    o_win = acc[:HEAD_DIM] / acc[HEAD_DIM:HEAD_DIM + 1]
    g = g_ref[0, 0, 0]
    o_ref[0, 0, 0] = (g[0:1] * o_cmp + g[1:2] * o_slc + g[2:3] * o_win).astype(BF16)


def _nsa(qt, kaug, vst, kw, vwt, kcp, vcpt, ovt, nb, cb, gt):
    bsz, nkv, nsteps, dh, cols = qt.shape
    seq = kw.shape[1]
    n_blk = seq // SEL_LEN
    ncp = kcp.shape[2]
    n_cmp = (seq - CMP_LEN) // CMP_STRIDE + 1
    assert n_blk % CHUNK_BLOCKS == 0 and n_blk >= WIN_BLOCKS + NSA_TILES and CHUNK_BLOCKS == FAR_BLOCKS + NSA_TILES
    per_bg = lambda shape: pl.BlockSpec((1, 1) + shape, lambda b, g, i: (b, g, 0, 0))
    per_step = lambda shape: pl.BlockSpec((1, 1, 1) + shape, lambda b, g, i: (b, g, i, 0, 0))
    head_cols = lambda a: pl.BlockSpec((1, seq, a.shape[2] // nkv), lambda b, g, i: (b, 0, g))
    kern = functools.partial(_nsa_kernel, n_blk=n_blk, n_cmp=n_cmp, n_sel=min(SEL_TOP, n_blk))
    return pl.pallas_call(
        kern,
        grid=(bsz, nkv, nsteps),
        in_specs=[per_step((dh, cols)),
                  head_cols(kaug), per_bg((vst.shape[2], seq)), head_cols(kw), per_bg((vwt.shape[2], seq)),
                  per_bg((ncp, dh)), per_bg((dh, ncp)),
                  pl.BlockSpec(ovt.shape, lambda b, g, i: (0, 0)),
                  pl.BlockSpec((1,) + nb.shape[1:], lambda b, g, i: (g, 0, 0, 0)),
                  pl.BlockSpec((1,) + cb.shape[1:], lambda b, g, i: (g, 0, 0, 0)),
                  per_step((SUBLANES, cols))],
        out_specs=per_step((dh, cols)),
        out_shape=jax.ShapeDtypeStruct((bsz, nkv, nsteps, dh, cols), BF16),
        scratch_shapes=[pltpu.VMEM((ncp, cols), F32), pltpu.VMEM((CHUNK_KEYS, cols), F32),
                        pltpu.VMEM(((WIN_BLOCKS + NSA_TILES) * SEL_LEN, cols), F32),
                        pltpu.VMEM((FAR_KEYS, cols), F32), pltpu.VMEM((FAR_KEYS, cols), F32)],
        compiler_params=_cparams(("arbitrary", "arbitrary", "arbitrary")),
        name="nsa_attention",
    )(qt, kaug, vst, kw, vwt, kcp, vcpt, ovt, nb, cb, gt)


def _s5_kernel(u_ref, wb_ref, a_ref, wc_ref, d_ref, z_ref, xs_ref, st_ref, *, bsz, n_state):
    @pl.when(pl.program_id(0) == 0)
    def _():
        st_ref[...] = jnp.zeros_like(st_ref)

    u = u_ref[...]
    xs_ref[...] = jnp.dot(u.astype(BF16), wb_ref[...], preferred_element_type=F32)
    steps = u.shape[0] // bsz
    for c0 in range(0, n_state, SCAN_LANES):
        re = pl.ds(c0, SCAN_LANES)
        im = pl.ds(n_state + c0, SCAN_LANES)
        ar = jnp.broadcast_to(a_ref[0:1, re], (bsz, SCAN_LANES))
        ai = jnp.broadcast_to(a_ref[0:1, im], (bsz, SCAN_LANES))

        def step(t, carry):
            xr, xi = carry
            rows = pl.ds(pl.multiple_of(t * bsz, bsz), bsz)
            nr = ar * xr - ai * xi + xs_ref[rows, re]
            ni = ar * xi + ai * xr + xs_ref[rows, im]
            xs_ref[rows, re] = nr
            xs_ref[rows, im] = ni
            return nr, ni

        xr, xi = lax.fori_loop(0, steps, step, (st_ref[:, re], st_ref[:, im]), unroll=8)
        st_ref[:, re] = xr
        st_ref[:, im] = xi
    y = jnp.dot(xs_ref[...].astype(BF16), wc_ref[...], preferred_element_type=F32) + d_ref[...] * u
    z_ref[...] = jax.nn.gelu(y).astype(BF16)


def _s5(u2, wb, a, wc, dsk, bsz):
    rows, width = u2.shape
    seq = rows // bsz
    chunk = min(SCAN_CHUNK, seq)
    n_state2 = wb.shape[1]
    return pl.pallas_call(
        functools.partial(_s5_kernel, bsz=bsz, n_state=n_state2 // 2),
        grid=(seq // chunk,),
        in_specs=[pl.BlockSpec((chunk * bsz, width), lambda i: (i, 0)),
                  pl.BlockSpec(wb.shape, lambda i: (0, 0)),
                  pl.BlockSpec(a.shape, lambda i: (0, 0)),
                  pl.BlockSpec(wc.shape, lambda i: (0, 0)),
                  pl.BlockSpec(dsk.shape, lambda i: (0, 0))],
        out_specs=pl.BlockSpec((chunk * bsz, width), lambda i: (i, 0)),
        out_shape=jax.ShapeDtypeStruct((rows, width), BF16),
        scratch_shapes=[pltpu.VMEM((chunk * bsz, n_state2), F32), pltpu.VMEM((bsz, n_state2), F32)],
        compiler_params=_cparams(("arbitrary",)),
        name="s5_scan",
    )(u2, wb, a, wc, dsk)


def _mixout_kernel(z_ref, yn_ref, mg_ref, x_ref, g1_ref, sh2_ref, sc2_ref, g2_ref, lg_ref, lb_ref,
                   glu_ref, wo_ref, sg_ref, su_ref, sd_ref, base_ref, h_ref, *, alpha):
    dm = x_ref.shape[2]
    glu = jnp.dot(z_ref[...], glu_ref[...], preferred_element_type=F32)
    y_ssm = glu[:, :dm] * jax.nn.sigmoid(glu[:, dm:])
    mg = mg_ref[0].astype(F32)
    merged = mg[:, :dm] * y_ssm + mg[:, dm:] * yn_ref[0].astype(F32)
    y = jnp.dot(merged.astype(BF16), wo_ref[...], preferred_element_type=F32)
    x1 = _norm_rows(alpha * x_ref[0] + g1_ref[0] * y) * lg_ref[...] + lb_ref[...]
    hf = _norm_rows(x1) * (1.0 + sc2_ref[0]) + sh2_ref[0]
    h_ref[0] = _pack_rows(hf)
    h = hf.astype(BF16)
    hs =jax.nn.silu(jnp.dot(h, sg_ref[...], preferred_element_type=F32)) * jnp.dot(h, su_ref[...], preferred_element_type=F32)
    shared = jnp.dot(hs.astype(BF16), sd_ref[...], preferred_element_type=F32)
    base_ref[0] = alpha * x1 + g2_ref[0] * shared


def _mix_out(z2d, y_nsa, mg, x, g1, sh2, sc2, g2, ln_g, ln_b, glu_w, w_out, sg, su, sd, alpha):
    bsz, seq, dm = x.shape
    tm = min(ROW_TILE, seq)
    width = z2d.shape[1] // bsz
    vec = pl.BlockSpec((1, 1, dm), lambda b, i: (b, 0, 0))
    row = pl.BlockSpec((1, dm), lambda b, i: (0, 0))
    full = lambda w: pl.BlockSpec(w.shape, lambda b, i: (0, 0))
    tile = lambda n: pl.BlockSpec((1, tm, n), lambda b, i: (b, i, 0))
    return pl.pallas_call(
        functools.partial(_mixout_kernel, alpha=alpha),
        grid=(bsz, seq // tm),
        in_specs=[pl.BlockSpec((tm, width), lambda b, i: (i, b)), tile(dm), tile(2 * dm), tile(dm),
                  vec, vec, vec, vec, row, row, full(glu_w), full(w_out), full(sg), full(su), full(sd)],
        out_specs=[tile(dm), tile(dm // 2)],
        out_shape=[jax.ShapeDtypeStruct((bsz, seq, dm), F32), jax.ShapeDtypeStruct((bsz, seq, dm // 2), jnp.uint32)],
        compiler_params=_cparams(("arbitrary", "arbitrary")),
        name="merge_outproj_ln_shared",
    )(z2d, y_nsa, mg, x, g1, sh2, sc2, g2, ln_g, ln_b, glu_w, w_out, sg, su, sd)


def _router_kernel(h_ref, rwt_ref, rb_ref, tri_ref, e_ref, w_ref, p_ref, cnt_ref):
    @pl.when(pl.program_id(0) == 0)
    def _():
        cnt_ref[...] = jnp.zeros_like(cnt_ref)

    h = jnp.concatenate(_unpack_rows(h_ref[...]), axis=1).astype(BF16)
    logits = lax.dot_general(rwt_ref[...], h, (((1,), (1,)), ((), ())), preferred_element_type=F32)
    scores = jax.nn.sigmoid(logits)
    cur = scores + rb_ref[...]
    n_exp = scores.shape[0]
    eid = lax.broadcasted_iota(jnp.int32, scores.shape, 0).astype(F32)
    chosen = jnp.zeros(scores.shape, F32)
    ids, vals = [], []
    for _ in range(TOP_K):
        best = jnp.max(cur, axis=0, keepdims=True)
        first = jnp.min(jnp.where(cur == best, eid, float(n_exp)), axis=0, keepdims=True)
        pick = eid == first
        ids.append(first)
        vals.append(jnp.sum(jnp.where(pick, scores, 0.0), axis=0, keepdims=True))
        chosen = jnp.where(pick, 1.0, chosen)
        cur = jnp.where(pick, -jnp.inf, cur)
    top_s = jnp.concatenate(vals, axis=0)
    w_ref[...] = top_s / jnp.sum(top_s, axis=0, keepdims=True) * ROUTED_SCALE
    top_e = jnp.concatenate(ids, axis=0)
    e_ref[...] = top_e.astype(jnp.int32)
    before = jnp.dot(chosen.astype(BF16), tri_ref[...], preferred_element_type=F32) + cnt_ref[...]
    ranks = [jnp.sum(jnp.where(eid == ids[k], before, 0.0), axis=0, keepdims=True) for k in range(TOP_K)]
    p_ref[...] = jnp.concatenate(ranks, axis=0).astype(jnp.int32)
    cnt_ref[...] += jnp.sum(chosen, axis=1, keepdims=True)


def _router(h2, rwt, rb, tri):
    n_tok, words = h2.shape
    n_exp, dm = rwt.shape
    tm = tri.shape[0]
    kt = pl.BlockSpec((TOP_K, tm), lambda i: (0, i))
    return pl.pallas_call(
        _router_kernel,
        grid=(n_tok // tm,),
        in_specs=[pl.BlockSpec((tm, words), lambda i: (i, 0)),
                  pl.BlockSpec((n_exp, dm), lambda i: (0, 0)),
                  pl.BlockSpec((n_exp, 1), lambda i: (0, 0)),
                  pl.BlockSpec((tm, tm), lambda i: (0, 0))],
        out_specs=[kt, kt, kt, pl.BlockSpec((n_exp, 1), lambda i: (0, 0))],
        out_shape=[jax.ShapeDtypeStruct((TOP_K, n_tok), jnp.int32),
                   jax.ShapeDtypeStruct((TOP_K, n_tok), F32),
                   jax.ShapeDtypeStruct((TOP_K, n_tok), jnp.int32),
                   jax.ShapeDtypeStruct((n_exp, 1), F32)],
        compiler_params=_cparams(("arbitrary",)),
        name="router_topk_rank",
    )(h2, rwt, rb, tri)


def _dest_kernel(start_ref, e_ref, r_ref, o_ref):
    e = e_ref[...]
    start = lax.fori_loop(0, start_ref.shape[0], lambda j, acc: jnp.where(e == j, start_ref[j], acc),
                          jnp.zeros(e.shape, jnp.int32))
    o_ref[...] = start + r_ref[...]


def _dest_rows(pad_start, top_e, rank):
    n_tok = top_e.shape[1]
    tm = min(DEST_TILE, n_tok)
    blk = pl.BlockSpec((TOP_K, tm), lambda i, ps: (0, i))
    return pl.pallas_call(
        _dest_kernel,
        grid_spec=pltpu.PrefetchScalarGridSpec(num_scalar_prefetch=1, grid=(n_tok // tm,), in_specs=[blk, blk], out_specs=blk),
        out_shape=jax.ShapeDtypeStruct(top_e.shape, jnp.int32),
        compiler_params=_cparams(("arbitrary",)),
        name="moe_dest_rows",
    )(pad_start, top_e, rank)


def _row_copy(src_ref, src_row, dst_ref, dst_row, sem):
    return pltpu.make_async_copy(src_ref.at[pl.ds(src_row, 1), :], dst_ref.at[pl.ds(dst_row, 1), :], sem)


def _dispatch_kernel(pend_ref, padded_ref, nused_ref, dest_ref, h_ref, xs_ref, zero_ref, sem, zsem):
    tm = h_ref.shape[0]

    @pl.when(pl.program_id(0) == 0)
    def _():
        zero_ref[...] = jnp.zeros_like(zero_ref)
        n_tiles = xs_ref.shape[0] // EXPERT_TILE

        def zero_tile(row):
            return pltpu.make_async_copy(zero_ref, xs_ref.at[pl.ds(pl.multiple_of(row, EXPERT_TILE), EXPERT_TILE), :], zsem)

        def per_expert(act):
            def body(e, carry):
                @pl.when(padded_ref[e] > 0)
                def _():
                    act(zero_tile(pend_ref[e] - EXPERT_TILE))
                return carry
            lax.fori_loop(0, pend_ref.shape[0], body, 0)

        def per_unused(act):
            def body(i, carry):
                act(zero_tile(i * EXPERT_TILE))
                return carry
            lax.fori_loop(nused_ref[0], n_tiles, body, 0)

        for loop in (per_expert, per_unused):
            loop(lambda copy: copy.start())
        for loop in (per_expert, per_unused):
            loop(lambda copy: copy.wait())

    def issue(t, carry):
        for k in range(TOP_K):
            _row_copy(h_ref, t, xs_ref, dest_ref[0, k, t], sem).start()
        return carry

    lax.fori_loop(0, tm, issue, 0)
    pltpu.make_async_copy(xs_ref.at[pl.ds(0, TOP_K * tm), :], xs_ref.at[pl.ds(0, TOP_K * tm), :], sem).wait()


def _dispatch(pad_end, padded, n_used, dest3, h2, n_rows):
    n_tok, words = h2.shape
    tm = dest3.shape[2]
    grid_spec = pltpu.PrefetchScalarGridSpec(
        num_scalar_prefetch=3,
        grid=(n_tok // tm,),
        in_specs=[pl.BlockSpec((1, TOP_K, tm), lambda i, *_: (i, 0, 0), memory_space=pltpu.SMEM),
                  pl.BlockSpec((tm, words), lambda i, *_: (i, 0))],
        out_specs=pl.BlockSpec(memory_space=pl.ANY),
        scratch_shapes=[pltpu.VMEM((EXPERT_TILE, words), jnp.uint32), pltpu.SemaphoreType.DMA(()),
                        pltpu.SemaphoreType.DMA(())],
    )
    return pl.pallas_call(
        _dispatch_kernel,
        grid_spec=grid_spec,
        out_shape=jax.ShapeDtypeStruct((n_rows, words), jnp.uint32),
        compiler_params=_cparams(("arbitrary",)),
        name="moe_dispatch",
    )(pad_end, padded, n_used, dest3, h2)


def _expert_kernel(te_ref, nu_ref, x_ref, wg_ref, wu_ref, wd_ref, y_ref):
    i = pl.program_id(0)

    @pl.when(i < nu_ref[0])
    def _():
        lo, hi = _unpack_rows(x_ref[...])
        lo, hi = lo.astype(BF16), hi.astype(BF16)
        half = lo.shape[1]

        def proj(w_ref):
            return (jnp.dot(lo, w_ref[0, :half, :].astype(BF16), preferred_element_type=F32)
                    + jnp.dot(hi, w_ref[0, half:, :].astype(BF16), preferred_element_type=F32))

        hmid = jax.nn.silu(proj(wg_ref)) * proj(wu_ref)
        y_ref[...] = _pack_rows(jnp.dot(hmid.astype(BF16), wd_ref[0].astype(BF16), preferred_element_type=F32))

    @pl.when(i >= nu_ref[0])
    def _():
        y_ref[...] = jnp.zeros_like(y_ref)


def _experts(tile_exp, n_used, xs, wg, wu, wd):
    n_rows, words = xs.shape
    dm, de = wg.shape[1], wg.shape[2]
    grid_spec = pltpu.PrefetchScalarGridSpec(
        num_scalar_prefetch=2,
        grid=(n_rows // EXPERT_TILE,),
        in_specs=[pl.BlockSpec((EXPERT_TILE, words), lambda i, te, nu: (i, 0)),
                  pl.BlockSpec((1, dm, de), lambda i, te, nu: (te[i], 0, 0)),
                  pl.BlockSpec((1, dm, de), lambda i, te, nu: (te[i], 0, 0)),
                  pl.BlockSpec((1, de, dm), lambda i, te, nu: (te[i], 0, 0))],
        out_specs=pl.BlockSpec((EXPERT_TILE, words), lambda i, te, nu: (i, 0)),
    )
    return pl.pallas_call(
        _expert_kernel,
        grid_spec=grid_spec,
        out_shape=jax.ShapeDtypeStruct((n_rows, words), jnp.uint32),
        compiler_params=_cparams(("arbitrary",)),
        name="expert_mlp",
    )(tile_exp, n_used, xs, wg, wu, wd)


def _final_kernel(dest_ref, w_ref, base_ref, g2_ref, lg_ref, lb_ref, ys_ref, o_ref, buf_ref, sem):
    tm = base_ref.shape[1]

    def issue(t, carry):
        for k in range(TOP_K):
            _row_copy(ys_ref, dest_ref[0, k, t], buf_ref.at[k], t, sem).start()
        return carry

    lax.fori_loop(0, tm, issue, 0)
    pltpu.make_async_copy(buf_ref, buf_ref, sem).wait()
    w = w_ref[...]
    lo_sum, hi_sum = jnp.zeros((tm, buf_ref.shape[2]), F32), jnp.zeros((tm, buf_ref.shape[2]), F32)
    for k in range(TOP_K):
        lo, hi = _unpack_rows(buf_ref[k])
        lo_sum += w[:, k:k + 1] * lo
        hi_sum += w[:, k:k + 1] * hi
    routed = jnp.concatenate([lo_sum, hi_sum], axis=1)
    o_ref[0] = _norm_rows(base_ref[0] + g2_ref[0] * routed) * lg_ref[...] + lb_ref[...]


def _final(dest3, w_tok, base, g2, ln_g, ln_b, ys):
    bsz, seq, dm = base.shape
    tm = dest3.shape[2]
    nt = seq // tm
    tile = pl.BlockSpec((1, tm, dm), lambda b, i: (b, i, 0))
    return pl.pallas_call(
        _final_kernel,
        grid=(bsz, nt),
        in_specs=[pl.BlockSpec((1, TOP_K, tm), lambda b, i: (b * nt + i, 0, 0), memory_space=pltpu.SMEM),
                  pl.BlockSpec((tm, TOP_K), lambda b, i: (b * nt + i, 0)),
                  tile, pl.BlockSpec((1, 1, dm), lambda b, i: (b, 0, 0)),
                  pl.BlockSpec((1, dm), lambda b, i: (0, 0)), pl.BlockSpec((1, dm), lambda b, i: (0, 0)),
                  pl.BlockSpec(memory_space=pl.ANY)],
        out_specs=tile,
        out_shape=jax.ShapeDtypeStruct((bsz, seq, dm), F32),
        scratch_shapes=[pltpu.VMEM((TOP_K, tm, ys.shape[1]), jnp.uint32), pltpu.SemaphoreType.DMA(())],
        compiler_params=_cparams(("arbitrary", "arbitrary")),
        name="combine_final_layernorm",
    )(dest3, w_tok, base, g2, ln_g, ln_b, ys)


def _rel_bucket(dist):
    dist = jnp.maximum(dist, 0)
    exact = REL_BUCKETS // 2
    log_ratio = jnp.log(jnp.maximum(dist, 1).astype(F32) / exact) / math.log(REL_MAX_DIST / exact)
    large = jnp.minimum(exact + (log_ratio * (REL_BUCKETS - exact)).astype(jnp.int32), REL_BUCKETS - 1)
    return jnp.where(dist < exact, dist, large)


def _bias_tiles(rel_bias):
    n_d = 4 * SEL_LEN
    vec = rel_bias[_rel_bucket(jnp.arange(n_d))].T
    far = rel_bias[REL_BUCKETS - 1]
    vec = ((vec - far[:, None]) * LOG2E).reshape(N_KV, GQA, n_d)
    tok = np.arange(Q_BLOCK)[None, :]
    key = np.arange(SEL_LEN)[:, None]

    def toeplitz(d):
        vals = vec[:, :, np.clip(d, 0, n_d - 1)]
        vals = jnp.where(jnp.asarray(d >= 0), vals, NEG)
        return jnp.transpose(vals, (0, 2, 1, 3)).reshape(N_KV, d.shape[0], ROWS)

    near = [toeplitz(delta + tok - key) for delta in (0, SEL_LEN, 2 * SEL_LEN)]
    zero = jnp.zeros((N_KV, SEL_LEN, ROWS), F32)
    edge = np.where(tok < key, 0.0, NEG).astype(np.float32)
    edge = jnp.broadcast_to(jnp.asarray(np.tile(edge, (1, GQA)))[None], (N_KV, SEL_LEN, ROWS))
    nb = jnp.stack(near + [zero, jnp.full_like(zero, NEG), edge], axis=1)
    w = np.arange(BAND_ROWS)[:, None]
    bands = []
    for ph in (0, 1):
        d = tok - CMP_STRIDE * (w - PAD_CMP - (Q_BLOCK // CMP_STRIDE) * ph) - (CMP_LEN - 1)
        vals = vec[:, :, np.clip(d, 0, n_d - 1)]
        vals = jnp.where(jnp.asarray(d >= 0), vals, 0.0)
        bands.append(jnp.transpose(vals, (0, 2, 1, 3)).reshape(N_KV, BAND_ROWS, ROWS))
    cb = jnp.stack(bands, axis=1)
    return nb.astype(F32), cb.astype(F32)


def _padded_cmp_rows(seq):
    return -(-(seq // CMP_STRIDE + 2 * PAD_CMP) // LANES) * LANES


def _overlap_t(seq):
    n_cmp = (seq - CMP_LEN) // CMP_STRIDE + 1
    n_blk = seq // SEL_LEN
    ncp = _padded_cmp_rows(seq)
    c_start = np.arange(n_cmp) * CMP_STRIDE
    c_end = c_start + CMP_LEN - 1
    blk = np.arange(n_blk)
    ov = ((c_start[:, None] < (blk[None, :] + 1) * SEL_LEN) & (c_end[:, None] >= blk[None, :] * SEL_LEN))
    out = np.zeros((n_blk, ncp), np.float32)
    out[:, PAD_CMP:PAD_CMP + n_cmp] = ov.T
    return jnp.asarray(out, BF16)


def _s5_params(lam_re, lam_im, log_step, b_re, b_im, c_re, c_im):
    lr, li = lam_re.astype(F32), lam_im.astype(F32)
    dt = jnp.exp(log_step.astype(F32))[:, None]
    mag = jnp.exp(lr * dt)
    ar, ai = mag * jnp.cos(li * dt), mag * jnp.sin(li * dt)
    den = lr * lr + li * li
    kr = ((ar - 1.0) * lr + ai * li) / den
    ki = (ai * lr - (ar - 1.0) * li) / den
    br, bi = b_re.astype(F32), b_im.astype(F32)
    bbr = kr[..., None] * br - ki[..., None] * bi
    bbi = kr[..., None] * bi + ki[..., None] * br
    n_g = lr.shape[0]
    eye = jnp.eye(n_g, dtype=F32)

    def drive(bb):
        return jnp.einsum('gpc,gh->gchp', bb, eye).reshape(n_g * SSM_GROUP, n_g * SSM_STATE)

    def readout(c):
        return jnp.einsum('gcp,gh->gphc', c, eye).reshape(n_g * SSM_STATE, n_g * SSM_GROUP)

    wb = jnp.concatenate([drive(bbr), drive(bbi)], axis=1).astype(BF16)
    wc = jnp.concatenate([readout(c_re.astype(F32)), -readout(c_im.astype(F32))], axis=0).astype(BF16)
    a = jnp.concatenate([ar.reshape(1, -1), ai.reshape(1, -1)], axis=1)
    return wb, jnp.broadcast_to(a, (SUBLANES, a.shape[1])), wc


def _layer(x, mod, w_in, lam_re, lam_im, log_step, b_re, b_im, c_re, c_im, d_skip, glu_w, cmp_pos, cmp_w1,
           cmp_w2, rel_bias, w_out, ln1_g, ln1_b, router_w, router_bias, e_gate, e_up, e_down, sg, su, sd,
           ln2_g, ln2_b, alpha):
    bsz, seq, dm = x.shape
    n_tok = bsz * seq
    sh1, sc1, g1, sh2, sc2, g2 = [m[:, None, :] for m in jnp.split(mod, 6, axis=-1)]
    ssm_w = dm // 2
    attn_w = N_HEADS * HEAD_DIM
    kv_w = N_KV * HEAD_DIM
    n_gate = 3 * N_HEADS
    offs = np.cumsum([0, ssm_w, attn_w] + [kv_w] * 6 + [n_gate, 2 * dm])

    wu = w_in[:, offs[0]:offs[1]].astype(BF16)
    wq = w_in[:, offs[1]:offs[2]] * (HEAD_DIM ** -0.5 * LOG2E)
    wa = jnp.concatenate([wq, w_in[:, offs[2]:offs[4]]], axis=1).astype(BF16)
    wg = jnp.pad(w_in[:, offs[8]:offs[9]], ((0, 0), (0, LANES - n_gate))).astype(BF16)
    wm = w_in[:, offs[9]:offs[10]].astype(BF16)
    nq = seq // Q_BLOCK
    n_blk = seq // SEL_LEN
    nch = seq // CMP_STRIDE

    def head_padded(w, width):
        w = w.reshape(dm, N_KV, HEAD_DIM)
        return jnp.pad(w, ((0, 0), (0, 0), (0, width - HEAD_DIM))).reshape(dm, N_KV * width).astype(BF16)

    aug_w = 2 * HEAD_DIM + -(-n_blk // LANES) * LANES
    u2d, act, gates, mg, kaug, kw, vst, vwt = _input_projection(
        x, sh1, sc1, wu, wa, wg, wm, head_padded(w_in[:, offs[4]:offs[5]], aug_w),
        head_padded(w_in[:, offs[6]:offs[7]], 2 * HEAD_DIM),
        w_in[:, offs[5]:offs[6]].T.astype(BF16), w_in[:, offs[7]:offs[8]].T.astype(BF16))

    def piece(i):
        return act[:, :, attn_w + i * kv_w: attn_w + (i + 1) * kv_w]

    def compress(raw, pos, w1, w2):
        xc = raw.reshape(bsz, nch, CMP_STRIDE, N_KV, HEAD_DIM).transpose(0, 3, 1, 2, 4).reshape(bsz, N_KV, nch, CMP_STRIDE * HEAD_DIM)
        half = CMP_STRIDE * HEAD_DIM
        w1cat = jnp.concatenate([w1[:half], w1[half:]], axis=1).astype(BF16)
        posb = jnp.dot(pos.reshape(1, -1), w1, precision=lax.Precision.HIGHEST)
        return _compress(xc, w1cat, posb, w2.astype(BF16))

    kc = compress(piece(0), cmp_pos[0], cmp_w1[0], cmp_w2[0])
    vc = compress(piece(1), cmp_pos[1], cmp_w1[1], cmp_w2[1])
    pad = ((0, 0), (0, 0), (PAD_CMP, _padded_cmp_rows(seq) - nch - PAD_CMP), (0, 0))
    kcp = jnp.pad(kc, pad).astype(BF16)
    vcpt = jnp.swapaxes(jnp.pad(vc, pad), 2, 3).astype(BF16)

    nst = nq // NSA_TILES
    q = act[:, :, :attn_w].reshape(bsz, nst, NSA_TILES, Q_BLOCK, N_KV, GQA, HEAD_DIM)
    qt = q.transpose(0, 4, 1, 6, 2, 5, 3).reshape(bsz, N_KV, nst, HEAD_DIM, COLS)
    gt =gates[:, :, :n_gate].reshape(bsz, nst, NSA_TILES, Q_BLOCK, N_KV, GQA, 3)
    gt = gt.transpose(0, 4, 1, 6, 2, 5, 3).reshape(bsz, N_KV, nst, 3, COLS)
    gt = jnp.pad(gt, ((0, 0), (0, 0), (0, 0), (0, SUBLANES - 3), (0, 0)))
    nb, cb = _bias_tiles(rel_bias)
    ot = _nsa(qt, kaug, vst, kw, vwt, kcp, vcpt, _overlap_t(seq), nb, cb, gt)
    ot = ot.reshape(bsz, N_KV, nst, HEAD_DIM, NSA_TILES, GQA, Q_BLOCK)
    y_nsa = ot.transpose(0, 2, 4, 6, 1, 5, 3).reshape(bsz, seq, attn_w)

    wb, a, wc = _s5_params(lam_re, lam_im, log_step, b_re, b_im, c_re, c_im)
    z2 = _s5(u2d.reshape(seq * bsz, ssm_w), wb, a, wc, d_skip.reshape(1, ssm_w).astype(F32), bsz)
    z2d = z2.reshape(seq, bsz * ssm_w)

    base, h2 = _mix_out(z2d, y_nsa, mg, x, g1, sh2, sc2, g2, ln1_g.reshape(1, dm), ln1_b.reshape(1, dm),
                        glu_w.astype(BF16), w_out.astype(BF16), sg.astype(BF16), su.astype(BF16), sd.astype(BF16), alpha)
    h2 = h2.reshape(n_tok, dm // 2)

    n_exp = router_w.shape[1]
    rt = min(ROUTER_TILE, n_tok)
    tri = jnp.asarray(np.triu(np.ones((rt, rt), np.float32), 1), BF16)
    top_e, top_w, rank, counts = _router(h2, router_w.T.astype(BF16), router_bias.reshape(n_exp, 1).astype(F32), tri)
    counts = counts[:, 0].astype(jnp.int32)
    padded = (counts + EXPERT_TILE - 1) // EXPERT_TILE * EXPERT_TILE
    pad_end = jnp.cumsum(padded)
    pad_start = pad_end - padded
    dest = _dest_rows(pad_start.astype(jnp.int32), top_e, rank)
    n_rows = n_tok * TOP_K + n_exp * EXPERT_TILE
    n_tiles = n_rows // EXPERT_TILE
    tile_exp = jnp.minimum(jnp.searchsorted(pad_end, jnp.arange(n_tiles) * EXPERT_TILE, side='right'), n_exp - 1).astype(jnp.int32)
    n_used = (pad_end[-1] // EXPERT_TILE).astype(jnp.int32).reshape(1)
    mt = min(MOVE_TILE, seq)
    dest3 = dest.reshape(TOP_K, n_tok // mt, mt).transpose(1, 0, 2)
    xs = _dispatch(pad_end.astype(jnp.int32), padded.astype(jnp.int32), n_used, dest3, h2, n_rows)
    ys = _experts(tile_exp, n_used, xs, e_gate, e_up, e_down)
    return _final(dest3, top_w.T, base, g2, ln2_g.reshape(1, dm), ln2_b.reshape(1, dm), ys)


def kernel(x, c, ada_w, ada_b, w_in, ssm_lambda_re, ssm_lambda_im, ssm_log_step, ssm_b_re, ssm_b_im, ssm_c_re, ssm_c_im, ssm_d, ssm_glu_w, cmp_pos, cmp_w1, cmp_w2, rel_bias, w_out, ln1_g, ln1_b, router_w, router_bias, exp_w_gate, exp_w_up, exp_w_down, sh_w_gate, sh_w_up, sh_w_down, ln2_g, ln2_b):
    depth = ada_w.shape[0]
    alpha = (2 * depth) ** 0.25
    for l in range(depth):
        mod = _modulation(c, ada_w[l], ada_b[l])
        x = _layer(x, mod, w_in[l], ssm_lambda_re[l], ssm_lambda_im[l], ssm_log_step[l], ssm_b_re[l], ssm_b_im[l],
                   ssm_c_re[l], ssm_c_im[l], ssm_d[l], ssm_glu_w[l], cmp_pos[l], cmp_w1[l], cmp_w2[l], rel_bias,
                   w_out[l], ln1_g[l], ln1_b[l], router_w[l], router_bias[l], exp_w_gate[l], exp_w_up[l],
                   exp_w_down[l], sh_w_gate[l], sh_w_up[l], sh_w_down[l], ln2_g[l], ln2_b[l], alpha)
    return x
```

```python
import functools
import math

import numpy as np
import jax
import jax.numpy as jnp
from jax import lax
from jax.experimental import pallas as pl
from jax.experimental.pallas import tpu as pltpu

F32 = jnp.float32
BF16 = jnp.bfloat16

SSM_GROUP = 16
SSM_STATE = 64
N_HEADS = 16
HEAD_DIM = 64
N_KV = 4
GQA = N_HEADS // N_KV
CMP_LEN = 32
CMP_STRIDE = 16
SEL_LEN = 64
SEL_TOP = 16
WINDOW = 512
Q_BLOCK = 64
FORCE_BONUS = 1.0e4
REL_BUCKETS = 32
REL_MAX_DIST = 128
TOP_K = 8
ROUTED_SCALE = 2.5
LN_EPS = 1e-5

LANES = 128
SUBLANES = 8
VMEM_LIMIT_BYTES = 56 * 1024 * 1024
ROW_TILE = 512
ROUTER_TILE = 256
EXPERT_TILE = 256
MOVE_TILE = 256
DEST_TILE = 2048
SCAN_CHUNK = 128
SCAN_LANES = 512
MASK_BIG = 32768.0
NEG = -1.0e30
LOG2E = math.log2(math.e)
WIN_BLOCKS = WINDOW // SEL_LEN
ROWS = GQA * Q_BLOCK
NSA_TILES = 4
COLS = NSA_TILES * ROWS
CHUNK_BLOCKS = 8
CHUNK_KEYS = CHUNK_BLOCKS * SEL_LEN
FAR_BLOCKS = NSA_TILES
FAR_KEYS = FAR_BLOCKS * SEL_LEN
PAD_CMP = 8
BAND_ROWS = 24


def _cparams(sem):
    return pltpu.CompilerParams(dimension_semantics=sem, vmem_limit_bytes=VMEM_LIMIT_BYTES)


def _pack_rows(x):
    n = x.shape[1] // 2
    xb = x.astype(jnp.bfloat16).astype(F32)
    lo = lax.shift_right_logical(lax.bitcast_convert_type(xb[:, :n], jnp.uint32), jnp.uint32(16))
    return lax.bitcast_convert_type(xb[:, n:], jnp.uint32) | lo


def _unpack_rows(w):
    lo = lax.bitcast_convert_type(lax.shift_left(w, jnp.uint32(16)), F32)
    hi = lax.bitcast_convert_type(w & jnp.uint32(0xFFFF0000), F32)
    return lo, hi


def _norm_rows(x):
    mu = jnp.mean(x, axis=-1, keepdims=True)
    xc = x - mu
    var = jnp.mean(xc * xc, axis=-1, keepdims=True)
    return xc * lax.rsqrt(var + LN_EPS)


def _mod_kernel(c_ref, w_ref, b_ref, o_ref):
    cond = jax.nn.silu(c_ref[...])
    o_ref[...] = jnp.dot(cond.astype(BF16), w_ref[...].astype(BF16), preferred_element_type=F32) + b_ref[...]


def _modulation(c, ada_w, ada_b):
    bsz, dm = c.shape
    n = ada_w.shape[1]
    tn = dm
    return pl.pallas_call(
        _mod_kernel,
        grid=(n // tn,),
        in_specs=[pl.BlockSpec((bsz, dm), lambda j: (0, 0)),
                  pl.BlockSpec((dm, tn), lambda j: (0, j)),
                  pl.BlockSpec((1, tn), lambda j: (0, j))],
        out_specs=pl.BlockSpec((bsz, tn), lambda j: (0, j)),
        out_shape=jax.ShapeDtypeStruct((bsz, n), F32),
        compiler_params=_cparams(("arbitrary",)),
        name="ada_modulation",
    )(c, ada_w, ada_b.reshape(1, n))


def _inproj_kernel(x_ref, sh_ref, sc_ref, wu_ref, wa_ref, wg_ref, wm_ref, wks_ref, wkw_ref, wvs_ref, wvw_ref,
                   u_ref, a_ref, g_ref, m_ref, ks_ref, kw_ref, vs_ref, vw_ref, *, aug_w):
    h = _norm_rows(x_ref[0]) * (1.0 + sc_ref[0]) + sh_ref[0]
    hb = h.astype(BF16)
    tm = hb.shape[0]
    u_ref[...] = jnp.dot(hb, wu_ref[...], preferred_element_type=F32)
    a_ref[0] = jnp.dot(hb, wa_ref[...], preferred_element_type=F32).astype(BF16)
    g_ref[0] = jax.nn.sigmoid(jnp.dot(hb, wg_ref[...], preferred_element_type=F32))
    m_ref[0] = jax.nn.sigmoid(jnp.dot(hb, wm_ref[...], preferred_element_type=F32)).astype(BF16)
    ks = jnp.dot(hb, wks_ref[...], preferred_element_type=F32)
    assert aug_w & (aug_w - 1) == 0
    col = (lax.broadcasted_iota(jnp.int32, ks.shape, 1) & (aug_w - 1)) - 2 * HEAD_DIM
    blk = jnp.right_shift(pl.program_id(1) * tm + lax.broadcasted_iota(jnp.int32, ks.shape, 0), SEL_LEN.bit_length() - 1)
    ks_ref[0] = jnp.where(col == blk, -MASK_BIG, ks).astype(BF16)
    kw_ref[0] = jnp.dot(hb, wkw_ref[...], preferred_element_type=F32).astype(BF16)
    rows = lax.broadcasted_iota(jnp.int32, (2 * SUBLANES, tm), 0)
    ones_rows = jnp.where(rows == 0, 1.0, 0.0).astype(BF16)
    for w_ref, v_ref in ((wvs_ref, vs_ref), (wvw_ref, vw_ref)):
        vt = lax.dot_general(w_ref[...], hb, (((1,), (1,)), ((), ())), preferred_element_type=F32).astype(BF16)
        for g in range(N_KV):
            v_ref[0, g] = jnp.concatenate([vt[g * HEAD_DIM:(g + 1) * HEAD_DIM], ones_rows], axis=0)


def _input_projection(x, sh1, sc1, wu, wa, wg, wm, wks, wkw, wvs_t, wvw_t):
    bsz, seq, dm = x.shape
    tm = min(ROW_TILE, seq)
    nu, na, ng, nm = wu.shape[1], wa.shape[1], wg.shape[1], wm.shape[1]
    full = lambda w: pl.BlockSpec(w.shape, lambda b, i: (0, 0))
    vec = pl.BlockSpec((1, 1, dm), lambda b, i: (b, 0, 0))
    rows_out = lambda n: pl.BlockSpec((1, tm, n), lambda b, i: (b, i, 0))
    vt_rows = HEAD_DIM + 2 * SUBLANES
    vt_out = pl.BlockSpec((1, N_KV, vt_rows, tm), lambda b, i: (b, 0, 0, i))
    return pl.pallas_call(
        functools.partial(_inproj_kernel, aug_w=wks.shape[1] // N_KV),
        grid=(bsz, seq // tm),
        in_specs=[pl.BlockSpec((1, tm, dm), lambda b, i: (b, i, 0)), vec, vec,
                  full(wu), full(wa), full(wg), full(wm), full(wks), full(wkw), full(wvs_t), full(wvw_t)],
        out_specs=[pl.BlockSpec((tm, nu), lambda b, i: (i, b)), rows_out(na), rows_out(ng), rows_out(nm),
                   rows_out(wks.shape[1]), rows_out(wkw.shape[1]), vt_out, vt_out],
        out_shape=[jax.ShapeDtypeStruct((seq, bsz * nu), F32),
                   jax.ShapeDtypeStruct((bsz, seq, na), BF16),
                   jax.ShapeDtypeStruct((bsz, seq, ng), F32),
                   jax.ShapeDtypeStruct((bsz, seq, nm), BF16),
                   jax.ShapeDtypeStruct((bsz, seq, wks.shape[1]), BF16),
                   jax.ShapeDtypeStruct((bsz, seq, wkw.shape[1]), BF16),
                   jax.ShapeDtypeStruct((bsz, N_KV, vt_rows, seq), BF16),
                   jax.ShapeDtypeStruct((bsz, N_KV, vt_rows, seq), BF16)],
        compiler_params=_cparams(("arbitrary", "arbitrary")),
        name="adaln_input_projection",
    )(x, sh1, sc1, wu, wa, wg, wm, wks, wkw, wvs_t, wvw_t)


def _compress_kernel(x_ref, w1_ref, pb_ref, w2_ref, o_ref, *, n_cmp):
    hid = w2_ref.shape[0]
    p = jnp.dot(x_ref[0, 0], w1_ref[...], preferred_element_type=F32)
    nrow = p.shape[0]
    nxt = pltpu.roll(p[:, hid:], nrow - 1, 0)
    hidv = jax.nn.gelu(p[:, :hid] + nxt + pb_ref[...])
    out = jnp.dot(hidv.astype(BF16), w2_ref[...], preferred_element_type=F32)
    rows = lax.broadcasted_iota(jnp.int32, out.shape, 0)
    o_ref[0, 0] = jnp.where(rows < n_cmp, out, 0.0)


def _compress(xc, w1cat, posb, w2):
    bsz, nkv, nch, kdim = xc.shape
    hid2 = w1cat.shape[1]
    return pl.pallas_call(
        functools.partial(_compress_kernel, n_cmp=nch - 1),
        grid=(bsz, nkv),
        in_specs=[pl.BlockSpec((1, 1, nch, kdim), lambda b, g: (b, g, 0, 0)),
                  pl.BlockSpec((kdim, hid2), lambda b, g: (0, 0)),
                  pl.BlockSpec((1, hid2 // 2), lambda b, g: (0, 0)),
                  pl.BlockSpec((hid2 // 2, HEAD_DIM), lambda b, g: (0, 0))],
        out_specs=pl.BlockSpec((1, 1, nch, HEAD_DIM), lambda b, g: (b, g, 0, 0)),
        out_shape=jax.ShapeDtypeStruct((bsz, nkv, nch, HEAD_DIM), F32),
        compiler_params=_cparams(("arbitrary", "arbitrary")),
        name="kv_compress",
    )(xc, w1cat, posb, w2)


def _swap_heads_tokens(a):
    t = jnp.concatenate([a, jnp.zeros_like(a)], axis=0).T
    p = [t[r * HEAD_DIM:(r + 1) * HEAD_DIM] for r in range(GQA)]
    return jnp.concatenate([p[r] + pltpu.roll(p[r + 1], Q_BLOCK, 1) for r in range(0, GQA, 2)], axis=1)


def _nsa_kernel(q_ref, kaug_ref, vst_ref, kw_ref, vwt_ref, kcp_ref, vcpt_ref, ovt_ref, nb_ref, cb_ref,
                g_ref, o_ref, s_ref, c_ref, w_ref, fa_ref, fb_ref, *, n_blk, n_cmp, n_sel):
    q0 = pl.program_id(2) * NSA_TILES
    qf = q_ref[0].astype(F32)
    qt = jnp.concatenate([_swap_heads_tokens(qf[n * Q_BLOCK:(n + 1) * Q_BLOCK]) for n in range(NSA_TILES)],
                         axis=1).astype(BF16)
    col_tile = lambda n: slice(n * ROWS, (n + 1) * ROWS)

    s_ref[...] = jnp.dot(kcp_ref[0, 0], qt, preferred_element_type=F32)
    for n in range(NSA_TILES):
        band = pl.multiple_of(SUBLANES * ((q0 + n) // 2), SUBLANES)
        s_ref[pl.ds(band, BAND_ROWS), col_tile(n)] += cb_ref[0, n % 2]
    s = s_ref[...]
    rho = lax.broadcasted_iota(jnp.int32, s.shape, 0)
    col = lax.broadcasted_iota(jnp.int32, s.shape, 1)
    tok = col & (Q_BLOCK - 1)
    qblk = q0 + jnp.right_shift(col, ROWS.bit_length() - 1)
    first_visible = CMP_STRIDE * (rho - PAD_CMP) + (CMP_LEN - 1) - Q_BLOCK * qblk
    vis = (rho >= PAD_CMP) & (rho < PAD_CMP + n_cmp) & (first_visible <= tok)
    s = jnp.where(vis, s, -jnp.inf)
    mx = jnp.max(s, axis=0, keepdims=True)
    mx = jnp.where(mx == -jnp.inf, 0.0, mx)
    e = jnp.exp2(s - mx)
    p_cmp = (e / jnp.maximum(jnp.sum(e, axis=0, keepdims=True), 1e-30)).astype(BF16)
    both = jnp.dot(jnp.concatenate([vcpt_ref[0, 0], ovt_ref[...]], axis=0), p_cmp, preferred_element_type=F32)
    o_cmp = both[:HEAD_DIM]

    imp4 = both[HEAD_DIM:]
    sums = []
    for n in range(NSA_TILES):
        two = imp4[:, n * ROWS:n * ROWS + LANES] + imp4[:, n * ROWS + LANES:(n + 1) * ROWS]
        sums.append(two + pltpu.roll(two, Q_BLOCK, 1))
    low = lax.broadcasted_iota(jnp.int32, sums[0].shape, 1) < Q_BLOCK
    imp = jnp.concatenate([jnp.where(low, sums[n], sums[n + 1]) for n in range(0, NSA_TILES, 2)], axis=1)
    blk = lax.broadcasted_iota(jnp.int32, imp.shape, 0)
    cur = q0 + jnp.right_shift(lax.broadcasted_iota(jnp.int32, imp.shape, 1), Q_BLOCK.bit_length() - 1)
    blkf = blk.astype(F32)
    forced = (blk == 0) | (blk == cur) | (blk == cur - 1)
    score = jnp.where(blk <= cur, imp + jnp.where(forced, FORCE_BONUS, 0.0), -jnp.inf)
    notsel = jnp.ones(imp.shape, F32)
    for _ in range(n_sel):
        best = jnp.max(score, axis=0, keepdims=True)
        first = jnp.min(jnp.where(score == best, blkf, float(n_blk)), axis=0, keepdims=True)
        pick = blkf == first
        notsel = jnp.where(pick, 0.0, notsel)
        score = jnp.where(pick, -jnp.inf, score)
    halves = []
    for n in range(0, NSA_TILES, 2):
        pair = notsel[:, (n // 2) * LANES:(n // 2 + 1) * LANES]
        swapped = pltpu.roll(pair, Q_BLOCK, 1)
        halves += [jnp.where(low, pair, swapped), jnp.where(low, swapped, pair)]
    notsel = jnp.concatenate([h for h in halves for _ in (0, 1)], axis=1).astype(BF16)
    qwin = jnp.concatenate([qt, jnp.zeros_like(qt)], axis=0)
    tail = kaug_ref.shape[2] - 2 * HEAD_DIM - n_blk
    qaug = jnp.concatenate([qwin, notsel] + ([jnp.zeros((tail, COLS), notsel.dtype)] if tail else []), axis=0)

    def update(carry, sc, vt, top=None):
        m, acc = carry
        m_new = jnp.maximum(m, jnp.max(sc, axis=0, keepdims=True) if top is None else top)
        p = jnp.exp2((sc - m_new).astype(BF16))
        return m_new, jnp.exp2(m - m_new) * acc + jnp.dot(vt, p, preferred_element_type=F32)

    def chunk_keys(c):
        return pl.ds(pl.multiple_of(c * FAR_KEYS, FAR_KEYS), FAR_KEYS)

    def far_scores(c, buf_ref):
        sc = jnp.dot(kaug_ref[0, chunk_keys(c), :], qaug, preferred_element_type=F32)
        buf_ref[...] = sc
        return jnp.max(sc, axis=0, keepdims=True)

    def far_pair(i, carry):
        m, acc, top_a = carry
        top_b = far_scores(2 * i + 1, fb_ref)
        m, acc = update((m, acc), fa_ref[...], vst_ref[0, 0, :, chunk_keys(2 * i)], top_a)
        top_a = far_scores(2 * i + 2, fa_ref)
        m, acc = update((m, acc), fb_ref[...], vst_ref[0, 0, :, chunk_keys(2 * i + 1)], top_b)
        return m, acc, top_a

    def far_last(_, carry, n_far):
        m, acc, top_a = carry
        return update((m, acc), fa_ref[...], vst_ref[0, 0, :, chunk_keys(n_far - 1)], top_a) + (top_a,)

    def near_chunk(k_ref, vt_ref, qmat, blk0, tile_of, carry, buf_ref):
        n_keys = buf_ref.shape[0]
        keys = pl.ds(pl.multiple_of(blk0 * SEL_LEN, NSA_TILES * SEL_LEN), n_keys)
        buf_ref[...] = jnp.dot(k_ref[0, keys, :], qmat, preferred_element_type=F32)
        for o in range(n_keys // SEL_LEN):
            for n in range(NSA_TILES):
                buf_ref[o * SEL_LEN:(o + 1) * SEL_LEN, col_tile(n)] += nb_ref[0, tile_of(blk0 + o, q0 + n)]
        return update(carry, buf_ref[...], vt_ref[0, 0, :, keys])

    def sel_tile(j, qi):
        d = qi - j
        return jnp.where(d < 0, 4, jnp.minimum(d, 3))

    def win_tile(j, qi):
        d = qi - j
        return jnp.where((d < 0) | (d > WIN_BLOCKS), 4, jnp.where(d == WIN_BLOCKS, 5, jnp.minimum(d, 3)))

    init = (jnp.full((1, COLS), NEG, F32), jnp.zeros((vst_ref.shape[2], COLS), F32))
    n_far = jnp.maximum(q0 - 2, 0) // FAR_BLOCKS
    carry = lax.fori_loop(0, n_far // 2, far_pair, init + (far_scores(0, fa_ref),))
    carry = lax.fori_loop(0, n_far % 2, functools.partial(far_last, n_far=n_far), carry)
    _, acc = near_chunk(kaug_ref, vst_ref, qaug, n_far * FAR_BLOCKS, sel_tile, carry[:2], c_ref)
    o_slc = acc[:HEAD_DIM] / acc[HEAD_DIM:HEAD_DIM + 1]
    _, acc = near_chunk(kw_ref, vwt_ref, qwin, jnp.maximum(q0 - WIN_BLOCKS, 0), win_tile, init, w_ref)
    o_win = acc[:HEAD_DIM] / acc[HEAD_DIM:HEAD_DIM + 1]
    g = g_ref[0, 0, 0]
    out = g[0:1] * o_cmp + g[1:2] * o_slc + g[2:3] * o_win
    o_ref[0] = jnp.concatenate([_swap_heads_tokens(out[:, col_tile(n)]) for n in range(NSA_TILES)],
                               axis=0).astype(BF16)


def _nsa(act, kaug, vst, kw, vwt, kcp, vcpt, ovt, nb, cb, gt):
    bsz, nkv, nsteps, _, cols = gt.shape
    dh = HEAD_DIM
    seq = kw.shape[1]
    step_tokens = NSA_TILES * Q_BLOCK
    q_blk = pl.BlockSpec((1, step_tokens, GQA * dh), lambda b, g, i: (b, i, g))
    n_blk = seq // SEL_LEN
    ncp = kcp.shape[2]
    n_cmp = (seq - CMP_LEN) // CMP_STRIDE + 1
    assert n_blk % CHUNK_BLOCKS == 0 and n_blk >= WIN_BLOCKS + NSA_TILES and CHUNK_BLOCKS == FAR_BLOCKS + NSA_TILES
    per_bg = lambda shape: pl.BlockSpec((1, 1) + shape, lambda b, g, i: (b, g, 0, 0))
    per_step = lambda shape: pl.BlockSpec((1, 1, 1) + shape, lambda b, g, i: (b, g, i, 0, 0))
    head_cols = lambda a: pl.BlockSpec((1, seq, a.shape[2] // nkv), lambda b, g, i: (b, 0, g))
    kern = functools.partial(_nsa_kernel, n_blk=n_blk, n_cmp=n_cmp, n_sel=min(SEL_TOP, n_blk))
    return pl.pallas_call(
        kern,
        grid=(bsz, nkv, nsteps),
        in_specs=[q_blk,
                  head_cols(kaug), per_bg((vst.shape[2], seq)), head_cols(kw), per_bg((vwt.shape[2], seq)),
                  per_bg((ncp, dh)), per_bg((dh, ncp)),
                  pl.BlockSpec(ovt.shape, lambda b, g, i: (0, 0)),
                  pl.BlockSpec((1,) + nb.shape[1:], lambda b, g, i: (g, 0, 0, 0)),
                  pl.BlockSpec((1,) + cb.shape[1:], lambda b, g, i: (g, 0, 0, 0)),
                  per_step((SUBLANES, cols))],
        out_specs=q_blk,
        out_shape=jax.ShapeDtypeStruct((bsz, seq, nkv * GQA * dh), BF16),
        scratch_shapes=[pltpu.VMEM((ncp, cols), F32), pltpu.VMEM((CHUNK_KEYS, cols), F32),
                        pltpu.VMEM(((WIN_BLOCKS + NSA_TILES) * SEL_LEN, cols), F32),
                        pltpu.VMEM((FAR_KEYS, cols), F32), pltpu.VMEM((FAR_KEYS, cols), F32)],
        compiler_params=_cparams(("arbitrary", "arbitrary", "arbitrary")),
        name="nsa_attention",
    )(act, kaug, vst, kw, vwt, kcp, vcpt, ovt, nb, cb, gt)


def _s5_kernel(u_ref, wb_ref, a_ref, wc_ref, d_ref, z_ref, xs_ref, st_ref, *, bsz, n_state):
    @pl.when(pl.program_id(0) == 0)
    def _():
        st_ref[...] = jnp.zeros_like(st_ref)

    u = u_ref[...]
    xs_ref[...] = jnp.dot(u.astype(BF16), wb_ref[...], preferred_element_type=F32)
    steps = u.shape[0] // bsz
    for c0 in range(0, n_state, SCAN_LANES):
        re = pl.ds(c0, SCAN_LANES)
        im = pl.ds(n_state + c0, SCAN_LANES)
        ar = jnp.broadcast_to(a_ref[0:1, re], (bsz, SCAN_LANES))
        ai = jnp.broadcast_to(a_ref[0:1, im], (bsz, SCAN_LANES))

        def step(t, carry):
            xr, xi = carry
            rows = pl.ds(pl.multiple_of(t * bsz, bsz), bsz)
            nr = ar * xr - ai * xi + xs_ref[rows, re]
            ni = ar * xi + ai * xr + xs_ref[rows, im]
            xs_ref[rows, re] = nr
            xs_ref[rows, im] = ni
            return nr, ni

        xr, xi = lax.fori_loop(0, steps, step, (st_ref[:, re], st_ref[:, im]), unroll=8)
        st_ref[:, re] = xr
        st_ref[:, im] = xi
    y = jnp.dot(xs_ref[...].astype(BF16), wc_ref[...], preferred_element_type=F32) + d_ref[...] * u
    z_ref[...] = jax.nn.gelu(y).astype(BF16)


def _s5(u2, wb, a, wc, dsk, bsz):
    rows, width = u2.shape
    seq = rows // bsz
    chunk = min(SCAN_CHUNK, seq)
    n_state2 = wb.shape[1]
    return pl.pallas_call(
        functools.partial(_s5_kernel, bsz=bsz, n_state=n_state2 // 2),
        grid=(seq // chunk,),
        in_specs=[pl.BlockSpec((chunk * bsz, width), lambda i: (i, 0)),
                  pl.BlockSpec(wb.shape, lambda i: (0, 0)),
                  pl.BlockSpec(a.shape, lambda i: (0, 0)),
                  pl.BlockSpec(wc.shape, lambda i: (0, 0)),
                  pl.BlockSpec(dsk.shape, lambda i: (0, 0))],
        out_specs=pl.BlockSpec((chunk * bsz, width), lambda i: (i, 0)),
        out_shape=jax.ShapeDtypeStruct((rows, width), BF16),
        scratch_shapes=[pltpu.VMEM((chunk * bsz, n_state2), F32), pltpu.VMEM((bsz, n_state2), F32)],
        compiler_params=_cparams(("arbitrary",)),
        name="s5_scan",
    )(u2, wb, a, wc, dsk)


def _mixout_kernel(z_ref, yn_ref, mg_ref, x_ref, g1_ref, sh2_ref, sc2_ref, g2_ref, lg_ref, lb_ref,
                   glu_ref, wo_ref, sg_ref, su_ref, sd_ref, base_ref, h_ref, *, alpha):
    dm = x_ref.shape[2]
    glu = jnp.dot(z_ref[...], glu_ref[...], preferred_element_type=F32)
    y_ssm = glu[:, :dm] * jax.nn.sigmoid(glu[:, dm:])
    mg = mg_ref[0].astype(F32)
    merged = mg[:, :dm] * y_ssm + mg[:, dm:] * yn_ref[0].astype(F32)
    y = jnp.dot(merged.astype(BF16), wo_ref[...], preferred_element_type=F32)
    x1 = _norm_rows(alpha * x_ref[0] + g1_ref[0] * y) * lg_ref[...] + lb_ref[...]
    hf = _norm_rows(x1) * (1.0 + sc2_ref[0]) + sh2_ref[0]
    h_ref[0] = _pack_rows(hf)
    h = hf.astype(BF16)
    hs =jax.nn.silu(jnp.dot(h, sg_ref[...], preferred_element_type=F32)) * jnp.dot(h, su_ref[...], preferred_element_type=F32)
    shared = jnp.dot(hs.astype(BF16), sd_ref[...], preferred_element_type=F32)
    base_ref[0] = alpha * x1 + g2_ref[0] * shared


def _mix_out(z2d, y_nsa, mg, x, g1, sh2, sc2, g2, ln_g, ln_b, glu_w, w_out, sg, su, sd, alpha):
    bsz, seq, dm = x.shape
    tm = min(ROW_TILE, seq)
    width = z2d.shape[1] // bsz
    vec = pl.BlockSpec((1, 1, dm), lambda b, i: (b, 0, 0))
    row = pl.BlockSpec((1, dm), lambda b, i: (0, 0))
    full = lambda w: pl.BlockSpec(w.shape, lambda b, i: (0, 0))
    tile = lambda n: pl.BlockSpec((1, tm, n), lambda b, i: (b, i, 0))
    return pl.pallas_call(
        functools.partial(_mixout_kernel, alpha=alpha),
        grid=(bsz, seq // tm),
        in_specs=[pl.BlockSpec((tm, width), lambda b, i: (i, b)), tile(dm), tile(2 * dm), tile(dm),
                  vec, vec, vec, vec, row, row, full(glu_w), full(w_out), full(sg), full(su), full(sd)],
        out_specs=[tile(dm), tile(dm // 2)],
        out_shape=[jax.ShapeDtypeStruct((bsz, seq, dm), F32), jax.ShapeDtypeStruct((bsz, seq, dm // 2), jnp.uint32)],
        compiler_params=_cparams(("arbitrary", "arbitrary")),
        name="merge_outproj_ln_shared",
    )(z2d, y_nsa, mg, x, g1, sh2, sc2, g2, ln_g, ln_b, glu_w, w_out, sg, su, sd)


def _router_kernel(h_ref, rwt_ref, rb_ref, tri_ref, e_ref, w_ref, p_ref, cnt_ref):
    @pl.when(pl.program_id(0) == 0)
    def _():
        cnt_ref[...] = jnp.zeros_like(cnt_ref)

    h = jnp.concatenate(_unpack_rows(h_ref[...]), axis=1).astype(BF16)
    logits = lax.dot_general(rwt_ref[...], h, (((1,), (1,)), ((), ())), preferred_element_type=F32)
    scores = jax.nn.sigmoid(logits)
    cur = scores + rb_ref[...]
    n_exp = scores.shape[0]
    eid = lax.broadcasted_iota(jnp.int32, scores.shape, 0).astype(F32)
    chosen = jnp.zeros(scores.shape, F32)
    ids, vals = [], []
    for _ in range(TOP_K):
        best = jnp.max(cur, axis=0, keepdims=True)
        first = jnp.min(jnp.where(cur == best, eid, float(n_exp)), axis=0, keepdims=True)
        pick = eid == first
        ids.append(first)
        vals.append(jnp.sum(jnp.where(pick, scores, 0.0), axis=0, keepdims=True))
        chosen = jnp.where(pick, 1.0, chosen)
        cur = jnp.where(pick, -jnp.inf, cur)
    top_s = jnp.concatenate(vals, axis=0)
    w_ref[...] = top_s / jnp.sum(top_s, axis=0, keepdims=True) * ROUTED_SCALE
    top_e = jnp.concatenate(ids, axis=0)
    e_ref[...] = top_e.astype(jnp.int32)
    before = jnp.dot(chosen.astype(BF16), tri_ref[...], preferred_element_type=F32) + cnt_ref[...]
    ranks = [jnp.sum(jnp.where(eid == ids[k], before, 0.0), axis=0, keepdims=True) for k in range(TOP_K)]
    p_ref[...] = jnp.concatenate(ranks, axis=0).astype(jnp.int32)
    cnt_ref[...] += jnp.sum(chosen, axis=1, keepdims=True)


def _router(h2, rwt, rb, tri):
    n_tok, words = h2.shape
    n_exp, dm = rwt.shape
    tm = tri.shape[0]
    kt = pl.BlockSpec((TOP_K, tm), lambda i: (0, i))
    return pl.pallas_call(
        _router_kernel,
        grid=(n_tok // tm,),
        in_specs=[pl.BlockSpec((tm, words), lambda i: (i, 0)),
                  pl.BlockSpec((n_exp, dm), lambda i: (0, 0)),
                  pl.BlockSpec((n_exp, 1), lambda i: (0, 0)),
                  pl.BlockSpec((tm, tm), lambda i: (0, 0))],
        out_specs=[kt, kt, kt, pl.BlockSpec((n_exp, 1), lambda i: (0, 0))],
        out_shape=[jax.ShapeDtypeStruct((TOP_K, n_tok), jnp.int32),
                   jax.ShapeDtypeStruct((TOP_K, n_tok), F32),
                   jax.ShapeDtypeStruct((TOP_K, n_tok), jnp.int32),
                   jax.ShapeDtypeStruct((n_exp, 1), F32)],
        compiler_params=_cparams(("arbitrary",)),
        name="router_topk_rank",
    )(h2, rwt, rb, tri)


def _dest_kernel(start_ref, e_ref, r_ref, o_ref):
    e = e_ref[...]
    start = lax.fori_loop(0, start_ref.shape[0], lambda j, acc: jnp.where(e == j, start_ref[j], acc),
                          jnp.zeros(e.shape, jnp.int32))
    o_ref[...] = start + r_ref[...]


def _dest_rows(pad_start, top_e, rank):
    n_tok = top_e.shape[1]
    tm = min(DEST_TILE, n_tok)
    blk = pl.BlockSpec((TOP_K, tm), lambda i, ps: (0, i))
    return pl.pallas_call(
        _dest_kernel,
        grid_spec=pltpu.PrefetchScalarGridSpec(num_scalar_prefetch=1, grid=(n_tok // tm,), in_specs=[blk, blk], out_specs=blk),
        out_shape=jax.ShapeDtypeStruct(top_e.shape, jnp.int32),
        compiler_params=_cparams(("arbitrary",)),
        name="moe_dest_rows",
    )(pad_start, top_e, rank)


def _row_copy(src_ref, src_row, dst_ref, dst_row, sem):
    return pltpu.make_async_copy(src_ref.at[pl.ds(src_row, 1), :], dst_ref.at[pl.ds(dst_row, 1), :], sem)


def _dispatch_kernel(pend_ref, padded_ref, nused_ref, dest_ref, h_ref, xs_ref, zero_ref, sem, zsem):
    tm = h_ref.shape[0]

    @pl.when(pl.program_id(0) == 0)
    def _():
        zero_ref[...] = jnp.zeros_like(zero_ref)
        n_tiles = xs_ref.shape[0] // EXPERT_TILE

        def zero_tile(row):
            return pltpu.make_async_copy(zero_ref, xs_ref.at[pl.ds(pl.multiple_of(row, EXPERT_TILE), EXPERT_TILE), :], zsem)

        def per_expert(act):
            def body(e, carry):
                @pl.when(padded_ref[e] > 0)
                def _():
                    act(zero_tile(pend_ref[e] - EXPERT_TILE))
                return carry
            lax.fori_loop(0, pend_ref.shape[0], body, 0)

        def per_unused(act):
            def body(i, carry):
                act(zero_tile(i * EXPERT_TILE))
                return carry
            lax.fori_loop(nused_ref[0], n_tiles, body, 0)

        for loop in (per_expert, per_unused):
            loop(lambda copy: copy.start())
        for loop in (per_expert, per_unused):
            loop(lambda copy: copy.wait())

    def issue(t8, carry):
        base = pl.multiple_of(t8 * SUBLANES, SUBLANES)
        for s in range(SUBLANES):
            for k in range(TOP_K):
                _row_copy(h_ref, base + s, xs_ref, dest_ref[0, k, base + s], sem).start()
        return carry

    lax.fori_loop(0, tm // SUBLANES, issue, 0)
    pltpu.make_async_copy(xs_ref.at[pl.ds(0, TOP_K * tm), :], xs_ref.at[pl.ds(0, TOP_K * tm), :], sem).wait()


def _dispatch(pad_end, padded, n_used, dest3, h2, n_rows):
    n_tok, words = h2.shape
    tm = dest3.shape[2]
    grid_spec = pltpu.PrefetchScalarGridSpec(
        num_scalar_prefetch=3,
        grid=(n_tok // tm,),
        in_specs=[pl.BlockSpec((1, TOP_K, tm), lambda i, *_: (i, 0, 0), memory_space=pltpu.SMEM),
                  pl.BlockSpec((tm, words), lambda i, *_: (i, 0))],
        out_specs=pl.BlockSpec(memory_space=pl.ANY),
        scratch_shapes=[pltpu.VMEM((EXPERT_TILE, words), jnp.uint32), pltpu.SemaphoreType.DMA(()),
                        pltpu.SemaphoreType.DMA(())],
    )
    return pl.pallas_call(
        _dispatch_kernel,
        grid_spec=grid_spec,
        out_shape=jax.ShapeDtypeStruct((n_rows, words), jnp.uint32),
        compiler_params=_cparams(("arbitrary",)),
        name="moe_dispatch",
    )(pad_end, padded, n_used, dest3, h2)


def _expert_kernel(te_ref, nu_ref, x_ref, wg_ref, wu_ref, wd_ref, y_ref):
    i = pl.program_id(0)

    @pl.when(i < nu_ref[0])
    def _():
        lo, hi = _unpack_rows(x_ref[...])
        lo, hi = lo.astype(BF16), hi.astype(BF16)
        half = lo.shape[1]

        def proj(w_ref):
            return (jnp.dot(lo, w_ref[0, :half, :].astype(BF16), preferred_element_type=F32)
                    + jnp.dot(hi, w_ref[0, half:, :].astype(BF16), preferred_element_type=F32))

        hmid = jax.nn.silu(proj(wg_ref)) * proj(wu_ref)
        y_ref[...] = _pack_rows(jnp.dot(hmid.astype(BF16), wd_ref[0].astype(BF16), preferred_element_type=F32))

    @pl.when(i >= nu_ref[0])
    def _():
        y_ref[...] = jnp.zeros_like(y_ref)


def _experts(tile_exp, n_used, xs, wg, wu, wd):
    n_rows, words = xs.shape
    dm, de = wg.shape[1], wg.shape[2]
    grid_spec = pltpu.PrefetchScalarGridSpec(
        num_scalar_prefetch=2,
        grid=(n_rows // EXPERT_TILE,),
        in_specs=[pl.BlockSpec((EXPERT_TILE, words), lambda i, te, nu: (i, 0)),
                  pl.BlockSpec((1, dm, de), lambda i, te, nu: (te[i], 0, 0)),
                  pl.BlockSpec((1, dm, de), lambda i, te, nu: (te[i], 0, 0)),
                  pl.BlockSpec((1, de, dm), lambda i, te, nu: (te[i], 0, 0))],
        out_specs=pl.BlockSpec((EXPERT_TILE, words), lambda i, te, nu: (i, 0)),
    )
    return pl.pallas_call(
        _expert_kernel,
        grid_spec=grid_spec,
        out_shape=jax.ShapeDtypeStruct((n_rows, words), jnp.uint32),
        compiler_params=_cparams(("arbitrary",)),
        name="expert_mlp",
    )(tile_exp, n_used, xs, wg, wu, wd)


def _final_kernel(dest_ref, w_ref, base_ref, g2_ref, lg_ref, lb_ref, ys_ref, o_ref, buf_ref, sem):
    tm = base_ref.shape[1]

    def issue(t8, carry):
        base = pl.multiple_of(t8 * SUBLANES, SUBLANES)
        for s in range(SUBLANES):
            for k in range(TOP_K):
                _row_copy(ys_ref, dest_ref[0, k, base + s], buf_ref.at[k], base + s, sem).start()
        return carry

    lax.fori_loop(0, tm // SUBLANES, issue, 0)
    pltpu.make_async_copy(buf_ref, buf_ref, sem).wait()
    w = w_ref[...]
    lo_sum, hi_sum = jnp.zeros((tm, buf_ref.shape[2]), F32), jnp.zeros((tm, buf_ref.shape[2]), F32)
    for k in range(TOP_K):
        lo, hi = _unpack_rows(buf_ref[k])
        lo_sum += w[:, k:k + 1] * lo
        hi_sum += w[:, k:k + 1] * hi
    routed = jnp.concatenate([lo_sum, hi_sum], axis=1)
    o_ref[0] = _norm_rows(base_ref[0] + g2_ref[0] * routed) * lg_ref[...] + lb_ref[...]


def _final(dest3, w_tok, base, g2, ln_g, ln_b, ys):
    bsz, seq, dm = base.shape
    tm = dest3.shape[2]
    nt = seq // tm
    tile = pl.BlockSpec((1, tm, dm), lambda b, i: (b, i, 0))
    return pl.pallas_call(
        _final_kernel,
        grid=(bsz, nt),
        in_specs=[pl.BlockSpec((1, TOP_K, tm), lambda b, i: (b * nt + i, 0, 0), memory_space=pltpu.SMEM),
                  pl.BlockSpec((tm, TOP_K), lambda b, i: (b * nt + i, 0)),
                  tile, pl.BlockSpec((1, 1, dm), lambda b, i: (b, 0, 0)),
                  pl.BlockSpec((1, dm), lambda b, i: (0, 0)), pl.BlockSpec((1, dm), lambda b, i: (0, 0)),
                  pl.BlockSpec(memory_space=pl.ANY)],
        out_specs=tile,
        out_shape=jax.ShapeDtypeStruct((bsz, seq, dm), F32),
        scratch_shapes=[pltpu.VMEM((TOP_K, tm, ys.shape[1]), jnp.uint32), pltpu.SemaphoreType.DMA(())],
        compiler_params=_cparams(("arbitrary", "arbitrary")),
        name="combine_final_layernorm",
    )(dest3, w_tok, base, g2, ln_g, ln_b, ys)


def _rel_bucket(dist):
    dist = jnp.maximum(dist, 0)
    exact = REL_BUCKETS // 2
    log_ratio = jnp.log(jnp.maximum(dist, 1).astype(F32) / exact) / math.log(REL_MAX_DIST / exact)
    large = jnp.minimum(exact + (log_ratio * (REL_BUCKETS - exact)).astype(jnp.int32), REL_BUCKETS - 1)
    return jnp.where(dist < exact, dist, large)


def _bias_tiles(rel_bias):
    n_d = 4 * SEL_LEN
    vec = rel_bias[_rel_bucket(jnp.arange(n_d))].T
    far = rel_bias[REL_BUCKETS - 1]
    vec = ((vec - far[:, None]) * LOG2E).reshape(N_KV, GQA, n_d)
    tok = np.arange(Q_BLOCK)[None, :]
    key = np.arange(SEL_LEN)[:, None]

    def toeplitz(d):
        vals = vec[:, :, np.clip(d, 0, n_d - 1)]
        vals = jnp.where(jnp.asarray(d >= 0), vals, NEG)
        return jnp.transpose(vals, (0, 2, 1, 3)).reshape(N_KV, d.shape[0], ROWS)

    near = [toeplitz(delta + tok - key) for delta in (0, SEL_LEN, 2 * SEL_LEN)]
    zero = jnp.zeros((N_KV, SEL_LEN, ROWS), F32)
    edge = np.where(tok < key, 0.0, NEG).astype(np.float32)
    edge = jnp.broadcast_to(jnp.asarray(np.tile(edge, (1, GQA)))[None], (N_KV, SEL_LEN, ROWS))
    nb = jnp.stack(near + [zero, jnp.full_like(zero, NEG), edge], axis=1)
    w = np.arange(BAND_ROWS)[:, None]
    bands = []
    for ph in (0, 1):
        d = tok - CMP_STRIDE * (w - PAD_CMP - (Q_BLOCK // CMP_STRIDE) * ph) - (CMP_LEN - 1)
        vals = vec[:, :, np.clip(d, 0, n_d - 1)]
        vals = jnp.where(jnp.asarray(d >= 0), vals, 0.0)
        bands.append(jnp.transpose(vals, (0, 2, 1, 3)).reshape(N_KV, BAND_ROWS, ROWS))
    cb = jnp.stack(bands, axis=1)
    return nb.astype(F32), cb.astype(F32)


def _padded_cmp_rows(seq):
    return -(-(seq // CMP_STRIDE + 2 * PAD_CMP) // LANES) * LANES


def _overlap_t(seq):
    n_cmp = (seq - CMP_LEN) // CMP_STRIDE + 1
    n_blk = seq // SEL_LEN
    ncp = _padded_cmp_rows(seq)
    c_start = np.arange(n_cmp) * CMP_STRIDE
    c_end = c_start + CMP_LEN - 1
    blk = np.arange(n_blk)
    ov = ((c_start[:, None] < (blk[None, :] + 1) * SEL_LEN) & (c_end[:, None] >= blk[None, :] * SEL_LEN))
    out = np.zeros((n_blk, ncp), np.float32)
    out[:, PAD_CMP:PAD_CMP + n_cmp] = ov.T
    return jnp.asarray(out, BF16)


def _s5_params(lam_re, lam_im, log_step, b_re, b_im, c_re, c_im):
    lr, li = lam_re.astype(F32), lam_im.astype(F32)
    dt = jnp.exp(log_step.astype(F32))[:, None]
    mag = jnp.exp(lr * dt)
    ar, ai = mag * jnp.cos(li * dt), mag * jnp.sin(li * dt)
    den = lr * lr + li * li
    kr = ((ar - 1.0) * lr + ai * li) / den
    ki = (ai * lr - (ar - 1.0) * li) / den
    br, bi = b_re.astype(F32), b_im.astype(F32)
    bbr = kr[..., None] * br - ki[..., None] * bi
    bbi = kr[..., None] * bi + ki[..., None] * br
    n_g = lr.shape[0]
    eye = jnp.eye(n_g, dtype=F32)

    def drive(bb):
        return jnp.einsum('gpc,gh->gchp', bb, eye).reshape(n_g * SSM_GROUP, n_g * SSM_STATE)

    def readout(c):
        return jnp.einsum('gcp,gh->gphc', c, eye).reshape(n_g * SSM_STATE, n_g * SSM_GROUP)

    wb = jnp.concatenate([drive(bbr), drive(bbi)], axis=1).astype(BF16)
    wc = jnp.concatenate([readout(c_re.astype(F32)), -readout(c_im.astype(F32))], axis=0).astype(BF16)
    a = jnp.concatenate([ar.reshape(1, -1), ai.reshape(1, -1)], axis=1)
    return wb, jnp.broadcast_to(a, (SUBLANES, a.shape[1])), wc


def _layer(x, mod, w_in, lam_re, lam_im, log_step, b_re, b_im, c_re, c_im, d_skip, glu_w, cmp_pos, cmp_w1,
           cmp_w2, rel_bias, w_out, ln1_g, ln1_b, router_w, router_bias, e_gate, e_up, e_down, sg, su, sd,
           ln2_g, ln2_b, alpha):
    bsz, seq, dm = x.shape
    n_tok = bsz * seq
    sh1, sc1, g1, sh2, sc2, g2 = [m[:, None, :] for m in jnp.split(mod, 6, axis=-1)]
    ssm_w = dm // 2
    attn_w = N_HEADS * HEAD_DIM
    kv_w = N_KV * HEAD_DIM
    n_gate = 3 * N_HEADS
    offs = np.cumsum([0, ssm_w, attn_w] + [kv_w] * 6 + [n_gate, 2 * dm])

    wu = w_in[:, offs[0]:offs[1]].astype(BF16)
    wq = w_in[:, offs[1]:offs[2]] * (HEAD_DIM ** -0.5 * LOG2E)
    wa = jnp.concatenate([wq, w_in[:, offs[2]:offs[4]]], axis=1).astype(BF16)
    wg = jnp.pad(w_in[:, offs[8]:offs[9]], ((0, 0), (0, LANES - n_gate))).astype(BF16)
    wm = w_in[:, offs[9]:offs[10]].astype(BF16)
    nq = seq // Q_BLOCK
    n_blk = seq // SEL_LEN
    nch = seq // CMP_STRIDE

    def head_padded(w, width):
        w = w.reshape(dm, N_KV, HEAD_DIM)
        return jnp.pad(w, ((0, 0), (0, 0), (0, width - HEAD_DIM))).reshape(dm, N_KV * width).astype(BF16)

    aug_w = 2 * HEAD_DIM + -(-n_blk // LANES) * LANES
    u2d, act, gates, mg, kaug, kw, vst, vwt = _input_projection(
        x, sh1, sc1, wu, wa, wg, wm, head_padded(w_in[:, offs[4]:offs[5]], aug_w),
        head_padded(w_in[:, offs[6]:offs[7]], 2 * HEAD_DIM),
        w_in[:, offs[5]:offs[6]].T.astype(BF16), w_in[:, offs[7]:offs[8]].T.astype(BF16))

    def piece(i):
        return act[:, :, attn_w + i * kv_w: attn_w + (i + 1) * kv_w]

    def compress(raw, pos, w1, w2):
        xc = raw.reshape(bsz, nch, CMP_STRIDE, N_KV, HEAD_DIM).transpose(0, 3, 1, 2, 4).reshape(bsz, N_KV, nch, CMP_STRIDE * HEAD_DIM)
        half = CMP_STRIDE * HEAD_DIM
        w1cat = jnp.concatenate([w1[:half], w1[half:]], axis=1).astype(BF16)
        posb = jnp.dot(pos.reshape(1, -1), w1, precision=lax.Precision.HIGHEST)
        return _compress(xc, w1cat, posb, w2.astype(BF16))

    kc = compress(piece(0), cmp_pos[0], cmp_w1[0], cmp_w2[0])
    vc = compress(piece(1), cmp_pos[1], cmp_w1[1], cmp_w2[1])
    pad = ((0, 0), (0, 0), (PAD_CMP, _padded_cmp_rows(seq) - nch - PAD_CMP), (0, 0))
    kcp = jnp.pad(kc, pad).astype(BF16)
    vcpt = jnp.swapaxes(jnp.pad(vc, pad), 2, 3).astype(BF16)

    nst = nq // NSA_TILES
    gt =gates[:, :, :n_gate].reshape(bsz, nst, NSA_TILES, Q_BLOCK, N_KV, GQA, 3)
    gt = gt.transpose(0, 4, 1, 6, 2, 5, 3).reshape(bsz, N_KV, nst, 3, COLS)
    gt = jnp.pad(gt, ((0, 0), (0, 0), (0, 0), (0, SUBLANES - 3), (0, 0)))
    nb, cb = _bias_tiles(rel_bias)
    y_nsa = _nsa(act, kaug, vst, kw, vwt, kcp, vcpt, _overlap_t(seq), nb, cb, gt)

    wb, a, wc = _s5_params(lam_re, lam_im, log_step, b_re, b_im, c_re, c_im)
    z2 = _s5(u2d.reshape(seq * bsz, ssm_w), wb, a, wc, d_skip.reshape(1, ssm_w).astype(F32), bsz)
    z2d = z2.reshape(seq, bsz * ssm_w)

    base, h2 = _mix_out(z2d, y_nsa, mg, x, g1, sh2, sc2, g2, ln1_g.reshape(1, dm), ln1_b.reshape(1, dm),
                        glu_w.astype(BF16), w_out.astype(BF16), sg.astype(BF16), su.astype(BF16), sd.astype(BF16), alpha)
    h2 = h2.reshape(n_tok, dm // 2)

    n_exp = router_w.shape[1]
    rt = min(ROUTER_TILE, n_tok)
    tri = jnp.asarray(np.triu(np.ones((rt, rt), np.float32), 1), BF16)
    top_e, top_w, rank, counts = _router(h2, router_w.T.astype(BF16), router_bias.reshape(n_exp, 1).astype(F32), tri)
    counts = counts[:, 0].astype(jnp.int32)
    padded = (counts + EXPERT_TILE - 1) // EXPERT_TILE * EXPERT_TILE
    pad_end = jnp.cumsum(padded)
    pad_start = pad_end - padded
    dest = _dest_rows(pad_start.astype(jnp.int32), top_e, rank)
    n_rows = n_tok * TOP_K + n_exp * EXPERT_TILE
    n_tiles = n_rows // EXPERT_TILE
    tile_exp = jnp.minimum(jnp.searchsorted(pad_end, jnp.arange(n_tiles) * EXPERT_TILE, side='right'), n_exp - 1).astype(jnp.int32)
    n_used = (pad_end[-1] // EXPERT_TILE).astype(jnp.int32).reshape(1)
    mt = min(MOVE_TILE, seq)
    dest3 = dest.reshape(TOP_K, n_tok // mt, mt).transpose(1, 0, 2)
    xs = _dispatch(pad_end.astype(jnp.int32), padded.astype(jnp.int32), n_used, dest3, h2, n_rows)
    ys = _experts(tile_exp, n_used, xs, e_gate, e_up, e_down)
    return _final(dest3, top_w.T, base, g2, ln2_g.reshape(1, dm), ln2_b.reshape(1, dm), ys)


def kernel(x, c, ada_w, ada_b, w_in, ssm_lambda_re, ssm_lambda_im, ssm_log_step, ssm_b_re, ssm_b_im, ssm_c_re, ssm_c_im, ssm_d, ssm_glu_w, cmp_pos, cmp_w1, cmp_w2, rel_bias, w_out, ln1_g, ln1_b, router_w, router_bias, exp_w_gate, exp_w_up, exp_w_down, sh_w_gate, sh_w_up, sh_w_down, ln2_g, ln2_b):
    depth = ada_w.shape[0]
    alpha = (2 * depth) ** 0.25
    for l in range(depth):
        mod = _modulation(c, ada_w[l], ada_b[l])
        x = _layer(x, mod, w_in[l], ssm_lambda_re[l], ssm_lambda_im[l], ssm_log_step[l], ssm_b_re[l], ssm_b_im[l],
                   ssm_c_re[l], ssm_c_im[l], ssm_d[l], ssm_glu_w[l], cmp_pos[l], cmp_w1[l], cmp_w2[l], rel_bias,
                   w_out[l], ln1_g[l], ln1_b[l], router_w[l], router_bias[l], exp_w_gate[l], exp_w_up[l],
                   exp_w_down[l], sh_w_gate[l], sh_w_up[l], sh_w_down[l], ln2_g[l], ln2_b[l], alpha)
    return x
```

```python
import functools
import math

import numpy as np
import jax
import jax.numpy as jnp
from jax import lax
from jax.experimental import pallas as pl
from jax.experimental.pallas import tpu as pltpu

F32 = jnp.float32
BF16 = jnp.bfloat16

SSM_GROUP = 16
SSM_STATE = 64
N_HEADS = 16
HEAD_DIM = 64
N_KV = 4
GQA = N_HEADS // N_KV
CMP_LEN = 32
CMP_STRIDE = 16
SEL_LEN = 64
SEL_TOP = 16
WINDOW = 512
Q_BLOCK = 64
FORCE_BONUS = 1.0e4
N_FORCED = 3
assert GQA < FORCE_BONUS
REL_BUCKETS = 32
REL_MAX_DIST = 128
TOP_K = 8
ROUTED_SCALE = 2.5
LN_EPS = 1e-5

LANES = 128
SUBLANES = 8
VMEM_LIMIT_BYTES = 56 * 1024 * 1024
ROW_TILE = 512
ROUTER_TILE = 256
EXPERT_TILE = 256
MOVE_TILE = 256
DEST_TILE = 2048
SCAN_CHUNK = 128
SCAN_LANES = 512
MASK_BIG = 32768.0
NEG = -1.0e30
LOG2E = math.log2(math.e)
WIN_BLOCKS = WINDOW // SEL_LEN
ROWS = GQA * Q_BLOCK
NSA_TILES = 4
COLS = NSA_TILES * ROWS
CHUNK_BLOCKS = 8
CHUNK_KEYS = CHUNK_BLOCKS * SEL_LEN
FAR_BLOCKS = NSA_TILES
FAR_KEYS = FAR_BLOCKS * SEL_LEN
PAD_CMP = 8
BAND_ROWS = 24


def _cparams(sem):
    return pltpu.CompilerParams(dimension_semantics=sem, vmem_limit_bytes=VMEM_LIMIT_BYTES)


def _pack_rows(x):
    n = x.shape[1] // 2
    xb = x.astype(jnp.bfloat16).astype(F32)
    lo = lax.shift_right_logical(lax.bitcast_convert_type(xb[:, :n], jnp.uint32), jnp.uint32(16))
    return lax.bitcast_convert_type(xb[:, n:], jnp.uint32) | lo


def _unpack_rows(w):
    lo = lax.bitcast_convert_type(lax.shift_left(w, jnp.uint32(16)), F32)
    hi = lax.bitcast_convert_type(w & jnp.uint32(0xFFFF0000), F32)
    return lo, hi


def _norm_rows(x):
    mu = jnp.mean(x, axis=-1, keepdims=True)
    xc = x - mu
    var = jnp.mean(xc * xc, axis=-1, keepdims=True)
    return xc * lax.rsqrt(var + LN_EPS)


def _mod_kernel(c_ref, w_ref, b_ref, o_ref):
    cond = jax.nn.silu(c_ref[...])
    o_ref[...] = jnp.dot(cond.astype(BF16), w_ref[...].astype(BF16), preferred_element_type=F32) + b_ref[...]


def _modulation(c, ada_w, ada_b):
    bsz, dm = c.shape
    n = ada_w.shape[1]
    tn = dm
    return pl.pallas_call(
        _mod_kernel,
        grid=(n // tn,),
        in_specs=[pl.BlockSpec((bsz, dm), lambda j: (0, 0)),
                  pl.BlockSpec((dm, tn), lambda j: (0, j)),
                  pl.BlockSpec((1, tn), lambda j: (0, j))],
        out_specs=pl.BlockSpec((bsz, tn), lambda j: (0, j)),
        out_shape=jax.ShapeDtypeStruct((bsz, n), F32),
        compiler_params=_cparams(("arbitrary",)),
        name="ada_modulation",
    )(c, ada_w, ada_b.reshape(1, n))


def _inproj_kernel(x_ref, sh_ref, sc_ref, wu_ref, wa_ref, wg_ref, wm_ref, wks_ref, wkw_ref, wvs_ref, wvw_ref,
                   u_ref, a_ref, g_ref, m_ref, ks_ref, kw_ref, vs_ref, vw_ref, *, aug_w):
    h = _norm_rows(x_ref[0]) * (1.0 + sc_ref[0]) + sh_ref[0]
    hb = h.astype(BF16)
    tm = hb.shape[0]
    u_ref[...] = jnp.dot(hb, wu_ref[...], preferred_element_type=F32)
    a_ref[0] = jnp.dot(hb, wa_ref[...], preferred_element_type=F32).astype(BF16)
    g_ref[0] = jax.nn.sigmoid(jnp.dot(hb, wg_ref[...], preferred_element_type=F32))
    m_ref[0] = jax.nn.sigmoid(jnp.dot(hb, wm_ref[...], preferred_element_type=F32)).astype(BF16)
    ks = jnp.dot(hb, wks_ref[...], preferred_element_type=F32)
    assert aug_w & (aug_w - 1) == 0
    col = (lax.broadcasted_iota(jnp.int32, ks.shape, 1) & (aug_w - 1)) - 2 * HEAD_DIM
    blk = jnp.right_shift(pl.program_id(1) * tm + lax.broadcasted_iota(jnp.int32, ks.shape, 0), SEL_LEN.bit_length() - 1)
    ks_ref[0] = jnp.where(col == blk, -MASK_BIG, ks).astype(BF16)
    kw_ref[0] = jnp.dot(hb, wkw_ref[...], preferred_element_type=F32).astype(BF16)
    rows = lax.broadcasted_iota(jnp.int32, (2 * SUBLANES, tm), 0)
    ones_rows = jnp.where(rows == 0, 1.0, 0.0).astype(BF16)
    for w_ref, v_ref in ((wvs_ref, vs_ref), (wvw_ref, vw_ref)):
        vt = lax.dot_general(w_ref[...], hb, (((1,), (1,)), ((), ())), preferred_element_type=F32).astype(BF16)
        for g in range(N_KV):
            v_ref[0, g] = jnp.concatenate([vt[g * HEAD_DIM:(g + 1) * HEAD_DIM], ones_rows], axis=0)


def _input_projection(x, sh1, sc1, wu, wa, wg, wm, wks, wkw, wvs_t, wvw_t):
    bsz, seq, dm = x.shape
    tm = min(ROW_TILE, seq)
    nu, na, ng, nm = wu.shape[1], wa.shape[1], wg.shape[1], wm.shape[1]
    full = lambda w: pl.BlockSpec(w.shape, lambda b, i: (0, 0))
    vec = pl.BlockSpec((1, 1, dm), lambda b, i: (b, 0, 0))
    rows_out = lambda n: pl.BlockSpec((1, tm, n), lambda b, i: (b, i, 0))
    vt_rows = HEAD_DIM + 2 * SUBLANES
    vt_out = pl.BlockSpec((1, N_KV, vt_rows, tm), lambda b, i: (b, 0, 0, i))
    return pl.pallas_call(
        functools.partial(_inproj_kernel, aug_w=wks.shape[1] // N_KV),
        grid=(bsz, seq // tm),
        in_specs=[pl.BlockSpec((1, tm, dm), lambda b, i: (b, i, 0)), vec, vec,
                  full(wu), full(wa), full(wg), full(wm), full(wks), full(wkw), full(wvs_t), full(wvw_t)],
        out_specs=[pl.BlockSpec((tm, nu), lambda b, i: (i, b)), rows_out(na), rows_out(ng), rows_out(nm),
                   rows_out(wks.shape[1]), rows_out(wkw.shape[1]), vt_out, vt_out],
        out_shape=[jax.ShapeDtypeStruct((seq, bsz * nu), F32),
                   jax.ShapeDtypeStruct((bsz, seq, na), BF16),
                   jax.ShapeDtypeStruct((bsz, seq, ng), F32),
                   jax.ShapeDtypeStruct((bsz, seq, nm), BF16),
                   jax.ShapeDtypeStruct((bsz, seq, wks.shape[1]), BF16),
                   jax.ShapeDtypeStruct((bsz, seq, wkw.shape[1]), BF16),
                   jax.ShapeDtypeStruct((bsz, N_KV, vt_rows, seq), BF16),
                   jax.ShapeDtypeStruct((bsz, N_KV, vt_rows, seq), BF16)],
        compiler_params=_cparams(("arbitrary", "arbitrary")),
        name="adaln_input_projection",
    )(x, sh1, sc1, wu, wa, wg, wm, wks, wkw, wvs_t, wvw_t)


def _compress_kernel(x_ref, w1_ref, pb_ref, w2_ref, o_ref, *, n_cmp):
    hid = w2_ref.shape[0]
    p = jnp.dot(x_ref[0, 0], w1_ref[...], preferred_element_type=F32)
    nrow = p.shape[0]
    nxt = pltpu.roll(p[:, hid:], nrow - 1, 0)
    hidv = jax.nn.gelu(p[:, :hid] + nxt + pb_ref[...])
    out = jnp.dot(hidv.astype(BF16), w2_ref[...], preferred_element_type=F32)
    rows = lax.broadcasted_iota(jnp.int32, out.shape, 0)
    o_ref[0, 0] = jnp.where(rows < n_cmp, out, 0.0)


def _compress(xc, w1cat, posb, w2):
    bsz, nkv, nch, kdim = xc.shape
    hid2 = w1cat.shape[1]
    return pl.pallas_call(
        functools.partial(_compress_kernel, n_cmp=nch - 1),
        grid=(bsz, nkv),
        in_specs=[pl.BlockSpec((1, 1, nch, kdim), lambda b, g: (b, g, 0, 0)),
                  pl.BlockSpec((kdim, hid2), lambda b, g: (0, 0)),
                  pl.BlockSpec((1, hid2 // 2), lambda b, g: (0, 0)),
                  pl.BlockSpec((hid2 // 2, HEAD_DIM), lambda b, g: (0, 0))],
        out_specs=pl.BlockSpec((1, 1, nch, HEAD_DIM), lambda b, g: (b, g, 0, 0)),
        out_shape=jax.ShapeDtypeStruct((bsz, nkv, nch, HEAD_DIM), F32),
        compiler_params=_cparams(("arbitrary", "arbitrary")),
        name="kv_compress",
    )(xc, w1cat, posb, w2)


def _swap_heads_tokens(a):
    t = jnp.concatenate([a, jnp.zeros_like(a)], axis=0).T
    p = [t[r * HEAD_DIM:(r + 1) * HEAD_DIM] for r in range(GQA)]
    return jnp.concatenate([p[r] + pltpu.roll(p[r + 1], Q_BLOCK, 1) for r in range(0, GQA, 2)], axis=1)


def _nsa_kernel(q_ref, kaug_ref, vst_ref, kw_ref, vwt_ref, kcp_ref, cl_ref, vis_ref, nb_ref, cb_ref,
                g_ref, o_ref, s_ref, c_ref, w_ref, fa_ref, fb_ref, *, n_blk, n_sel):
    q0 = pl.program_id(2) * NSA_TILES
    qf = q_ref[0].astype(F32)
    qt = jnp.concatenate([_swap_heads_tokens(qf[n * Q_BLOCK:(n + 1) * Q_BLOCK]) for n in range(NSA_TILES)],
                         axis=1).astype(BF16)
    col_tile = lambda n: slice(n * ROWS, (n + 1) * ROWS)

    s_ref[...] = jnp.dot(kcp_ref[0, 0], qt, preferred_element_type=F32)
    for n in range(NSA_TILES):
        band = pl.multiple_of(SUBLANES * ((q0 + n) // 2), SUBLANES)
        s_ref[pl.ds(band, BAND_ROWS), col_tile(n)] += cb_ref[0, n % 2]
    s = jnp.where(vis_ref[...] <= Q_BLOCK * q0, s_ref[...], -jnp.inf)
    mx = jnp.max(s, axis=0, keepdims=True)
    mx = jnp.where(mx == -jnp.inf, 0.0, mx)
    e = jnp.exp2((s - mx).astype(BF16))
    both = jnp.dot(cl_ref[0, 0], e, preferred_element_type=F32)
    inv = 1.0 / jnp.maximum(both[HEAD_DIM:HEAD_DIM + 1], 1e-30)
    o_cmp = both[:HEAD_DIM] * inv

    imp4 = both[HEAD_DIM + 2 * SUBLANES:] * inv
    sums = []
    for n in range(NSA_TILES):
        two = imp4[:, n * ROWS:n * ROWS + LANES] + imp4[:, n * ROWS + LANES:(n + 1) * ROWS]
        sums.append(two + pltpu.roll(two, Q_BLOCK, 1))
    low = lax.broadcasted_iota(jnp.int32, sums[0].shape, 1) < Q_BLOCK
    imp = jnp.concatenate([jnp.where(low, sums[n], sums[n + 1]) for n in range(0, NSA_TILES, 2)], axis=1)
    blk = lax.broadcasted_iota(jnp.int32, imp.shape, 0)
    cur = q0 + jnp.right_shift(lax.broadcasted_iota(jnp.int32, imp.shape, 1), Q_BLOCK.bit_length() - 1)
    blkf = blk.astype(F32)
    forced = (blk == 0) | (blk == cur) | (blk == cur - 1)
    score = jnp.where((blk <= cur) & ~forced, imp, -jnp.inf)
    notsel = jnp.where(forced, 0.0, 1.0)
    for _ in range(n_sel - N_FORCED):
        best = jnp.max(score, axis=0, keepdims=True)
        first = jnp.min(jnp.where(score == best, blkf, float(n_blk)), axis=0, keepdims=True)
        pick = blkf == first
        notsel = jnp.where(pick, 0.0, notsel)
        score = jnp.where(pick, -jnp.inf, score)
    halves = []
    for n in range(0, NSA_TILES, 2):
        pair = notsel[:, (n // 2) * LANES:(n // 2 + 1) * LANES]
        swapped = pltpu.roll(pair, Q_BLOCK, 1)
        halves += [jnp.where(low, pair, swapped), jnp.where(low, swapped, pair)]
    notsel = jnp.concatenate([h for h in halves for _ in (0, 1)], axis=1).astype(BF16)
    qwin = jnp.concatenate([qt, jnp.zeros_like(qt)], axis=0)
    tail = kaug_ref.shape[2] - 2 * HEAD_DIM - n_blk
    qaug = jnp.concatenate([qwin, notsel] + ([jnp.zeros((tail, COLS), notsel.dtype)] if tail else []), axis=0)

    def update(carry, sc, vt, top=None):
        m, acc = carry
        m_new = jnp.maximum(m, jnp.max(sc, axis=0, keepdims=True) if top is None else top)
        p = jnp.exp2((sc - m_new).astype(BF16))
        return m_new, jnp.exp2(m - m_new) * acc + jnp.dot(vt, p, preferred_element_type=F32)

    def chunk_keys(c):
        return pl.ds(pl.multiple_of(c * FAR_KEYS, FAR_KEYS), FAR_KEYS)

    def far_scores(c, buf_ref):
        sc = jnp.dot(kaug_ref[0, chunk_keys(c), :], qaug, preferred_element_type=F32)
        buf_ref[...] = sc
        return jnp.max(sc, axis=0, keepdims=True)

    def far_pair(i, carry):
        m, acc, top_a = carry
        top_b = far_scores(2 * i + 1, fb_ref)
        m, acc = update((m, acc), fa_ref[...], vst_ref[0, 0, :, chunk_keys(2 * i)], top_a)
        top_a = far_scores(2 * i + 2, fa_ref)
        m, acc = update((m, acc), fb_ref[...], vst_ref[0, 0, :, chunk_keys(2 * i + 1)], top_b)
        return m, acc, top_a

    def far_last(_, carry, n_far):
        m, acc, top_a = carry
        return update((m, acc), fa_ref[...], vst_ref[0, 0, :, chunk_keys(n_far - 1)], top_a) + (top_a,)

    def near_chunk(k_ref, vt_ref, qmat, blk0, tile_of, carry, buf_ref):
        n_keys = buf_ref.shape[0]
        keys = pl.ds(pl.multiple_of(blk0 * SEL_LEN, NSA_TILES * SEL_LEN), n_keys)
        buf_ref[...] = jnp.dot(k_ref[0, keys, :], qmat, preferred_element_type=F32)
        for o in range(n_keys // SEL_LEN):
            for n in range(NSA_TILES):
                buf_ref[o * SEL_LEN:(o + 1) * SEL_LEN, col_tile(n)] += nb_ref[0, tile_of(blk0 + o, q0 + n)]
        return update(carry, buf_ref[...], vt_ref[0, 0, :, keys])

    def sel_tile(j, qi):
        d = qi - j
        return jnp.where(d < 0, 4, jnp.minimum(d, 3))

    def win_tile(j, qi):
        d = qi - j
        return jnp.where((d < 0) | (d > WIN_BLOCKS), 4, jnp.where(d == WIN_BLOCKS, 5, jnp.minimum(d, 3)))

    init = (jnp.full((1, COLS), NEG, F32), jnp.zeros((vst_ref.shape[2], COLS), F32))
    n_far = jnp.maximum(q0 - 2, 0) // FAR_BLOCKS
    carry = lax.fori_loop(0, n_far // 2, far_pair, init + (far_scores(0, fa_ref),))
    carry = lax.fori_loop(0, n_far % 2, functools.partial(far_last, n_far=n_far), carry)
    _, acc = near_chunk(kaug_ref, vst_ref, qaug, n_far * FAR_BLOCKS, sel_tile, carry[:2], c_ref)
    o_slc = acc[:HEAD_DIM] / acc[HEAD_DIM:HEAD_DIM + 1]
    _, acc = near_chunk(kw_ref, vwt_ref, qwin, jnp.maximum(q0 - WIN_BLOCKS, 0), win_tile, init, w_ref)
    o_win = acc[:HEAD_DIM] / acc[HEAD_DIM:HEAD_DIM + 1]
    g = g_ref[0, 0, 0]
    out = g[0:1] * o_cmp + g[1:2] * o_slc + g[2:3] * o_win
    o_ref[0] = jnp.concatenate([_swap_heads_tokens(out[:, col_tile(n)]) for n in range(NSA_TILES)],
                               axis=0).astype(BF16)


def _nsa(act, kaug, vst, kw, vwt, kcp, cmp_lhs, vis, nb, cb, gt):
    bsz, nkv, nsteps, _, cols = gt.shape
    dh = HEAD_DIM
    seq = kw.shape[1]
    step_tokens = NSA_TILES * Q_BLOCK
    q_blk = pl.BlockSpec((1, step_tokens, GQA * dh), lambda b, g, i: (b, i, g))
    n_blk = seq // SEL_LEN
    ncp = kcp.shape[2]
    assert n_blk % CHUNK_BLOCKS == 0 and n_blk >= WIN_BLOCKS + NSA_TILES and CHUNK_BLOCKS == FAR_BLOCKS + NSA_TILES
    per_bg = lambda shape: pl.BlockSpec((1, 1) + shape, lambda b, g, i: (b, g, 0, 0))
    per_step = lambda shape: pl.BlockSpec((1, 1, 1) + shape, lambda b, g, i: (b, g, i, 0, 0))
    head_cols = lambda a: pl.BlockSpec((1, seq, a.shape[2] // nkv), lambda b, g, i: (b, 0, g))
    kern = functools.partial(_nsa_kernel, n_blk=n_blk, n_sel=min(SEL_TOP, n_blk))
    return pl.pallas_call(
        kern,
        grid=(bsz, nkv, nsteps),
        in_specs=[q_blk,
                  head_cols(kaug), per_bg((vst.shape[2], seq)), head_cols(kw), per_bg((vwt.shape[2], seq)),
                  per_bg((ncp, dh)), per_bg((cmp_lhs.shape[2], ncp)),
                  pl.BlockSpec(vis.shape, lambda b, g, i: (0, 0)),
                  pl.BlockSpec((1,) + nb.shape[1:], lambda b, g, i: (g, 0, 0, 0)),
                  pl.BlockSpec((1,) + cb.shape[1:], lambda b, g, i: (g, 0, 0, 0)),
                  per_step((SUBLANES, cols))],
        out_specs=q_blk,
        out_shape=jax.ShapeDtypeStruct((bsz, seq, nkv * GQA * dh), BF16),
        scratch_shapes=[pltpu.VMEM((ncp, cols), F32), pltpu.VMEM((CHUNK_KEYS, cols), F32),
                        pltpu.VMEM(((WIN_BLOCKS + NSA_TILES) * SEL_LEN, cols), F32),
                        pltpu.VMEM((FAR_KEYS, cols), F32), pltpu.VMEM((FAR_KEYS, cols), F32)],
        compiler_params=_cparams(("arbitrary", "arbitrary", "arbitrary")),
        name="nsa_attention",
    )(act, kaug, vst, kw, vwt, kcp, cmp_lhs, vis, nb, cb, gt)


def _s5_kernel(u_ref, wb_ref, a_ref, wc_ref, d_ref, z_ref, xs_ref, st_ref, *, bsz, n_state):
    @pl.when(pl.program_id(0) == 0)
    def _():
        st_ref[...] = jnp.zeros_like(st_ref)

    u = u_ref[...]
    xs_ref[...] = jnp.dot(u.astype(BF16), wb_ref[...], preferred_element_type=F32)
    steps = u.shape[0] // bsz
    for c0 in range(0, n_state, SCAN_LANES):
        re = pl.ds(c0, SCAN_LANES)
        im = pl.ds(n_state + c0, SCAN_LANES)
        ar = jnp.broadcast_to(a_ref[0:1, re], (bsz, SCAN_LANES))
        ai = jnp.broadcast_to(a_ref[0:1, im], (bsz, SCAN_LANES))

        def step(t, carry):
            xr, xi = carry
            rows = pl.ds(pl.multiple_of(t * bsz, bsz), bsz)
            nr = ar * xr - ai * xi + xs_ref[rows, re]
            ni = ar * xi + ai * xr + xs_ref[rows, im]
            xs_ref[rows, re] = nr
            xs_ref[rows, im] = ni
            return nr, ni

        xr, xi = lax.fori_loop(0, steps, step, (st_ref[:, re], st_ref[:, im]), unroll=8)
        st_ref[:, re] = xr
        st_ref[:, im] = xi
    y = jnp.dot(xs_ref[...].astype(BF16), wc_ref[...], preferred_element_type=F32) + d_ref[...] * u
    z_ref[...] = jax.nn.gelu(y).astype(BF16)


def _s5(u2, wb, a, wc, dsk, bsz):
    rows, width = u2.shape
    seq = rows // bsz
    chunk = min(SCAN_CHUNK, seq)
    n_state2 = wb.shape[1]
    return pl.pallas_call(
        functools.partial(_s5_kernel, bsz=bsz, n_state=n_state2 // 2),
        grid=(seq // chunk,),
        in_specs=[pl.BlockSpec((chunk * bsz, width), lambda i: (i, 0)),
                  pl.BlockSpec(wb.shape, lambda i: (0, 0)),
                  pl.BlockSpec(a.shape, lambda i: (0, 0)),
                  pl.BlockSpec(wc.shape, lambda i: (0, 0)),
                  pl.BlockSpec(dsk.shape, lambda i: (0, 0))],
        out_specs=pl.BlockSpec((chunk * bsz, width), lambda i: (i, 0)),
        out_shape=jax.ShapeDtypeStruct((rows, width), BF16),
        scratch_shapes=[pltpu.VMEM((chunk * bsz, n_state2), F32), pltpu.VMEM((bsz, n_state2), F32)],
        compiler_params=_cparams(("arbitrary",)),
        name="s5_scan",
    )(u2, wb, a, wc, dsk)


def _mixout_kernel(z_ref, yn_ref, mg_ref, x_ref, g1_ref, sh2_ref, sc2_ref, g2_ref, lg_ref, lb_ref,
                   glu_ref, wo_ref, sg_ref, su_ref, sd_ref, base_ref, h_ref, *, alpha):
    dm = x_ref.shape[2]
    glu = jnp.dot(z_ref[...], glu_ref[...], preferred_element_type=F32)
    y_ssm = glu[:, :dm] * jax.nn.sigmoid(glu[:, dm:])
    mg = mg_ref[0].astype(F32)
    merged = mg[:, :dm] * y_ssm + mg[:, dm:] * yn_ref[0].astype(F32)
    y = jnp.dot(merged.astype(BF16), wo_ref[...], preferred_element_type=F32)
    x1 = _norm_rows(alpha * x_ref[0] + g1_ref[0] * y) * lg_ref[...] + lb_ref[...]
    hf = _norm_rows(x1) * (1.0 + sc2_ref[0]) + sh2_ref[0]
    h_ref[0] = _pack_rows(hf)
    h = hf.astype(BF16)
    hs =jax.nn.silu(jnp.dot(h, sg_ref[...], preferred_element_type=F32)) * jnp.dot(h, su_ref[...], preferred_element_type=F32)
    shared = jnp.dot(hs.astype(BF16), sd_ref[...], preferred_element_type=F32)
    base_ref[0] = alpha * x1 + g2_ref[0] * shared


def _mix_out(z2d, y_nsa, mg, x, g1, sh2, sc2, g2, ln_g, ln_b, glu_w, w_out, sg, su, sd, alpha):
    bsz, seq, dm = x.shape
    tm = min(ROW_TILE, seq)
    width = z2d.shape[1] // bsz
    vec = pl.BlockSpec((1, 1, dm), lambda b, i: (b, 0, 0))
    row = pl.BlockSpec((1, dm), lambda b, i: (0, 0))
    full = lambda w: pl.BlockSpec(w.shape, lambda b, i: (0, 0))
    tile = lambda n: pl.BlockSpec((1, tm, n), lambda b, i: (b, i, 0))
    return pl.pallas_call(
        functools.partial(_mixout_kernel, alpha=alpha),
        grid=(bsz, seq // tm),
        in_specs=[pl.BlockSpec((tm, width), lambda b, i: (i, b)), tile(dm), tile(2 * dm), tile(dm),
                  vec, vec, vec, vec, row, row, full(glu_w), full(w_out), full(sg), full(su), full(sd)],
        out_specs=[tile(dm), tile(dm // 2)],
        out_shape=[jax.ShapeDtypeStruct((bsz, seq, dm), F32), jax.ShapeDtypeStruct((bsz, seq, dm // 2), jnp.uint32)],
        compiler_params=_cparams(("arbitrary", "arbitrary")),
        name="merge_outproj_ln_shared",
    )(z2d, y_nsa, mg, x, g1, sh2, sc2, g2, ln_g, ln_b, glu_w, w_out, sg, su, sd)


def _router_kernel(h_ref, rwt_ref, rb_ref, tri_ref, e_ref, w_ref, p_ref, cnt_ref):
    @pl.when(pl.program_id(0) == 0)
    def _():
        cnt_ref[...] = jnp.zeros_like(cnt_ref)

    h = jnp.concatenate(_unpack_rows(h_ref[...]), axis=1).astype(BF16)
    logits = lax.dot_general(rwt_ref[...], h, (((1,), (1,)), ((), ())), preferred_element_type=F32)
    scores = jax.nn.sigmoid(logits)
    cur = scores + rb_ref[...]
    n_exp = scores.shape[0]
    eid = lax.broadcasted_iota(jnp.int32, scores.shape, 0).astype(F32)
    chosen = jnp.zeros(scores.shape, F32)
    ids, vals = [], []
    for _ in range(TOP_K):
        best = jnp.max(cur, axis=0, keepdims=True)
        first = jnp.min(jnp.where(cur == best, eid, float(n_exp)), axis=0, keepdims=True)
        pick = eid == first
        ids.append(first)
        vals.append(jnp.sum(jnp.where(pick, scores, 0.0), axis=0, keepdims=True))
        chosen = jnp.where(pick, 1.0, chosen)
        cur = jnp.where(pick, -jnp.inf, cur)
    top_s = jnp.concatenate(vals, axis=0)
    w_ref[...] = top_s / jnp.sum(top_s, axis=0, keepdims=True) * ROUTED_SCALE
    top_e = jnp.concatenate(ids, axis=0)
    e_ref[...] = top_e.astype(jnp.int32)
    before = jnp.dot(chosen.astype(BF16), tri_ref[...], preferred_element_type=F32) + cnt_ref[...]
    ranks = [jnp.sum(jnp.where(eid == ids[k], before, 0.0), axis=0, keepdims=True) for k in range(TOP_K)]
    p_ref[...] = jnp.concatenate(ranks, axis=0).astype(jnp.int32)
    cnt_ref[...] += jnp.sum(chosen, axis=1, keepdims=True)


def _router(h2, rwt, rb, tri):
    n_tok, words = h2.shape
    n_exp, dm = rwt.shape
    tm = tri.shape[0]
    kt = pl.BlockSpec((TOP_K, tm), lambda i: (0, i))
    return pl.pallas_call(
        _router_kernel,
        grid=(n_tok // tm,),
        in_specs=[pl.BlockSpec((tm, words), lambda i: (i, 0)),
                  pl.BlockSpec((n_exp, dm), lambda i: (0, 0)),
                  pl.BlockSpec((n_exp, 1), lambda i: (0, 0)),
                  pl.BlockSpec((tm, tm), lambda i: (0, 0))],
        out_specs=[kt, kt, kt, pl.BlockSpec((n_exp, 1), lambda i: (0, 0))],
        out_shape=[jax.ShapeDtypeStruct((TOP_K, n_tok), jnp.int32),
                   jax.ShapeDtypeStruct((TOP_K, n_tok), F32),
                   jax.ShapeDtypeStruct((TOP_K, n_tok), jnp.int32),
                   jax.ShapeDtypeStruct((n_exp, 1), F32)],
        compiler_params=_cparams(("arbitrary",)),
        name="router_topk_rank",
    )(h2, rwt, rb, tri)


def _dest_kernel(start_ref, e_ref, r_ref, o_ref):
    e = e_ref[...]
    start = lax.fori_loop(0, start_ref.shape[0], lambda j, acc: jnp.where(e == j, start_ref[j], acc),
                          jnp.zeros(e.shape, jnp.int32))
    o_ref[...] = start + r_ref[...]


def _dest_rows(pad_start, top_e, rank):
    n_tok = top_e.shape[1]
    tm = min(DEST_TILE, n_tok)
    blk = pl.BlockSpec((TOP_K, tm), lambda i, ps: (0, i))
    return pl.pallas_call(
        _dest_kernel,
        grid_spec=pltpu.PrefetchScalarGridSpec(num_scalar_prefetch=1, grid=(n_tok // tm,), in_specs=[blk, blk], out_specs=blk),
        out_shape=jax.ShapeDtypeStruct(top_e.shape, jnp.int32),
        compiler_params=_cparams(("arbitrary",)),
        name="moe_dest_rows",
    )(pad_start, top_e, rank)


def _row_copy(src_ref, src_row, dst_ref, dst_row, sem):
    return pltpu.make_async_copy(src_ref.at[pl.ds(src_row, 1), :], dst_ref.at[pl.ds(dst_row, 1), :], sem)


def _dispatch_kernel(pend_ref, padded_ref, nused_ref, dest_ref, h_ref, xs_ref, zero_ref, sem, zsem):
    tm = h_ref.shape[0]

    @pl.when(pl.program_id(0) == 0)
    def _():
        zero_ref[...] = jnp.zeros_like(zero_ref)
        n_tiles = xs_ref.shape[0] // EXPERT_TILE

        def zero_tile(row):
            return pltpu.make_async_copy(zero_ref, xs_ref.at[pl.ds(pl.multiple_of(row, EXPERT_TILE), EXPERT_TILE), :], zsem)

        def per_expert(act):
            def body(e, carry):
                @pl.when(padded_ref[e] > 0)
                def _():
                    act(zero_tile(pend_ref[e] - EXPERT_TILE))
                return carry
            lax.fori_loop(0, pend_ref.shape[0], body, 0)

        def per_unused(act):
            def body(i, carry):
                act(zero_tile(i * EXPERT_TILE))
                return carry
            lax.fori_loop(nused_ref[0], n_tiles, body, 0)

        for loop in (per_expert, per_unused):
            loop(lambda copy: copy.start())
        for loop in (per_expert, per_unused):
            loop(lambda copy: copy.wait())

    def issue(t8, carry):
        base = pl.multiple_of(t8 * SUBLANES, SUBLANES)
        for s in range(SUBLANES):
            for k in range(TOP_K):
                _row_copy(h_ref, base + s, xs_ref, dest_ref[0, k, base + s], sem).start(priority=k % 2)
        return carry

    lax.fori_loop(0, tm // SUBLANES, issue, 0)
    pltpu.make_async_copy(xs_ref.at[pl.ds(0, TOP_K * tm), :], xs_ref.at[pl.ds(0, TOP_K * tm), :], sem).wait()


def _dispatch(pad_end, padded, n_used, dest3, h2, n_rows):
    n_tok, words = h2.shape
    tm = dest3.shape[2]
    grid_spec = pltpu.PrefetchScalarGridSpec(
        num_scalar_prefetch=3,
        grid=(n_tok // tm,),
        in_specs=[pl.BlockSpec((1, TOP_K, tm), lambda i, *_: (i, 0, 0), memory_space=pltpu.SMEM),
                  pl.BlockSpec((tm, words), lambda i, *_: (i, 0))],
        out_specs=pl.BlockSpec(memory_space=pl.ANY),
        scratch_shapes=[pltpu.VMEM((EXPERT_TILE, words), jnp.uint32), pltpu.SemaphoreType.DMA(()),
                        pltpu.SemaphoreType.DMA(())],
    )
    return pl.pallas_call(
        _dispatch_kernel,
        grid_spec=grid_spec,
        out_shape=jax.ShapeDtypeStruct((n_rows, words), jnp.uint32),
        compiler_params=_cparams(("arbitrary",)),
        name="moe_dispatch",
    )(pad_end, padded, n_used, dest3, h2)


def _expert_kernel(te_ref, nu_ref, x_ref, wg_ref, wu_ref, wd_ref, y_ref):
    i = pl.program_id(0)

    @pl.when(i < nu_ref[0])
    def _():
        lo, hi = _unpack_rows(x_ref[...])
        lo, hi = lo.astype(BF16), hi.astype(BF16)
        half = lo.shape[1]

        def proj(w_ref):
            return (jnp.dot(lo, w_ref[0, :half, :].astype(BF16), preferred_element_type=F32)
                    + jnp.dot(hi, w_ref[0, half:, :].astype(BF16), preferred_element_type=F32))

        hmid = jax.nn.silu(proj(wg_ref)) * proj(wu_ref)
        y_ref[...] = _pack_rows(jnp.dot(hmid.astype(BF16), wd_ref[0].astype(BF16), preferred_element_type=F32))

    @pl.when(i >= nu_ref[0])
    def _():
        y_ref[...] = jnp.zeros_like(y_ref)


def _experts(tile_exp, n_used, xs, wg, wu, wd):
    n_rows, words = xs.shape
    dm, de = wg.shape[1], wg.shape[2]
    grid_spec = pltpu.PrefetchScalarGridSpec(
        num_scalar_prefetch=2,
        grid=(n_rows // EXPERT_TILE,),
        in_specs=[pl.BlockSpec((EXPERT_TILE, words), lambda i, te, nu: (i, 0)),
                  pl.BlockSpec((1, dm, de), lambda i, te, nu: (te[i], 0, 0)),
                  pl.BlockSpec((1, dm, de), lambda i, te, nu: (te[i], 0, 0)),
                  pl.BlockSpec((1, de, dm), lambda i, te, nu: (te[i], 0, 0))],
        out_specs=pl.BlockSpec((EXPERT_TILE, words), lambda i, te, nu: (i, 0)),
    )
    return pl.pallas_call(
        _expert_kernel,
        grid_spec=grid_spec,
        out_shape=jax.ShapeDtypeStruct((n_rows, words), jnp.uint32),
        compiler_params=_cparams(("arbitrary",)),
        name="expert_mlp",
    )(tile_exp, n_used, xs, wg, wu, wd)


def _final_kernel(dest_ref, w_ref, base_ref, g2_ref, lg_ref, lb_ref, ys_ref, o_ref, buf_ref, sem):
    tm = base_ref.shape[1]

    def issue(t8, carry):
        base = pl.multiple_of(t8 * SUBLANES, SUBLANES)
        for s in range(SUBLANES):
            for k in range(TOP_K):
                _row_copy(ys_ref, dest_ref[0, k, base + s], buf_ref.at[k], base + s, sem).start(priority=k % 2)
        return carry

    lax.fori_loop(0, tm // SUBLANES, issue, 0)
    pltpu.make_async_copy(buf_ref, buf_ref, sem).wait()
    w = w_ref[...]
    lo_sum, hi_sum = jnp.zeros((tm, buf_ref.shape[2]), F32), jnp.zeros((tm, buf_ref.shape[2]), F32)
    for k in range(TOP_K):
        lo, hi = _unpack_rows(buf_ref[k])
        lo_sum += w[:, k:k + 1] * lo
        hi_sum += w[:, k:k + 1] * hi
    routed = jnp.concatenate([lo_sum, hi_sum], axis=1)
    o_ref[0] = _norm_rows(base_ref[0] + g2_ref[0] * routed) * lg_ref[...] + lb_ref[...]


def _final(dest3, w_tok, base, g2, ln_g, ln_b, ys):
    bsz, seq, dm = base.shape
    tm = dest3.shape[2]
    nt = seq // tm
    tile = pl.BlockSpec((1, tm, dm), lambda b, i: (b, i, 0))
    return pl.pallas_call(
        _final_kernel,
        grid=(bsz, nt),
        in_specs=[pl.BlockSpec((1, TOP_K, tm), lambda b, i: (b * nt + i, 0, 0), memory_space=pltpu.SMEM),
                  pl.BlockSpec((tm, TOP_K), lambda b, i: (b * nt + i, 0)),
                  tile, pl.BlockSpec((1, 1, dm), lambda b, i: (b, 0, 0)),
                  pl.BlockSpec((1, dm), lambda b, i: (0, 0)), pl.BlockSpec((1, dm), lambda b, i: (0, 0)),
                  pl.BlockSpec(memory_space=pl.ANY)],
        out_specs=tile,
        out_shape=jax.ShapeDtypeStruct((bsz, seq, dm), F32),
        scratch_shapes=[pltpu.VMEM((TOP_K, tm, ys.shape[1]), jnp.uint32), pltpu.SemaphoreType.DMA(())],
        compiler_params=_cparams(("arbitrary", "arbitrary")),
        name="combine_final_layernorm",
    )(dest3, w_tok, base, g2, ln_g, ln_b, ys)


def _rel_bucket(dist):
    dist = jnp.maximum(dist, 0)
    exact = REL_BUCKETS // 2
    log_ratio = jnp.log(jnp.maximum(dist, 1).astype(F32) / exact) / math.log(REL_MAX_DIST / exact)
    large = jnp.minimum(exact + (log_ratio * (REL_BUCKETS - exact)).astype(jnp.int32), REL_BUCKETS - 1)
    return jnp.where(dist < exact, dist, large)


def _bias_tiles(rel_bias):
    n_d = 4 * SEL_LEN
    vec = rel_bias[_rel_bucket(jnp.arange(n_d))].T
    far = rel_bias[REL_BUCKETS - 1]
    vec = ((vec - far[:, None]) * LOG2E).reshape(N_KV, GQA, n_d)
    tok = np.arange(Q_BLOCK)[None, :]
    key = np.arange(SEL_LEN)[:, None]

    def toeplitz(d):
        vals = vec[:, :, np.clip(d, 0, n_d - 1)]
        vals = jnp.where(jnp.asarray(d >= 0), vals, NEG)
        return jnp.transpose(vals, (0, 2, 1, 3)).reshape(N_KV, d.shape[0], ROWS)

    near = [toeplitz(delta + tok - key) for delta in (0, SEL_LEN, 2 * SEL_LEN)]
    zero = jnp.zeros((N_KV, SEL_LEN, ROWS), F32)
    edge = np.where(tok < key, 0.0, NEG).astype(np.float32)
    edge = jnp.broadcast_to(jnp.asarray(np.tile(edge, (1, GQA)))[None], (N_KV, SEL_LEN, ROWS))
    nb = jnp.stack(near + [zero, jnp.full_like(zero, NEG), edge], axis=1)
    w = np.arange(BAND_ROWS)[:, None]
    bands = []
    for ph in (0, 1):
        d = tok - CMP_STRIDE * (w - PAD_CMP - (Q_BLOCK // CMP_STRIDE) * ph) - (CMP_LEN - 1)
        vals = vec[:, :, np.clip(d, 0, n_d - 1)]
        vals = jnp.where(jnp.asarray(d >= 0), vals, 0.0)
        bands.append(jnp.transpose(vals, (0, 2, 1, 3)).reshape(N_KV, BAND_ROWS, ROWS))
    cb = jnp.stack(bands, axis=1)
    return nb.astype(F32), cb.astype(F32)


def _padded_cmp_rows(seq):
    return -(-(seq // CMP_STRIDE + 2 * PAD_CMP) // LANES) * LANES


def _overlap_t(seq):
    n_cmp = (seq - CMP_LEN) // CMP_STRIDE + 1
    n_blk = seq // SEL_LEN
    ncp = _padded_cmp_rows(seq)
    c_start = np.arange(n_cmp) * CMP_STRIDE
    c_end = c_start + CMP_LEN - 1
    blk = np.arange(n_blk)
    ov = ((c_start[:, None] < (blk[None, :] + 1) * SEL_LEN) & (c_end[:, None] >= blk[None, :] * SEL_LEN))
    out = np.zeros((2 * SUBLANES + n_blk, ncp), np.float32)
    out[0] = 1.0
    out[2 * SUBLANES:, PAD_CMP:PAD_CMP + n_cmp] = ov.T
    return jnp.asarray(out, BF16)


def _cmp_visibility(seq):
    n_cmp = (seq - CMP_LEN) // CMP_STRIDE + 1
    ncp = _padded_cmp_rows(seq)
    rho = np.arange(ncp)[:, None]
    col = np.arange(COLS)[None, :]
    thr = CMP_STRIDE * (rho - PAD_CMP) + (CMP_LEN - 1) - Q_BLOCK * (col // ROWS) - col % Q_BLOCK
    valid = (rho >= PAD_CMP) & (rho < PAD_CMP + n_cmp)
    return jnp.asarray(np.where(valid, thr, np.iinfo(np.int32).max), jnp.int32)


def _s5_params(lam_re, lam_im, log_step, b_re, b_im, c_re, c_im):
    lr, li = lam_re.astype(F32), lam_im.astype(F32)
    dt = jnp.exp(log_step.astype(F32))[:, None]
    mag = jnp.exp(lr * dt)
    ar, ai = mag * jnp.cos(li * dt), mag * jnp.sin(li * dt)
    den = lr * lr + li * li
    kr = ((ar - 1.0) * lr + ai * li) / den
    ki = (ai * lr - (ar - 1.0) * li) / den
    br, bi = b_re.astype(F32), b_im.astype(F32)
    bbr = kr[..., None] * br - ki[..., None] * bi
    bbi = kr[..., None] * bi + ki[..., None] * br
    n_g = lr.shape[0]
    eye = jnp.eye(n_g, dtype=F32)

    def drive(bb):
        return jnp.einsum('gpc,gh->gchp', bb, eye).reshape(n_g * SSM_GROUP, n_g * SSM_STATE)

    def readout(c):
        return jnp.einsum('gcp,gh->gphc', c, eye).reshape(n_g * SSM_STATE, n_g * SSM_GROUP)

    wb = jnp.concatenate([drive(bbr), drive(bbi)], axis=1).astype(BF16)
    wc = jnp.concatenate([readout(c_re.astype(F32)), -readout(c_im.astype(F32))], axis=0).astype(BF16)
    a = jnp.concatenate([ar.reshape(1, -1), ai.reshape(1, -1)], axis=1)
    return wb, jnp.broadcast_to(a, (SUBLANES, a.shape[1])), wc


def _layer(x, mod, w_in, lam_re, lam_im, log_step, b_re, b_im, c_re, c_im, d_skip, glu_w, cmp_pos, cmp_w1,
           cmp_w2, rel_bias, w_out, ln1_g, ln1_b, router_w, router_bias, e_gate, e_up, e_down, sg, su, sd,
           ln2_g, ln2_b, alpha):
    bsz, seq, dm = x.shape
    n_tok = bsz * seq
    sh1, sc1, g1, sh2, sc2, g2 = [m[:, None, :] for m in jnp.split(mod, 6, axis=-1)]
    ssm_w = dm // 2
    attn_w = N_HEADS * HEAD_DIM
    kv_w = N_KV * HEAD_DIM
    n_gate = 3 * N_HEADS
    offs = np.cumsum([0, ssm_w, attn_w] + [kv_w] * 6 + [n_gate, 2 * dm])

    wu = w_in[:, offs[0]:offs[1]].astype(BF16)
    wq = w_in[:, offs[1]:offs[2]] * (HEAD_DIM ** -0.5 * LOG2E)
    wa = jnp.concatenate([wq, w_in[:, offs[2]:offs[4]]], axis=1).astype(BF16)
    wg = jnp.pad(w_in[:, offs[8]:offs[9]], ((0, 0), (0, LANES - n_gate))).astype(BF16)
    wm = w_in[:, offs[9]:offs[10]].astype(BF16)
    nq = seq // Q_BLOCK
    n_blk = seq // SEL_LEN
    nch = seq // CMP_STRIDE

    def head_padded(w, width):
        w = w.reshape(dm, N_KV, HEAD_DIM)
        return jnp.pad(w, ((0, 0), (0, 0), (0, width - HEAD_DIM))).reshape(dm, N_KV * width).astype(BF16)

    aug_w = 2 * HEAD_DIM + -(-n_blk // LANES) * LANES
    u2d, act, gates, mg, kaug, kw, vst, vwt = _input_projection(
        x, sh1, sc1, wu, wa, wg, wm, head_padded(w_in[:, offs[4]:offs[5]], aug_w),
        head_padded(w_in[:, offs[6]:offs[7]], 2 * HEAD_DIM),
        w_in[:, offs[5]:offs[6]].T.astype(BF16), w_in[:, offs[7]:offs[8]].T.astype(BF16))

    def piece(i):
        return act[:, :, attn_w + i * kv_w: attn_w + (i + 1) * kv_w]

    def compress(raw, pos, w1, w2):
        xc = raw.reshape(bsz, nch, CMP_STRIDE, N_KV, HEAD_DIM).transpose(0, 3, 1, 2, 4).reshape(bsz, N_KV, nch, CMP_STRIDE * HEAD_DIM)
        half = CMP_STRIDE * HEAD_DIM
        w1cat = jnp.concatenate([w1[:half], w1[half:]], axis=1).astype(BF16)
        posb = jnp.dot(pos.reshape(1, -1), w1, precision=lax.Precision.HIGHEST)
        return _compress(xc, w1cat, posb, w2.astype(BF16))

    kc = compress(piece(0), cmp_pos[0], cmp_w1[0], cmp_w2[0])
    vc = compress(piece(1), cmp_pos[1], cmp_w1[1], cmp_w2[1])
    pad = ((0, 0), (0, 0), (PAD_CMP, _padded_cmp_rows(seq) - nch - PAD_CMP), (0, 0))
    kcp = jnp.pad(kc, pad).astype(BF16)
    vcpt = jnp.swapaxes(jnp.pad(vc, pad), 2, 3).astype(BF16)
    ones_ovt = _overlap_t(seq)
    cmp_lhs = jnp.concatenate([vcpt, jnp.broadcast_to(ones_ovt, (bsz, N_KV) + ones_ovt.shape)], axis=2)

    nst = nq // NSA_TILES
    gt = gates[:, :, :n_gate].reshape(bsz, nst, NSA_TILES, Q_BLOCK, N_KV, GQA, 3)
    gt = gt.transpose(0, 4, 1, 6, 2, 5, 3).reshape(bsz, N_KV, nst, 3, COLS)
    gt = jnp.pad(gt, ((0, 0), (0, 0), (0, 0), (0, SUBLANES - 3), (0, 0)))
    nb, cb = _bias_tiles(rel_bias)
    y_nsa = _nsa(act, kaug, vst, kw, vwt, kcp, cmp_lhs, _cmp_visibility(seq), nb, cb, gt)

    wb, a, wc = _s5_params(lam_re, lam_im, log_step, b_re, b_im, c_re, c_im)
    z2 = _s5(u2d.reshape(seq * bsz, ssm_w), wb, a, wc, d_skip.reshape(1, ssm_w).astype(F32), bsz)
    z2d = z2.reshape(seq, bsz * ssm_w)

    base, h2 = _mix_out(z2d, y_nsa, mg, x, g1, sh2, sc2, g2, ln1_g.reshape(1, dm), ln1_b.reshape(1, dm),
                        glu_w.astype(BF16), w_out.astype(BF16), sg.astype(BF16), su.astype(BF16), sd.astype(BF16), alpha)
    h2 = h2.reshape(n_tok, dm // 2)

    n_exp = router_w.shape[1]
    rt = min(ROUTER_TILE, n_tok)
    tri = jnp.asarray(np.triu(np.ones((rt, rt), np.float32), 1), BF16)
    top_e, top_w, rank, counts = _router(h2, router_w.T.astype(BF16), router_bias.reshape(n_exp, 1).astype(F32), tri)
    counts = counts[:, 0].astype(jnp.int32)
    padded = (counts + EXPERT_TILE - 1) // EXPERT_TILE * EXPERT_TILE
    pad_end = jnp.cumsum(padded)
    pad_start = pad_end - padded
    dest = _dest_rows(pad_start.astype(jnp.int32), top_e, rank)
    n_rows = n_tok * TOP_K + n_exp * EXPERT_TILE
    n_tiles = n_rows // EXPERT_TILE
    tile_exp = jnp.minimum(jnp.searchsorted(pad_end, jnp.arange(n_tiles) * EXPERT_TILE, side='right'), n_exp - 1).astype(jnp.int32)
    n_used = (pad_end[-1] // EXPERT_TILE).astype(jnp.int32).reshape(1)
    mt = min(MOVE_TILE, seq)
    dest3 = dest.reshape(TOP_K, n_tok // mt, mt).transpose(1, 0, 2)
    xs = _dispatch(pad_end.astype(jnp.int32), padded.astype(jnp.int32), n_used, dest3, h2, n_rows)
    ys = _experts(tile_exp, n_used, xs, e_gate, e_up, e_down)
    return _final(dest3, top_w.T, base, g2, ln2_g.reshape(1, dm), ln2_b.reshape(1, dm), ys)


def kernel(x, c, ada_w, ada_b, w_in, ssm_lambda_re, ssm_lambda_im, ssm_log_step, ssm_b_re, ssm_b_im, ssm_c_re, ssm_c_im, ssm_d, ssm_glu_w, cmp_pos, cmp_w1, cmp_w2, rel_bias, w_out, ln1_g, ln1_b, router_w, router_bias, exp_w_gate, exp_w_up, exp_w_down, sh_w_gate, sh_w_up, sh_w_down, ln2_g, ln2_b):
    depth = ada_w.shape[0]
    alpha = (2 * depth) ** 0.25
    for l in range(depth):
        mod = _modulation(c, ada_w[l], ada_b[l])
        x = _layer(x, mod, w_in[l], ssm_lambda_re[l], ssm_lambda_im[l], ssm_log_step[l], ssm_b_re[l], ssm_b_im[l],
                   ssm_c_re[l], ssm_c_im[l], ssm_d[l], ssm_glu_w[l], cmp_pos[l], cmp_w1[l], cmp_w2[l], rel_bias,
                   w_out[l], ln1_g[l], ln1_b[l], router_w[l], router_bias[l], exp_w_gate[l], exp_w_up[l],
                   exp_w_down[l], sh_w_gate[l], sh_w_up[l], sh_w_down[l], ln2_g[l], ln2_b[l], alpha)
    return x
```

```python
import functools
import math

import numpy as np
import jax
import jax.numpy as jnp
from jax import lax
from jax.experimental import pallas as pl
from jax.experimental.pallas import tpu as pltpu

F32 = jnp.float32
BF16 = jnp.bfloat16

SSM_GROUP = 16
SSM_STATE = 64
N_HEADS = 16
HEAD_DIM = 64
N_KV = 4
GQA = N_HEADS // N_KV
CMP_LEN = 32
CMP_STRIDE = 16
SEL_LEN = 64
SEL_TOP = 16
WINDOW = 512
Q_BLOCK = 64
FORCE_BONUS = 1.0e4
N_FORCED = 3
assert GQA < FORCE_BONUS
REL_BUCKETS = 32
REL_MAX_DIST = 128
TOP_K = 8
ROUTED_SCALE = 2.5
LN_EPS = 1e-5

LANES = 128
SUBLANES = 8
VMEM_LIMIT_BYTES = 56 * 1024 * 1024
ROW_TILE = 512
ROUTER_TILE = 256
EXPERT_TILE = 512
MOVE_TILE = 256
DEST_TILE = 2048
SCAN_CHUNK = 128
SCAN_LANES = 512
S5_DIAG_BLOCKS = 2
MASK_BIG = 32768.0
NEG = -1.0e30
LOG2E = math.log2(math.e)
WIN_BLOCKS = WINDOW // SEL_LEN
ROWS = GQA * Q_BLOCK
NSA_TILES = 4
COLS = NSA_TILES * ROWS
CHUNK_BLOCKS = 8
CHUNK_KEYS = CHUNK_BLOCKS * SEL_LEN
FAR_BLOCKS = NSA_TILES
FAR_KEYS = FAR_BLOCKS * SEL_LEN
PAD_CMP = 8
BAND_ROWS = 24


def _cparams(sem):
    return pltpu.CompilerParams(dimension_semantics=sem, vmem_limit_bytes=VMEM_LIMIT_BYTES)


def _pack_rows(x):
    n = x.shape[1] // 2
    xb = x.astype(jnp.bfloat16).astype(F32)
    lo = lax.shift_right_logical(lax.bitcast_convert_type(xb[:, :n], jnp.uint32), jnp.uint32(16))
    return lax.bitcast_convert_type(xb[:, n:], jnp.uint32) | lo


def _unpack_rows(w):
    lo = lax.bitcast_convert_type(lax.shift_left(w, jnp.uint32(16)), F32)
    hi = lax.bitcast_convert_type(w & jnp.uint32(0xFFFF0000), F32)
    return lo, hi


def _norm_rows(x):
    mu = jnp.mean(x, axis=-1, keepdims=True)
    xc = x - mu
    var = jnp.mean(xc * xc, axis=-1, keepdims=True)
    return xc * lax.rsqrt(var + LN_EPS)


def _mod_kernel(c_ref, w_ref, b_ref, o_ref):
    cond = jax.nn.silu(c_ref[...])
    o_ref[...] = jnp.dot(cond.astype(BF16), w_ref[...].astype(BF16), preferred_element_type=F32) + b_ref[...]


def _modulation(c, ada_w, ada_b):
    bsz, dm = c.shape
    n = ada_w.shape[1]
    tn = dm
    return pl.pallas_call(
        _mod_kernel,
        grid=(n // tn,),
        in_specs=[pl.BlockSpec((bsz, dm), lambda j: (0, 0)),
                  pl.BlockSpec((dm, tn), lambda j: (0, j)),
                  pl.BlockSpec((1, tn), lambda j: (0, j))],
        out_specs=pl.BlockSpec((bsz, tn), lambda j: (0, j)),
        out_shape=jax.ShapeDtypeStruct((bsz, n), F32),
        compiler_params=_cparams(("arbitrary",)),
        name="ada_modulation",
    )(c, ada_w, ada_b.reshape(1, n))


def _inproj_kernel(x_ref, sh_ref, sc_ref, wu_ref, wa_ref, wg_ref, wm_ref, wks_ref, wkw_ref, wvs_ref, wvw_ref,
                   u_ref, a_ref, g_ref, m_ref, ks_ref, kw_ref, vs_ref, vw_ref, *, aug_w):
    h = _norm_rows(x_ref[0]) * (1.0 + sc_ref[0]) + sh_ref[0]
    hb = h.astype(BF16)
    tm = hb.shape[0]
    u_ref[...] = jnp.dot(hb, wu_ref[...], preferred_element_type=F32)
    a_ref[0] = jnp.dot(hb, wa_ref[...], preferred_element_type=F32).astype(BF16)
    g_ref[0] = jax.nn.sigmoid(jnp.dot(hb, wg_ref[...], preferred_element_type=F32))
    m_ref[0] = jax.nn.sigmoid(jnp.dot(hb, wm_ref[...], preferred_element_type=F32)).astype(BF16)
    ks = jnp.dot(hb, wks_ref[...], preferred_element_type=F32)
    assert aug_w & (aug_w - 1) == 0
    col = (lax.broadcasted_iota(jnp.int32, ks.shape, 1) & (aug_w - 1)) - 2 * HEAD_DIM
    blk = jnp.right_shift(pl.program_id(1) * tm + lax.broadcasted_iota(jnp.int32, ks.shape, 0), SEL_LEN.bit_length() - 1)
    ks_ref[0] = jnp.where(col == blk, -MASK_BIG, ks).astype(BF16)
    kw_ref[0] = jnp.dot(hb, wkw_ref[...], preferred_element_type=F32).astype(BF16)
    rows = lax.broadcasted_iota(jnp.int32, (2 * SUBLANES, tm), 0)
    ones_rows = jnp.where(rows == 0, 1.0, 0.0).astype(BF16)
    for w_ref, v_ref in ((wvs_ref, vs_ref), (wvw_ref, vw_ref)):
        vt = lax.dot_general(w_ref[...], hb, (((1,), (1,)), ((), ())), preferred_element_type=F32).astype(BF16)
        for g in range(N_KV):
            v_ref[0, g] = jnp.concatenate([vt[g * HEAD_DIM:(g + 1) * HEAD_DIM], ones_rows], axis=0)


def _input_projection(x, sh1, sc1, wu, wa, wg, wm, wks, wkw, wvs_t, wvw_t):
    bsz, seq, dm = x.shape
    tm = min(ROW_TILE, seq)
    nu, na, ng, nm = wu.shape[1], wa.shape[1], wg.shape[1], wm.shape[1]
    full = lambda w: pl.BlockSpec(w.shape, lambda b, i: (0, 0))
    vec = pl.BlockSpec((1, 1, dm), lambda b, i: (b, 0, 0))
    rows_out = lambda n: pl.BlockSpec((1, tm, n), lambda b, i: (b, i, 0))
    vt_rows = HEAD_DIM + 2 * SUBLANES
    vt_out = pl.BlockSpec((1, N_KV, vt_rows, tm), lambda b, i: (b, 0, 0, i))
    return pl.pallas_call(
        functools.partial(_inproj_kernel, aug_w=wks.shape[1] // N_KV),
        grid=(bsz, seq // tm),
        in_specs=[pl.BlockSpec((1, tm, dm), lambda b, i: (b, i, 0)), vec, vec,
                  full(wu), full(wa), full(wg), full(wm), full(wks), full(wkw), full(wvs_t), full(wvw_t)],
        out_specs=[pl.BlockSpec((tm, nu), lambda b, i: (i, b)), rows_out(na), rows_out(ng), rows_out(nm),
                   rows_out(wks.shape[1]), rows_out(wkw.shape[1]), vt_out, vt_out],
        out_shape=[jax.ShapeDtypeStruct((seq, bsz * nu), F32),
                   jax.ShapeDtypeStruct((bsz, seq, na), BF16),
                   jax.ShapeDtypeStruct((bsz, seq, ng), F32),
                   jax.ShapeDtypeStruct((bsz, seq, nm), BF16),
                   jax.ShapeDtypeStruct((bsz, seq, wks.shape[1]), BF16),
                   jax.ShapeDtypeStruct((bsz, seq, wkw.shape[1]), BF16),
                   jax.ShapeDtypeStruct((bsz, N_KV, vt_rows, seq), BF16),
                   jax.ShapeDtypeStruct((bsz, N_KV, vt_rows, seq), BF16)],
        compiler_params=_cparams(("arbitrary", "arbitrary")),
        name="adaln_input_projection",
    )(x, sh1, sc1, wu, wa, wg, wm, wks, wkw, wvs_t, wvw_t)


def _compress_kernel(x_ref, w1_ref, pb_ref, w2_ref, o_ref, *, n_cmp):
    hid = w2_ref.shape[0]
    p = jnp.dot(x_ref[0, 0], w1_ref[...], preferred_element_type=F32)
    nrow = p.shape[0]
    nxt = pltpu.roll(p[:, hid:], nrow - 1, 0)
    hidv = jax.nn.gelu(p[:, :hid] + nxt + pb_ref[...])
    out = jnp.dot(hidv.astype(BF16), w2_ref[...], preferred_element_type=F32)
    rows = lax.broadcasted_iota(jnp.int32, out.shape, 0)
    o_ref[0, 0] = jnp.where(rows < n_cmp, out, 0.0)


def _compress(xc, w1cat, posb, w2):
    bsz, nkv, nch, kdim = xc.shape
    hid2 = w1cat.shape[1]
    return pl.pallas_call(
        functools.partial(_compress_kernel, n_cmp=nch - 1),
        grid=(bsz, nkv),
        in_specs=[pl.BlockSpec((1, 1, nch, kdim), lambda b, g: (b, g, 0, 0)),
                  pl.BlockSpec((kdim, hid2), lambda b, g: (0, 0)),
                  pl.BlockSpec((1, hid2 // 2), lambda b, g: (0, 0)),
                  pl.BlockSpec((hid2 // 2, HEAD_DIM), lambda b, g: (0, 0))],
        out_specs=pl.BlockSpec((1, 1, nch, HEAD_DIM), lambda b, g: (b, g, 0, 0)),
        out_shape=jax.ShapeDtypeStruct((bsz, nkv, nch, HEAD_DIM), F32),
        compiler_params=_cparams(("arbitrary", "arbitrary")),
        name="kv_compress",
    )(xc, w1cat, posb, w2)


def _swap_heads_tokens(a):
    t = jnp.concatenate([a, jnp.zeros_like(a)], axis=0).T
    p = [t[r * HEAD_DIM:(r + 1) * HEAD_DIM] for r in range(GQA)]
    return jnp.concatenate([p[r] + pltpu.roll(p[r + 1], Q_BLOCK, 1) for r in range(0, GQA, 2)], axis=1)


def _nsa_kernel(q_ref, kaug_ref, vst_ref, kw_ref, vwt_ref, kcp_ref, cl_ref, vis_ref, nb_ref, cb_ref,
                g_ref, o_ref, s_ref, c_ref, w_ref, fa_ref, fb_ref, *, n_blk, n_sel):
    q0 = pl.program_id(2) * NSA_TILES
    qf = q_ref[0].astype(F32)
    qt = jnp.concatenate([_swap_heads_tokens(qf[n * Q_BLOCK:(n + 1) * Q_BLOCK]) for n in range(NSA_TILES)],
                         axis=1).astype(BF16)
    col_tile = lambda n: slice(n * ROWS, (n + 1) * ROWS)

    s_ref[...] = jnp.dot(kcp_ref[0, 0], qt, preferred_element_type=F32)
    for n in range(NSA_TILES):
        band = pl.multiple_of(SUBLANES * ((q0 + n) // 2), SUBLANES)
        s_ref[pl.ds(band, BAND_ROWS), col_tile(n)] += cb_ref[0, n % 2]
    s = jnp.where(vis_ref[...] <= Q_BLOCK * q0, s_ref[...], -jnp.inf)
    mx = jnp.max(s, axis=0, keepdims=True)
    mx = jnp.where(mx == -jnp.inf, 0.0, mx)
    e = jnp.exp2((s - mx).astype(BF16))
    both = jnp.dot(cl_ref[0, 0], e, preferred_element_type=F32)
    inv = 1.0 / jnp.maximum(both[HEAD_DIM:HEAD_DIM + 1], 1e-30)
    o_cmp = both[:HEAD_DIM] * inv

    imp4 = both[HEAD_DIM + 2 * SUBLANES:] * inv
    sums = []
    for n in range(NSA_TILES):
        two = imp4[:, n * ROWS:n * ROWS + LANES] + imp4[:, n * ROWS + LANES:(n + 1) * ROWS]
        sums.append(two + pltpu.roll(two, Q_BLOCK, 1))
    low = lax.broadcasted_iota(jnp.int32, sums[0].shape, 1) < Q_BLOCK
    imp = jnp.concatenate([jnp.where(low, sums[n], sums[n + 1]) for n in range(0, NSA_TILES, 2)], axis=1)
    blk = lax.broadcasted_iota(jnp.int32, imp.shape, 0)
    cur = q0 + jnp.right_shift(lax.broadcasted_iota(jnp.int32, imp.shape, 1), Q_BLOCK.bit_length() - 1)
    blkf = blk.astype(F32)
    forced = (blk == 0) | (blk == cur) | (blk == cur - 1)
    score = jnp.where((blk <= cur) & ~forced, imp, -jnp.inf)
    notsel = jnp.where(forced, 0.0, 1.0)
    for _ in range(n_sel - N_FORCED):
        best = jnp.max(score, axis=0, keepdims=True)
        first = jnp.min(jnp.where(score == best, blkf, float(n_blk)), axis=0, keepdims=True)
        pick = blkf == first
        notsel = jnp.where(pick, 0.0, notsel)
        score = jnp.where(pick, -jnp.inf, score)
    halves = []
    for n in range(0, NSA_TILES, 2):
        pair = notsel[:, (n // 2) * LANES:(n // 2 + 1) * LANES]
        swapped = pltpu.roll(pair, Q_BLOCK, 1)
        halves += [jnp.where(low, pair, swapped), jnp.where(low, swapped, pair)]
    notsel = jnp.concatenate([h for h in halves for _ in (0, 1)], axis=1).astype(BF16)
    qwin = jnp.concatenate([qt, jnp.zeros_like(qt)], axis=0)
    tail = kaug_ref.shape[2] - 2 * HEAD_DIM - n_blk
    qaug = jnp.concatenate([qwin, notsel] + ([jnp.zeros((tail, COLS), notsel.dtype)] if tail else []), axis=0)

    def update(carry, sc, vt, top=None):
        m, acc = carry
        m_new = jnp.maximum(m, jnp.max(sc, axis=0, keepdims=True) if top is None else top)
        p = jnp.exp2((sc - m_new).astype(BF16))
        return m_new, jnp.exp2(m - m_new) * acc + jnp.dot(vt, p, preferred_element_type=F32)

    def chunk_keys(c):
        return pl.ds(pl.multiple_of(c * FAR_KEYS, FAR_KEYS), FAR_KEYS)

    def far_scores(c, buf_ref):
        sc = jnp.dot(kaug_ref[0, chunk_keys(c), :], qaug, preferred_element_type=F32)
        buf_ref[...] = sc
        return jnp.max(sc, axis=0, keepdims=True)

    def far_pair(i, carry):
        m, acc, top_a = carry
        top_b = far_scores(2 * i + 1, fb_ref)
        m, acc = update((m, acc), fa_ref[...], vst_ref[0, 0, :, chunk_keys(2 * i)], top_a)
        top_a = far_scores(2 * i + 2, fa_ref)
        m, acc = update((m, acc), fb_ref[...], vst_ref[0, 0, :, chunk_keys(2 * i + 1)], top_b)
        return m, acc, top_a

    def far_last(_, carry, n_far):
        m, acc, top_a = carry
        return update((m, acc), fa_ref[...], vst_ref[0, 0, :, chunk_keys(n_far - 1)], top_a) + (top_a,)

    def near_chunk(k_ref, vt_ref, qmat, blk0, tile_of, carry, buf_ref):
        n_keys = buf_ref.shape[0]
        keys = pl.ds(pl.multiple_of(blk0 * SEL_LEN, NSA_TILES * SEL_LEN), n_keys)
        buf_ref[...] = jnp.dot(k_ref[0, keys, :], qmat, preferred_element_type=F32)
        for o in range(n_keys // SEL_LEN):
            for n in range(NSA_TILES):
                buf_ref[o * SEL_LEN:(o + 1) * SEL_LEN, col_tile(n)] += nb_ref[0, tile_of(blk0 + o, q0 + n)]
        return update(carry, buf_ref[...], vt_ref[0, 0, :, keys])

    def sel_tile(j, qi):
        d = qi - j
        return jnp.where(d < 0, 4, jnp.minimum(d, 3))

    def win_tile(j, qi):
        d = qi - j
        return jnp.where((d < 0) | (d > WIN_BLOCKS), 4, jnp.where(d == WIN_BLOCKS, 5, jnp.minimum(d, 3)))

    init = (jnp.full((1, COLS), NEG, F32), jnp.zeros((vst_ref.shape[2], COLS), F32))
    n_far = jnp.maximum(q0 - 2, 0) // FAR_BLOCKS
    carry = lax.fori_loop(0, n_far // 2, far_pair, init + (far_scores(0, fa_ref),))
    carry = lax.fori_loop(0, n_far % 2, functools.partial(far_last, n_far=n_far), carry)
    _, acc = near_chunk(kaug_ref, vst_ref, qaug, n_far * FAR_BLOCKS, sel_tile, carry[:2], c_ref)
    o_slc = acc[:HEAD_DIM] / acc[HEAD_DIM:HEAD_DIM + 1]
    _, acc = near_chunk(kw_ref, vwt_ref, qwin, jnp.maximum(q0 - WIN_BLOCKS, 0), win_tile, init, w_ref)
    o_win = acc[:HEAD_DIM] / acc[HEAD_DIM:HEAD_DIM + 1]
    g = g_ref[0, 0, 0]
    out = g[0:1] * o_cmp + g[1:2] * o_slc + g[2:3] * o_win
    o_ref[0] = jnp.concatenate([_swap_heads_tokens(out[:, col_tile(n)]) for n in range(NSA_TILES)],
                               axis=0).astype(BF16)


def _nsa(act, kaug, vst, kw, vwt, kcp, cmp_lhs, vis, nb, cb, gt):
    bsz, nkv, nsteps, _, cols = gt.shape
    dh = HEAD_DIM
    seq = kw.shape[1]
    step_tokens = NSA_TILES * Q_BLOCK
    q_blk = pl.BlockSpec((1, step_tokens, GQA * dh), lambda b, g, i: (b, i, g))
    n_blk = seq // SEL_LEN
    ncp = kcp.shape[2]
    assert n_blk % CHUNK_BLOCKS == 0 and n_blk >= WIN_BLOCKS + NSA_TILES and CHUNK_BLOCKS == FAR_BLOCKS + NSA_TILES
    per_bg = lambda shape: pl.BlockSpec((1, 1) + shape, lambda b, g, i: (b, g, 0, 0))
    per_step = lambda shape: pl.BlockSpec((1, 1, 1) + shape, lambda b, g, i: (b, g, i, 0, 0))
    head_cols = lambda a: pl.BlockSpec((1, seq, a.shape[2] // nkv), lambda b, g, i: (b, 0, g))
    kern = functools.partial(_nsa_kernel, n_blk=n_blk, n_sel=min(SEL_TOP, n_blk))
    return pl.pallas_call(
        kern,
        grid=(bsz, nkv, nsteps),
        in_specs=[q_blk,
                  head_cols(kaug), per_bg((vst.shape[2], seq)), head_cols(kw), per_bg((vwt.shape[2], seq)),
                  per_bg((ncp, dh)), per_bg((cmp_lhs.shape[2], ncp)),
                  pl.BlockSpec(vis.shape, lambda b, g, i: (0, 0)),
                  pl.BlockSpec((1,) + nb.shape[1:], lambda b, g, i: (g, 0, 0, 0)),
                  pl.BlockSpec((1,) + cb.shape[1:], lambda b, g, i: (g, 0, 0, 0)),
                  per_step((SUBLANES, cols))],
        out_specs=q_blk,
        out_shape=jax.ShapeDtypeStruct((bsz, seq, nkv * GQA * dh), BF16),
        scratch_shapes=[pltpu.VMEM((ncp, cols), F32), pltpu.VMEM((CHUNK_KEYS, cols), F32),
                        pltpu.VMEM(((WIN_BLOCKS + NSA_TILES) * SEL_LEN, cols), F32),
                        pltpu.VMEM((FAR_KEYS, cols), F32), pltpu.VMEM((FAR_KEYS, cols), F32)],
        compiler_params=_cparams(("arbitrary", "arbitrary", "arbitrary")),
        name="nsa_attention",
    )(act, kaug, vst, kw, vwt, kcp, cmp_lhs, vis, nb, cb, gt)


def _s5_kernel(u_ref, wb_ref, a_ref, wc_ref, d_ref, z_ref, xs_ref, st_ref, *, bsz, n_state):
    @pl.when(pl.program_id(0) == 0)
    def _():
        st_ref[...] = jnp.zeros_like(st_ref)

    u = u_ref[...]
    ub = u.astype(BF16)
    n_diag = wb_ref.shape[0]
    c_in, c_st = u.shape[1] // n_diag, n_state // n_diag
    for k in range(n_diag):
        drv = jnp.dot(ub[:, k * c_in:(k + 1) * c_in], wb_ref[k], preferred_element_type=F32)
        xs_ref[:, k * c_st:(k + 1) * c_st] = drv[:, :c_st]
        xs_ref[:, n_state + k * c_st:n_state + (k + 1) * c_st] = drv[:, c_st:]
    steps = u.shape[0] // bsz
    for c0 in range(0, n_state, SCAN_LANES):
        re = pl.ds(c0, SCAN_LANES)
        im = pl.ds(n_state + c0, SCAN_LANES)
        ar = jnp.broadcast_to(a_ref[0:1, re], (bsz, SCAN_LANES))
        ai = jnp.broadcast_to(a_ref[0:1, im], (bsz, SCAN_LANES))

        def step(t, carry):
            xr, xi = carry
            rows = pl.ds(pl.multiple_of(t * bsz, bsz), bsz)
            nr = ar * xr - ai * xi + xs_ref[rows, re]
            ni = ar * xi + ai * xr + xs_ref[rows, im]
            xs_ref[rows, re] = nr
            xs_ref[rows, im] = ni
            return nr, ni

        xr, xi = lax.fori_loop(0, steps, step, (st_ref[:, re], st_ref[:, im]), unroll=8)
        st_ref[:, re] = xr
        st_ref[:, im] = xi
    ys = []
    for k in range(n_diag):
        st = jnp.concatenate([xs_ref[:, k * c_st:(k + 1) * c_st],
                              xs_ref[:, n_state + k * c_st:n_state + (k + 1) * c_st]], axis=1)
        ys.append(jnp.dot(st.astype(BF16), wc_ref[k], preferred_element_type=F32))
    y = jnp.concatenate(ys, axis=1) + d_ref[...] * u
    z_ref[...] = jax.nn.gelu(y).astype(BF16)


def _s5(u2, wb, a, wc, dsk, bsz):
    rows, width = u2.shape
    seq = rows // bsz
    chunk = min(SCAN_CHUNK, seq)
    n_state2 = a.shape[1]
    return pl.pallas_call(
        functools.partial(_s5_kernel, bsz=bsz, n_state=n_state2 // 2),
        grid=(seq // chunk,),
        in_specs=[pl.BlockSpec((chunk * bsz, width), lambda i: (i, 0)),
                  pl.BlockSpec(wb.shape, lambda i: (0, 0, 0)),
                  pl.BlockSpec(a.shape, lambda i: (0, 0)),
                  pl.BlockSpec(wc.shape, lambda i: (0, 0, 0)),
                  pl.BlockSpec(dsk.shape, lambda i: (0, 0))],
        out_specs=pl.BlockSpec((chunk * bsz, width), lambda i: (i, 0)),
        out_shape=jax.ShapeDtypeStruct((rows, width), BF16),
        scratch_shapes=[pltpu.VMEM((chunk * bsz, n_state2), F32), pltpu.VMEM((bsz, n_state2), F32)],
        compiler_params=_cparams(("arbitrary",)),
        name="s5_scan",
    )(u2, wb, a, wc, dsk)


def _mixout_kernel(z_ref, yn_ref, mg_ref, x_ref, g1_ref, sh2_ref, sc2_ref, g2_ref, lg_ref, lb_ref,
                   glu_ref, wo_ref, sg_ref, su_ref, sd_ref, base_ref, h_ref, *, alpha):
    dm = x_ref.shape[2]
    glu = jnp.dot(z_ref[...], glu_ref[...], preferred_element_type=F32)
    y_ssm = glu[:, :dm] * jax.nn.sigmoid(glu[:, dm:])
    mg = mg_ref[0].astype(F32)
    merged = mg[:, :dm] * y_ssm + mg[:, dm:] * yn_ref[0].astype(F32)
    y = jnp.dot(merged.astype(BF16), wo_ref[...], preferred_element_type=F32)
    x1 = _norm_rows(alpha * x_ref[0] + g1_ref[0] * y) * lg_ref[...] + lb_ref[...]
    hf = _norm_rows(x1) * (1.0 + sc2_ref[0]) + sh2_ref[0]
    h_ref[0] = _pack_rows(hf)
    h = hf.astype(BF16)
    hs =jax.nn.silu(jnp.dot(h, sg_ref[...], preferred_element_type=F32)) * jnp.dot(h, su_ref[...], preferred_element_type=F32)
    shared = jnp.dot(hs.astype(BF16), sd_ref[...], preferred_element_type=F32)
    base_ref[0] = alpha * x1 + g2_ref[0] * shared


def _mix_out(z2d, y_nsa, mg, x, g1, sh2, sc2, g2, ln_g, ln_b, glu_w, w_out, sg, su, sd, alpha):
    bsz, seq, dm = x.shape
    tm = min(ROW_TILE, seq)
    width = z2d.shape[1] // bsz
    vec = pl.BlockSpec((1, 1, dm), lambda b, i: (b, 0, 0))
    row = pl.BlockSpec((1, dm), lambda b, i: (0, 0))
    full = lambda w: pl.BlockSpec(w.shape, lambda b, i: (0, 0))
    tile = lambda n: pl.BlockSpec((1, tm, n), lambda b, i: (b, i, 0))
    return pl.pallas_call(
        functools.partial(_mixout_kernel, alpha=alpha),
        grid=(bsz, seq // tm),
        in_specs=[pl.BlockSpec((tm, width), lambda b, i: (i, b)), tile(dm), tile(2 * dm), tile(dm),
                  vec, vec, vec, vec, row, row, full(glu_w), full(w_out), full(sg), full(su), full(sd)],
        out_specs=[tile(dm), tile(dm // 2)],
        out_shape=[jax.ShapeDtypeStruct((bsz, seq, dm), F32), jax.ShapeDtypeStruct((bsz, seq, dm // 2), jnp.uint32)],
        compiler_params=_cparams(("arbitrary", "arbitrary")),
        name="merge_outproj_ln_shared",
    )(z2d, y_nsa, mg, x, g1, sh2, sc2, g2, ln_g, ln_b, glu_w, w_out, sg, su, sd)


def _router_kernel(h_ref, rwt_ref, rb_ref, tri_ref, e_ref, w_ref, p_ref, cnt_ref):
    @pl.when(pl.program_id(0) == 0)
    def _():
        cnt_ref[...] = jnp.zeros_like(cnt_ref)

    h = jnp.concatenate(_unpack_rows(h_ref[...]), axis=1).astype(BF16)
    logits = lax.dot_general(rwt_ref[...], h, (((1,), (1,)), ((), ())), preferred_element_type=F32)
    scores = jax.nn.sigmoid(logits)
    cur = scores + rb_ref[...]
    n_exp = scores.shape[0]
    eid = lax.broadcasted_iota(jnp.int32, scores.shape, 0).astype(F32)
    chosen = jnp.zeros(scores.shape, F32)
    ids, vals = [], []
    for _ in range(TOP_K):
        best = jnp.max(cur, axis=0, keepdims=True)
        first = jnp.min(jnp.where(cur == best, eid, float(n_exp)), axis=0, keepdims=True)
        pick = eid == first
        ids.append(first)
        vals.append(jnp.sum(jnp.where(pick, scores, 0.0), axis=0, keepdims=True))
        chosen = jnp.where(pick, 1.0, chosen)
        cur = jnp.where(pick, -jnp.inf, cur)
    top_s = jnp.concatenate(vals, axis=0)
    w_ref[...] = top_s / jnp.sum(top_s, axis=0, keepdims=True) * ROUTED_SCALE
    top_e = jnp.concatenate(ids, axis=0)
    e_ref[...] = top_e.astype(jnp.int32)
    before = jnp.dot(chosen.astype(BF16), tri_ref[...], preferred_element_type=F32) + cnt_ref[...]
    ranks = [jnp.sum(jnp.where(eid == ids[k], before, 0.0), axis=0, keepdims=True) for k in range(TOP_K)]
    p_ref[...] = jnp.concatenate(ranks, axis=0).astype(jnp.int32)
    cnt_ref[...] += jnp.sum(chosen, axis=1, keepdims=True)


def _router(h2, rwt, rb, tri):
    n_tok, words = h2.shape
    n_exp, dm = rwt.shape
    tm = tri.shape[0]
    kt = pl.BlockSpec((TOP_K, tm), lambda i: (0, i))
    return pl.pallas_call(
        _router_kernel,
        grid=(n_tok // tm,),
        in_specs=[pl.BlockSpec((tm, words), lambda i: (i, 0)),
                  pl.BlockSpec((n_exp, dm), lambda i: (0, 0)),
                  pl.BlockSpec((n_exp, 1), lambda i: (0, 0)),
                  pl.BlockSpec((tm, tm), lambda i: (0, 0))],
        out_specs=[kt, kt, kt, pl.BlockSpec((n_exp, 1), lambda i: (0, 0))],
        out_shape=[jax.ShapeDtypeStruct((TOP_K, n_tok), jnp.int32),
                   jax.ShapeDtypeStruct((TOP_K, n_tok), F32),
                   jax.ShapeDtypeStruct((TOP_K, n_tok), jnp.int32),
                   jax.ShapeDtypeStruct((n_exp, 1), F32)],
        compiler_params=_cparams(("arbitrary",)),
        name="router_topk_rank",
    )(h2, rwt, rb, tri)


def _dest_kernel(start_ref, e_ref, r_ref, o_ref):
    e = e_ref[...]
    start = lax.fori_loop(0, start_ref.shape[0], lambda j, acc: jnp.where(e == j, start_ref[j], acc),
                          jnp.zeros(e.shape, jnp.int32))
    o_ref[...] = start + r_ref[...]


def _dest_rows(pad_start, top_e, rank):
    n_tok = top_e.shape[1]
    tm = min(DEST_TILE, n_tok)
    blk = pl.BlockSpec((TOP_K, tm), lambda i, ps: (0, i))
    return pl.pallas_call(
        _dest_kernel,
        grid_spec=pltpu.PrefetchScalarGridSpec(num_scalar_prefetch=1, grid=(n_tok // tm,), in_specs=[blk, blk], out_specs=blk),
        out_shape=jax.ShapeDtypeStruct(top_e.shape, jnp.int32),
        compiler_params=_cparams(("arbitrary",)),
        name="moe_dest_rows",
    )(pad_start, top_e, rank)


def _row_copy(src_ref, src_row, dst_ref, dst_row, sem):
    return pltpu.make_async_copy(src_ref.at[pl.ds(src_row, 1), :], dst_ref.at[pl.ds(dst_row, 1), :], sem)


def _dispatch_kernel(pend_ref, padded_ref, nused_ref, dest_ref, h_ref, xs_ref, zero_ref, sem, zsem):
    tm = h_ref.shape[0]

    @pl.when(pl.program_id(0) == 0)
    def _():
        zero_ref[...] = jnp.zeros_like(zero_ref)
        n_tiles = xs_ref.shape[0] // EXPERT_TILE

        def zero_tile(row):
            return pltpu.make_async_copy(zero_ref, xs_ref.at[pl.ds(pl.multiple_of(row, EXPERT_TILE), EXPERT_TILE), :], zsem)

        def per_expert(act):
            def body(e, carry):
                @pl.when(padded_ref[e] > 0)
                def _():
                    act(zero_tile(pend_ref[e] - EXPERT_TILE))
                return carry
            lax.fori_loop(0, pend_ref.shape[0], body, 0)

        def per_unused(act):
            def body(i, carry):
                act(zero_tile(i * EXPERT_TILE))
                return carry
            lax.fori_loop(nused_ref[0], n_tiles, body, 0)

        for loop in (per_expert, per_unused):
            loop(lambda copy: copy.start())
        for loop in (per_expert, per_unused):
            loop(lambda copy: copy.wait())

    def issue(t8, carry):
        base = pl.multiple_of(t8 * SUBLANES, SUBLANES)
        for s in range(SUBLANES):
            for k in range(TOP_K):
                _row_copy(h_ref, base + s, xs_ref, dest_ref[0, k, base + s], sem).start(priority=k % 2)
        return carry

    lax.fori_loop(0, tm // SUBLANES, issue, 0)
    pltpu.make_async_copy(xs_ref.at[pl.ds(0, TOP_K * tm), :], xs_ref.at[pl.ds(0, TOP_K * tm), :], sem).wait()


def _dispatch(pad_end, padded, n_used, dest3, h2, n_rows):
    n_tok, words = h2.shape
    tm = dest3.shape[2]
    grid_spec = pltpu.PrefetchScalarGridSpec(
        num_scalar_prefetch=3,
        grid=(n_tok // tm,),
        in_specs=[pl.BlockSpec((1, TOP_K, tm), lambda i, *_: (i, 0, 0), memory_space=pltpu.SMEM),
                  pl.BlockSpec((tm, words), lambda i, *_: (i, 0))],
        out_specs=pl.BlockSpec(memory_space=pl.ANY),
        scratch_shapes=[pltpu.VMEM((EXPERT_TILE, words), jnp.uint32), pltpu.SemaphoreType.DMA(()),
                        pltpu.SemaphoreType.DMA(())],
    )
    return pl.pallas_call(
        _dispatch_kernel,
        grid_spec=grid_spec,
        out_shape=jax.ShapeDtypeStruct((n_rows, words), jnp.uint32),
        compiler_params=_cparams(("arbitrary",)),
        name="moe_dispatch",
    )(pad_end, padded, n_used, dest3, h2)


def _expert_kernel(te_ref, nu_ref, x_ref, wg_ref, wu_ref, wd_ref, y_ref):
    i = pl.program_id(0)

    @pl.when(i < nu_ref[0])
    def _():
        lo, hi = _unpack_rows(x_ref[...])
        lo, hi = lo.astype(BF16), hi.astype(BF16)
        half = lo.shape[1]

        def proj(w_ref):
            return (jnp.dot(lo, w_ref[0, :half, :].astype(BF16), preferred_element_type=F32)
                    + jnp.dot(hi, w_ref[0, half:, :].astype(BF16), preferred_element_type=F32))

        hmid = jax.nn.silu(proj(wg_ref)) * proj(wu_ref)
        y_ref[...] = _pack_rows(jnp.dot(hmid.astype(BF16), wd_ref[0].astype(BF16), preferred_element_type=F32))

    @pl.when(i >= nu_ref[0])
    def _():
        y_ref[...] = jnp.zeros_like(y_ref)


def _experts(tile_exp, n_used, xs, wg, wu, wd):
    n_rows, words = xs.shape
    dm, de = wg.shape[1], wg.shape[2]
    grid_spec = pltpu.PrefetchScalarGridSpec(
        num_scalar_prefetch=2,
        grid=(n_rows // EXPERT_TILE,),
        in_specs=[pl.BlockSpec((EXPERT_TILE, words), lambda i, te, nu: (i, 0)),
                  pl.BlockSpec((1, dm, de), lambda i, te, nu: (te[i], 0, 0)),
                  pl.BlockSpec((1, dm, de), lambda i, te, nu: (te[i], 0, 0)),
                  pl.BlockSpec((1, de, dm), lambda i, te, nu: (te[i], 0, 0))],
        out_specs=pl.BlockSpec((EXPERT_TILE, words), lambda i, te, nu: (i, 0)),
    )
    return pl.pallas_call(
        _expert_kernel,
        grid_spec=grid_spec,
        out_shape=jax.ShapeDtypeStruct((n_rows, words), jnp.uint32),
        compiler_params=_cparams(("arbitrary",)),
        name="expert_mlp",
    )(tile_exp, n_used, xs, wg, wu, wd)


def _final_kernel(dest_ref, next_ref, w_ref, base_ref, g2_ref, lg_ref, lb_ref, ys_ref, o_ref, buf_ref, sem):
    i = pl.program_id(0)
    tm = base_ref.shape[1]
    slot = i % 2

    def gather(rows_ref, to):
        def issue(t8, carry):
            base = pl.multiple_of(t8 * SUBLANES, SUBLANES)
            for s in range(SUBLANES):
                for k in range(TOP_K):
                    _row_copy(ys_ref, rows_ref[0, k, base + s], buf_ref.at[to, k], base + s,
                              sem.at[to]).start(priority=k % 2)
            return carry

        lax.fori_loop(0, tm // SUBLANES, issue, 0)

    @pl.when(i == 0)
    def _():
        gather(dest_ref, slot)

    @pl.when(i + 1 < pl.num_programs(0))
    def _():
        gather(next_ref, 1 - slot)

    pltpu.make_async_copy(buf_ref.at[slot], buf_ref.at[slot], sem.at[slot]).wait()
    w = w_ref[...]
    lo_sum, hi_sum = jnp.zeros((tm, buf_ref.shape[3]), F32), jnp.zeros((tm, buf_ref.shape[3]), F32)
    for k in range(TOP_K):
        lo, hi = _unpack_rows(buf_ref[slot, k])
        lo_sum += w[:, k:k + 1] * lo
        hi_sum += w[:, k:k + 1] * hi
    routed = jnp.concatenate([lo_sum, hi_sum], axis=1)
    o_ref[0] = _norm_rows(base_ref[0] + g2_ref[0] * routed) * lg_ref[...] + lb_ref[...]


def _final(dest3, w_tok, base, g2, ln_g, ln_b, ys):
    bsz, seq, dm = base.shape
    tm = dest3.shape[2]
    nt = seq // tm
    n_tiles = bsz * nt
    tile = pl.BlockSpec((1, tm, dm), lambda i: (i // nt, i % nt, 0))
    rows_of = lambda f: pl.BlockSpec((1, TOP_K, tm), lambda i: (f(i), 0, 0), memory_space=pltpu.SMEM)
    return pl.pallas_call(
        _final_kernel,
        grid=(n_tiles,),
        in_specs=[rows_of(lambda i: i), rows_of(lambda i: jnp.minimum(i + 1, n_tiles - 1)),
                  pl.BlockSpec((tm, TOP_K), lambda i: (i, 0)),
                  tile, pl.BlockSpec((1, 1, dm), lambda i: (i // nt, 0, 0)),
                  pl.BlockSpec((1, dm), lambda i: (0, 0)), pl.BlockSpec((1, dm), lambda i: (0, 0)),
                  pl.BlockSpec(memory_space=pl.ANY)],
        out_specs=tile,
        out_shape=jax.ShapeDtypeStruct((bsz, seq, dm), F32),
        scratch_shapes=[pltpu.VMEM((2, TOP_K, tm, ys.shape[1]), jnp.uint32), pltpu.SemaphoreType.DMA((2,))],
        compiler_params=_cparams(("arbitrary",)),
        name="combine_final_layernorm",
    )(dest3, dest3, w_tok, base, g2, ln_g, ln_b, ys)


def _rel_bucket(dist):
    dist = jnp.maximum(dist, 0)
    exact = REL_BUCKETS // 2
    log_ratio = jnp.log(jnp.maximum(dist, 1).astype(F32) / exact) / math.log(REL_MAX_DIST / exact)
    large = jnp.minimum(exact + (log_ratio * (REL_BUCKETS - exact)).astype(jnp.int32), REL_BUCKETS - 1)
    return jnp.where(dist < exact, dist, large)


def _bias_tiles(rel_bias):
    n_d = 4 * SEL_LEN
    vec = rel_bias[_rel_bucket(jnp.arange(n_d))].T
    far = rel_bias[REL_BUCKETS - 1]
    vec = ((vec - far[:, None]) * LOG2E).reshape(N_KV, GQA, n_d)
    tok = np.arange(Q_BLOCK)[None, :]
    key = np.arange(SEL_LEN)[:, None]
    padded = jnp.pad(vec, ((0, 0), (0, 0), (Q_BLOCK, 0)))

    def by_distance(d):
        first = np.clip(d[:, 0], -Q_BLOCK, n_d - Q_BLOCK) + Q_BLOCK
        rows = [lax.slice_in_dim(padded, int(f), int(f) + Q_BLOCK, axis=2) for f in first]
        return jnp.stack(rows, axis=2)

    def toeplitz(d):
        vals = jnp.where(jnp.asarray(d >= 0), by_distance(d), NEG)
        return jnp.transpose(vals, (0, 2, 1, 3)).reshape(N_KV, d.shape[0], ROWS)

    near = [toeplitz(delta + tok - key) for delta in (0, SEL_LEN, 2 * SEL_LEN)]
    zero = jnp.zeros((N_KV, SEL_LEN, ROWS), F32)
    edge = np.where(tok < key, 0.0, NEG).astype(np.float32)
    edge = jnp.broadcast_to(jnp.asarray(np.tile(edge, (1, GQA)))[None], (N_KV, SEL_LEN, ROWS))
    nb = jnp.stack(near + [zero, jnp.full_like(zero, NEG), edge], axis=1)
    w = np.arange(BAND_ROWS)[:, None]
    bands = []
    for ph in (0, 1):
        d = tok - CMP_STRIDE * (w - PAD_CMP - (Q_BLOCK // CMP_STRIDE) * ph) - (CMP_LEN - 1)
        vals = jnp.where(jnp.asarray(d >= 0), by_distance(d), 0.0)
        bands.append(jnp.transpose(vals, (0, 2, 1, 3)).reshape(N_KV, BAND_ROWS, ROWS))
    cb = jnp.stack(bands, axis=1)
    return nb.astype(F32), cb.astype(F32)


def _padded_cmp_rows(seq):
    return -(-(seq // CMP_STRIDE + 2 * PAD_CMP) // LANES) * LANES


def _overlap_t(seq):
    n_cmp = (seq - CMP_LEN) // CMP_STRIDE + 1
    n_blk = seq // SEL_LEN
    ncp = _padded_cmp_rows(seq)
    c_start = np.arange(n_cmp) * CMP_STRIDE
    c_end = c_start + CMP_LEN - 1
    blk = np.arange(n_blk)
    ov = ((c_start[:, None] < (blk[None, :] + 1) * SEL_LEN) & (c_end[:, None] >= blk[None, :] * SEL_LEN))
    out = np.zeros((2 * SUBLANES + n_blk, ncp), np.float32)
    out[0] = 1.0
    out[2 * SUBLANES:, PAD_CMP:PAD_CMP + n_cmp] = ov.T
    return jnp.asarray(out, BF16)


def _cmp_visibility(seq):
    n_cmp = (seq - CMP_LEN) // CMP_STRIDE + 1
    ncp = _padded_cmp_rows(seq)
    rho = np.arange(ncp)[:, None]
    col = np.arange(COLS)[None, :]
    thr = CMP_STRIDE * (rho - PAD_CMP) + (CMP_LEN - 1) - Q_BLOCK * (col // ROWS) - col % Q_BLOCK
    valid = (rho >= PAD_CMP) & (rho < PAD_CMP + n_cmp)
    return jnp.asarray(np.where(valid, thr, np.iinfo(np.int32).max), jnp.int32)


def _s5_params(lam_re, lam_im, log_step, b_re, b_im, c_re, c_im):
    lr, li = lam_re.astype(F32), lam_im.astype(F32)
    dt = jnp.exp(log_step.astype(F32))[:, None]
    mag = jnp.exp(lr * dt)
    ar, ai = mag * jnp.cos(li * dt), mag * jnp.sin(li * dt)
    den = lr * lr + li * li
    kr = ((ar - 1.0) * lr + ai * li) / den
    ki = (ai * lr - (ar - 1.0) * li) / den
    br, bi = b_re.astype(F32), b_im.astype(F32)
    bbr = kr[..., None] * br - ki[..., None] * bi
    bbi = kr[..., None] * bi + ki[..., None] * br
    n_g = lr.shape[0]
    eye = jnp.eye(n_g, dtype=F32)

    def drive(bb):
        return jnp.einsum('gpc,gh->gchp', bb, eye).reshape(n_g * SSM_GROUP, n_g * SSM_STATE)

    def readout(c):
        return jnp.einsum('gcp,gh->gphc', c, eye).reshape(n_g * SSM_STATE, n_g * SSM_GROUP)

    d_re, d_im = drive(bbr), drive(bbi)
    r_re, r_im = readout(c_re.astype(F32)), -readout(c_im.astype(F32))
    c_in, c_st = d_re.shape[0] // S5_DIAG_BLOCKS, d_re.shape[1] // S5_DIAG_BLOCKS
    blk = lambda m, k, rows, cols: m[k * rows:(k + 1) * rows, k * cols:(k + 1) * cols]
    wb = jnp.stack([jnp.concatenate([blk(d_re, k, c_in, c_st), blk(d_im, k, c_in, c_st)], axis=1)
                    for k in range(S5_DIAG_BLOCKS)]).astype(BF16)
    wc = jnp.stack([jnp.concatenate([blk(r_re, k, c_st, c_in), blk(r_im, k, c_st, c_in)], axis=0)
                    for k in range(S5_DIAG_BLOCKS)]).astype(BF16)
    a = jnp.concatenate([ar.reshape(1, -1), ai.reshape(1, -1)], axis=1)
    return wb, jnp.broadcast_to(a, (SUBLANES, a.shape[1])), wc


def _layer(x, mod, w_in, lam_re, lam_im, log_step, b_re, b_im, c_re, c_im, d_skip, glu_w, cmp_pos, cmp_w1,
           cmp_w2, rel_bias, w_out, ln1_g, ln1_b, router_w, router_bias, e_gate, e_up, e_down, sg, su, sd,
           ln2_g, ln2_b, alpha):
    bsz, seq, dm = x.shape
    n_tok = bsz * seq
    sh1, sc1, g1, sh2, sc2, g2 = [m[:, None, :] for m in jnp.split(mod, 6, axis=-1)]
    ssm_w = dm // 2
    attn_w = N_HEADS * HEAD_DIM
    kv_w = N_KV * HEAD_DIM
    n_gate = 3 * N_HEADS
    offs = np.cumsum([0, ssm_w, attn_w] + [kv_w] * 6 + [n_gate, 2 * dm])

    wu = w_in[:, offs[0]:offs[1]].astype(BF16)
    wq = w_in[:, offs[1]:offs[2]] * (HEAD_DIM ** -0.5 * LOG2E)
    wa = jnp.concatenate([wq, w_in[:, offs[2]:offs[4]]], axis=1).astype(BF16)
    wg = jnp.pad(w_in[:, offs[8]:offs[9]], ((0, 0), (0, LANES - n_gate))).astype(BF16)
    wm = w_in[:, offs[9]:offs[10]].astype(BF16)
    nq = seq // Q_BLOCK
    n_blk = seq // SEL_LEN
    nch = seq // CMP_STRIDE

    def head_padded(w, width):
        w = w.reshape(dm, N_KV, HEAD_DIM)
        return jnp.pad(w, ((0, 0), (0, 0), (0, width - HEAD_DIM))).reshape(dm, N_KV * width).astype(BF16)

    aug_w = 2 * HEAD_DIM + -(-n_blk // LANES) * LANES
    u2d, act, gates, mg, kaug, kw, vst, vwt = _input_projection(
        x, sh1, sc1, wu, wa, wg, wm, head_padded(w_in[:, offs[4]:offs[5]], aug_w),
        head_padded(w_in[:, offs[6]:offs[7]], 2 * HEAD_DIM),
        w_in[:, offs[5]:offs[6]].T.astype(BF16), w_in[:, offs[7]:offs[8]].T.astype(BF16))

    def piece(i):
        return act[:, :, attn_w + i * kv_w: attn_w + (i + 1) * kv_w]

    def compress(raw, pos, w1, w2):
        xc = raw.reshape(bsz, nch, CMP_STRIDE, N_KV, HEAD_DIM).transpose(0, 3, 1, 2, 4).reshape(bsz, N_KV, nch, CMP_STRIDE * HEAD_DIM)
        half = CMP_STRIDE * HEAD_DIM
        w1cat = jnp.concatenate([w1[:half], w1[half:]], axis=1).astype(BF16)
        posb = jnp.dot(pos.reshape(1, -1), w1, precision=lax.Precision.HIGHEST)
        return _compress(xc, w1cat, posb, w2.astype(BF16))

    kc = compress(piece(0), cmp_pos[0], cmp_w1[0], cmp_w2[0])
    vc = compress(piece(1), cmp_pos[1], cmp_w1[1], cmp_w2[1])
    pad = ((0, 0), (0, 0), (PAD_CMP, _padded_cmp_rows(seq) - nch - PAD_CMP), (0, 0))
    kcp = jnp.pad(kc, pad).astype(BF16)
    vcpt = jnp.swapaxes(jnp.pad(vc, pad), 2, 3).astype(BF16)
    ones_ovt = _overlap_t(seq)
    cmp_lhs = jnp.concatenate([vcpt, jnp.broadcast_to(ones_ovt, (bsz, N_KV) + ones_ovt.shape)], axis=2)

    nst = nq // NSA_TILES
    gt = gates[:, :, :n_gate].reshape(bsz, nst, NSA_TILES, Q_BLOCK, N_KV, GQA, 3)
    gt = gt.transpose(0, 4, 1, 6, 2, 5, 3).reshape(bsz, N_KV, nst, 3, COLS)
    gt = jnp.pad(gt, ((0, 0), (0, 0), (0, 0), (0, SUBLANES - 3), (0, 0)))
    nb, cb = _bias_tiles(rel_bias)
    y_nsa = _nsa(act, kaug, vst, kw, vwt, kcp, cmp_lhs, _cmp_visibility(seq), nb, cb, gt)

    wb, a, wc = _s5_params(lam_re, lam_im, log_step, b_re, b_im, c_re, c_im)
    z2 = _s5(u2d.reshape(seq * bsz, ssm_w), wb, a, wc, d_skip.reshape(1, ssm_w).astype(F32), bsz)
    z2d = z2.reshape(seq, bsz * ssm_w)

    base, h2 = _mix_out(z2d, y_nsa, mg, x, g1, sh2, sc2, g2, ln1_g.reshape(1, dm), ln1_b.reshape(1, dm),
                        glu_w.astype(BF16), w_out.astype(BF16), sg.astype(BF16), su.astype(BF16), sd.astype(BF16), alpha)
    h2 = h2.reshape(n_tok, dm // 2)

    n_exp = router_w.shape[1]
    rt = min(ROUTER_TILE, n_tok)
    tri = jnp.asarray(np.triu(np.ones((rt, rt), np.float32), 1), BF16)
    top_e, top_w, rank, counts = _router(h2, router_w.T.astype(BF16), router_bias.reshape(n_exp, 1).astype(F32), tri)
    counts = counts[:, 0].astype(jnp.int32)
    padded = (counts + EXPERT_TILE - 1) // EXPERT_TILE * EXPERT_TILE
    pad_end = jnp.cumsum(padded)
    pad_start = pad_end - padded
    dest = _dest_rows(pad_start.astype(jnp.int32), top_e, rank)
    n_rows = n_tok * TOP_K + n_exp * EXPERT_TILE
    n_tiles = n_rows // EXPERT_TILE
    tile_exp = jnp.minimum(jnp.searchsorted(pad_end, jnp.arange(n_tiles) * EXPERT_TILE, side='right'), n_exp - 1).astype(jnp.int32)
    n_used = (pad_end[-1] // EXPERT_TILE).astype(jnp.int32).reshape(1)
    mt = min(MOVE_TILE, seq)
    dest3 = dest.reshape(TOP_K, n_tok // mt, mt).transpose(1, 0, 2)
    xs = _dispatch(pad_end.astype(jnp.int32), padded.astype(jnp.int32), n_used, dest3, h2, n_rows)
    ys = _experts(tile_exp, n_used, xs, e_gate, e_up, e_down)
    return _final(dest3, top_w.T, base, g2, ln2_g.reshape(1, dm), ln2_b.reshape(1, dm), ys)


def kernel(x, c, ada_w, ada_b, w_in, ssm_lambda_re, ssm_lambda_im, ssm_log_step, ssm_b_re, ssm_b_im, ssm_c_re, ssm_c_im, ssm_d, ssm_glu_w, cmp_pos, cmp_w1, cmp_w2, rel_bias, w_out, ln1_g, ln1_b, router_w, router_bias, exp_w_gate, exp_w_up, exp_w_down, sh_w_gate, sh_w_up, sh_w_down, ln2_g, ln2_b):
    depth = ada_w.shape[0]
    alpha = (2 * depth) ** 0.25
    for l in range(depth):
        mod = _modulation(c, ada_w[l], ada_b[l])
        x = _layer(x, mod, w_in[l], ssm_lambda_re[l], ssm_lambda_im[l], ssm_log_step[l], ssm_b_re[l], ssm_b_im[l],
                   ssm_c_re[l], ssm_c_im[l], ssm_d[l], ssm_glu_w[l], cmp_pos[l], cmp_w1[l], cmp_w2[l], rel_bias,
                   w_out[l], ln1_g[l], ln1_b[l], router_w[l], router_bias[l], exp_w_gate[l], exp_w_up[l],
                   exp_w_down[l], sh_w_gate[l], sh_w_up[l], sh_w_down[l], ln2_g[l], ln2_b[l], alpha)
    return x
```

```python
import functools
import math

import numpy as np
import jax
import jax.numpy as jnp
from jax import lax
from jax.experimental import pallas as pl
from jax.experimental.pallas import tpu as pltpu

F32 = jnp.float32
BF16 = jnp.bfloat16

SSM_GROUP = 16
SSM_STATE = 64
N_HEADS = 16
HEAD_DIM = 64
N_KV = 4
GQA = N_HEADS // N_KV
CMP_LEN = 32
CMP_STRIDE = 16
SEL_LEN = 64
SEL_TOP = 16
WINDOW = 512
Q_BLOCK = 64
FORCE_BONUS = 1.0e4
N_FORCED = 3
assert GQA < FORCE_BONUS
REL_BUCKETS = 32
REL_MAX_DIST = 128
TOP_K = 8
ROUTED_SCALE = 2.5
LN_EPS = 1e-5

LANES = 128
SUBLANES = 8
VMEM_LIMIT_BYTES = 56 * 1024 * 1024
ROW_TILE = 512
ROUTER_TILE = 256
EXPERT_TILE = 512
MOVE_TILE = 256
DEST_TILE = 2048
SCAN_CHUNK = 128
SCAN_LANES = 512
S5_DIAG_BLOCKS = 2
MASK_BIG = 32768.0
NEG = -1.0e30
LOG2E = math.log2(math.e)
WIN_BLOCKS = WINDOW // SEL_LEN
ROWS = GQA * Q_BLOCK
NSA_TILES = 4
COLS = NSA_TILES * ROWS
CHUNK_BLOCKS = 8
CHUNK_KEYS = CHUNK_BLOCKS * SEL_LEN
FAR_BLOCKS = NSA_TILES
FAR_KEYS = FAR_BLOCKS * SEL_LEN
PAD_CMP = 8
BAND_ROWS = 24


def _cparams(sem):
    return pltpu.CompilerParams(dimension_semantics=sem, vmem_limit_bytes=VMEM_LIMIT_BYTES)


def _pack_rows(x):
    n = x.shape[1] // 2
    xb = x.astype(jnp.bfloat16).astype(F32)
    lo = lax.shift_right_logical(lax.bitcast_convert_type(xb[:, :n], jnp.uint32), jnp.uint32(16))
    return lax.bitcast_convert_type(xb[:, n:], jnp.uint32) | lo


def _unpack_rows(w):
    lo = lax.bitcast_convert_type(lax.shift_left(w, jnp.uint32(16)), F32)
    hi = lax.bitcast_convert_type(w & jnp.uint32(0xFFFF0000), F32)
    return lo, hi


def _norm_rows(x):
    mu = jnp.mean(x, axis=-1, keepdims=True)
    xc = x - mu
    var = jnp.mean(xc * xc, axis=-1, keepdims=True)
    return xc * lax.rsqrt(var + LN_EPS)


def _mod_kernel(c_ref, w_ref, b_ref, o_ref):
    cond = jax.nn.silu(c_ref[...])
    o_ref[...] = jnp.dot(cond.astype(BF16), w_ref[...].astype(BF16), preferred_element_type=F32) + b_ref[...]


def _modulation(c, ada_w, ada_b):
    bsz, dm = c.shape
    n = ada_w.shape[1]
    tn = dm
    return pl.pallas_call(
        _mod_kernel,
        grid=(n // tn,),
        in_specs=[pl.BlockSpec((bsz, dm), lambda j: (0, 0)),
                  pl.BlockSpec((dm, tn), lambda j: (0, j)),
                  pl.BlockSpec((1, tn), lambda j: (0, j))],
        out_specs=pl.BlockSpec((bsz, tn), lambda j: (0, j)),
        out_shape=jax.ShapeDtypeStruct((bsz, n), F32),
        compiler_params=_cparams(("arbitrary",)),
        name="ada_modulation",
    )(c, ada_w, ada_b.reshape(1, n))


def _inproj_kernel(x_ref, sh_ref, sc_ref, wu_ref, wa_ref, wg_ref, wm_ref, wks_ref, wkw_ref, wvs_ref, wvw_ref,
                   u_ref, a_ref, g_ref, m_ref, ks_ref, kw_ref, vs_ref, vw_ref, *, aug_w):
    h = _norm_rows(x_ref[0]) * (1.0 + sc_ref[0]) + sh_ref[0]
    hb = h.astype(BF16)
    tm = hb.shape[0]
    u_ref[...] = jnp.dot(hb, wu_ref[...], preferred_element_type=F32)
    a_ref[0] = jnp.dot(hb, wa_ref[...], preferred_element_type=F32).astype(BF16)
    g_ref[0] = jax.nn.sigmoid(jnp.dot(hb, wg_ref[...], preferred_element_type=F32))
    m_ref[0] = jax.nn.sigmoid(jnp.dot(hb, wm_ref[...], preferred_element_type=F32)).astype(BF16)
    ks = jnp.dot(hb, wks_ref[...], preferred_element_type=F32)
    assert aug_w & (aug_w - 1) == 0
    col = (lax.broadcasted_iota(jnp.int32, ks.shape, 1) & (aug_w - 1)) - 2 * HEAD_DIM
    blk = jnp.right_shift(pl.program_id(1) * tm + lax.broadcasted_iota(jnp.int32, ks.shape, 0), SEL_LEN.bit_length() - 1)
    ks_ref[0] = jnp.where(col == blk, -MASK_BIG, ks).astype(BF16)
    kw_ref[0] = jnp.dot(hb, wkw_ref[...], preferred_element_type=F32).astype(BF16)
    rows = lax.broadcasted_iota(jnp.int32, (2 * SUBLANES, tm), 0)
    ones_rows = jnp.where(rows == 0, 1.0, 0.0).astype(BF16)
    for w_ref, v_ref in ((wvs_ref, vs_ref), (wvw_ref, vw_ref)):
        vt = lax.dot_general(w_ref[...], hb, (((1,), (1,)), ((), ())), preferred_element_type=F32).astype(BF16)
        for g in range(N_KV):
            v_ref[0, g] = jnp.concatenate([vt[g * HEAD_DIM:(g + 1) * HEAD_DIM], ones_rows], axis=0)


def _input_projection(x, sh1, sc1, wu, wa, wg, wm, wks, wkw, wvs_t, wvw_t):
    bsz, seq, dm = x.shape
    tm = min(ROW_TILE, seq)
    nu, na, ng, nm = wu.shape[1], wa.shape[1], wg.shape[1], wm.shape[1]
    full = lambda w: pl.BlockSpec(w.shape, lambda b, i: (0, 0))
    vec = pl.BlockSpec((1, 1, dm), lambda b, i: (b, 0, 0))
    rows_out = lambda n: pl.BlockSpec((1, tm, n), lambda b, i: (b, i, 0))
    vt_rows = HEAD_DIM + 2 * SUBLANES
    vt_out = pl.BlockSpec((1, N_KV, vt_rows, tm), lambda b, i: (b, 0, 0, i))
    return pl.pallas_call(
        functools.partial(_inproj_kernel, aug_w=wks.shape[1] // N_KV),
        grid=(bsz, seq // tm),
        in_specs=[pl.BlockSpec((1, tm, dm), lambda b, i: (b, i, 0)), vec, vec,
                  full(wu), full(wa), full(wg), full(wm), full(wks), full(wkw), full(wvs_t), full(wvw_t)],
        out_specs=[pl.BlockSpec((tm, nu), lambda b, i: (i, b)), rows_out(na), rows_out(ng), rows_out(nm),
                   rows_out(wks.shape[1]), rows_out(wkw.shape[1]), vt_out, vt_out],
        out_shape=[jax.ShapeDtypeStruct((seq, bsz * nu), F32),
                   jax.ShapeDtypeStruct((bsz, seq, na), BF16),
                   jax.ShapeDtypeStruct((bsz, seq, ng), F32),
                   jax.ShapeDtypeStruct((bsz, seq, nm), BF16),
                   jax.ShapeDtypeStruct((bsz, seq, wks.shape[1]), BF16),
                   jax.ShapeDtypeStruct((bsz, seq, wkw.shape[1]), BF16),
                   jax.ShapeDtypeStruct((bsz, N_KV, vt_rows, seq), BF16),
                   jax.ShapeDtypeStruct((bsz, N_KV, vt_rows, seq), BF16)],
        compiler_params=_cparams(("arbitrary", "arbitrary")),
        name="adaln_input_projection",
    )(x, sh1, sc1, wu, wa, wg, wm, wks, wkw, wvs_t, wvw_t)


def _compress_kernel(x_ref, w1_ref, pb_ref, w2_ref, o_ref, *, n_cmp):
    hid = w2_ref.shape[0]
    p = jnp.dot(x_ref[0, 0], w1_ref[...], preferred_element_type=F32)
    nrow = p.shape[0]
    nxt = pltpu.roll(p[:, hid:], nrow - 1, 0)
    hidv = jax.nn.gelu(p[:, :hid] + nxt + pb_ref[...])
    out = jnp.dot(hidv.astype(BF16), w2_ref[...], preferred_element_type=F32)
    rows = lax.broadcasted_iota(jnp.int32, out.shape, 0)
    o_ref[0, 0] = jnp.where(rows < n_cmp, out, 0.0)


def _compress(xc, w1cat, posb, w2):
    bsz, nkv, nch, kdim = xc.shape
    hid2 = w1cat.shape[1]
    return pl.pallas_call(
        functools.partial(_compress_kernel, n_cmp=nch - 1),
        grid=(bsz, nkv),
        in_specs=[pl.BlockSpec((1, 1, nch, kdim), lambda b, g: (b, g, 0, 0)),
                  pl.BlockSpec((kdim, hid2), lambda b, g: (0, 0)),
                  pl.BlockSpec((1, hid2 // 2), lambda b, g: (0, 0)),
                  pl.BlockSpec((hid2 // 2, HEAD_DIM), lambda b, g: (0, 0))],
        out_specs=pl.BlockSpec((1, 1, nch, HEAD_DIM), lambda b, g: (b, g, 0, 0)),
        out_shape=jax.ShapeDtypeStruct((bsz, nkv, nch, HEAD_DIM), F32),
        compiler_params=_cparams(("arbitrary", "arbitrary")),
        name="kv_compress",
    )(xc, w1cat, posb, w2)


def _swap_heads_tokens(a):
    t = jnp.concatenate([a, jnp.zeros_like(a)], axis=0).T
    p = [t[r * HEAD_DIM:(r + 1) * HEAD_DIM] for r in range(GQA)]
    return jnp.concatenate([p[r] + pltpu.roll(p[r + 1], Q_BLOCK, 1) for r in range(0, GQA, 2)], axis=1)


def _nsa_kernel(q_ref, kaug_ref, vst_ref, kw_ref, vwt_ref, kcp_ref, cl_ref, vis_ref, nb_ref, cb_ref,
                g_ref, o_ref, s_ref, c_ref, w_ref, fa_ref, fb_ref, *, n_blk, n_sel):
    q0 = pl.program_id(2) * NSA_TILES
    qf = q_ref[0].astype(F32)
    qt = jnp.concatenate([_swap_heads_tokens(qf[n * Q_BLOCK:(n + 1) * Q_BLOCK]) for n in range(NSA_TILES)],
                         axis=1).astype(BF16)
    col_tile = lambda n: slice(n * ROWS, (n + 1) * ROWS)

    s_ref[...] = jnp.dot(kcp_ref[0, 0], qt, preferred_element_type=F32)
    for n in range(NSA_TILES):
        band = pl.multiple_of(SUBLANES * ((q0 + n) // 2), SUBLANES)
        s_ref[pl.ds(band, BAND_ROWS), col_tile(n)] += cb_ref[0, n % 2]
    s = jnp.where(vis_ref[...] <= Q_BLOCK * q0, s_ref[...], -jnp.inf)
    mx = jnp.max(s, axis=0, keepdims=True)
    mx = jnp.where(mx == -jnp.inf, 0.0, mx)
    e = jnp.exp2((s - mx).astype(BF16))
    both = jnp.dot(cl_ref[0, 0], e, preferred_element_type=F32)
    inv = 1.0 / jnp.maximum(both[HEAD_DIM:HEAD_DIM + 1], 1e-30)
    o_cmp = both[:HEAD_DIM] * inv

    imp4 = both[HEAD_DIM + 2 * SUBLANES:] * inv
    sums = []
    for n in range(NSA_TILES):
        two = imp4[:, n * ROWS:n * ROWS + LANES] + imp4[:, n * ROWS + LANES:(n + 1) * ROWS]
        sums.append(two + pltpu.roll(two, Q_BLOCK, 1))
    low = lax.broadcasted_iota(jnp.int32, sums[0].shape, 1) < Q_BLOCK
    imp = jnp.concatenate([jnp.where(low, sums[n], sums[n + 1]) for n in range(0, NSA_TILES, 2)], axis=1)
    blk = lax.broadcasted_iota(jnp.int32, imp.shape, 0)
    cur = q0 + jnp.right_shift(lax.broadcasted_iota(jnp.int32, imp.shape, 1), Q_BLOCK.bit_length() - 1)
    blkf = blk.astype(F32)
    forced = (blk == 0) | (blk == cur) | (blk == cur - 1)
    score = jnp.where((blk <= cur) & ~forced, imp, -jnp.inf)
    notsel = jnp.where(forced, 0.0, 1.0)
    for _ in range(n_sel - N_FORCED):
        best = jnp.max(score, axis=0, keepdims=True)
        first = jnp.min(jnp.where(score == best, blkf, float(n_blk)), axis=0, keepdims=True)
        pick = blkf == first
        notsel = jnp.where(pick, 0.0, notsel)
        score = jnp.where(pick, -jnp.inf, score)
    halves = []
    for n in range(0, NSA_TILES, 2):
        pair = notsel[:, (n // 2) * LANES:(n // 2 + 1) * LANES]
        swapped = pltpu.roll(pair, Q_BLOCK, 1)
        halves += [jnp.where(low, pair, swapped), jnp.where(low, swapped, pair)]
    notsel = jnp.concatenate([h for h in halves for _ in (0, 1)], axis=1).astype(BF16)
    qwin = jnp.concatenate([qt, jnp.zeros_like(qt)], axis=0)
    tail = kaug_ref.shape[2] - 2 * HEAD_DIM - n_blk
    qaug = jnp.concatenate([qwin, notsel] + ([jnp.zeros((tail, COLS), notsel.dtype)] if tail else []), axis=0)

    def update(carry, sc, vt, top=None):
        m, acc = carry
        m_new = jnp.maximum(m, jnp.max(sc, axis=0, keepdims=True) if top is None else top)
        p = jnp.exp2((sc - m_new).astype(BF16))
        return m_new, jnp.exp2(m - m_new) * acc + jnp.dot(vt, p, preferred_element_type=F32)

    def chunk_keys(c):
        return pl.ds(pl.multiple_of(c * FAR_KEYS, FAR_KEYS), FAR_KEYS)

    def far_scores(c, buf_ref):
        sc = jnp.dot(kaug_ref[0, chunk_keys(c), :], qaug, preferred_element_type=F32)
        buf_ref[...] = sc
        return jnp.max(sc, axis=0, keepdims=True)

    def far_pair(i, carry):
        m, acc, top_a = carry
        top_b = far_scores(2 * i + 1, fb_ref)
        m, acc = update((m, acc), fa_ref[...], vst_ref[0, 0, :, chunk_keys(2 * i)], top_a)
        top_a = far_scores(2 * i + 2, fa_ref)
        m, acc = update((m, acc), fb_ref[...], vst_ref[0, 0, :, chunk_keys(2 * i + 1)], top_b)
        return m, acc, top_a

    def far_last(_, carry, n_far):
        m, acc, top_a = carry
        return update((m, acc), fa_ref[...], vst_ref[0, 0, :, chunk_keys(n_far - 1)], top_a) + (top_a,)

    def near_chunk(k_ref, vt_ref, qmat, blk0, tile_of, carry, buf_ref):
        n_keys = buf_ref.shape[0]
        keys = pl.ds(pl.multiple_of(blk0 * SEL_LEN, NSA_TILES * SEL_LEN), n_keys)
        buf_ref[...] = jnp.dot(k_ref[0, keys, :], qmat, preferred_element_type=F32)
        for o in range(n_keys // SEL_LEN):
            for n in range(NSA_TILES):
                buf_ref[o * SEL_LEN:(o + 1) * SEL_LEN, col_tile(n)] += nb_ref[0, tile_of(blk0 + o, q0 + n)]
        return update(carry, buf_ref[...], vt_ref[0, 0, :, keys])

    def sel_tile(j, qi):
        d = qi - j
        return jnp.where(d < 0, 4, jnp.minimum(d, 3))

    def win_tile(j, qi):
        d = qi - j
        return jnp.where((d < 0) | (d > WIN_BLOCKS), 4, jnp.where(d == WIN_BLOCKS, 5, jnp.minimum(d, 3)))

    init = (jnp.full((1, COLS), NEG, F32), jnp.zeros((vst_ref.shape[2], COLS), F32))
    _, acc = near_chunk(kw_ref, vwt_ref, qwin, jnp.maximum(q0 - WIN_BLOCKS, 0), win_tile, init, w_ref)
    o_win = acc[:HEAD_DIM] / acc[HEAD_DIM:HEAD_DIM + 1]
    n_far = jnp.maximum(q0 - 2, 0) // FAR_BLOCKS
    carry = lax.fori_loop(0, n_far // 2, far_pair, init + (far_scores(0, fa_ref),))
    carry = lax.fori_loop(0, n_far % 2, functools.partial(far_last, n_far=n_far), carry)
    _, acc = near_chunk(kaug_ref, vst_ref, qaug, n_far * FAR_BLOCKS, sel_tile, carry[:2], c_ref)
    o_slc = acc[:HEAD_DIM] / acc[HEAD_DIM:HEAD_DIM + 1]
    g = g_ref[0, 0, 0]
    out = g[0:1] * o_cmp + g[1:2] * o_slc + g[2:3] * o_win
    o_ref[0] = jnp.concatenate([_swap_heads_tokens(out[:, col_tile(n)]) for n in range(NSA_TILES)],
                               axis=0).astype(BF16)


def _nsa(act, kaug, vst, kw, vwt, kcp, cmp_lhs, vis, nb, cb, gt):
    bsz, nkv, nsteps, _, cols = gt.shape
    dh = HEAD_DIM
    seq = kw.shape[1]
    step_tokens = NSA_TILES * Q_BLOCK
    q_blk = pl.BlockSpec((1, step_tokens, GQA * dh), lambda b, g, i: (b, i, g))
    n_blk = seq // SEL_LEN
    ncp = kcp.shape[2]
    assert n_blk % CHUNK_BLOCKS == 0 and n_blk >= WIN_BLOCKS + NSA_TILES and CHUNK_BLOCKS == FAR_BLOCKS + NSA_TILES
    per_bg = lambda shape: pl.BlockSpec((1, 1) + shape, lambda b, g, i: (b, g, 0, 0))
    per_step = lambda shape: pl.BlockSpec((1, 1, 1) + shape, lambda b, g, i: (b, g, i, 0, 0))
    head_cols = lambda a: pl.BlockSpec((1, seq, a.shape[2] // nkv), lambda b, g, i: (b, 0, g))
    kern = functools.partial(_nsa_kernel, n_blk=n_blk, n_sel=min(SEL_TOP, n_blk))
    return pl.pallas_call(
        kern,
        grid=(bsz, nkv, nsteps),
        in_specs=[q_blk,
                  head_cols(kaug), per_bg((vst.shape[2], seq)), head_cols(kw), per_bg((vwt.shape[2], seq)),
                  per_bg((ncp, dh)), per_bg((cmp_lhs.shape[2], ncp)),
                  pl.BlockSpec(vis.shape, lambda b, g, i: (0, 0)),
                  pl.BlockSpec((1,) + nb.shape[1:], lambda b, g, i: (g, 0, 0, 0)),
                  pl.BlockSpec((1,) + cb.shape[1:], lambda b, g, i: (g, 0, 0, 0)),
                  per_step((SUBLANES, cols))],
        out_specs=q_blk,
        out_shape=jax.ShapeDtypeStruct((bsz, seq, nkv * GQA * dh), BF16),
        scratch_shapes=[pltpu.VMEM((ncp, cols), F32), pltpu.VMEM((CHUNK_KEYS, cols), F32),
                        pltpu.VMEM(((WIN_BLOCKS + NSA_TILES) * SEL_LEN, cols), F32),
                        pltpu.VMEM((FAR_KEYS, cols), F32), pltpu.VMEM((FAR_KEYS, cols), F32)],
        compiler_params=_cparams(("arbitrary", "arbitrary", "arbitrary")),
        name="nsa_attention",
    )(act, kaug, vst, kw, vwt, kcp, cmp_lhs, vis, nb, cb, gt)


def _s5_kernel(u_ref, wb_ref, a_ref, wc_ref, d_ref, z_ref, xs_ref, st_ref, *, bsz, n_state):
    @pl.when(pl.program_id(0) == 0)
    def _():
        st_ref[...] = jnp.zeros_like(st_ref)

    u = u_ref[...]
    ub = u.astype(BF16)
    n_diag = wb_ref.shape[0]
    c_in, c_st = u.shape[1] // n_diag, n_state // n_diag
    for k in range(n_diag):
        drv = jnp.dot(ub[:, k * c_in:(k + 1) * c_in], wb_ref[k], preferred_element_type=F32)
        xs_ref[:, k * c_st:(k + 1) * c_st] = drv[:, :c_st]
        xs_ref[:, n_state + k * c_st:n_state + (k + 1) * c_st] = drv[:, c_st:]
    steps = u.shape[0] // bsz
    for c0 in range(0, n_state, SCAN_LANES):
        re = pl.ds(c0, SCAN_LANES)
        im = pl.ds(n_state + c0, SCAN_LANES)
        ar = jnp.broadcast_to(a_ref[0:1, re], (bsz, SCAN_LANES))
        ai = jnp.broadcast_to(a_ref[0:1, im], (bsz, SCAN_LANES))

        def step(t, carry):
            xr, xi = carry
            rows = pl.ds(pl.multiple_of(t * bsz, bsz), bsz)
            nr = ar * xr - ai * xi + xs_ref[rows, re]
            ni = ar * xi + ai * xr + xs_ref[rows, im]
            xs_ref[rows, re] = nr
            xs_ref[rows, im] = ni
            return nr, ni

        xr, xi = lax.fori_loop(0, steps, step, (st_ref[:, re], st_ref[:, im]), unroll=8)
        st_ref[:, re] = xr
        st_ref[:, im] = xi
    ys = []
    for k in range(n_diag):
        st = jnp.concatenate([xs_ref[:, k * c_st:(k + 1) * c_st],
                              xs_ref[:, n_state + k * c_st:n_state + (k + 1) * c_st]], axis=1)
        ys.append(jnp.dot(st.astype(BF16), wc_ref[k], preferred_element_type=F32))
    y = jnp.concatenate(ys, axis=1) + d_ref[...] * u
    z_ref[...] = jax.nn.gelu(y).astype(BF16)


def _s5(u2, wb, a, wc, dsk, bsz):
    rows, width = u2.shape
    seq = rows // bsz
    chunk = min(SCAN_CHUNK, seq)
    n_state2 = a.shape[1]
    return pl.pallas_call(
        functools.partial(_s5_kernel, bsz=bsz, n_state=n_state2 // 2),
        grid=(seq // chunk,),
        in_specs=[pl.BlockSpec((chunk * bsz, width), lambda i: (i, 0)),
                  pl.BlockSpec(wb.shape, lambda i: (0, 0, 0)),
                  pl.BlockSpec(a.shape, lambda i: (0, 0)),
                  pl.BlockSpec(wc.shape, lambda i: (0, 0, 0)),
                  pl.BlockSpec(dsk.shape, lambda i: (0, 0))],
        out_specs=pl.BlockSpec((chunk * bsz, width), lambda i: (i, 0)),
        out_shape=jax.ShapeDtypeStruct((rows, width), BF16),
        scratch_shapes=[pltpu.VMEM((chunk * bsz, n_state2), F32), pltpu.VMEM((bsz, n_state2), F32)],
        compiler_params=_cparams(("arbitrary",)),
        name="s5_scan",
    )(u2, wb, a, wc, dsk)


def _mixout_kernel(z_ref, yn_ref, mg_ref, x_ref, g1_ref, sh2_ref, sc2_ref, g2_ref, lg_ref, lb_ref,
                   glu_ref, wo_ref, sg_ref, su_ref, sd_ref, base_ref, h_ref, *, alpha):
    dm = x_ref.shape[2]
    glu = jnp.dot(z_ref[...], glu_ref[...], preferred_element_type=F32)
    y_ssm = glu[:, :dm] * jax.nn.sigmoid(glu[:, dm:])
    mg = mg_ref[0].astype(F32)
    merged = mg[:, :dm] * y_ssm + mg[:, dm:] * yn_ref[0].astype(F32)
    y = jnp.dot(merged.astype(BF16), wo_ref[...], preferred_element_type=F32)
    x1 = _norm_rows(alpha * x_ref[0] + g1_ref[0] * y) * lg_ref[...] + lb_ref[...]
    hf = _norm_rows(x1) * (1.0 + sc2_ref[0]) + sh2_ref[0]
    h_ref[0] = _pack_rows(hf)
    h = hf.astype(BF16)
    hs =jax.nn.silu(jnp.dot(h, sg_ref[...], preferred_element_type=F32)) * jnp.dot(h, su_ref[...], preferred_element_type=F32)
    shared = jnp.dot(hs.astype(BF16), sd_ref[...], preferred_element_type=F32)
    base_ref[0] = alpha * x1 + g2_ref[0] * shared


def _mix_out(z2d, y_nsa, mg, x, g1, sh2, sc2, g2, ln_g, ln_b, glu_w, w_out, sg, su, sd, alpha):
    bsz, seq, dm = x.shape
    tm = min(ROW_TILE, seq)
    width = z2d.shape[1] // bsz
    vec = pl.BlockSpec((1, 1, dm), lambda b, i: (b, 0, 0))
    row = pl.BlockSpec((1, dm), lambda b, i: (0, 0))
    full = lambda w: pl.BlockSpec(w.shape, lambda b, i: (0, 0))
    tile = lambda n: pl.BlockSpec((1, tm, n), lambda b, i: (b, i, 0))
    return pl.pallas_call(
        functools.partial(_mixout_kernel, alpha=alpha),
        grid=(bsz, seq // tm),
        in_specs=[pl.BlockSpec((tm, width), lambda b, i: (i, b)), tile(dm), tile(2 * dm), tile(dm),
                  vec, vec, vec, vec, row, row, full(glu_w), full(w_out), full(sg), full(su), full(sd)],
        out_specs=[tile(dm), tile(dm // 2)],
        out_shape=[jax.ShapeDtypeStruct((bsz, seq, dm), F32), jax.ShapeDtypeStruct((bsz, seq, dm // 2), jnp.uint32)],
        compiler_params=_cparams(("arbitrary", "arbitrary")),
        name="merge_outproj_ln_shared",
    )(z2d, y_nsa, mg, x, g1, sh2, sc2, g2, ln_g, ln_b, glu_w, w_out, sg, su, sd)


def _router_kernel(h_ref, rwt_ref, rb_ref, tri_ref, e_ref, w_ref, p_ref, cnt_ref):
    @pl.when(pl.program_id(0) == 0)
    def _():
        cnt_ref[...] = jnp.zeros_like(cnt_ref)

    h = jnp.concatenate(_unpack_rows(h_ref[...]), axis=1).astype(BF16)
    logits = lax.dot_general(rwt_ref[...], h, (((1,), (1,)), ((), ())), preferred_element_type=F32)
    scores = jax.nn.sigmoid(logits)
    cur = scores + rb_ref[...]
    n_exp = scores.shape[0]
    eid = lax.broadcasted_iota(jnp.int32, scores.shape, 0).astype(F32)
    chosen = jnp.zeros(scores.shape, F32)
    ids, vals = [], []
    for _ in range(TOP_K):
        best = jnp.max(cur, axis=0, keepdims=True)
        first = jnp.min(jnp.where(cur == best, eid, float(n_exp)), axis=0, keepdims=True)
        pick = eid == first
        ids.append(first)
        vals.append(jnp.sum(jnp.where(pick, scores, 0.0), axis=0, keepdims=True))
        chosen = jnp.where(pick, 1.0, chosen)
        cur = jnp.where(pick, -jnp.inf, cur)
    top_s = jnp.concatenate(vals, axis=0)
    w_ref[...] = top_s / jnp.sum(top_s, axis=0, keepdims=True) * ROUTED_SCALE
    top_e = jnp.concatenate(ids, axis=0)
    e_ref[...] = top_e.astype(jnp.int32)
    before = jnp.dot(chosen.astype(BF16), tri_ref[...], preferred_element_type=F32) + cnt_ref[...]
    ranks = [jnp.sum(jnp.where(eid == ids[k], before, 0.0), axis=0, keepdims=True) for k in range(TOP_K)]
    p_ref[...] = jnp.concatenate(ranks, axis=0).astype(jnp.int32)
    cnt_ref[...] += jnp.sum(chosen, axis=1, keepdims=True)


def _router(h2, rwt, rb, tri):
    n_tok, words = h2.shape
    n_exp, dm = rwt.shape
    tm = tri.shape[0]
    kt = pl.BlockSpec((TOP_K, tm), lambda i: (0, i))
    return pl.pallas_call(
        _router_kernel,
        grid=(n_tok // tm,),
        in_specs=[pl.BlockSpec((tm, words), lambda i: (i, 0)),
                  pl.BlockSpec((n_exp, dm), lambda i: (0, 0)),
                  pl.BlockSpec((n_exp, 1), lambda i: (0, 0)),
                  pl.BlockSpec((tm, tm), lambda i: (0, 0))],
        out_specs=[kt, kt, kt, pl.BlockSpec((n_exp, 1), lambda i: (0, 0))],
        out_shape=[jax.ShapeDtypeStruct((TOP_K, n_tok), jnp.int32),
                   jax.ShapeDtypeStruct((TOP_K, n_tok), F32),
                   jax.ShapeDtypeStruct((TOP_K, n_tok), jnp.int32),
                   jax.ShapeDtypeStruct((n_exp, 1), F32)],
        compiler_params=_cparams(("arbitrary",)),
        name="router_topk_rank",
    )(h2, rwt, rb, tri)


def _dest_kernel(start_ref, e_ref, r_ref, o_ref):
    e = e_ref[...]
    start = lax.fori_loop(0, start_ref.shape[0], lambda j, acc: jnp.where(e == j, start_ref[j], acc),
                          jnp.zeros(e.shape, jnp.int32))
    o_ref[...] = start + r_ref[...]


def _dest_rows(pad_start, top_e, rank):
    n_tok = top_e.shape[1]
    tm = min(DEST_TILE, n_tok)
    blk = pl.BlockSpec((TOP_K, tm), lambda i, ps: (0, i))
    return pl.pallas_call(
        _dest_kernel,
        grid_spec=pltpu.PrefetchScalarGridSpec(num_scalar_prefetch=1, grid=(n_tok // tm,), in_specs=[blk, blk], out_specs=blk),
        out_shape=jax.ShapeDtypeStruct(top_e.shape, jnp.int32),
        compiler_params=_cparams(("arbitrary",)),
        name="moe_dest_rows",
    )(pad_start, top_e, rank)


def _row_copy(src_ref, src_row, dst_ref, dst_row, sem):
    return pltpu.make_async_copy(src_ref.at[pl.ds(src_row, 1), :], dst_ref.at[pl.ds(dst_row, 1), :], sem)


def _dispatch_kernel(pend_ref, padded_ref, nused_ref, dest_ref, h_ref, xs_ref, zero_ref, sem, zsem):
    tm = h_ref.shape[0]

    @pl.when(pl.program_id(0) == 0)
    def _():
        zero_ref[...] = jnp.zeros_like(zero_ref)
        n_tiles = xs_ref.shape[0] // EXPERT_TILE

        def zero_tile(row):
            return pltpu.make_async_copy(zero_ref, xs_ref.at[pl.ds(pl.multiple_of(row, EXPERT_TILE), EXPERT_TILE), :], zsem)

        def per_expert(act):
            def body(e, carry):
                @pl.when(padded_ref[e] > 0)
                def _():
                    act(zero_tile(pend_ref[e] - EXPERT_TILE))
                return carry
            lax.fori_loop(0, pend_ref.shape[0], body, 0)

        def per_unused(act):
            def body(i, carry):
                act(zero_tile(i * EXPERT_TILE))
                return carry
            lax.fori_loop(nused_ref[0], n_tiles, body, 0)

        for loop in (per_expert, per_unused):
            loop(lambda copy: copy.start())
        for loop in (per_expert, per_unused):
            loop(lambda copy: copy.wait())

    def issue(t8, carry):
        base = pl.multiple_of(t8 * SUBLANES, SUBLANES)
        for s in range(SUBLANES):
            for k in range(TOP_K):
                _row_copy(h_ref, base + s, xs_ref, dest_ref[0, k, base + s], sem).start(priority=k % 2)
        return carry

    lax.fori_loop(0, tm // SUBLANES, issue, 0)
    pltpu.make_async_copy(xs_ref.at[pl.ds(0, TOP_K * tm), :], xs_ref.at[pl.ds(0, TOP_K * tm), :], sem).wait()


def _dispatch(pad_end, padded, n_used, dest3, h2, n_rows):
    n_tok, words = h2.shape
    tm = dest3.shape[2]
    grid_spec = pltpu.PrefetchScalarGridSpec(
        num_scalar_prefetch=3,
        grid=(n_tok // tm,),
        in_specs=[pl.BlockSpec((1, TOP_K, tm), lambda i, *_: (i, 0, 0), memory_space=pltpu.SMEM),
                  pl.BlockSpec((tm, words), lambda i, *_: (i, 0))],
        out_specs=pl.BlockSpec(memory_space=pl.ANY),
        scratch_shapes=[pltpu.VMEM((EXPERT_TILE, words), jnp.uint32), pltpu.SemaphoreType.DMA(()),
                        pltpu.SemaphoreType.DMA(())],
    )
    return pl.pallas_call(
        _dispatch_kernel,
        grid_spec=grid_spec,
        out_shape=jax.ShapeDtypeStruct((n_rows, words), jnp.uint32),
        compiler_params=_cparams(("arbitrary",)),
        name="moe_dispatch",
    )(pad_end, padded, n_used, dest3, h2)


def _expert_kernel(te_ref, nu_ref, x_ref, wg_ref, wu_ref, wd_ref, y_ref):
    i = pl.program_id(0)

    @pl.when(i < nu_ref[0])
    def _():
        lo, hi = _unpack_rows(x_ref[...])
        lo, hi = lo.astype(BF16), hi.astype(BF16)
        half = lo.shape[1]

        def proj(w_ref):
            return (jnp.dot(lo, w_ref[0, :half, :].astype(BF16), preferred_element_type=F32)
                    + jnp.dot(hi, w_ref[0, half:, :].astype(BF16), preferred_element_type=F32))

        hmid = jax.nn.silu(proj(wg_ref)) * proj(wu_ref)
        y_ref[...] = _pack_rows(jnp.dot(hmid.astype(BF16), wd_ref[0].astype(BF16), preferred_element_type=F32))

    @pl.when(i >= nu_ref[0])
    def _():
        y_ref[...] = jnp.zeros_like(y_ref)


def _experts(tile_exp, n_used, xs, wg, wu, wd):
    n_rows, words = xs.shape
    dm, de = wg.shape[1], wg.shape[2]
    grid_spec = pltpu.PrefetchScalarGridSpec(
        num_scalar_prefetch=2,
        grid=(n_rows // EXPERT_TILE,),
        in_specs=[pl.BlockSpec((EXPERT_TILE, words), lambda i, te, nu: (i, 0)),
                  pl.BlockSpec((1, dm, de), lambda i, te, nu: (te[i], 0, 0)),
                  pl.BlockSpec((1, dm, de), lambda i, te, nu: (te[i], 0, 0)),
                  pl.BlockSpec((1, de, dm), lambda i, te, nu: (te[i], 0, 0))],
        out_specs=pl.BlockSpec((EXPERT_TILE, words), lambda i, te, nu: (i, 0)),
    )
    return pl.pallas_call(
        _expert_kernel,
        grid_spec=grid_spec,
        out_shape=jax.ShapeDtypeStruct((n_rows, words), jnp.uint32),
        compiler_params=_cparams(("arbitrary",)),
        name="expert_mlp",
    )(tile_exp, n_used, xs, wg, wu, wd)


def _final_kernel(dest_ref, next_ref, w_ref, base_ref, g2_ref, lg_ref, lb_ref, ys_ref, o_ref, buf_ref, sem):
    i = pl.program_id(0)
    tm = base_ref.shape[1]
    slot = i % 2

    def gather(rows_ref, to):
        def issue(t8, carry):
            base = pl.multiple_of(t8 * SUBLANES, SUBLANES)
            for s in range(SUBLANES):
                for k in range(TOP_K):
                    _row_copy(ys_ref, rows_ref[0, k, base + s], buf_ref.at[to, k], base + s,
                              sem.at[to]).start(priority=k % 2)
            return carry

        lax.fori_loop(0, tm // SUBLANES, issue, 0)

    @pl.when(i == 0)
    def _():
        gather(dest_ref, slot)

    @pl.when(i + 1 < pl.num_programs(0))
    def _():
        gather(next_ref, 1 - slot)

    pltpu.make_async_copy(buf_ref.at[slot], buf_ref.at[slot], sem.at[slot]).wait()
    w = w_ref[...]
    lo_sum, hi_sum = jnp.zeros((tm, buf_ref.shape[3]), F32), jnp.zeros((tm, buf_ref.shape[3]), F32)
    for k in range(TOP_K):
        lo, hi = _unpack_rows(buf_ref[slot, k])
        lo_sum += w[:, k:k + 1] * lo
        hi_sum += w[:, k:k + 1] * hi
    routed = jnp.concatenate([lo_sum, hi_sum], axis=1)
    o_ref[0] = _norm_rows(base_ref[0] + g2_ref[0] * routed) * lg_ref[...] + lb_ref[...]


def _final(dest3, w_tok, base, g2, ln_g, ln_b, ys):
    bsz, seq, dm = base.shape
    tm = dest3.shape[2]
    nt = seq // tm
    n_tiles = bsz * nt
    tile = pl.BlockSpec((1, tm, dm), lambda i: (i // nt, i % nt, 0))
    rows_of = lambda f: pl.BlockSpec((1, TOP_K, tm), lambda i: (f(i), 0, 0), memory_space=pltpu.SMEM)
    return pl.pallas_call(
        _final_kernel,
        grid=(n_tiles,),
        in_specs=[rows_of(lambda i: i), rows_of(lambda i: jnp.minimum(i + 1, n_tiles - 1)),
                  pl.BlockSpec((tm, TOP_K), lambda i: (i, 0)),
                  tile, pl.BlockSpec((1, 1, dm), lambda i: (i // nt, 0, 0)),
                  pl.BlockSpec((1, dm), lambda i: (0, 0)), pl.BlockSpec((1, dm), lambda i: (0, 0)),
                  pl.BlockSpec(memory_space=pl.ANY)],
        out_specs=tile,
        out_shape=jax.ShapeDtypeStruct((bsz, seq, dm), F32),
        scratch_shapes=[pltpu.VMEM((2, TOP_K, tm, ys.shape[1]), jnp.uint32), pltpu.SemaphoreType.DMA((2,))],
        compiler_params=_cparams(("arbitrary",)),
        name="combine_final_layernorm",
    )(dest3, dest3, w_tok, base, g2, ln_g, ln_b, ys)


def _rel_bucket(dist):
    dist = np.maximum(dist, 0)
    exact = REL_BUCKETS // 2
    log_ratio = np.log(np.maximum(dist, 1).astype(np.float32) / np.float32(exact)) / np.float32(math.log(REL_MAX_DIST / exact))
    large = np.minimum(exact + (log_ratio * (REL_BUCKETS - exact)).astype(np.int32), REL_BUCKETS - 1)
    return np.where(dist < exact, dist, large)


def _bias_tiles(rel_bias):
    n_d = 4 * SEL_LEN
    pick = (_rel_bucket(np.arange(n_d))[:, None] == np.arange(REL_BUCKETS)[None, :]).astype(np.float32)
    vec = jnp.dot(jnp.asarray(pick), rel_bias.astype(F32), precision=lax.Precision.HIGHEST).T
    far = rel_bias[REL_BUCKETS - 1]
    vec = ((vec - far[:, None]) * LOG2E).reshape(N_KV, GQA, n_d)
    tok = np.arange(Q_BLOCK)[None, :]
    key = np.arange(SEL_LEN)[:, None]
    padded = jnp.pad(vec, ((0, 0), (0, 0), (Q_BLOCK, 0)))

    def by_distance(d):
        first = np.clip(d[:, 0], -Q_BLOCK, n_d - Q_BLOCK) + Q_BLOCK
        rows = [lax.slice_in_dim(padded, int(f), int(f) + Q_BLOCK, axis=2) for f in first]
        return jnp.stack(rows, axis=2)

    def toeplitz(d):
        vals = jnp.where(jnp.asarray(d >= 0), by_distance(d), NEG)
        return jnp.transpose(vals, (0, 2, 1, 3)).reshape(N_KV, d.shape[0], ROWS)

    near = [toeplitz(delta + tok - key) for delta in (0, SEL_LEN, 2 * SEL_LEN)]
    zero = jnp.zeros((N_KV, SEL_LEN, ROWS), F32)
    edge = np.where(tok < key, 0.0, NEG).astype(np.float32)
    edge = jnp.broadcast_to(jnp.asarray(np.tile(edge, (1, GQA)))[None], (N_KV, SEL_LEN, ROWS))
    nb = jnp.stack(near + [zero, jnp.full_like(zero, NEG), edge], axis=1)
    w = np.arange(BAND_ROWS)[:, None]
    bands = []
    for ph in (0, 1):
        d = tok - CMP_STRIDE * (w - PAD_CMP - (Q_BLOCK // CMP_STRIDE) * ph) - (CMP_LEN - 1)
        vals = jnp.where(jnp.asarray(d >= 0), by_distance(d), 0.0)
        bands.append(jnp.transpose(vals, (0, 2, 1, 3)).reshape(N_KV, BAND_ROWS, ROWS))
    cb = jnp.stack(bands, axis=1)
    return nb.astype(F32), cb.astype(F32)


def _padded_cmp_rows(seq):
    return -(-(seq // CMP_STRIDE + 2 * PAD_CMP) // LANES) * LANES


def _overlap_t(seq):
    n_cmp = (seq - CMP_LEN) // CMP_STRIDE + 1
    n_blk = seq // SEL_LEN
    ncp = _padded_cmp_rows(seq)
    c_start = np.arange(n_cmp) * CMP_STRIDE
    c_end = c_start + CMP_LEN - 1
    blk = np.arange(n_blk)
    ov = ((c_start[:, None] < (blk[None, :] + 1) * SEL_LEN) & (c_end[:, None] >= blk[None, :] * SEL_LEN))
    out = np.zeros((2 * SUBLANES + n_blk, ncp), np.float32)
    out[0] = 1.0
    out[2 * SUBLANES:, PAD_CMP:PAD_CMP + n_cmp] = ov.T
    return jnp.asarray(out, BF16)


def _cmp_visibility(seq):
    n_cmp = (seq - CMP_LEN) // CMP_STRIDE + 1
    ncp = _padded_cmp_rows(seq)
    rho = np.arange(ncp)[:, None]
    col = np.arange(COLS)[None, :]
    thr = CMP_STRIDE * (rho - PAD_CMP) + (CMP_LEN - 1) - Q_BLOCK * (col // ROWS) - col % Q_BLOCK
    valid = (rho >= PAD_CMP) & (rho < PAD_CMP + n_cmp)
    return jnp.asarray(np.where(valid, thr, np.iinfo(np.int32).max), jnp.int32)


def _s5_params(lam_re, lam_im, log_step, b_re, b_im, c_re, c_im):
    lr, li = lam_re.astype(F32), lam_im.astype(F32)
    dt = jnp.exp(log_step.astype(F32))[:, None]
    mag = jnp.exp(lr * dt)
    ar, ai = mag * jnp.cos(li * dt), mag * jnp.sin(li * dt)
    den = lr * lr + li * li
    kr = ((ar - 1.0) * lr + ai * li) / den
    ki = (ai * lr - (ar - 1.0) * li) / den
    br, bi = b_re.astype(F32), b_im.astype(F32)
    bbr = kr[..., None] * br - ki[..., None] * bi
    bbi = kr[..., None] * bi + ki[..., None] * br
    n_g = lr.shape[0]
    eye = jnp.eye(n_g, dtype=F32)

    def drive(bb):
        return jnp.einsum('gpc,gh->gchp', bb, eye).reshape(n_g * SSM_GROUP, n_g * SSM_STATE)

    def readout(c):
        return jnp.einsum('gcp,gh->gphc', c, eye).reshape(n_g * SSM_STATE, n_g * SSM_GROUP)

    d_re, d_im = drive(bbr), drive(bbi)
    r_re, r_im = readout(c_re.astype(F32)), -readout(c_im.astype(F32))
    c_in, c_st = d_re.shape[0] // S5_DIAG_BLOCKS, d_re.shape[1] // S5_DIAG_BLOCKS
    blk = lambda m, k, rows, cols: m[k * rows:(k + 1) * rows, k * cols:(k + 1) * cols]
    wb = jnp.stack([jnp.concatenate([blk(d_re, k, c_in, c_st), blk(d_im, k, c_in, c_st)], axis=1)
                    for k in range(S5_DIAG_BLOCKS)]).astype(BF16)
    wc = jnp.stack([jnp.concatenate([blk(r_re, k, c_st, c_in), blk(r_im, k, c_st, c_in)], axis=0)
                    for k in range(S5_DIAG_BLOCKS)]).astype(BF16)
    a = jnp.concatenate([ar.reshape(1, -1), ai.reshape(1, -1)], axis=1)
    return wb, jnp.broadcast_to(a, (SUBLANES, a.shape[1])), wc


def _layer(x, mod, w_in, lam_re, lam_im, log_step, b_re, b_im, c_re, c_im, d_skip, glu_w, cmp_pos, cmp_w1,
           cmp_w2, rel_bias, w_out, ln1_g, ln1_b, router_w, router_bias, e_gate, e_up, e_down, sg, su, sd,
           ln2_g, ln2_b, alpha):
    bsz, seq, dm = x.shape
    n_tok = bsz * seq
    sh1, sc1, g1, sh2, sc2, g2 = [m[:, None, :] for m in jnp.split(mod, 6, axis=-1)]
    ssm_w = dm // 2
    attn_w = N_HEADS * HEAD_DIM
    kv_w = N_KV * HEAD_DIM
    n_gate = 3 * N_HEADS
    offs = np.cumsum([0, ssm_w, attn_w] + [kv_w] * 6 + [n_gate, 2 * dm])

    wu = w_in[:, offs[0]:offs[1]].astype(BF16)
    wq = w_in[:, offs[1]:offs[2]] * (HEAD_DIM ** -0.5 * LOG2E)
    wa = jnp.concatenate([wq, w_in[:, offs[2]:offs[4]]], axis=1).astype(BF16)
    wg = jnp.pad(w_in[:, offs[8]:offs[9]], ((0, 0), (0, LANES - n_gate))).astype(BF16)
    wm = w_in[:, offs[9]:offs[10]].astype(BF16)
    nq = seq // Q_BLOCK
    n_blk = seq // SEL_LEN
    nch = seq // CMP_STRIDE

    def head_padded(w, width):
        w = w.reshape(dm, N_KV, HEAD_DIM)
        return jnp.pad(w, ((0, 0), (0, 0), (0, width - HEAD_DIM))).reshape(dm, N_KV * width).astype(BF16)

    aug_w = 2 * HEAD_DIM + -(-n_blk // LANES) * LANES
    u2d, act, gates, mg, kaug, kw, vst, vwt = _input_projection(
        x, sh1, sc1, wu, wa, wg, wm, head_padded(w_in[:, offs[4]:offs[5]], aug_w),
        head_padded(w_in[:, offs[6]:offs[7]], 2 * HEAD_DIM),
        w_in[:, offs[5]:offs[6]].T.astype(BF16), w_in[:, offs[7]:offs[8]].T.astype(BF16))

    def piece(i):
        return act[:, :, attn_w + i * kv_w: attn_w + (i + 1) * kv_w]

    def compress(raw, pos, w1, w2):
        xc = raw.reshape(bsz, nch, CMP_STRIDE, N_KV, HEAD_DIM).transpose(0, 3, 1, 2, 4).reshape(bsz, N_KV, nch, CMP_STRIDE * HEAD_DIM)
        half = CMP_STRIDE * HEAD_DIM
        w1cat = jnp.concatenate([w1[:half], w1[half:]], axis=1).astype(BF16)
        posb = jnp.dot(pos.reshape(1, -1), w1, precision=lax.Precision.HIGHEST)
        return _compress(xc, w1cat, posb, w2.astype(BF16))

    kc = compress(piece(0), cmp_pos[0], cmp_w1[0], cmp_w2[0])
    vc = compress(piece(1), cmp_pos[1], cmp_w1[1], cmp_w2[1])
    pad = ((0, 0), (0, 0), (PAD_CMP, _padded_cmp_rows(seq) - nch - PAD_CMP), (0, 0))
    kcp = jnp.pad(kc, pad).astype(BF16)
    vcpt = jnp.swapaxes(jnp.pad(vc, pad), 2, 3).astype(BF16)
    ones_ovt = _overlap_t(seq)
    cmp_lhs = jnp.concatenate([vcpt, jnp.broadcast_to(ones_ovt, (bsz, N_KV) + ones_ovt.shape)], axis=2)

    nst = nq // NSA_TILES
    gt = gates[:, :, :n_gate].reshape(bsz, nst, NSA_TILES, Q_BLOCK, N_KV, GQA, 3)
    gt = gt.transpose(0, 4, 1, 6, 2, 5, 3).reshape(bsz, N_KV, nst, 3, COLS)
    gt = jnp.pad(gt, ((0, 0), (0, 0), (0, 0), (0, SUBLANES - 3), (0, 0)))
    nb, cb = _bias_tiles(rel_bias)
    y_nsa = _nsa(act, kaug, vst, kw, vwt, kcp, cmp_lhs, _cmp_visibility(seq), nb, cb, gt)

    wb, a, wc = _s5_params(lam_re, lam_im, log_step, b_re, b_im, c_re, c_im)
    z2 = _s5(u2d.reshape(seq * bsz, ssm_w), wb, a, wc, d_skip.reshape(1, ssm_w).astype(F32), bsz)
    z2d = z2.reshape(seq, bsz * ssm_w)

    base, h2 = _mix_out(z2d, y_nsa, mg, x, g1, sh2, sc2, g2, ln1_g.reshape(1, dm), ln1_b.reshape(1, dm),
                        glu_w.astype(BF16), w_out.astype(BF16), sg.astype(BF16), su.astype(BF16), sd.astype(BF16), alpha)
    h2 = h2.reshape(n_tok, dm // 2)

    n_exp = router_w.shape[1]
    rt = min(ROUTER_TILE, n_tok)
    tri = jnp.asarray(np.triu(np.ones((rt, rt), np.float32), 1), BF16)
    top_e, top_w, rank, counts = _router(h2, router_w.T.astype(BF16), router_bias.reshape(n_exp, 1).astype(F32), tri)
    counts = counts[:, 0].astype(jnp.int32)
    padded = (counts + EXPERT_TILE - 1) // EXPERT_TILE * EXPERT_TILE
    pad_end = jnp.cumsum(padded)
    pad_start = pad_end - padded
    dest = _dest_rows(pad_start.astype(jnp.int32), top_e, rank)
    n_rows = n_tok * TOP_K + n_exp * EXPERT_TILE
    n_tiles = n_rows // EXPERT_TILE
    tile_exp = jnp.minimum(jnp.searchsorted(pad_end, jnp.arange(n_tiles) * EXPERT_TILE, side='right'), n_exp - 1).astype(jnp.int32)
    n_used = (pad_end[-1] // EXPERT_TILE).astype(jnp.int32).reshape(1)
    mt = min(MOVE_TILE, seq)
    dest3 = dest.reshape(TOP_K, n_tok // mt, mt).transpose(1, 0, 2)
    xs = _dispatch(pad_end.astype(jnp.int32), padded.astype(jnp.int32), n_used, dest3, h2, n_rows)
    ys = _experts(tile_exp, n_used, xs, e_gate, e_up, e_down)
    return _final(dest3, top_w.T, base, g2, ln2_g.reshape(1, dm), ln2_b.reshape(1, dm), ys)


def kernel(x, c, ada_w, ada_b, w_in, ssm_lambda_re, ssm_lambda_im, ssm_log_step, ssm_b_re, ssm_b_im, ssm_c_re, ssm_c_im, ssm_d, ssm_glu_w, cmp_pos, cmp_w1, cmp_w2, rel_bias, w_out, ln1_g, ln1_b, router_w, router_bias, exp_w_gate, exp_w_up, exp_w_down, sh_w_gate, sh_w_up, sh_w_down, ln2_g, ln2_b):
    depth = ada_w.shape[0]
    alpha = (2 * depth) ** 0.25
    for l in range(depth):
        mod = _modulation(c, ada_w[l], ada_b[l])
        x = _layer(x, mod, w_in[l], ssm_lambda_re[l], ssm_lambda_im[l], ssm_log_step[l], ssm_b_re[l], ssm_b_im[l],
                   ssm_c_re[l], ssm_c_im[l], ssm_d[l], ssm_glu_w[l], cmp_pos[l], cmp_w1[l], cmp_w2[l], rel_bias,
                   w_out[l], ln1_g[l], ln1_b[l], router_w[l], router_bias[l], exp_w_gate[l], exp_w_up[l],
                   exp_w_down[l], sh_w_gate[l], sh_w_up[l], sh_w_down[l], ln2_g[l], ln2_b[l], alpha)
    return x
```

```python
import functools
import math

import numpy as np
import jax
import jax.numpy as jnp
from jax import lax
from jax.experimental import pallas as pl
from jax.experimental.pallas import tpu as pltpu

F32 = jnp.float32
BF16 = jnp.bfloat16

SSM_GROUP = 16
SSM_STATE = 64
N_HEADS = 16
HEAD_DIM = 64
N_KV = 4
GQA = N_HEADS // N_KV
CMP_LEN = 32
CMP_STRIDE = 16
SEL_LEN = 64
SEL_TOP = 16
WINDOW = 512
Q_BLOCK = 64
FORCE_BONUS = 1.0e4
N_FORCED = 3
assert GQA < FORCE_BONUS
REL_BUCKETS = 32
REL_MAX_DIST = 128
TOP_K = 8
ROUTED_SCALE = 2.5
LN_EPS = 1e-5

LANES = 128
SUBLANES = 8
VMEM_LIMIT_BYTES = 56 * 1024 * 1024
ROW_TILE = 512
ROUTER_TILE = 256
EXPERT_TILE = 512
MOVE_TILE = 256
DEST_TILE = 2048
SCAN_CHUNK = 128
SCAN_LANES = 512
S5_DIAG_BLOCKS = 2
MASK_BIG = 32768.0
NEG = -1.0e30
LOG2E = math.log2(math.e)
WIN_BLOCKS = WINDOW // SEL_LEN
ROWS = GQA * Q_BLOCK
NSA_TILES = 4
COLS = NSA_TILES * ROWS
CHUNK_BLOCKS = 8
CHUNK_KEYS = CHUNK_BLOCKS * SEL_LEN
FAR_BLOCKS = NSA_TILES
FAR_KEYS = FAR_BLOCKS * SEL_LEN
PAD_CMP = 8
BAND_ROWS = 24


def _cparams(sem):
    return pltpu.CompilerParams(dimension_semantics=sem, vmem_limit_bytes=VMEM_LIMIT_BYTES)


def _pack_rows(x):
    n = x.shape[1] // 2
    xb = x.astype(jnp.bfloat16).astype(F32)
    lo = lax.shift_right_logical(lax.bitcast_convert_type(xb[:, :n], jnp.uint32), jnp.uint32(16))
    return lax.bitcast_convert_type(xb[:, n:], jnp.uint32) | lo


def _unpack_rows(w):
    lo = lax.bitcast_convert_type(lax.shift_left(w, jnp.uint32(16)), F32)
    hi = lax.bitcast_convert_type(w & jnp.uint32(0xFFFF0000), F32)
    return lo, hi


def _norm_rows(x):
    mu = jnp.mean(x, axis=-1, keepdims=True)
    xc = x - mu
    var = jnp.mean(xc * xc, axis=-1, keepdims=True)
    return xc * lax.rsqrt(var + LN_EPS)


def _mod_kernel(c_ref, w_ref, b_ref, o_ref):
    cond = jax.nn.silu(c_ref[...])
    o_ref[...] = jnp.dot(cond.astype(BF16), w_ref[...].astype(BF16), preferred_element_type=F32) + b_ref[...]


def _modulation(c, ada_w, ada_b):
    bsz, dm = c.shape
    n = ada_w.shape[1]
    tn = dm
    return pl.pallas_call(
        _mod_kernel,
        grid=(n // tn,),
        in_specs=[pl.BlockSpec((bsz, dm), lambda j: (0, 0)),
                  pl.BlockSpec((dm, tn), lambda j: (0, j)),
                  pl.BlockSpec((1, tn), lambda j: (0, j))],
        out_specs=pl.BlockSpec((bsz, tn), lambda j: (0, j)),
        out_shape=jax.ShapeDtypeStruct((bsz, n), F32),
        compiler_params=_cparams(("arbitrary",)),
        name="ada_modulation",
    )(c, ada_w, ada_b.reshape(1, n))


def _inproj_kernel(x_ref, sh_ref, sc_ref, wu_ref, wa_ref, wg_ref, wm_ref, wks_ref, wkw_ref, wvs_ref, wvw_ref,
                   u_ref, a_ref, g_ref, m_ref, ks_ref, kw_ref, vs_ref, vw_ref, *, aug_w):
    h = _norm_rows(x_ref[0]) * (1.0 + sc_ref[0]) + sh_ref[0]
    hb = h.astype(BF16)
    tm = hb.shape[0]
    u_ref[...] = jnp.dot(hb, wu_ref[...], preferred_element_type=F32)
    a_ref[0] = jnp.dot(hb, wa_ref[...], preferred_element_type=F32).astype(BF16)
    g_ref[0] = jax.nn.sigmoid(jnp.dot(hb, wg_ref[...], preferred_element_type=F32))
    m_ref[0] = jax.nn.sigmoid(jnp.dot(hb, wm_ref[...], preferred_element_type=F32)).astype(BF16)
    low = lax.broadcasted_iota(jnp.int32, (tm, LANES), 1) < HEAD_DIM

    def spread_heads(k4):
        cols = [k4[:, (g // 2) * LANES:(g // 2 + 1) * LANES] for g in range(N_KV)]
        return [jnp.where(low, pltpu.roll(c, HEAD_DIM, 1) if g % 2 else c, 0.0) for g, c in enumerate(cols)]

    oh_shape = (tm, aug_w - 2 * HEAD_DIM)
    blk = jnp.right_shift(pl.program_id(1) * tm + lax.broadcasted_iota(jnp.int32, oh_shape, 0), SEL_LEN.bit_length() - 1)
    onehot = jnp.where(lax.broadcasted_iota(jnp.int32, oh_shape, 1) == blk, -MASK_BIG, 0.0)
    ks = spread_heads(jnp.dot(hb, wks_ref[...], preferred_element_type=F32))
    ks_ref[0] = jnp.concatenate([part for k in ks for part in (k, onehot)], axis=1).astype(BF16)
    kw_ref[0] = jnp.concatenate(spread_heads(jnp.dot(hb, wkw_ref[...], preferred_element_type=F32)), axis=1).astype(BF16)
    rows = lax.broadcasted_iota(jnp.int32, (2 * SUBLANES, tm), 0)
    ones_rows = jnp.where(rows == 0, 1.0, 0.0).astype(BF16)
    for w_ref, v_ref in ((wvs_ref, vs_ref), (wvw_ref, vw_ref)):
        vt = lax.dot_general(w_ref[...], hb, (((1,), (1,)), ((), ())), preferred_element_type=F32).astype(BF16)
        for g in range(N_KV):
            v_ref[0, g] = jnp.concatenate([vt[g * HEAD_DIM:(g + 1) * HEAD_DIM], ones_rows], axis=0)


def _input_projection(x, sh1, sc1, wu, wa, wg, wm, wks, wkw, wvs_t, wvw_t, aug_w):
    bsz, seq, dm = x.shape
    tm = min(ROW_TILE, seq)
    nu, na, ng, nm = wu.shape[1], wa.shape[1], wg.shape[1], wm.shape[1]
    assert 2 * HEAD_DIM == LANES
    n_ks, n_kw = N_KV * aug_w, N_KV * 2 * HEAD_DIM
    full = lambda w: pl.BlockSpec(w.shape, lambda b, i: (0, 0))
    vec = pl.BlockSpec((1, 1, dm), lambda b, i: (b, 0, 0))
    rows_out = lambda n: pl.BlockSpec((1, tm, n), lambda b, i: (b, i, 0))
    vt_rows = HEAD_DIM + 2 * SUBLANES
    vt_out = pl.BlockSpec((1, N_KV, vt_rows, tm), lambda b, i: (b, 0, 0, i))
    return pl.pallas_call(
        functools.partial(_inproj_kernel, aug_w=aug_w),
        grid=(bsz, seq // tm),
        in_specs=[pl.BlockSpec((1, tm, dm), lambda b, i: (b, i, 0)), vec, vec,
                  full(wu), full(wa), full(wg), full(wm), full(wks), full(wkw), full(wvs_t), full(wvw_t)],
        out_specs=[pl.BlockSpec((tm, nu), lambda b, i: (i, b)), rows_out(na), rows_out(ng), rows_out(nm),
                   rows_out(n_ks), rows_out(n_kw), vt_out, vt_out],
        out_shape=[jax.ShapeDtypeStruct((seq, bsz * nu), F32),
                   jax.ShapeDtypeStruct((bsz, seq, na), BF16),
                   jax.ShapeDtypeStruct((bsz, seq, ng), F32),
                   jax.ShapeDtypeStruct((bsz, seq, nm), BF16),
                   jax.ShapeDtypeStruct((bsz, seq, n_ks), BF16),
                   jax.ShapeDtypeStruct((bsz, seq, n_kw), BF16),
                   jax.ShapeDtypeStruct((bsz, N_KV, vt_rows, seq), BF16),
                   jax.ShapeDtypeStruct((bsz, N_KV, vt_rows, seq), BF16)],
        compiler_params=_cparams(("arbitrary", "arbitrary")),
        name="adaln_input_projection",
    )(x, sh1, sc1, wu, wa, wg, wm, wks, wkw, wvs_t, wvw_t)


def _compress_kernel(x_ref, w1_ref, pb_ref, w2_ref, o_ref, *, n_cmp):
    hid = w2_ref.shape[0]
    p = jnp.dot(x_ref[0, 0], w1_ref[...], preferred_element_type=F32)
    nrow = p.shape[0]
    nxt = pltpu.roll(p[:, hid:], nrow - 1, 0)
    hidv = jax.nn.gelu(p[:, :hid] + nxt + pb_ref[...])
    out = jnp.dot(hidv.astype(BF16), w2_ref[...], preferred_element_type=F32)
    rows = lax.broadcasted_iota(jnp.int32, out.shape, 0)
    o_ref[0, 0] = jnp.where(rows < n_cmp, out, 0.0)


def _compress(xc, w1cat, posb, w2):
    bsz, nkv, nch, kdim = xc.shape
    hid2 = w1cat.shape[1]
    return pl.pallas_call(
        functools.partial(_compress_kernel, n_cmp=nch - 1),
        grid=(bsz, nkv),
        in_specs=[pl.BlockSpec((1, 1, nch, kdim), lambda b, g: (b, g, 0, 0)),
                  pl.BlockSpec((kdim, hid2), lambda b, g: (0, 0)),
                  pl.BlockSpec((1, hid2 // 2), lambda b, g: (0, 0)),
                  pl.BlockSpec((hid2 // 2, HEAD_DIM), lambda b, g: (0, 0))],
        out_specs=pl.BlockSpec((1, 1, nch, HEAD_DIM), lambda b, g: (b, g, 0, 0)),
        out_shape=jax.ShapeDtypeStruct((bsz, nkv, nch, HEAD_DIM), F32),
        compiler_params=_cparams(("arbitrary", "arbitrary")),
        name="kv_compress",
    )(xc, w1cat, posb, w2)


def _swap_heads_tokens(a):
    t = jnp.concatenate([a, jnp.zeros_like(a)], axis=0).T
    p = [t[r * HEAD_DIM:(r + 1) * HEAD_DIM] for r in range(GQA)]
    return jnp.concatenate([p[r] + pltpu.roll(p[r + 1], Q_BLOCK, 1) for r in range(0, GQA, 2)], axis=1)


def _nsa_kernel(q_ref, kaug_ref, vst_ref, kw_ref, vwt_ref, kcp_ref, cl_ref, vis_ref, nb_ref, cb_ref,
                g_ref, o_ref, s_ref, c_ref, w_ref, fa_ref, fb_ref, *, n_blk, n_sel):
    q0 = pl.program_id(2) * NSA_TILES
    qf = q_ref[0].astype(F32)
    qt = jnp.concatenate([_swap_heads_tokens(qf[n * Q_BLOCK:(n + 1) * Q_BLOCK]) for n in range(NSA_TILES)],
                         axis=1).astype(BF16)
    col_tile = lambda n: slice(n * ROWS, (n + 1) * ROWS)

    s_ref[...] = jnp.dot(kcp_ref[0, 0], qt, preferred_element_type=F32)
    for n in range(NSA_TILES):
        band = pl.multiple_of(SUBLANES * ((q0 + n) // 2), SUBLANES)
        s_ref[pl.ds(band, BAND_ROWS), col_tile(n)] += cb_ref[0, n % 2]
    s = jnp.where(vis_ref[...] <= Q_BLOCK * q0, s_ref[...], -jnp.inf)
    mx = jnp.max(s, axis=0, keepdims=True)
    mx = jnp.where(mx == -jnp.inf, 0.0, mx)
    e = jnp.exp2((s - mx).astype(BF16))
    both = jnp.dot(cl_ref[0, 0], e, preferred_element_type=F32)
    inv = 1.0 / jnp.maximum(both[HEAD_DIM:HEAD_DIM + 1], 1e-30)
    o_cmp = both[:HEAD_DIM] * inv

    imp4 = both[HEAD_DIM + 2 * SUBLANES:] * inv
    sums = []
    for n in range(NSA_TILES):
        two = imp4[:, n * ROWS:n * ROWS + LANES] + imp4[:, n * ROWS + LANES:(n + 1) * ROWS]
        sums.append(two + pltpu.roll(two, Q_BLOCK, 1))
    low = lax.broadcasted_iota(jnp.int32, sums[0].shape, 1) < Q_BLOCK
    imp = jnp.concatenate([jnp.where(low, sums[n], sums[n + 1]) for n in range(0, NSA_TILES, 2)], axis=1)
    blk = lax.broadcasted_iota(jnp.int32, imp.shape, 0)
    cur = q0 + jnp.right_shift(lax.broadcasted_iota(jnp.int32, imp.shape, 1), Q_BLOCK.bit_length() - 1)
    blkf = blk.astype(F32)
    forced = (blk == 0) | (blk == cur) | (blk == cur - 1)
    score = jnp.where((blk <= cur) & ~forced, imp, -jnp.inf)
    notsel = jnp.where(forced, 0.0, 1.0)
    for _ in range(n_sel - N_FORCED):
        best = jnp.max(score, axis=0, keepdims=True)
        first = jnp.min(jnp.where(score == best, blkf, float(n_blk)), axis=0, keepdims=True)
        pick = blkf == first
        notsel = jnp.where(pick, 0.0, notsel)
        score = jnp.where(pick, -jnp.inf, score)
    halves = []
    for n in range(0, NSA_TILES, 2):
        pair = notsel[:, (n // 2) * LANES:(n // 2 + 1) * LANES]
        swapped = pltpu.roll(pair, Q_BLOCK, 1)
        halves += [jnp.where(low, pair, swapped), jnp.where(low, swapped, pair)]
    notsel = jnp.concatenate([h for h in halves for _ in (0, 1)], axis=1).astype(BF16)
    qwin = jnp.concatenate([qt, jnp.zeros_like(qt)], axis=0)
    tail = kaug_ref.shape[2] - 2 * HEAD_DIM - n_blk
    qaug = jnp.concatenate([qwin, notsel] + ([jnp.zeros((tail, COLS), notsel.dtype)] if tail else []), axis=0)

    def update(carry, sc, vt, top=None):
        m, acc = carry
        m_new = jnp.maximum(m, jnp.max(sc, axis=0, keepdims=True) if top is None else top)
        p = jnp.exp2((sc - m_new).astype(BF16))
        return m_new, jnp.exp2(m - m_new) * acc + jnp.dot(vt, p, preferred_element_type=F32)

    def chunk_keys(c):
        return pl.ds(pl.multiple_of(c * FAR_KEYS, FAR_KEYS), FAR_KEYS)

    def far_scores(c, buf_ref):
        sc = jnp.dot(kaug_ref[0, chunk_keys(c), :], qaug, preferred_element_type=F32)
        buf_ref[...] = sc
        return jnp.max(sc, axis=0, keepdims=True)

    def far_pair(i, carry):
        m, acc, top_a = carry
        top_b = far_scores(2 * i + 1, fb_ref)
        m, acc = update((m, acc), fa_ref[...], vst_ref[0, 0, :, chunk_keys(2 * i)], top_a)
        top_a = far_scores(2 * i + 2, fa_ref)
        m, acc = update((m, acc), fb_ref[...], vst_ref[0, 0, :, chunk_keys(2 * i + 1)], top_b)
        return m, acc, top_a

    def far_last(_, carry, n_far):
        m, acc, top_a = carry
        return update((m, acc), fa_ref[...], vst_ref[0, 0, :, chunk_keys(n_far - 1)], top_a) + (top_a,)

    def near_chunk(k_ref, vt_ref, qmat, blk0, tile_of, carry, buf_ref):
        n_keys = buf_ref.shape[0]
        keys = pl.ds(pl.multiple_of(blk0 * SEL_LEN, NSA_TILES * SEL_LEN), n_keys)
        buf_ref[...] = jnp.dot(k_ref[0, keys, :], qmat, preferred_element_type=F32)
        for o in range(n_keys // SEL_LEN):
            for n in range(NSA_TILES):
                buf_ref[o * SEL_LEN:(o + 1) * SEL_LEN, col_tile(n)] += nb_ref[0, tile_of(blk0 + o, q0 + n)]
        return update(carry, buf_ref[...], vt_ref[0, 0, :, keys])

    def sel_tile(j, qi):
        d = qi - j
        return jnp.where(d < 0, 4, jnp.minimum(d, 3))

    def win_tile(j, qi):
        d = qi - j
        return jnp.where((d < 0) | (d > WIN_BLOCKS), 4, jnp.where(d == WIN_BLOCKS, 5, jnp.minimum(d, 3)))

    init = (jnp.full((1, COLS), NEG, F32), jnp.zeros((vst_ref.shape[2], COLS), F32))
    _, acc = near_chunk(kw_ref, vwt_ref, qwin, jnp.maximum(q0 - WIN_BLOCKS, 0), win_tile, init, w_ref)
    o_win = acc[:HEAD_DIM] / acc[HEAD_DIM:HEAD_DIM + 1]
    n_far = jnp.maximum(q0 - 2, 0) // FAR_BLOCKS
    carry = lax.fori_loop(0, n_far // 2, far_pair, init + (far_scores(0, fa_ref),))
    carry = lax.fori_loop(0, n_far % 2, functools.partial(far_last, n_far=n_far), carry)
    _, acc = near_chunk(kaug_ref, vst_ref, qaug, n_far * FAR_BLOCKS, sel_tile, carry[:2], c_ref)
    o_slc = acc[:HEAD_DIM] / acc[HEAD_DIM:HEAD_DIM + 1]
    g = g_ref[0, 0, 0]
    out = g[0:1] * o_cmp + g[1:2] * o_slc + g[2:3] * o_win
    o_ref[0] = jnp.concatenate([_swap_heads_tokens(out[:, col_tile(n)]) for n in range(NSA_TILES)],
                               axis=0).astype(BF16)


def _nsa(act, kaug, vst, kw, vwt, kcp, cmp_lhs, vis, nb, cb, gt):
    bsz, nkv, nsteps, _, cols = gt.shape
    dh = HEAD_DIM
    seq = kw.shape[1]
    step_tokens = NSA_TILES * Q_BLOCK
    q_blk = pl.BlockSpec((1, step_tokens, GQA * dh), lambda b, g, i: (b, i, g))
    n_blk = seq // SEL_LEN
    ncp = kcp.shape[2]
    assert n_blk % CHUNK_BLOCKS == 0 and n_blk >= WIN_BLOCKS + NSA_TILES and CHUNK_BLOCKS == FAR_BLOCKS + NSA_TILES
    per_bg = lambda shape: pl.BlockSpec((1, 1) + shape, lambda b, g, i: (b, g, 0, 0))
    per_step = lambda shape: pl.BlockSpec((1, 1, 1) + shape, lambda b, g, i: (b, g, i, 0, 0))
    head_cols = lambda a: pl.BlockSpec((1, seq, a.shape[2] // nkv), lambda b, g, i: (b, 0, g))
    kern = functools.partial(_nsa_kernel, n_blk=n_blk, n_sel=min(SEL_TOP, n_blk))
    return pl.pallas_call(
        kern,
        grid=(bsz, nkv, nsteps),
        in_specs=[q_blk,
                  head_cols(kaug), per_bg((vst.shape[2], seq)), head_cols(kw), per_bg((vwt.shape[2], seq)),
                  per_bg((ncp, dh)), per_bg((cmp_lhs.shape[2], ncp)),
                  pl.BlockSpec(vis.shape, lambda b, g, i: (0, 0)),
                  pl.BlockSpec((1,) + nb.shape[1:], lambda b, g, i: (g, 0, 0, 0)),
                  pl.BlockSpec((1,) + cb.shape[1:], lambda b, g, i: (g, 0, 0, 0)),
                  per_step((SUBLANES, cols))],
        out_specs=q_blk,
        out_shape=jax.ShapeDtypeStruct((bsz, seq, nkv * GQA * dh), BF16),
        scratch_shapes=[pltpu.VMEM((ncp, cols), F32), pltpu.VMEM((CHUNK_KEYS, cols), F32),
                        pltpu.VMEM(((WIN_BLOCKS + NSA_TILES) * SEL_LEN, cols), F32),
                        pltpu.VMEM((FAR_KEYS, cols), F32), pltpu.VMEM((FAR_KEYS, cols), F32)],
        compiler_params=_cparams(("arbitrary", "arbitrary", "arbitrary")),
        name="nsa_attention",
    )(act, kaug, vst, kw, vwt, kcp, cmp_lhs, vis, nb, cb, gt)


def _s5_kernel(u_ref, wb_ref, a_ref, wc_ref, d_ref, z_ref, xs_ref, st_ref, *, bsz, n_state):
    @pl.when(pl.program_id(0) == 0)
    def _():
        st_ref[...] = jnp.zeros_like(st_ref)

    u = u_ref[...]
    ub = u.astype(BF16)
    n_diag = wb_ref.shape[0]
    c_in, c_st = u.shape[1] // n_diag, n_state // n_diag
    for k in range(n_diag):
        drv = jnp.dot(ub[:, k * c_in:(k + 1) * c_in], wb_ref[k], preferred_element_type=F32)
        xs_ref[:, k * c_st:(k + 1) * c_st] = drv[:, :c_st]
        xs_ref[:, n_state + k * c_st:n_state + (k + 1) * c_st] = drv[:, c_st:]
    steps = u.shape[0] // bsz
    for c0 in range(0, n_state, SCAN_LANES):
        re = pl.ds(c0, SCAN_LANES)
        im = pl.ds(n_state + c0, SCAN_LANES)
        ar = jnp.broadcast_to(a_ref[0:1, re], (bsz, SCAN_LANES))
        ai = jnp.broadcast_to(a_ref[0:1, im], (bsz, SCAN_LANES))

        def step(t, carry):
            xr, xi = carry
            rows = pl.ds(pl.multiple_of(t * bsz, bsz), bsz)
            nr = ar * xr - ai * xi + xs_ref[rows, re]
            ni = ar * xi + ai * xr + xs_ref[rows, im]
            xs_ref[rows, re] = nr
            xs_ref[rows, im] = ni
            return nr, ni

        xr, xi = lax.fori_loop(0, steps, step, (st_ref[:, re], st_ref[:, im]), unroll=8)
        st_ref[:, re] = xr
        st_ref[:, im] = xi
    ys = []
    for k in range(n_diag):
        st = jnp.concatenate([xs_ref[:, k * c_st:(k + 1) * c_st],
                              xs_ref[:, n_state + k * c_st:n_state + (k + 1) * c_st]], axis=1)
        ys.append(jnp.dot(st.astype(BF16), wc_ref[k], preferred_element_type=F32))
    y = jnp.concatenate(ys, axis=1) + d_ref[...] * u
    z_ref[...] = jax.nn.gelu(y).astype(BF16)


def _s5(u2, wb, a, wc, dsk, bsz):
    rows, width = u2.shape
    seq = rows // bsz
    chunk = min(SCAN_CHUNK, seq)
    n_state2 = a.shape[1]
    return pl.pallas_call(
        functools.partial(_s5_kernel, bsz=bsz, n_state=n_state2 // 2),
        grid=(seq // chunk,),
        in_specs=[pl.BlockSpec((chunk * bsz, width), lambda i: (i, 0)),
                  pl.BlockSpec(wb.shape, lambda i: (0, 0, 0)),
                  pl.BlockSpec(a.shape, lambda i: (0, 0)),
                  pl.BlockSpec(wc.shape, lambda i: (0, 0, 0)),
                  pl.BlockSpec(dsk.shape, lambda i: (0, 0))],
        out_specs=pl.BlockSpec((chunk * bsz, width), lambda i: (i, 0)),
        out_shape=jax.ShapeDtypeStruct((rows, width), BF16),
        scratch_shapes=[pltpu.VMEM((chunk * bsz, n_state2), F32), pltpu.VMEM((bsz, n_state2), F32)],
        compiler_params=_cparams(("arbitrary",)),
        name="s5_scan",
    )(u2, wb, a, wc, dsk)


def _mixout_kernel(z_ref, yn_ref, mg_ref, x_ref, g1_ref, sh2_ref, sc2_ref, g2_ref, lg_ref, lb_ref,
                   glu_ref, wo_ref, sg_ref, su_ref, sd_ref, base_ref, h_ref, *, alpha):
    dm = x_ref.shape[2]
    glu = jnp.dot(z_ref[...], glu_ref[...], preferred_element_type=F32)
    y_ssm = glu[:, :dm] * jax.nn.sigmoid(glu[:, dm:])
    mg = mg_ref[0].astype(F32)
    merged = mg[:, :dm] * y_ssm + mg[:, dm:] * yn_ref[0].astype(F32)
    y = jnp.dot(merged.astype(BF16), wo_ref[...], preferred_element_type=F32)
    x1 = _norm_rows(alpha * x_ref[0] + g1_ref[0] * y) * lg_ref[...] + lb_ref[...]
    hf = _norm_rows(x1) * (1.0 + sc2_ref[0]) + sh2_ref[0]
    h_ref[0] = _pack_rows(hf)
    h = hf.astype(BF16)
    hs =jax.nn.silu(jnp.dot(h, sg_ref[...], preferred_element_type=F32)) * jnp.dot(h, su_ref[...], preferred_element_type=F32)
    shared = jnp.dot(hs.astype(BF16), sd_ref[...], preferred_element_type=F32)
    base_ref[0] = alpha * x1 + g2_ref[0] * shared


def _mix_out(z2d, y_nsa, mg, x, g1, sh2, sc2, g2, ln_g, ln_b, glu_w, w_out, sg, su, sd, alpha):
    bsz, seq, dm = x.shape
    tm = min(ROW_TILE, seq)
    width = z2d.shape[1] // bsz
    vec = pl.BlockSpec((1, 1, dm), lambda b, i: (b, 0, 0))
    row = pl.BlockSpec((1, dm), lambda b, i: (0, 0))
    full = lambda w: pl.BlockSpec(w.shape, lambda b, i: (0, 0))
    tile = lambda n: pl.BlockSpec((1, tm, n), lambda b, i: (b, i, 0))
    return pl.pallas_call(
        functools.partial(_mixout_kernel, alpha=alpha),
        grid=(bsz, seq // tm),
        in_specs=[pl.BlockSpec((tm, width), lambda b, i: (i, b)), tile(dm), tile(2 * dm), tile(dm),
                  vec, vec, vec, vec, row, row, full(glu_w), full(w_out), full(sg), full(su), full(sd)],
        out_specs=[tile(dm), tile(dm // 2)],
        out_shape=[jax.ShapeDtypeStruct((bsz, seq, dm), F32), jax.ShapeDtypeStruct((bsz, seq, dm // 2), jnp.uint32)],
        compiler_params=_cparams(("arbitrary", "arbitrary")),
        name="merge_outproj_ln_shared",
    )(z2d, y_nsa, mg, x, g1, sh2, sc2, g2, ln_g, ln_b, glu_w, w_out, sg, su, sd)


def _router_kernel(h_ref, rwt_ref, rb_ref, tri_ref, e_ref, w_ref, p_ref, cnt_ref):
    @pl.when(pl.program_id(0) == 0)
    def _():
        cnt_ref[...] = jnp.zeros_like(cnt_ref)

    h = jnp.concatenate(_unpack_rows(h_ref[...]), axis=1).astype(BF16)
    logits = lax.dot_general(rwt_ref[...], h, (((1,), (1,)), ((), ())), preferred_element_type=F32)
    scores = jax.nn.sigmoid(logits)
    cur = scores + rb_ref[...]
    n_exp = scores.shape[0]
    eid = lax.broadcasted_iota(jnp.int32, scores.shape, 0).astype(F32)
    chosen = jnp.zeros(scores.shape, F32)
    ids, vals = [], []
    for _ in range(TOP_K):
        best = jnp.max(cur, axis=0, keepdims=True)
        first = jnp.min(jnp.where(cur == best, eid, float(n_exp)), axis=0, keepdims=True)
        pick = eid == first
        ids.append(first)
        vals.append(jnp.sum(jnp.where(pick, scores, 0.0), axis=0, keepdims=True))
        chosen = jnp.where(pick, 1.0, chosen)
        cur = jnp.where(pick, -jnp.inf, cur)
    top_s = jnp.concatenate(vals, axis=0)
    w_ref[...] = top_s / jnp.sum(top_s, axis=0, keepdims=True) * ROUTED_SCALE
    top_e = jnp.concatenate(ids, axis=0)
    e_ref[...] = top_e.astype(jnp.int32)
    before = jnp.dot(chosen.astype(BF16), tri_ref[...], preferred_element_type=F32) + cnt_ref[...]
    ranks = [jnp.sum(jnp.where(eid == ids[k], before, 0.0), axis=0, keepdims=True) for k in range(TOP_K)]
    p_ref[...] = jnp.concatenate(ranks, axis=0).astype(jnp.int32)
    cnt_ref[...] += jnp.sum(chosen, axis=1, keepdims=True)


def _router(h2, rwt, rb, tri):
    n_tok, words = h2.shape
    n_exp, dm = rwt.shape
    tm = tri.shape[0]
    kt = pl.BlockSpec((TOP_K, tm), lambda i: (0, i))
    return pl.pallas_call(
        _router_kernel,
        grid=(n_tok // tm,),
        in_specs=[pl.BlockSpec((tm, words), lambda i: (i, 0)),
                  pl.BlockSpec((n_exp, dm), lambda i: (0, 0)),
                  pl.BlockSpec((n_exp, 1), lambda i: (0, 0)),
                  pl.BlockSpec((tm, tm), lambda i: (0, 0))],
        out_specs=[kt, kt, kt, pl.BlockSpec((n_exp, 1), lambda i: (0, 0))],
        out_shape=[jax.ShapeDtypeStruct((TOP_K, n_tok), jnp.int32),
                   jax.ShapeDtypeStruct((TOP_K, n_tok), F32),
                   jax.ShapeDtypeStruct((TOP_K, n_tok), jnp.int32),
                   jax.ShapeDtypeStruct((n_exp, 1), F32)],
        compiler_params=_cparams(("arbitrary",)),
        name="router_topk_rank",
    )(h2, rwt, rb, tri)


def _dest_kernel(start_ref, e_ref, r_ref, o_ref):
    e = e_ref[...]
    start = lax.fori_loop(0, start_ref.shape[0], lambda j, acc: jnp.where(e == j, start_ref[j], acc),
                          jnp.zeros(e.shape, jnp.int32))
    o_ref[...] = start + r_ref[...]


def _dest_rows(pad_start, top_e, rank):
    n_tok = top_e.shape[1]
    tm = min(DEST_TILE, n_tok)
    blk = pl.BlockSpec((TOP_K, tm), lambda i, ps: (0, i))
    return pl.pallas_call(
        _dest_kernel,
        grid_spec=pltpu.PrefetchScalarGridSpec(num_scalar_prefetch=1, grid=(n_tok // tm,), in_specs=[blk, blk], out_specs=blk),
        out_shape=jax.ShapeDtypeStruct(top_e.shape, jnp.int32),
        compiler_params=_cparams(("arbitrary",)),
        name="moe_dest_rows",
    )(pad_start, top_e, rank)


def _row_copy(src_ref, src_row, dst_ref, dst_row, sem):
    return pltpu.make_async_copy(src_ref.at[pl.ds(src_row, 1), :], dst_ref.at[pl.ds(dst_row, 1), :], sem)


def _dispatch_kernel(pend_ref, padded_ref, nused_ref, dest_ref, h_ref, xs_ref, zero_ref, sem, zsem):
    tm = h_ref.shape[0]

    @pl.when(pl.program_id(0) == 0)
    def _():
        zero_ref[...] = jnp.zeros_like(zero_ref)
        n_tiles = xs_ref.shape[0] // EXPERT_TILE

        def zero_tile(row):
            return pltpu.make_async_copy(zero_ref, xs_ref.at[pl.ds(pl.multiple_of(row, EXPERT_TILE), EXPERT_TILE), :], zsem)

        def per_expert(act):
            def body(e, carry):
                @pl.when(padded_ref[e] > 0)
                def _():
                    act(zero_tile(pend_ref[e] - EXPERT_TILE))
                return carry
            lax.fori_loop(0, pend_ref.shape[0], body, 0)

        def per_unused(act):
            def body(i, carry):
                act(zero_tile(i * EXPERT_TILE))
                return carry
            lax.fori_loop(nused_ref[0], n_tiles, body, 0)

        for loop in (per_expert, per_unused):
            loop(lambda copy: copy.start())
        for loop in (per_expert, per_unused):
            loop(lambda copy: copy.wait())

    def issue(t8, carry):
        base = pl.multiple_of(t8 * SUBLANES, SUBLANES)
        for s in range(SUBLANES):
            for k in range(TOP_K):
                _row_copy(h_ref, base + s, xs_ref, dest_ref[0, k, base + s], sem).start(priority=k % 2)
        return carry

    lax.fori_loop(0, tm // SUBLANES, issue, 0)
    pltpu.make_async_copy(xs_ref.at[pl.ds(0, TOP_K * tm), :], xs_ref.at[pl.ds(0, TOP_K * tm), :], sem).wait()


def _dispatch(pad_end, padded, n_used, dest3, h2, n_rows):
    n_tok, words = h2.shape
    tm = dest3.shape[2]
    grid_spec = pltpu.PrefetchScalarGridSpec(
        num_scalar_prefetch=3,
        grid=(n_tok // tm,),
        in_specs=[pl.BlockSpec((1, TOP_K, tm), lambda i, *_: (i, 0, 0), memory_space=pltpu.SMEM),
                  pl.BlockSpec((tm, words), lambda i, *_: (i, 0))],
        out_specs=pl.BlockSpec(memory_space=pl.ANY),
        scratch_shapes=[pltpu.VMEM((EXPERT_TILE, words), jnp.uint32), pltpu.SemaphoreType.DMA(()),
                        pltpu.SemaphoreType.DMA(())],
    )
    return pl.pallas_call(
        _dispatch_kernel,
        grid_spec=grid_spec,
        out_shape=jax.ShapeDtypeStruct((n_rows, words), jnp.uint32),
        compiler_params=_cparams(("arbitrary",)),
        name="moe_dispatch",
    )(pad_end, padded, n_used, dest3, h2)


def _expert_kernel(te_ref, nu_ref, x_ref, wg_ref, wu_ref, wd_ref, y_ref):
    i = pl.program_id(0)

    @pl.when(i < nu_ref[0])
    def _():
        lo, hi = _unpack_rows(x_ref[...])
        lo, hi = lo.astype(BF16), hi.astype(BF16)
        half = lo.shape[1]

        def proj(w_ref):
            return (jnp.dot(lo, w_ref[0, :half, :].astype(BF16), preferred_element_type=F32)
                    + jnp.dot(hi, w_ref[0, half:, :].astype(BF16), preferred_element_type=F32))

        hmid = jax.nn.silu(proj(wg_ref)) * proj(wu_ref)
        y_ref[...] = _pack_rows(jnp.dot(hmid.astype(BF16), wd_ref[0].astype(BF16), preferred_element_type=F32))

    @pl.when(i >= nu_ref[0])
    def _():
        y_ref[...] = jnp.zeros_like(y_ref)


def _experts(tile_exp, n_used, xs, wg, wu, wd):
    n_rows, words = xs.shape
    dm, de = wg.shape[1], wg.shape[2]
    grid_spec = pltpu.PrefetchScalarGridSpec(
        num_scalar_prefetch=2,
        grid=(n_rows // EXPERT_TILE,),
        in_specs=[pl.BlockSpec((EXPERT_TILE, words), lambda i, te, nu: (i, 0)),
                  pl.BlockSpec((1, dm, de), lambda i, te, nu: (te[i], 0, 0)),
                  pl.BlockSpec((1, dm, de), lambda i, te, nu: (te[i], 0, 0)),
                  pl.BlockSpec((1, de, dm), lambda i, te, nu: (te[i], 0, 0))],
        out_specs=pl.BlockSpec((EXPERT_TILE, words), lambda i, te, nu: (i, 0)),
    )
    return pl.pallas_call(
        _expert_kernel,
        grid_spec=grid_spec,
        out_shape=jax.ShapeDtypeStruct((n_rows, words), jnp.uint32),
        compiler_params=_cparams(("arbitrary",)),
        name="expert_mlp",
    )(tile_exp, n_used, xs, wg, wu, wd)


def _final_kernel(dest_ref, next_ref, w_ref, base_ref, g2_ref, lg_ref, lb_ref, ys_ref, o_ref, buf_ref, sem):
    i = pl.program_id(0)
    tm = base_ref.shape[1]
    slot = i % 2

    def gather(rows_ref, to):
        def issue(t8, carry):
            base = pl.multiple_of(t8 * SUBLANES, SUBLANES)
            for s in range(SUBLANES):
                for k in range(TOP_K):
                    _row_copy(ys_ref, rows_ref[0, k, base + s], buf_ref.at[to, k], base + s,
                              sem.at[to]).start(priority=k % 2)
            return carry

        lax.fori_loop(0, tm // SUBLANES, issue, 0)

    @pl.when(i == 0)
    def _():
        gather(dest_ref, slot)

    @pl.when(i + 1 < pl.num_programs(0))
    def _():
        gather(next_ref, 1 - slot)

    pltpu.make_async_copy(buf_ref.at[slot], buf_ref.at[slot], sem.at[slot]).wait()
    w = w_ref[...]
    lo_sum, hi_sum = jnp.zeros((tm, buf_ref.shape[3]), F32), jnp.zeros((tm, buf_ref.shape[3]), F32)
    for k in range(TOP_K):
        lo, hi = _unpack_rows(buf_ref[slot, k])
        lo_sum += w[:, k:k + 1] * lo
        hi_sum += w[:, k:k + 1] * hi
    routed = jnp.concatenate([lo_sum, hi_sum], axis=1)
    o_ref[0] = _norm_rows(base_ref[0] + g2_ref[0] * routed) * lg_ref[...] + lb_ref[...]


def _final(dest3, w_tok, base, g2, ln_g, ln_b, ys):
    bsz, seq, dm = base.shape
    tm = dest3.shape[2]
    nt = seq // tm
    n_tiles = bsz * nt
    tile = pl.BlockSpec((1, tm, dm), lambda i: (i // nt, i % nt, 0))
    rows_of = lambda f: pl.BlockSpec((1, TOP_K, tm), lambda i: (f(i), 0, 0), memory_space=pltpu.SMEM)
    return pl.pallas_call(
        _final_kernel,
        grid=(n_tiles,),
        in_specs=[rows_of(lambda i: i), rows_of(lambda i: jnp.minimum(i + 1, n_tiles - 1)),
                  pl.BlockSpec((tm, TOP_K), lambda i: (i, 0)),
                  tile, pl.BlockSpec((1, 1, dm), lambda i: (i // nt, 0, 0)),
                  pl.BlockSpec((1, dm), lambda i: (0, 0)), pl.BlockSpec((1, dm), lambda i: (0, 0)),
                  pl.BlockSpec(memory_space=pl.ANY)],
        out_specs=tile,
        out_shape=jax.ShapeDtypeStruct((bsz, seq, dm), F32),
        scratch_shapes=[pltpu.VMEM((2, TOP_K, tm, ys.shape[1]), jnp.uint32), pltpu.SemaphoreType.DMA((2,))],
        compiler_params=_cparams(("arbitrary",)),
        name="combine_final_layernorm",
    )(dest3, dest3, w_tok, base, g2, ln_g, ln_b, ys)


def _rel_bucket(dist):
    dist = np.maximum(dist, 0)
    exact = REL_BUCKETS // 2
    log_ratio = np.log(np.maximum(dist, 1).astype(np.float32) / np.float32(exact)) / np.float32(math.log(REL_MAX_DIST / exact))
    large = np.minimum(exact + (log_ratio * (REL_BUCKETS - exact)).astype(np.int32), REL_BUCKETS - 1)
    return np.where(dist < exact, dist, large)


def _bias_tiles(rel_bias):
    n_d = 4 * SEL_LEN
    pick = (_rel_bucket(np.arange(n_d))[:, None] == np.arange(REL_BUCKETS)[None, :]).astype(np.float32)
    vec = jnp.dot(jnp.asarray(pick), rel_bias.astype(F32), precision=lax.Precision.HIGHEST).T
    far = rel_bias[REL_BUCKETS - 1]
    vec = ((vec - far[:, None]) * LOG2E).reshape(N_KV, GQA, n_d)
    tok = np.arange(Q_BLOCK)[None, :]
    key = np.arange(SEL_LEN)[:, None]
    padded = jnp.pad(vec, ((0, 0), (0, 0), (Q_BLOCK, 0)))

    def by_distance(d):
        first = np.clip(d[:, 0], -Q_BLOCK, n_d - Q_BLOCK) + Q_BLOCK
        rows = [lax.slice_in_dim(padded, int(f), int(f) + Q_BLOCK, axis=2) for f in first]
        return jnp.stack(rows, axis=2)

    def toeplitz(d):
        vals = jnp.where(jnp.asarray(d >= 0), by_distance(d), NEG)
        return jnp.transpose(vals, (0, 2, 1, 3)).reshape(N_KV, d.shape[0], ROWS)

    near = [toeplitz(delta + tok - key) for delta in (0, SEL_LEN, 2 * SEL_LEN)]
    zero = jnp.zeros((N_KV, SEL_LEN, ROWS), F32)
    edge = np.where(tok < key, 0.0, NEG).astype(np.float32)
    edge = jnp.broadcast_to(jnp.asarray(np.tile(edge, (1, GQA)))[None], (N_KV, SEL_LEN, ROWS))
    nb = jnp.stack(near + [zero, jnp.full_like(zero, NEG), edge], axis=1)
    w = np.arange(BAND_ROWS)[:, None]
    bands = []
    for ph in (0, 1):
        d = tok - CMP_STRIDE * (w - PAD_CMP - (Q_BLOCK // CMP_STRIDE) * ph) - (CMP_LEN - 1)
        vals = jnp.where(jnp.asarray(d >= 0), by_distance(d), 0.0)
        bands.append(jnp.transpose(vals, (0, 2, 1, 3)).reshape(N_KV, BAND_ROWS, ROWS))
    cb = jnp.stack(bands, axis=1)
    return nb.astype(F32), cb.astype(F32)


def _padded_cmp_rows(seq):
    return -(-(seq // CMP_STRIDE + 2 * PAD_CMP) // LANES) * LANES


def _overlap_t(seq):
    n_cmp = (seq - CMP_LEN) // CMP_STRIDE + 1
    n_blk = seq // SEL_LEN
    ncp = _padded_cmp_rows(seq)
    c_start = np.arange(n_cmp) * CMP_STRIDE
    c_end = c_start + CMP_LEN - 1
    blk = np.arange(n_blk)
    ov = ((c_start[:, None] < (blk[None, :] + 1) * SEL_LEN) & (c_end[:, None] >= blk[None, :] * SEL_LEN))
    out = np.zeros((2 * SUBLANES + n_blk, ncp), np.float32)
    out[0] = 1.0
    out[2 * SUBLANES:, PAD_CMP:PAD_CMP + n_cmp] = ov.T
    return jnp.asarray(out, BF16)


def _cmp_visibility(seq):
    n_cmp = (seq - CMP_LEN) // CMP_STRIDE + 1
    ncp = _padded_cmp_rows(seq)
    rho = np.arange(ncp)[:, None]
    col = np.arange(COLS)[None, :]
    thr = CMP_STRIDE * (rho - PAD_CMP) + (CMP_LEN - 1) - Q_BLOCK * (col // ROWS) - col % Q_BLOCK
    valid = (rho >= PAD_CMP) & (rho < PAD_CMP + n_cmp)
    return jnp.asarray(np.where(valid, thr, np.iinfo(np.int32).max), jnp.int32)


def _s5_params(lam_re, lam_im, log_step, b_re, b_im, c_re, c_im):
    lr, li = lam_re.astype(F32), lam_im.astype(F32)
    dt = jnp.exp(log_step.astype(F32))[:, None]
    mag = jnp.exp(lr * dt)
    ar, ai = mag * jnp.cos(li * dt), mag * jnp.sin(li * dt)
    den = lr * lr + li * li
    kr = ((ar - 1.0) * lr + ai * li) / den
    ki = (ai * lr - (ar - 1.0) * li) / den
    br, bi = b_re.astype(F32), b_im.astype(F32)
    bbr = kr[..., None] * br - ki[..., None] * bi
    bbi = kr[..., None] * bi + ki[..., None] * br
    n_g = lr.shape[0]
    eye = jnp.eye(n_g, dtype=F32)

    def drive(bb):
        return jnp.einsum('gpc,gh->gchp', bb, eye).reshape(n_g * SSM_GROUP, n_g * SSM_STATE)

    def readout(c):
        return jnp.einsum('gcp,gh->gphc', c, eye).reshape(n_g * SSM_STATE, n_g * SSM_GROUP)

    d_re, d_im = drive(bbr), drive(bbi)
    r_re, r_im = readout(c_re.astype(F32)), -readout(c_im.astype(F32))
    c_in, c_st = d_re.shape[0] // S5_DIAG_BLOCKS, d_re.shape[1] // S5_DIAG_BLOCKS
    blk = lambda m, k, rows, cols: m[k * rows:(k + 1) * rows, k * cols:(k + 1) * cols]
    wb = jnp.stack([jnp.concatenate([blk(d_re, k, c_in, c_st), blk(d_im, k, c_in, c_st)], axis=1)
                    for k in range(S5_DIAG_BLOCKS)]).astype(BF16)
    wc = jnp.stack([jnp.concatenate([blk(r_re, k, c_st, c_in), blk(r_im, k, c_st, c_in)], axis=0)
                    for k in range(S5_DIAG_BLOCKS)]).astype(BF16)
    a = jnp.concatenate([ar.reshape(1, -1), ai.reshape(1, -1)], axis=1)
    return wb, jnp.broadcast_to(a, (SUBLANES, a.shape[1])), wc


def _layer(x, mod, w_in, lam_re, lam_im, log_step, b_re, b_im, c_re, c_im, d_skip, glu_w, cmp_pos, cmp_w1,
           cmp_w2, rel_bias, w_out, ln1_g, ln1_b, router_w, router_bias, e_gate, e_up, e_down, sg, su, sd,
           ln2_g, ln2_b, alpha):
    bsz, seq, dm = x.shape
    n_tok = bsz * seq
    sh1, sc1, g1, sh2, sc2, g2 = [m[:, None, :] for m in jnp.split(mod, 6, axis=-1)]
    ssm_w = dm // 2
    attn_w = N_HEADS * HEAD_DIM
    kv_w = N_KV * HEAD_DIM
    n_gate = 3 * N_HEADS
    offs = np.cumsum([0, ssm_w, attn_w] + [kv_w] * 6 + [n_gate, 2 * dm])

    wu = w_in[:, offs[0]:offs[1]].astype(BF16)
    wq = w_in[:, offs[1]:offs[2]] * (HEAD_DIM ** -0.5 * LOG2E)
    wa = jnp.concatenate([wq, w_in[:, offs[2]:offs[4]]], axis=1).astype(BF16)
    wg = jnp.pad(w_in[:, offs[8]:offs[9]], ((0, 0), (0, LANES - n_gate))).astype(BF16)
    wm = w_in[:, offs[9]:offs[10]].astype(BF16)
    nq = seq // Q_BLOCK
    n_blk = seq // SEL_LEN
    nch = seq // CMP_STRIDE

    aug_w = 2 * HEAD_DIM + -(-n_blk // LANES) * LANES
    u2d, act, gates, mg, kaug, kw, vst, vwt = _input_projection(
        x, sh1, sc1, wu, wa, wg, wm, w_in[:, offs[4]:offs[5]].astype(BF16), w_in[:, offs[6]:offs[7]].astype(BF16),
        w_in[:, offs[5]:offs[6]].T.astype(BF16), w_in[:, offs[7]:offs[8]].T.astype(BF16), aug_w)

    def piece(i):
        return act[:, :, attn_w + i * kv_w: attn_w + (i + 1) * kv_w]

    def compress(raw, pos, w1, w2):
        xc = raw.reshape(bsz, nch, CMP_STRIDE, N_KV, HEAD_DIM).transpose(0, 3, 1, 2, 4).reshape(bsz, N_KV, nch, CMP_STRIDE * HEAD_DIM)
        half = CMP_STRIDE * HEAD_DIM
        w1cat = jnp.concatenate([w1[:half], w1[half:]], axis=1).astype(BF16)
        posb = jnp.dot(pos.reshape(1, -1), w1, precision=lax.Precision.HIGHEST)
        return _compress(xc, w1cat, posb, w2.astype(BF16))

    kc = compress(piece(0), cmp_pos[0], cmp_w1[0], cmp_w2[0])
    vc = compress(piece(1), cmp_pos[1], cmp_w1[1], cmp_w2[1])
    pad = ((0, 0), (0, 0), (PAD_CMP, _padded_cmp_rows(seq) - nch - PAD_CMP), (0, 0))
    kcp = jnp.pad(kc, pad).astype(BF16)
    vcpt = jnp.swapaxes(jnp.pad(vc, pad), 2, 3).astype(BF16)
    ones_ovt = _overlap_t(seq)
    cmp_lhs = jnp.concatenate([vcpt, jnp.broadcast_to(ones_ovt, (bsz, N_KV) + ones_ovt.shape)], axis=2)

    nst = nq // NSA_TILES
    gt = gates[:, :, :n_gate].reshape(bsz, nst, NSA_TILES, Q_BLOCK, N_KV, GQA, 3)
    gt = gt.transpose(0, 4, 1, 6, 2, 5, 3).reshape(bsz, N_KV, nst, 3, COLS)
    gt = jnp.pad(gt, ((0, 0), (0, 0), (0, 0), (0, SUBLANES - 3), (0, 0)))
    nb, cb = _bias_tiles(rel_bias)
    y_nsa = _nsa(act, kaug, vst, kw, vwt, kcp, cmp_lhs, _cmp_visibility(seq), nb, cb, gt)

    wb, a, wc = _s5_params(lam_re, lam_im, log_step, b_re, b_im, c_re, c_im)
    z2 = _s5(u2d.reshape(seq * bsz, ssm_w), wb, a, wc, d_skip.reshape(1, ssm_w).astype(F32), bsz)
    z2d = z2.reshape(seq, bsz * ssm_w)

    base, h2 = _mix_out(z2d, y_nsa, mg, x, g1, sh2, sc2, g2, ln1_g.reshape(1, dm), ln1_b.reshape(1, dm),
                        glu_w.astype(BF16), w_out.astype(BF16), sg.astype(BF16), su.astype(BF16), sd.astype(BF16), alpha)
    h2 = h2.reshape(n_tok, dm // 2)

    n_exp = router_w.shape[1]
    rt = min(ROUTER_TILE, n_tok)
    tri = jnp.asarray(np.triu(np.ones((rt, rt), np.float32), 1), BF16)
    top_e, top_w, rank, counts = _router(h2, router_w.T.astype(BF16), router_bias.reshape(n_exp, 1).astype(F32), tri)
    counts = counts[:, 0].astype(jnp.int32)
    padded = (counts + EXPERT_TILE - 1) // EXPERT_TILE * EXPERT_TILE
    pad_end = jnp.cumsum(padded)
    pad_start = pad_end - padded
    dest = _dest_rows(pad_start.astype(jnp.int32), top_e, rank)
    n_rows = n_tok * TOP_K + n_exp * EXPERT_TILE
    n_tiles = n_rows // EXPERT_TILE
    tile_exp = jnp.sum(pad_end[None, :] <= (jnp.arange(n_tiles) * EXPERT_TILE)[:, None], axis=1)
    tile_exp = jnp.minimum(tile_exp, n_exp - 1).astype(jnp.int32)
    n_used = (pad_end[-1] // EXPERT_TILE).astype(jnp.int32).reshape(1)
    mt = min(MOVE_TILE, seq)
    dest3 = dest.reshape(TOP_K, n_tok // mt, mt).transpose(1, 0, 2)
    xs = _dispatch(pad_end.astype(jnp.int32), padded.astype(jnp.int32), n_used, dest3, h2, n_rows)
    ys = _experts(tile_exp, n_used, xs, e_gate, e_up, e_down)
    return _final(dest3, top_w.T, base, g2, ln2_g.reshape(1, dm), ln2_b.reshape(1, dm), ys)


def kernel(x, c, ada_w, ada_b, w_in, ssm_lambda_re, ssm_lambda_im, ssm_log_step, ssm_b_re, ssm_b_im, ssm_c_re, ssm_c_im, ssm_d, ssm_glu_w, cmp_pos, cmp_w1, cmp_w2, rel_bias, w_out, ln1_g, ln1_b, router_w, router_bias, exp_w_gate, exp_w_up, exp_w_down, sh_w_gate, sh_w_up, sh_w_down, ln2_g, ln2_b):
    depth = ada_w.shape[0]
    alpha = (2 * depth) ** 0.25
    for l in range(depth):
        mod = _modulation(c, ada_w[l], ada_b[l])
        x = _layer(x, mod, w_in[l], ssm_lambda_re[l], ssm_lambda_im[l], ssm_log_step[l], ssm_b_re[l], ssm_b_im[l],
                   ssm_c_re[l], ssm_c_im[l], ssm_d[l], ssm_glu_w[l], cmp_pos[l], cmp_w1[l], cmp_w2[l], rel_bias,
                   w_out[l], ln1_g[l], ln1_b[l], router_w[l], router_bias[l], exp_w_gate[l], exp_w_up[l],
                   exp_w_down[l], sh_w_gate[l], sh_w_up[l], sh_w_down[l], ln2_g[l], ln2_b[l], alpha)
    return x
```

```python
import functools
import math

import numpy as np
import jax
import jax.numpy as jnp
from jax import lax
from jax.experimental import pallas as pl
from jax.experimental.pallas import tpu as pltpu

F32 = jnp.float32
BF16 = jnp.bfloat16

SSM_GROUP = 16
SSM_STATE = 64
N_HEADS = 16
HEAD_DIM = 64
N_KV = 4
GQA = N_HEADS // N_KV
CMP_LEN = 32
CMP_STRIDE = 16
SEL_LEN = 64
SEL_TOP = 16
WINDOW = 512
Q_BLOCK = 64
FORCE_BONUS = 1.0e4
N_FORCED = 3
assert GQA < FORCE_BONUS
REL_BUCKETS = 32
REL_MAX_DIST = 128
TOP_K = 8
ROUTED_SCALE = 2.5
LN_EPS = 1e-5

LANES = 128
SUBLANES = 8
VMEM_LIMIT_BYTES = 56 * 1024 * 1024
ROW_TILE = 512
ROUTER_TILE = 256
EXPERT_TILE = 512
MOVE_TILE = 512
DEST_TILE = 2048
SCAN_CHUNK = 128
SCAN_LANES = 512
S5_DIAG_BLOCKS = 2
MASK_BIG = 32768.0
NEG = -1.0e30
LOG2E = math.log2(math.e)
WIN_BLOCKS = WINDOW // SEL_LEN
ROWS = GQA * Q_BLOCK
NSA_TILES = 4
COLS = NSA_TILES * ROWS
CHUNK_BLOCKS = 8
CHUNK_KEYS = CHUNK_BLOCKS * SEL_LEN
FAR_BLOCKS = NSA_TILES
FAR_KEYS = FAR_BLOCKS * SEL_LEN
PAD_CMP = 8
BAND_ROWS = 24


def _cparams(sem):
    return pltpu.CompilerParams(dimension_semantics=sem, vmem_limit_bytes=VMEM_LIMIT_BYTES)


def _pack_rows(x):
    n = x.shape[1] // 2
    xb = x.astype(jnp.bfloat16).astype(F32)
    lo = lax.shift_right_logical(lax.bitcast_convert_type(xb[:, :n], jnp.uint32), jnp.uint32(16))
    return lax.bitcast_convert_type(xb[:, n:], jnp.uint32) | lo


def _unpack_rows(w):
    lo = lax.bitcast_convert_type(lax.shift_left(w, jnp.uint32(16)), F32)
    hi = lax.bitcast_convert_type(w & jnp.uint32(0xFFFF0000), F32)
    return lo, hi


def _norm_rows(x):
    mu = jnp.mean(x, axis=-1, keepdims=True)
    xc = x - mu
    var = jnp.mean(xc * xc, axis=-1, keepdims=True)
    return xc * lax.rsqrt(var + LN_EPS)


def _mod_kernel(c_ref, w_ref, b_ref, o_ref):
    cond = jax.nn.silu(c_ref[...])
    o_ref[...] = jnp.dot(cond.astype(BF16), w_ref[...].astype(BF16), preferred_element_type=F32) + b_ref[...]


def _modulation(c, ada_w, ada_b):
    bsz, dm = c.shape
    n = ada_w.shape[1]
    tn = dm
    return pl.pallas_call(
        _mod_kernel,
        grid=(n // tn,),
        in_specs=[pl.BlockSpec((bsz, dm), lambda j: (0, 0)),
                  pl.BlockSpec((dm, tn), lambda j: (0, j)),
                  pl.BlockSpec((1, tn), lambda j: (0, j))],
        out_specs=pl.BlockSpec((bsz, tn), lambda j: (0, j)),
        out_shape=jax.ShapeDtypeStruct((bsz, n), F32),
        compiler_params=_cparams(("arbitrary",)),
        name="ada_modulation",
    )(c, ada_w, ada_b.reshape(1, n))


def _inproj_kernel(x_ref, sh_ref, sc_ref, wu_ref, wa_ref, wg_ref, wm_ref, wks_ref, wkw_ref, wvs_ref, wvw_ref,
                   u_ref, a_ref, g_ref, m_ref, ks_ref, kw_ref, vs_ref, vw_ref, *, aug_w):
    h = _norm_rows(x_ref[0]) * (1.0 + sc_ref[0]) + sh_ref[0]
    hb = h.astype(BF16)
    tm = hb.shape[0]
    u_ref[...] = jnp.dot(hb, wu_ref[...], preferred_element_type=F32)
    a_ref[0] = jnp.dot(hb, wa_ref[...], preferred_element_type=F32).astype(BF16)
    g_ref[0] = jax.nn.sigmoid(jnp.dot(hb, wg_ref[...], preferred_element_type=F32))
    m_ref[0] = jax.nn.sigmoid(jnp.dot(hb, wm_ref[...], preferred_element_type=F32)).astype(BF16)
    low = lax.broadcasted_iota(jnp.int32, (tm, LANES), 1) < HEAD_DIM

    def spread_heads(k4):
        cols = [k4[:, (g // 2) * LANES:(g // 2 + 1) * LANES] for g in range(N_KV)]
        return [jnp.where(low, pltpu.roll(c, HEAD_DIM, 1) if g % 2 else c, 0.0) for g, c in enumerate(cols)]

    oh_shape = (tm, aug_w - 2 * HEAD_DIM)
    blk = jnp.right_shift(pl.program_id(1) * tm + lax.broadcasted_iota(jnp.int32, oh_shape, 0), SEL_LEN.bit_length() - 1)
    onehot = jnp.where(lax.broadcasted_iota(jnp.int32, oh_shape, 1) == blk, -MASK_BIG, 0.0)
    ks = spread_heads(jnp.dot(hb, wks_ref[...], preferred_element_type=F32))
    ks_ref[0] = jnp.concatenate([part for k in ks for part in (k, onehot)], axis=1).astype(BF16)
    kw_ref[0] = jnp.concatenate(spread_heads(jnp.dot(hb, wkw_ref[...], preferred_element_type=F32)), axis=1).astype(BF16)
    rows = lax.broadcasted_iota(jnp.int32, (2 * SUBLANES, tm), 0)
    ones_rows = jnp.where(rows == 0, 1.0, 0.0).astype(BF16)
    for w_ref, v_ref in ((wvs_ref, vs_ref), (wvw_ref, vw_ref)):
        vt = lax.dot_general(w_ref[...], hb, (((1,), (1,)), ((), ())), preferred_element_type=F32).astype(BF16)
        for g in range(N_KV):
            v_ref[0, g] = jnp.concatenate([vt[g * HEAD_DIM:(g + 1) * HEAD_DIM], ones_rows], axis=0)


def _input_projection(x, sh1, sc1, wu, wa, wg, wm, wks, wkw, wvs_t, wvw_t, aug_w):
    bsz, seq, dm = x.shape
    tm = min(ROW_TILE, seq)
    nu, na, ng, nm = wu.shape[1], wa.shape[1], wg.shape[1], wm.shape[1]
    assert 2 * HEAD_DIM == LANES
    n_ks, n_kw = N_KV * aug_w, N_KV * 2 * HEAD_DIM
    full = lambda w: pl.BlockSpec(w.shape, lambda b, i: (0, 0))
    vec = pl.BlockSpec((1, 1, dm), lambda b, i: (b, 0, 0))
    rows_out = lambda n: pl.BlockSpec((1, tm, n), lambda b, i: (b, i, 0))
    vt_rows = HEAD_DIM + 2 * SUBLANES
    vt_out = pl.BlockSpec((1, N_KV, vt_rows, tm), lambda b, i: (b, 0, 0, i))
    return pl.pallas_call(
        functools.partial(_inproj_kernel, aug_w=aug_w),
        grid=(bsz, seq // tm),
        in_specs=[pl.BlockSpec((1, tm, dm), lambda b, i: (b, i, 0)), vec, vec,
                  full(wu), full(wa), full(wg), full(wm), full(wks), full(wkw), full(wvs_t), full(wvw_t)],
        out_specs=[pl.BlockSpec((tm, nu), lambda b, i: (i, b)), rows_out(na), rows_out(ng), rows_out(nm),
                   rows_out(n_ks), rows_out(n_kw), vt_out, vt_out],
        out_shape=[jax.ShapeDtypeStruct((seq, bsz * nu), F32),
                   jax.ShapeDtypeStruct((bsz, seq, na), BF16),
                   jax.ShapeDtypeStruct((bsz, seq, ng), F32),
                   jax.ShapeDtypeStruct((bsz, seq, nm), BF16),
                   jax.ShapeDtypeStruct((bsz, seq, n_ks), BF16),
                   jax.ShapeDtypeStruct((bsz, seq, n_kw), BF16),
                   jax.ShapeDtypeStruct((bsz, N_KV, vt_rows, seq), BF16),
                   jax.ShapeDtypeStruct((bsz, N_KV, vt_rows, seq), BF16)],
        compiler_params=_cparams(("arbitrary", "arbitrary")),
        name="adaln_input_projection",
    )(x, sh1, sc1, wu, wa, wg, wm, wks, wkw, wvs_t, wvw_t)


def _compress_kernel(x_ref, w1_ref, pb_ref, w2_ref, o_ref, *, n_cmp):
    hid = w2_ref.shape[0]
    p = jnp.dot(x_ref[0, 0], w1_ref[...], preferred_element_type=F32)
    nrow = p.shape[0]
    nxt = pltpu.roll(p[:, hid:], nrow - 1, 0)
    hidv = jax.nn.gelu(p[:, :hid] + nxt + pb_ref[...])
    out = jnp.dot(hidv.astype(BF16), w2_ref[...], preferred_element_type=F32)
    rows = lax.broadcasted_iota(jnp.int32, out.shape, 0)
    o_ref[0, 0] = jnp.where(rows < n_cmp, out, 0.0)


def _compress(xc, w1cat, posb, w2):
    bsz, nkv, nch, kdim = xc.shape
    hid2 = w1cat.shape[1]
    return pl.pallas_call(
        functools.partial(_compress_kernel, n_cmp=nch - 1),
        grid=(bsz, nkv),
        in_specs=[pl.BlockSpec((1, 1, nch, kdim), lambda b, g: (b, g, 0, 0)),
                  pl.BlockSpec((kdim, hid2), lambda b, g: (0, 0)),
                  pl.BlockSpec((1, hid2 // 2), lambda b, g: (0, 0)),
                  pl.BlockSpec((hid2 // 2, HEAD_DIM), lambda b, g: (0, 0))],
        out_specs=pl.BlockSpec((1, 1, nch, HEAD_DIM), lambda b, g: (b, g, 0, 0)),
        out_shape=jax.ShapeDtypeStruct((bsz, nkv, nch, HEAD_DIM), F32),
        compiler_params=_cparams(("arbitrary", "arbitrary")),
        name="kv_compress",
    )(xc, w1cat, posb, w2)


def _swap_heads_tokens(a):
    t = jnp.concatenate([a, jnp.zeros_like(a)], axis=0).T
    p = [t[r * HEAD_DIM:(r + 1) * HEAD_DIM] for r in range(GQA)]
    return jnp.concatenate([p[r] + pltpu.roll(p[r + 1], Q_BLOCK, 1) for r in range(0, GQA, 2)], axis=1)


def _nsa_kernel(q_ref, kaug_ref, vst_ref, kw_ref, vwt_ref, kcp_ref, cl_ref, vis_ref, nb_ref, cb_ref,
                g_ref, o_ref, s_ref, c_ref, w_ref, fa_ref, fb_ref, *, n_blk, n_sel):
    q0 = pl.program_id(2) * NSA_TILES
    qf = q_ref[0].astype(F32)
    qt = jnp.concatenate([_swap_heads_tokens(qf[n * Q_BLOCK:(n + 1) * Q_BLOCK]) for n in range(NSA_TILES)],
                         axis=1).astype(BF16)
    col_tile = lambda n: slice(n * ROWS, (n + 1) * ROWS)

    s_ref[...] = jnp.dot(kcp_ref[0, 0], qt, preferred_element_type=F32)
    for n in range(NSA_TILES):
        band = pl.multiple_of(SUBLANES * ((q0 + n) // 2), SUBLANES)
        s_ref[pl.ds(band, BAND_ROWS), col_tile(n)] += cb_ref[0, n % 2]
    s = jnp.where(vis_ref[...] <= Q_BLOCK * q0, s_ref[...], -jnp.inf)
    mx = jnp.max(s, axis=0, keepdims=True)
    mx = jnp.where(mx == -jnp.inf, 0.0, mx)
    e = jnp.exp2((s - mx).astype(BF16))
    both = jnp.dot(cl_ref[0, 0], e, preferred_element_type=F32)
    inv = 1.0 / jnp.maximum(both[HEAD_DIM:HEAD_DIM + 1], 1e-30)
    o_cmp = both[:HEAD_DIM] * inv

    imp4 = both[HEAD_DIM + 2 * SUBLANES:] * inv
    sums = []
    for n in range(NSA_TILES):
        two = imp4[:, n * ROWS:n * ROWS + LANES] + imp4[:, n * ROWS + LANES:(n + 1) * ROWS]
        sums.append(two + pltpu.roll(two, Q_BLOCK, 1))
    low = lax.broadcasted_iota(jnp.int32, sums[0].shape, 1) < Q_BLOCK
    imp = jnp.concatenate([jnp.where(low, sums[n], sums[n + 1]) for n in range(0, NSA_TILES, 2)], axis=1)
    blk = lax.broadcasted_iota(jnp.int32, imp.shape, 0)
    cur = q0 + jnp.right_shift(lax.broadcasted_iota(jnp.int32, imp.shape, 1), Q_BLOCK.bit_length() - 1)
    blkf = blk.astype(F32)
    forced = (blk == 0) | (blk == cur) | (blk == cur - 1)
    score = jnp.where((blk <= cur) & ~forced, imp, -jnp.inf)
    notsel = jnp.where(forced, 0.0, 1.0)
    for _ in range(n_sel - N_FORCED):
        best = jnp.max(score, axis=0, keepdims=True)
        first = jnp.min(jnp.where(score == best, blkf, float(n_blk)), axis=0, keepdims=True)
        pick = blkf == first
        notsel = jnp.where(pick, 0.0, notsel)
        score = jnp.where(pick, -jnp.inf, score)
    halves = []
    for n in range(0, NSA_TILES, 2):
        pair = notsel[:, (n // 2) * LANES:(n // 2 + 1) * LANES]
        swapped = pltpu.roll(pair, Q_BLOCK, 1)
        halves += [jnp.where(low, pair, swapped), jnp.where(low, swapped, pair)]
    notsel = jnp.concatenate([h for h in halves for _ in (0, 1)], axis=1).astype(BF16)
    qwin = jnp.concatenate([qt, jnp.zeros_like(qt)], axis=0)
    tail = kaug_ref.shape[2] - 2 * HEAD_DIM - n_blk
    qaug = jnp.concatenate([qwin, notsel] + ([jnp.zeros((tail, COLS), notsel.dtype)] if tail else []), axis=0)

    def update(carry, sc, vt, top=None):
        m, acc = carry
        m_new = jnp.maximum(m, jnp.max(sc, axis=0, keepdims=True) if top is None else top)
        p = jnp.exp2((sc - m_new).astype(BF16))
        return m_new, jnp.exp2(m - m_new) * acc + jnp.dot(vt, p, preferred_element_type=F32)

    def chunk_keys(c):
        return pl.ds(pl.multiple_of(c * FAR_KEYS, FAR_KEYS), FAR_KEYS)

    def far_scores(c, buf_ref):
        sc = jnp.dot(kaug_ref[0, chunk_keys(c), :], qaug, preferred_element_type=F32)
        buf_ref[...] = sc
        return jnp.max(sc, axis=0, keepdims=True)

    def far_pair(i, carry):
        m, acc, top_a = carry
        top_b = far_scores(2 * i + 1, fb_ref)
        m, acc = update((m, acc), fa_ref[...], vst_ref[0, 0, :, chunk_keys(2 * i)], top_a)
        top_a = far_scores(2 * i + 2, fa_ref)
        m, acc = update((m, acc), fb_ref[...], vst_ref[0, 0, :, chunk_keys(2 * i + 1)], top_b)
        return m, acc, top_a

    def far_last(_, carry, n_far):
        m, acc, top_a = carry
        return update((m, acc), fa_ref[...], vst_ref[0, 0, :, chunk_keys(n_far - 1)], top_a) + (top_a,)

    def near_chunk(k_ref, vt_ref, qmat, blk0, tile_of, carry, buf_ref):
        n_keys = buf_ref.shape[0]
        keys = pl.ds(pl.multiple_of(blk0 * SEL_LEN, NSA_TILES * SEL_LEN), n_keys)
        buf_ref[...] = jnp.dot(k_ref[0, keys, :], qmat, preferred_element_type=F32)
        for o in range(n_keys // SEL_LEN):
            for n in range(NSA_TILES):
                buf_ref[o * SEL_LEN:(o + 1) * SEL_LEN, col_tile(n)] += nb_ref[0, tile_of(blk0 + o, q0 + n)]
        return update(carry, buf_ref[...], vt_ref[0, 0, :, keys])

    def sel_tile(j, qi):
        d = qi - j
        return jnp.where(d < 0, 4, jnp.minimum(d, 3))

    def win_tile(j, qi):
        d = qi - j
        return jnp.where((d < 0) | (d > WIN_BLOCKS), 4, jnp.where(d == WIN_BLOCKS, 5, jnp.minimum(d, 3)))

    init = (jnp.full((1, COLS), NEG, F32), jnp.zeros((vst_ref.shape[2], COLS), F32))
    _, acc = near_chunk(kw_ref, vwt_ref, qwin, jnp.maximum(q0 - WIN_BLOCKS, 0), win_tile, init, w_ref)
    o_win = acc[:HEAD_DIM] / acc[HEAD_DIM:HEAD_DIM + 1]
    n_far = jnp.maximum(q0 - 2, 0) // FAR_BLOCKS
    carry = lax.fori_loop(0, n_far // 2, far_pair, init + (far_scores(0, fa_ref),))
    carry = lax.fori_loop(0, n_far % 2, functools.partial(far_last, n_far=n_far), carry)
    _, acc = near_chunk(kaug_ref, vst_ref, qaug, n_far * FAR_BLOCKS, sel_tile, carry[:2], c_ref)
    o_slc = acc[:HEAD_DIM] / acc[HEAD_DIM:HEAD_DIM + 1]
    g = g_ref[0, 0, 0]
    out = g[0:1] * o_cmp + g[1:2] * o_slc + g[2:3] * o_win
    o_ref[0] = jnp.concatenate([_swap_heads_tokens(out[:, col_tile(n)]) for n in range(NSA_TILES)],
                               axis=0).astype(BF16)


def _nsa(act, kaug, vst, kw, vwt, kcp, cmp_lhs, vis, nb, cb, gt):
    bsz, nkv, nsteps, _, cols = gt.shape
    dh = HEAD_DIM
    seq = kw.shape[1]
    step_tokens = NSA_TILES * Q_BLOCK
    q_blk = pl.BlockSpec((1, step_tokens, GQA * dh), lambda b, g, i: (b, i, g))
    n_blk = seq // SEL_LEN
    ncp = kcp.shape[2]
    assert n_blk % CHUNK_BLOCKS == 0 and n_blk >= WIN_BLOCKS + NSA_TILES and CHUNK_BLOCKS == FAR_BLOCKS + NSA_TILES
    per_bg = lambda shape: pl.BlockSpec((1, 1) + shape, lambda b, g, i: (b, g, 0, 0))
    per_step = lambda shape: pl.BlockSpec((1, 1, 1) + shape, lambda b, g, i: (b, g, i, 0, 0))
    head_cols = lambda a: pl.BlockSpec((1, seq, a.shape[2] // nkv), lambda b, g, i: (b, 0, g))
    kern = functools.partial(_nsa_kernel, n_blk=n_blk, n_sel=min(SEL_TOP, n_blk))
    return pl.pallas_call(
        kern,
        grid=(bsz, nkv, nsteps),
        in_specs=[q_blk,
                  head_cols(kaug), per_bg((vst.shape[2], seq)), head_cols(kw), per_bg((vwt.shape[2], seq)),
                  per_bg((ncp, dh)), per_bg((cmp_lhs.shape[2], ncp)),
                  pl.BlockSpec(vis.shape, lambda b, g, i: (0, 0)),
                  pl.BlockSpec((1,) + nb.shape[1:], lambda b, g, i: (g, 0, 0, 0)),
                  pl.BlockSpec((1,) + cb.shape[1:], lambda b, g, i: (g, 0, 0, 0)),
                  per_step((SUBLANES, cols))],
        out_specs=q_blk,
        out_shape=jax.ShapeDtypeStruct((bsz, seq, nkv * GQA * dh), BF16),
        scratch_shapes=[pltpu.VMEM((ncp, cols), F32), pltpu.VMEM((CHUNK_KEYS, cols), F32),
                        pltpu.VMEM(((WIN_BLOCKS + NSA_TILES) * SEL_LEN, cols), F32),
                        pltpu.VMEM((FAR_KEYS, cols), F32), pltpu.VMEM((FAR_KEYS, cols), F32)],
        compiler_params=_cparams(("arbitrary", "arbitrary", "arbitrary")),
        name="nsa_attention",
    )(act, kaug, vst, kw, vwt, kcp, cmp_lhs, vis, nb, cb, gt)


def _s5_kernel(u_ref, wb_ref, a_ref, wc_ref, d_ref, z_ref, xs_ref, st_ref, *, bsz, n_state):
    @pl.when(pl.program_id(0) == 0)
    def _():
        st_ref[...] = jnp.zeros_like(st_ref)

    u = u_ref[...]
    ub = u.astype(BF16)
    n_diag = wb_ref.shape[0]
    c_in, c_st = u.shape[1] // n_diag, n_state // n_diag
    for k in range(n_diag):
        drv = jnp.dot(ub[:, k * c_in:(k + 1) * c_in], wb_ref[k], preferred_element_type=F32)
        xs_ref[:, k * c_st:(k + 1) * c_st] = drv[:, :c_st]
        xs_ref[:, n_state + k * c_st:n_state + (k + 1) * c_st] = drv[:, c_st:]
    steps = u.shape[0] // bsz
    for c0 in range(0, n_state, SCAN_LANES):
        re = pl.ds(c0, SCAN_LANES)
        im = pl.ds(n_state + c0, SCAN_LANES)
        ar = jnp.broadcast_to(a_ref[0:1, re], (bsz, SCAN_LANES))
        ai = jnp.broadcast_to(a_ref[0:1, im], (bsz, SCAN_LANES))

        def step(t, carry):
            xr, xi = carry
            rows = pl.ds(pl.multiple_of(t * bsz, bsz), bsz)
            nr = ar * xr - ai * xi + xs_ref[rows, re]
            ni = ar * xi + ai * xr + xs_ref[rows, im]
            xs_ref[rows, re] = nr
            xs_ref[rows, im] = ni
            return nr, ni

        xr, xi = lax.fori_loop(0, steps, step, (st_ref[:, re], st_ref[:, im]), unroll=8)
        st_ref[:, re] = xr
        st_ref[:, im] = xi
    ys = []
    for k in range(n_diag):
        st = jnp.concatenate([xs_ref[:, k * c_st:(k + 1) * c_st],
                              xs_ref[:, n_state + k * c_st:n_state + (k + 1) * c_st]], axis=1)
        ys.append(jnp.dot(st.astype(BF16), wc_ref[k], preferred_element_type=F32))
    y = jnp.concatenate(ys, axis=1) + d_ref[...] * u
    z_ref[...] = jax.nn.gelu(y).astype(BF16)


def _s5(u2, wb, a, wc, dsk, bsz):
    rows, width = u2.shape
    seq = rows // bsz
    chunk = min(SCAN_CHUNK, seq)
    n_state2 = a.shape[1]
    return pl.pallas_call(
        functools.partial(_s5_kernel, bsz=bsz, n_state=n_state2 // 2),
        grid=(seq // chunk,),
        in_specs=[pl.BlockSpec((chunk * bsz, width), lambda i: (i, 0)),
                  pl.BlockSpec(wb.shape, lambda i: (0, 0, 0)),
                  pl.BlockSpec(a.shape, lambda i: (0, 0)),
                  pl.BlockSpec(wc.shape, lambda i: (0, 0, 0)),
                  pl.BlockSpec(dsk.shape, lambda i: (0, 0))],
        out_specs=pl.BlockSpec((chunk * bsz, width), lambda i: (i, 0)),
        out_shape=jax.ShapeDtypeStruct((rows, width), BF16),
        scratch_shapes=[pltpu.VMEM((chunk * bsz, n_state2), F32), pltpu.VMEM((bsz, n_state2), F32)],
        compiler_params=_cparams(("arbitrary",)),
        name="s5_scan",
    )(u2, wb, a, wc, dsk)


def _mixout_kernel(z_ref, yn_ref, mg_ref, x_ref, g1_ref, sh2_ref, sc2_ref, g2_ref, lg_ref, lb_ref,
                   glu_ref, wo_ref, sg_ref, su_ref, sd_ref, base_ref, h_ref, *, alpha):
    dm = x_ref.shape[2]
    glu = jnp.dot(z_ref[...], glu_ref[...], preferred_element_type=F32)
    y_ssm = glu[:, :dm] * jax.nn.sigmoid(glu[:, dm:])
    mg = mg_ref[0].astype(F32)
    merged = mg[:, :dm] * y_ssm + mg[:, dm:] * yn_ref[0].astype(F32)
    y = jnp.dot(merged.astype(BF16), wo_ref[...], preferred_element_type=F32)
    x1 = _norm_rows(alpha * x_ref[0] + g1_ref[0] * y) * lg_ref[...] + lb_ref[...]
    hf = _norm_rows(x1) * (1.0 + sc2_ref[0]) + sh2_ref[0]
    h_ref[0] = _pack_rows(hf)
    h = hf.astype(BF16)
    hs =jax.nn.silu(jnp.dot(h, sg_ref[...], preferred_element_type=F32)) * jnp.dot(h, su_ref[...], preferred_element_type=F32)
    shared = jnp.dot(hs.astype(BF16), sd_ref[...], preferred_element_type=F32)
    base_ref[0] = alpha * x1 + g2_ref[0] * shared


def _mix_out(z2d, y_nsa, mg, x, g1, sh2, sc2, g2, ln_g, ln_b, glu_w, w_out, sg, su, sd, alpha):
    bsz, seq, dm = x.shape
    tm = min(ROW_TILE, seq)
    width = z2d.shape[1] // bsz
    vec = pl.BlockSpec((1, 1, dm), lambda b, i: (b, 0, 0))
    row = pl.BlockSpec((1, dm), lambda b, i: (0, 0))
    full = lambda w: pl.BlockSpec(w.shape, lambda b, i: (0, 0))
    tile = lambda n: pl.BlockSpec((1, tm, n), lambda b, i: (b, i, 0))
    return pl.pallas_call(
        functools.partial(_mixout_kernel, alpha=alpha),
        grid=(bsz, seq // tm),
        in_specs=[pl.BlockSpec((tm, width), lambda b, i: (i, b)), tile(dm), tile(2 * dm), tile(dm),
                  vec, vec, vec, vec, row, row, full(glu_w), full(w_out), full(sg), full(su), full(sd)],
        out_specs=[tile(dm), tile(dm // 2)],
        out_shape=[jax.ShapeDtypeStruct((bsz, seq, dm), F32), jax.ShapeDtypeStruct((bsz, seq, dm // 2), jnp.uint32)],
        compiler_params=_cparams(("arbitrary", "arbitrary")),
        name="merge_outproj_ln_shared",
    )(z2d, y_nsa, mg, x, g1, sh2, sc2, g2, ln_g, ln_b, glu_w, w_out, sg, su, sd)


def _router_kernel(h_ref, rwt_ref, rb_ref, tri_ref, e_ref, w_ref, p_ref, cnt_ref):
    @pl.when(pl.program_id(0) == 0)
    def _():
        cnt_ref[...] = jnp.zeros_like(cnt_ref)

    h = jnp.concatenate(_unpack_rows(h_ref[...]), axis=1).astype(BF16)
    logits = lax.dot_general(rwt_ref[...], h, (((1,), (1,)), ((), ())), preferred_element_type=F32)
    scores = jax.nn.sigmoid(logits)
    cur = scores + rb_ref[...]
    n_exp = scores.shape[0]
    eid = lax.broadcasted_iota(jnp.int32, scores.shape, 0).astype(F32)
    chosen = jnp.zeros(scores.shape, F32)
    ids, vals = [], []
    for _ in range(TOP_K):
        best = jnp.max(cur, axis=0, keepdims=True)
        first = jnp.min(jnp.where(cur == best, eid, float(n_exp)), axis=0, keepdims=True)
        pick = eid == first
        ids.append(first)
        vals.append(jnp.sum(jnp.where(pick, scores, 0.0), axis=0, keepdims=True))
        chosen = jnp.where(pick, 1.0, chosen)
        cur = jnp.where(pick, -jnp.inf, cur)
    top_s = jnp.concatenate(vals, axis=0)
    w_ref[...] = top_s / jnp.sum(top_s, axis=0, keepdims=True) * ROUTED_SCALE
    top_e = jnp.concatenate(ids, axis=0)
    e_ref[...] = top_e.astype(jnp.int32)
    before = jnp.dot(chosen.astype(BF16), tri_ref[...], preferred_element_type=F32) + cnt_ref[...]
    ranks = [jnp.sum(jnp.where(eid == ids[k], before, 0.0), axis=0, keepdims=True) for k in range(TOP_K)]
    p_ref[...] = jnp.concatenate(ranks, axis=0).astype(jnp.int32)
    cnt_ref[...] += jnp.sum(chosen, axis=1, keepdims=True)


def _router(h2, rwt, rb, tri):
    n_tok, words = h2.shape
    n_exp, dm = rwt.shape
    tm = tri.shape[0]
    kt = pl.BlockSpec((TOP_K, tm), lambda i: (0, i))
    return pl.pallas_call(
        _router_kernel,
        grid=(n_tok // tm,),
        in_specs=[pl.BlockSpec((tm, words), lambda i: (i, 0)),
                  pl.BlockSpec((n_exp, dm), lambda i: (0, 0)),
                  pl.BlockSpec((n_exp, 1), lambda i: (0, 0)),
                  pl.BlockSpec((tm, tm), lambda i: (0, 0))],
        out_specs=[kt, kt, kt, pl.BlockSpec((n_exp, 1), lambda i: (0, 0))],
        out_shape=[jax.ShapeDtypeStruct((TOP_K, n_tok), jnp.int32),
                   jax.ShapeDtypeStruct((TOP_K, n_tok), F32),
                   jax.ShapeDtypeStruct((TOP_K, n_tok), jnp.int32),
                   jax.ShapeDtypeStruct((n_exp, 1), F32)],
        compiler_params=_cparams(("arbitrary",)),
        name="router_topk_rank",
    )(h2, rwt, rb, tri)


def _dest_kernel(start_ref, e_ref, r_ref, o_ref):
    e = e_ref[...]
    start = lax.fori_loop(0, start_ref.shape[0], lambda j, acc: jnp.where(e == j, start_ref[j], acc),
                          jnp.zeros(e.shape, jnp.int32))
    o_ref[...] = start + r_ref[...]


def _dest_rows(pad_start, top_e, rank):
    n_tok = top_e.shape[1]
    tm = min(DEST_TILE, n_tok)
    blk = pl.BlockSpec((TOP_K, tm), lambda i, ps: (0, i))
    return pl.pallas_call(
        _dest_kernel,
        grid_spec=pltpu.PrefetchScalarGridSpec(num_scalar_prefetch=1, grid=(n_tok // tm,), in_specs=[blk, blk], out_specs=blk),
        out_shape=jax.ShapeDtypeStruct(top_e.shape, jnp.int32),
        compiler_params=_cparams(("arbitrary",)),
        name="moe_dest_rows",
    )(pad_start, top_e, rank)


def _row_copy(src_ref, src_row, dst_ref, dst_row, sem):
    return pltpu.make_async_copy(src_ref.at[pl.ds(src_row, 1), :], dst_ref.at[pl.ds(dst_row, 1), :], sem)


def _dispatch_kernel(pend_ref, padded_ref, nused_ref, dest_ref, h_ref, xs_ref, zero_ref, sem, zsem):
    tm = h_ref.shape[0]

    @pl.when(pl.program_id(0) == 0)
    def _():
        zero_ref[...] = jnp.zeros_like(zero_ref)
        n_tiles = xs_ref.shape[0] // EXPERT_TILE

        def zero_tile(row):
            return pltpu.make_async_copy(zero_ref, xs_ref.at[pl.ds(pl.multiple_of(row, EXPERT_TILE), EXPERT_TILE), :], zsem)

        def per_expert(act):
            def body(e, carry):
                @pl.when(padded_ref[e] > 0)
                def _():
                    act(zero_tile(pend_ref[e] - EXPERT_TILE))
                return carry
            lax.fori_loop(0, pend_ref.shape[0], body, 0)

        def per_unused(act):
            def body(i, carry):
                act(zero_tile(i * EXPERT_TILE))
                return carry
            lax.fori_loop(nused_ref[0], n_tiles, body, 0)

        for loop in (per_expert, per_unused):
            loop(lambda copy: copy.start())
        for loop in (per_expert, per_unused):
            loop(lambda copy: copy.wait())

    def issue(t8, carry):
        base = pl.multiple_of(t8 * SUBLANES, SUBLANES)
        for s in range(SUBLANES):
            for k in range(TOP_K):
                _row_copy(h_ref, base + s, xs_ref, dest_ref[0, k, base + s], sem).start(priority=k % 2)
        return carry

    lax.fori_loop(0, tm // SUBLANES, issue, 0)
    pltpu.make_async_copy(xs_ref.at[pl.ds(0, TOP_K * tm), :], xs_ref.at[pl.ds(0, TOP_K * tm), :], sem).wait()


def _dispatch(pad_end, padded, n_used, dest3, h2, n_rows):
    n_tok, words = h2.shape
    tm = dest3.shape[2]
    grid_spec = pltpu.PrefetchScalarGridSpec(
        num_scalar_prefetch=3,
        grid=(n_tok // tm,),
        in_specs=[pl.BlockSpec((1, TOP_K, tm), lambda i, *_: (i, 0, 0), memory_space=pltpu.SMEM),
                  pl.BlockSpec((tm, words), lambda i, *_: (i, 0))],
        out_specs=pl.BlockSpec(memory_space=pl.ANY),
        scratch_shapes=[pltpu.VMEM((EXPERT_TILE, words), jnp.uint32), pltpu.SemaphoreType.DMA(()),
                        pltpu.SemaphoreType.DMA(())],
    )
    return pl.pallas_call(
        _dispatch_kernel,
        grid_spec=grid_spec,
        out_shape=jax.ShapeDtypeStruct((n_rows, words), jnp.uint32),
        compiler_params=_cparams(("arbitrary",)),
        name="moe_dispatch",
    )(pad_end, padded, n_used, dest3, h2)


def _expert_kernel(te_ref, nu_ref, x_ref, wg_ref, wu_ref, wd_ref, y_ref):
    i = pl.program_id(0)

    @pl.when(i < nu_ref[0])
    def _():
        lo, hi = _unpack_rows(x_ref[...])
        lo, hi = lo.astype(BF16), hi.astype(BF16)
        half = lo.shape[1]

        def proj(w_ref):
            return (jnp.dot(lo, w_ref[0, :half, :].astype(BF16), preferred_element_type=F32)
                    + jnp.dot(hi, w_ref[0, half:, :].astype(BF16), preferred_element_type=F32))

        hmid = jax.nn.silu(proj(wg_ref)) * proj(wu_ref)
        y_ref[...] = _pack_rows(jnp.dot(hmid.astype(BF16), wd_ref[0].astype(BF16), preferred_element_type=F32))

    @pl.when(i >= nu_ref[0])
    def _():
        y_ref[...] = jnp.zeros_like(y_ref)


def _experts(tile_exp, n_used, xs, wg, wu, wd):
    n_rows, words = xs.shape
    dm, de = wg.shape[1], wg.shape[2]
    grid_spec = pltpu.PrefetchScalarGridSpec(
        num_scalar_prefetch=2,
        grid=(n_rows // EXPERT_TILE,),
        in_specs=[pl.BlockSpec((EXPERT_TILE, words), lambda i, te, nu: (i, 0)),
                  pl.BlockSpec((1, dm, de), lambda i, te, nu: (te[i], 0, 0)),
                  pl.BlockSpec((1, dm, de), lambda i, te, nu: (te[i], 0, 0)),
                  pl.BlockSpec((1, de, dm), lambda i, te, nu: (te[i], 0, 0))],
        out_specs=pl.BlockSpec((EXPERT_TILE, words), lambda i, te, nu: (i, 0)),
    )
    return pl.pallas_call(
        _expert_kernel,
        grid_spec=grid_spec,
        out_shape=jax.ShapeDtypeStruct((n_rows, words), jnp.uint32),
        compiler_params=_cparams(("arbitrary",)),
        name="expert_mlp",
    )(tile_exp, n_used, xs, wg, wu, wd)


def _final_kernel(dest_ref, next_ref, w_ref, base_ref, g2_ref, lg_ref, lb_ref, ys_ref, o_ref, buf_ref, sem):
    i = pl.program_id(0)
    tm = base_ref.shape[1]
    slot = i % 2

    def gather(rows_ref, to):
        def issue(t8, carry):
            base = pl.multiple_of(t8 * SUBLANES, SUBLANES)
            for s in range(SUBLANES):
                for k in range(TOP_K):
                    _row_copy(ys_ref, rows_ref[0, k, base + s], buf_ref.at[to, k], base + s,
                              sem.at[to]).start(priority=k % 2)
            return carry

        lax.fori_loop(0, tm // SUBLANES, issue, 0)

    @pl.when(i == 0)
    def _():
        gather(dest_ref, slot)

    @pl.when(i + 1 < pl.num_programs(0))
    def _():
        gather(next_ref, 1 - slot)

    pltpu.make_async_copy(buf_ref.at[slot], buf_ref.at[slot], sem.at[slot]).wait()
    w = w_ref[...]
    lo_sum, hi_sum = jnp.zeros((tm, buf_ref.shape[3]), F32), jnp.zeros((tm, buf_ref.shape[3]), F32)
    for k in range(TOP_K):
        lo, hi = _unpack_rows(buf_ref[slot, k])
        lo_sum += w[:, k:k + 1] * lo
        hi_sum += w[:, k:k + 1] * hi
    routed = jnp.concatenate([lo_sum, hi_sum], axis=1)
    o_ref[0] = _norm_rows(base_ref[0] + g2_ref[0] * routed) * lg_ref[...] + lb_ref[...]


def _final(dest3, w_tok, base, g2, ln_g, ln_b, ys):
    bsz, seq, dm = base.shape
    tm = dest3.shape[2]
    nt = seq // tm
    n_tiles = bsz * nt
    tile = pl.BlockSpec((1, tm, dm), lambda i: (i // nt, i % nt, 0))
    rows_of = lambda f: pl.BlockSpec((1, TOP_K, tm), lambda i: (f(i), 0, 0), memory_space=pltpu.SMEM)
    return pl.pallas_call(
        _final_kernel,
        grid=(n_tiles,),
        in_specs=[rows_of(lambda i: i), rows_of(lambda i: jnp.minimum(i + 1, n_tiles - 1)),
                  pl.BlockSpec((tm, TOP_K), lambda i: (i, 0)),
                  tile, pl.BlockSpec((1, 1, dm), lambda i: (i // nt, 0, 0)),
                  pl.BlockSpec((1, dm), lambda i: (0, 0)), pl.BlockSpec((1, dm), lambda i: (0, 0)),
                  pl.BlockSpec(memory_space=pl.ANY)],
        out_specs=tile,
        out_shape=jax.ShapeDtypeStruct((bsz, seq, dm), F32),
        scratch_shapes=[pltpu.VMEM((2, TOP_K, tm, ys.shape[1]), jnp.uint32), pltpu.SemaphoreType.DMA((2,))],
        compiler_params=_cparams(("arbitrary",)),
        name="combine_final_layernorm",
    )(dest3, dest3, w_tok, base, g2, ln_g, ln_b, ys)


def _rel_bucket(dist):
    dist = np.maximum(dist, 0)
    exact = REL_BUCKETS // 2
    log_ratio = np.log(np.maximum(dist, 1).astype(np.float32) / np.float32(exact)) / np.float32(math.log(REL_MAX_DIST / exact))
    large = np.minimum(exact + (log_ratio * (REL_BUCKETS - exact)).astype(np.int32), REL_BUCKETS - 1)
    return np.where(dist < exact, dist, large)


def _bias_tiles(rel_bias):
    n_d = 4 * SEL_LEN
    pick = (_rel_bucket(np.arange(n_d))[:, None] == np.arange(REL_BUCKETS)[None, :]).astype(np.float32)
    vec = jnp.dot(jnp.asarray(pick), rel_bias.astype(F32), precision=lax.Precision.HIGHEST).T
    far = rel_bias[REL_BUCKETS - 1]
    vec = ((vec - far[:, None]) * LOG2E).reshape(N_KV, GQA, n_d)
    tok = np.arange(Q_BLOCK)[None, :]
    key = np.arange(SEL_LEN)[:, None]
    padded = jnp.pad(vec, ((0, 0), (0, 0), (Q_BLOCK, 0)))

    def by_distance(d):
        first = np.clip(d[:, 0], -Q_BLOCK, n_d - Q_BLOCK) + Q_BLOCK
        rows = [lax.slice_in_dim(padded, int(f), int(f) + Q_BLOCK, axis=2) for f in first]
        return jnp.stack(rows, axis=2)

    def toeplitz(d):
        vals = jnp.where(jnp.asarray(d >= 0), by_distance(d), NEG)
        return jnp.transpose(vals, (0, 2, 1, 3)).reshape(N_KV, d.shape[0], ROWS)

    near = [toeplitz(delta + tok - key) for delta in (0, SEL_LEN, 2 * SEL_LEN)]
    zero = jnp.zeros((N_KV, SEL_LEN, ROWS), F32)
    edge = np.where(tok < key, 0.0, NEG).astype(np.float32)
    edge = jnp.broadcast_to(jnp.asarray(np.tile(edge, (1, GQA)))[None], (N_KV, SEL_LEN, ROWS))
    nb = jnp.stack(near + [zero, jnp.full_like(zero, NEG), edge], axis=1)
    w = np.arange(BAND_ROWS)[:, None]
    bands = []
    for ph in (0, 1):
        d = tok - CMP_STRIDE * (w - PAD_CMP - (Q_BLOCK // CMP_STRIDE) * ph) - (CMP_LEN - 1)
        vals = jnp.where(jnp.asarray(d >= 0), by_distance(d), 0.0)
        bands.append(jnp.transpose(vals, (0, 2, 1, 3)).reshape(N_KV, BAND_ROWS, ROWS))
    cb = jnp.stack(bands, axis=1)
    return nb.astype(F32), cb.astype(F32)


def _padded_cmp_rows(seq):
    return -(-(seq // CMP_STRIDE + 2 * PAD_CMP) // LANES) * LANES


def _overlap_t(seq):
    n_cmp = (seq - CMP_LEN) // CMP_STRIDE + 1
    n_blk = seq // SEL_LEN
    ncp = _padded_cmp_rows(seq)
    c_start = np.arange(n_cmp) * CMP_STRIDE
    c_end = c_start + CMP_LEN - 1
    blk = np.arange(n_blk)
    ov = ((c_start[:, None] < (blk[None, :] + 1) * SEL_LEN) & (c_end[:, None] >= blk[None, :] * SEL_LEN))
    out = np.zeros((2 * SUBLANES + n_blk, ncp), np.float32)
    out[0] = 1.0
    out[2 * SUBLANES:, PAD_CMP:PAD_CMP + n_cmp] = ov.T
    return jnp.asarray(out, BF16)


def _cmp_visibility(seq):
    n_cmp = (seq - CMP_LEN) // CMP_STRIDE + 1
    ncp = _padded_cmp_rows(seq)
    rho = np.arange(ncp)[:, None]
    col = np.arange(COLS)[None, :]
    thr = CMP_STRIDE * (rho - PAD_CMP) + (CMP_LEN - 1) - Q_BLOCK * (col // ROWS) - col % Q_BLOCK
    valid = (rho >= PAD_CMP) & (rho < PAD_CMP + n_cmp)
    return jnp.asarray(np.where(valid, thr, np.iinfo(np.int32).max), jnp.int32)


def _s5_params(lam_re, lam_im, log_step, b_re, b_im, c_re, c_im):
    lr, li = lam_re.astype(F32), lam_im.astype(F32)
    dt = jnp.exp(log_step.astype(F32))[:, None]
    mag = jnp.exp(lr * dt)
    ar, ai = mag * jnp.cos(li * dt), mag * jnp.sin(li * dt)
    den = lr * lr + li * li
    kr = ((ar - 1.0) * lr + ai * li) / den
    ki = (ai * lr - (ar - 1.0) * li) / den
    br, bi = b_re.astype(F32), b_im.astype(F32)
    bbr = kr[..., None] * br - ki[..., None] * bi
    bbi = kr[..., None] * bi + ki[..., None] * br
    n_g = lr.shape[0]
    eye = jnp.eye(n_g, dtype=F32)

    def drive(bb):
        return jnp.einsum('gpc,gh->gchp', bb, eye).reshape(n_g * SSM_GROUP, n_g * SSM_STATE)

    def readout(c):
        return jnp.einsum('gcp,gh->gphc', c, eye).reshape(n_g * SSM_STATE, n_g * SSM_GROUP)

    d_re, d_im = drive(bbr), drive(bbi)
    r_re, r_im = readout(c_re.astype(F32)), -readout(c_im.astype(F32))
    c_in, c_st = d_re.shape[0] // S5_DIAG_BLOCKS, d_re.shape[1] // S5_DIAG_BLOCKS
    blk = lambda m, k, rows, cols: m[k * rows:(k + 1) * rows, k * cols:(k + 1) * cols]
    wb = jnp.stack([jnp.concatenate([blk(d_re, k, c_in, c_st), blk(d_im, k, c_in, c_st)], axis=1)
                    for k in range(S5_DIAG_BLOCKS)]).astype(BF16)
    wc = jnp.stack([jnp.concatenate([blk(r_re, k, c_st, c_in), blk(r_im, k, c_st, c_in)], axis=0)
                    for k in range(S5_DIAG_BLOCKS)]).astype(BF16)
    a = jnp.concatenate([ar.reshape(1, -1), ai.reshape(1, -1)], axis=1)
    return wb, jnp.broadcast_to(a, (SUBLANES, a.shape[1])), wc


def _layer(x, mod, w_in, lam_re, lam_im, log_step, b_re, b_im, c_re, c_im, d_skip, glu_w, cmp_pos, cmp_w1,
           cmp_w2, rel_bias, w_out, ln1_g, ln1_b, router_w, router_bias, e_gate, e_up, e_down, sg, su, sd,
           ln2_g, ln2_b, alpha):
    bsz, seq, dm = x.shape
    n_tok = bsz * seq
    sh1, sc1, g1, sh2, sc2, g2 = [m[:, None, :] for m in jnp.split(mod, 6, axis=-1)]
    ssm_w = dm // 2
    attn_w = N_HEADS * HEAD_DIM
    kv_w = N_KV * HEAD_DIM
    n_gate = 3 * N_HEADS
    offs = np.cumsum([0, ssm_w, attn_w] + [kv_w] * 6 + [n_gate, 2 * dm])

    wu = w_in[:, offs[0]:offs[1]].astype(BF16)
    wq = w_in[:, offs[1]:offs[2]] * (HEAD_DIM ** -0.5 * LOG2E)
    wa = jnp.concatenate([wq, w_in[:, offs[2]:offs[4]]], axis=1).astype(BF16)
    wg = jnp.pad(w_in[:, offs[8]:offs[9]], ((0, 0), (0, LANES - n_gate))).astype(BF16)
    wm = w_in[:, offs[9]:offs[10]].astype(BF16)
    nq = seq // Q_BLOCK
    n_blk = seq // SEL_LEN
    nch = seq // CMP_STRIDE

    aug_w = 2 * HEAD_DIM + -(-n_blk // LANES) * LANES
    u2d, act, gates, mg, kaug, kw, vst, vwt = _input_projection(
        x, sh1, sc1, wu, wa, wg, wm, w_in[:, offs[4]:offs[5]].astype(BF16), w_in[:, offs[6]:offs[7]].astype(BF16),
        w_in[:, offs[5]:offs[6]].T.astype(BF16), w_in[:, offs[7]:offs[8]].T.astype(BF16), aug_w)

    def piece(i):
        return act[:, :, attn_w + i * kv_w: attn_w + (i + 1) * kv_w]

    def compress(raw, pos, w1, w2):
        xc = raw.reshape(bsz, nch, CMP_STRIDE, N_KV, HEAD_DIM).transpose(0, 3, 1, 2, 4).reshape(bsz, N_KV, nch, CMP_STRIDE * HEAD_DIM)
        half = CMP_STRIDE * HEAD_DIM
        w1cat = jnp.concatenate([w1[:half], w1[half:]], axis=1).astype(BF16)
        posb = jnp.dot(pos.reshape(1, -1), w1, precision=lax.Precision.HIGHEST)
        return _compress(xc, w1cat, posb, w2.astype(BF16))

    kc = compress(piece(0), cmp_pos[0], cmp_w1[0], cmp_w2[0])
    vc = compress(piece(1), cmp_pos[1], cmp_w1[1], cmp_w2[1])
    pad = ((0, 0), (0, 0), (PAD_CMP, _padded_cmp_rows(seq) - nch - PAD_CMP), (0, 0))
    kcp = jnp.pad(kc, pad).astype(BF16)
    vcpt = jnp.swapaxes(jnp.pad(vc, pad), 2, 3).astype(BF16)
    ones_ovt = _overlap_t(seq)
    cmp_lhs = jnp.concatenate([vcpt, jnp.broadcast_to(ones_ovt, (bsz, N_KV) + ones_ovt.shape)], axis=2)

    nst = nq // NSA_TILES
    gt = gates[:, :, :n_gate].reshape(bsz, nst, NSA_TILES, Q_BLOCK, N_KV, GQA, 3)
    gt = gt.transpose(0, 4, 1, 6, 2, 5, 3).reshape(bsz, N_KV, nst, 3, COLS)
    gt = jnp.pad(gt, ((0, 0), (0, 0), (0, 0), (0, SUBLANES - 3), (0, 0)))
    nb, cb = _bias_tiles(rel_bias)
    y_nsa = _nsa(act, kaug, vst, kw, vwt, kcp, cmp_lhs, _cmp_visibility(seq), nb, cb, gt)

    wb, a, wc = _s5_params(lam_re, lam_im, log_step, b_re, b_im, c_re, c_im)
    z2 = _s5(u2d.reshape(seq * bsz, ssm_w), wb, a, wc, d_skip.reshape(1, ssm_w).astype(F32), bsz)
    z2d = z2.reshape(seq, bsz * ssm_w)

    base, h2 = _mix_out(z2d, y_nsa, mg, x, g1, sh2, sc2, g2, ln1_g.reshape(1, dm), ln1_b.reshape(1, dm),
                        glu_w.astype(BF16), w_out.astype(BF16), sg.astype(BF16), su.astype(BF16), sd.astype(BF16), alpha)
    h2 = h2.reshape(n_tok, dm // 2)

    n_exp = router_w.shape[1]
    rt = min(ROUTER_TILE, n_tok)
    tri = jnp.asarray(np.triu(np.ones((rt, rt), np.float32), 1), BF16)
    top_e, top_w, rank, counts = _router(h2, router_w.T.astype(BF16), router_bias.reshape(n_exp, 1).astype(F32), tri)
    counts = counts[:, 0].astype(jnp.int32)
    padded = (counts + EXPERT_TILE - 1) // EXPERT_TILE * EXPERT_TILE
    pad_end = jnp.cumsum(padded)
    pad_start = pad_end - padded
    dest = _dest_rows(pad_start.astype(jnp.int32), top_e, rank)
    n_rows = n_tok * TOP_K + n_exp * EXPERT_TILE
    n_tiles = n_rows // EXPERT_TILE
    tile_exp = jnp.sum(pad_end[None, :] <= (jnp.arange(n_tiles) * EXPERT_TILE)[:, None], axis=1)
    tile_exp = jnp.minimum(tile_exp, n_exp - 1).astype(jnp.int32)
    n_used = (pad_end[-1] // EXPERT_TILE).astype(jnp.int32).reshape(1)
    mt = min(MOVE_TILE, seq)
    dest3 = dest.reshape(TOP_K, n_tok // mt, mt).transpose(1, 0, 2)
    xs = _dispatch(pad_end.astype(jnp.int32), padded.astype(jnp.int32), n_used, dest3, h2, n_rows)
    ys = _experts(tile_exp, n_used, xs, e_gate, e_up, e_down)
    return _final(dest3, top_w.T, base, g2, ln2_g.reshape(1, dm), ln2_b.reshape(1, dm), ys)


def kernel(x, c, ada_w, ada_b, w_in, ssm_lambda_re, ssm_lambda_im, ssm_log_step, ssm_b_re, ssm_b_im, ssm_c_re, ssm_c_im, ssm_d, ssm_glu_w, cmp_pos, cmp_w1, cmp_w2, rel_bias, w_out, ln1_g, ln1_b, router_w, router_bias, exp_w_gate, exp_w_up, exp_w_down, sh_w_gate, sh_w_up, sh_w_down, ln2_g, ln2_b):
    depth = ada_w.shape[0]
    alpha = (2 * depth) ** 0.25
    for l in range(depth):
        mod = _modulation(c, ada_w[l], ada_b[l])
        x = _layer(x, mod, w_in[l], ssm_lambda_re[l], ssm_lambda_im[l], ssm_log_step[l], ssm_b_re[l], ssm_b_im[l],
                   ssm_c_re[l], ssm_c_im[l], ssm_d[l], ssm_glu_w[l], cmp_pos[l], cmp_w1[l], cmp_w2[l], rel_bias,
                   w_out[l], ln1_g[l], ln1_b[l], router_w[l], router_bias[l], exp_w_gate[l], exp_w_up[l],
                   exp_w_down[l], sh_w_gate[l], sh_w_up[l], sh_w_down[l], ln2_g[l], ln2_b[l], alpha)
    return x
```

```python
import functools
import math

import numpy as np
import jax
import jax.numpy as jnp
from jax import lax
from jax.experimental import pallas as pl
from jax.experimental.pallas import tpu as pltpu

F32 = jnp.float32
BF16 = jnp.bfloat16

SSM_GROUP = 16
SSM_STATE = 64
N_HEADS = 16
HEAD_DIM = 64
N_KV = 4
GQA = N_HEADS // N_KV
CMP_LEN = 32
CMP_STRIDE = 16
SEL_LEN = 64
SEL_TOP = 16
WINDOW = 512
Q_BLOCK = 64
FORCE_BONUS = 1.0e4
N_FORCED = 3
assert GQA < FORCE_BONUS
REL_BUCKETS = 32
REL_MAX_DIST = 128
TOP_K = 8
ROUTED_SCALE = 2.5
LN_EPS = 1e-5

LANES = 128
SUBLANES = 8
VMEM_LIMIT_BYTES = 56 * 1024 * 1024
ROW_TILE = 512
ROUTER_TILE = 256
EXPERT_TILE = 512
MOVE_TILE = 512
DEST_TILE = 2048
SCAN_CHUNK = 128
SCAN_LANES = 512
S5_DIAG_BLOCKS = 2
MASK_BIG = 2.0 ** 100
NEG = -1.0e30
LOG2E = math.log2(math.e)
WIN_BLOCKS = WINDOW // SEL_LEN
ROWS = GQA * Q_BLOCK
NSA_TILES = 4
COLS = NSA_TILES * ROWS
CHUNK_BLOCKS = 8
CHUNK_KEYS = CHUNK_BLOCKS * SEL_LEN
FAR_BLOCKS = NSA_TILES
FAR_KEYS = FAR_BLOCKS * SEL_LEN
PAD_CMP = 8
BAND_ROWS = 24


def _cparams(sem):
    return pltpu.CompilerParams(dimension_semantics=sem, vmem_limit_bytes=VMEM_LIMIT_BYTES)


def _pack_rows(x):
    n = x.shape[1] // 2
    xb = x.astype(jnp.bfloat16).astype(F32)
    lo = lax.shift_right_logical(lax.bitcast_convert_type(xb[:, :n], jnp.uint32), jnp.uint32(16))
    return lax.bitcast_convert_type(xb[:, n:], jnp.uint32) | lo


def _unpack_rows(w):
    lo = lax.bitcast_convert_type(lax.shift_left(w, jnp.uint32(16)), F32)
    hi = lax.bitcast_convert_type(w & jnp.uint32(0xFFFF0000), F32)
    return lo, hi


def _norm_rows(x):
    mu = jnp.mean(x, axis=-1, keepdims=True)
    xc = x - mu
    var = jnp.mean(xc * xc, axis=-1, keepdims=True)
    return xc * lax.rsqrt(var + LN_EPS)


def _mod_kernel(c_ref, w_ref, b_ref, o_ref):
    cond = jax.nn.silu(c_ref[...])
    o_ref[...] = jnp.dot(cond.astype(BF16), w_ref[...].astype(BF16), preferred_element_type=F32) + b_ref[...]


def _modulation(c, ada_w, ada_b):
    bsz, dm = c.shape
    n = ada_w.shape[1]
    tn = dm
    return pl.pallas_call(
        _mod_kernel,
        grid=(n // tn,),
        in_specs=[pl.BlockSpec((bsz, dm), lambda j: (0, 0)),
                  pl.BlockSpec((dm, tn), lambda j: (0, j)),
                  pl.BlockSpec((1, tn), lambda j: (0, j))],
        out_specs=pl.BlockSpec((bsz, tn), lambda j: (0, j)),
        out_shape=jax.ShapeDtypeStruct((bsz, n), F32),
        compiler_params=_cparams(("arbitrary",)),
        name="ada_modulation",
    )(c, ada_w, ada_b.reshape(1, n))


def _inproj_kernel(x_ref, sh_ref, sc_ref, wu_ref, wa_ref, wg_ref, wm_ref, wks_ref, wkw_ref, wvs_ref, wvw_ref,
                   u_ref, a_ref, g_ref, m_ref, ks_ref, kw_ref, vs_ref, vw_ref, *, aug_w):
    h = _norm_rows(x_ref[0]) * (1.0 + sc_ref[0]) + sh_ref[0]
    hb = h.astype(BF16)
    tm = hb.shape[0]
    u_ref[...] = jnp.dot(hb, wu_ref[...], preferred_element_type=F32)
    a_ref[0] = jnp.dot(hb, wa_ref[...], preferred_element_type=F32).astype(BF16)
    g_ref[0] = jax.nn.sigmoid(jnp.dot(hb, wg_ref[...], preferred_element_type=F32))
    m_ref[0] = jax.nn.sigmoid(jnp.dot(hb, wm_ref[...], preferred_element_type=F32)).astype(BF16)
    low = lax.broadcasted_iota(jnp.int32, (tm, LANES), 1) < HEAD_DIM

    def spread_heads(k4):
        cols = [k4[:, (g // 2) * LANES:(g // 2 + 1) * LANES] for g in range(N_KV)]
        return [jnp.where(low, pltpu.roll(c, HEAD_DIM, 1) if g % 2 else c, 0.0) for g, c in enumerate(cols)]

    oh_shape = (tm, aug_w - 2 * HEAD_DIM)
    blk = jnp.right_shift(pl.program_id(1) * tm + lax.broadcasted_iota(jnp.int32, oh_shape, 0), SEL_LEN.bit_length() - 1)
    onehot = jnp.where(lax.broadcasted_iota(jnp.int32, oh_shape, 1) == blk, -MASK_BIG, 0.0)
    ks = spread_heads(jnp.dot(hb, wks_ref[...], preferred_element_type=F32))
    ks_ref[0] = jnp.concatenate([part for k in ks for part in (k, onehot)], axis=1).astype(BF16)
    kw_ref[0] = jnp.concatenate(spread_heads(jnp.dot(hb, wkw_ref[...], preferred_element_type=F32)), axis=1).astype(BF16)
    rows = lax.broadcasted_iota(jnp.int32, (2 * SUBLANES, tm), 0)
    ones_rows = jnp.where(rows == 0, 1.0, 0.0).astype(BF16)
    for w_ref, v_ref in ((wvs_ref, vs_ref), (wvw_ref, vw_ref)):
        vt = lax.dot_general(w_ref[...], hb, (((1,), (1,)), ((), ())), preferred_element_type=F32).astype(BF16)
        for g in range(N_KV):
            v_ref[0, g] = jnp.concatenate([vt[g * HEAD_DIM:(g + 1) * HEAD_DIM], ones_rows], axis=0)


def _input_projection(x, sh1, sc1, wu, wa, wg, wm, wks, wkw, wvs_t, wvw_t, aug_w):
    bsz, seq, dm = x.shape
    tm = min(ROW_TILE, seq)
    nu, na, ng, nm = wu.shape[1], wa.shape[1], wg.shape[1], wm.shape[1]
    assert 2 * HEAD_DIM == LANES
    n_ks, n_kw = N_KV * aug_w, N_KV * 2 * HEAD_DIM
    full = lambda w: pl.BlockSpec(w.shape, lambda b, i: (0, 0))
    vec = pl.BlockSpec((1, 1, dm), lambda b, i: (b, 0, 0))
    rows_out = lambda n: pl.BlockSpec((1, tm, n), lambda b, i: (b, i, 0))
    vt_rows = HEAD_DIM + 2 * SUBLANES
    vt_out = pl.BlockSpec((1, N_KV, vt_rows, tm), lambda b, i: (b, 0, 0, i))
    return pl.pallas_call(
        functools.partial(_inproj_kernel, aug_w=aug_w),
        grid=(bsz, seq // tm),
        in_specs=[pl.BlockSpec((1, tm, dm), lambda b, i: (b, i, 0)), vec, vec,
                  full(wu), full(wa), full(wg), full(wm), full(wks), full(wkw), full(wvs_t), full(wvw_t)],
        out_specs=[pl.BlockSpec((tm, nu), lambda b, i: (i, b)), rows_out(na), rows_out(ng), rows_out(nm),
                   rows_out(n_ks), rows_out(n_kw), vt_out, vt_out],
        out_shape=[jax.ShapeDtypeStruct((seq, bsz * nu), F32),
                   jax.ShapeDtypeStruct((bsz, seq, na), BF16),
                   jax.ShapeDtypeStruct((bsz, seq, ng), F32),
                   jax.ShapeDtypeStruct((bsz, seq, nm), BF16),
                   jax.ShapeDtypeStruct((bsz, seq, n_ks), BF16),
                   jax.ShapeDtypeStruct((bsz, seq, n_kw), BF16),
                   jax.ShapeDtypeStruct((bsz, N_KV, vt_rows, seq), BF16),
                   jax.ShapeDtypeStruct((bsz, N_KV, vt_rows, seq), BF16)],
        compiler_params=_cparams(("arbitrary", "arbitrary")),
        name="adaln_input_projection",
    )(x, sh1, sc1, wu, wa, wg, wm, wks, wkw, wvs_t, wvw_t)


def _compress_kernel(x_ref, w1_ref, pb_ref, w2_ref, o_ref, *, n_cmp):
    hid = w2_ref.shape[0]
    p = jnp.dot(x_ref[0, 0], w1_ref[...], preferred_element_type=F32)
    nrow = p.shape[0]
    nxt = pltpu.roll(p[:, hid:], nrow - 1, 0)
    hidv = jax.nn.gelu(p[:, :hid] + nxt + pb_ref[...])
    out = jnp.dot(hidv.astype(BF16), w2_ref[...], preferred_element_type=F32)
    rows = lax.broadcasted_iota(jnp.int32, out.shape, 0)
    o_ref[0, 0] = jnp.where(rows < n_cmp, out, 0.0)


def _compress(xc, w1cat, posb, w2):
    bsz, nkv, nch, kdim = xc.shape
    hid2 = w1cat.shape[1]
    return pl.pallas_call(
        functools.partial(_compress_kernel, n_cmp=nch - 1),
        grid=(bsz, nkv),
        in_specs=[pl.BlockSpec((1, 1, nch, kdim), lambda b, g: (b, g, 0, 0)),
                  pl.BlockSpec((kdim, hid2), lambda b, g: (0, 0)),
                  pl.BlockSpec((1, hid2 // 2), lambda b, g: (0, 0)),
                  pl.BlockSpec((hid2 // 2, HEAD_DIM), lambda b, g: (0, 0))],
        out_specs=pl.BlockSpec((1, 1, nch, HEAD_DIM), lambda b, g: (b, g, 0, 0)),
        out_shape=jax.ShapeDtypeStruct((bsz, nkv, nch, HEAD_DIM), F32),
        compiler_params=_cparams(("arbitrary", "arbitrary")),
        name="kv_compress",
    )(xc, w1cat, posb, w2)


def _swap_heads_tokens(a):
    t = jnp.concatenate([a, jnp.zeros_like(a)], axis=0).T
    p = [t[r * HEAD_DIM:(r + 1) * HEAD_DIM] for r in range(GQA)]
    return jnp.concatenate([p[r] + pltpu.roll(p[r + 1], Q_BLOCK, 1) for r in range(0, GQA, 2)], axis=1)


def _nsa_kernel(q_ref, kaug_ref, vst_ref, kw_ref, vwt_ref, kcp_ref, cl_ref, vis_ref, nb_ref, cb_ref,
                g_ref, o_ref, s_ref, c_ref, w_ref, fa_ref, fb_ref, *, n_blk, n_sel):
    q0 = pl.program_id(2) * NSA_TILES
    qf = q_ref[0].astype(F32)
    qt = jnp.concatenate([_swap_heads_tokens(qf[n * Q_BLOCK:(n + 1) * Q_BLOCK]) for n in range(NSA_TILES)],
                         axis=1).astype(BF16)
    col_tile = lambda n: slice(n * ROWS, (n + 1) * ROWS)

    s_ref[...] = jnp.dot(kcp_ref[0, 0], qt, preferred_element_type=F32)
    for n in range(NSA_TILES):
        band = pl.multiple_of(SUBLANES * ((q0 + n) // 2), SUBLANES)
        s_ref[pl.ds(band, BAND_ROWS), col_tile(n)] += cb_ref[0, n % 2]
    s = jnp.where(vis_ref[...] <= Q_BLOCK * q0, s_ref[...], -jnp.inf)
    mx = jnp.max(s, axis=0, keepdims=True)
    mx = jnp.where(mx == -jnp.inf, 0.0, mx)
    e = jnp.exp2((s - mx).astype(BF16))
    both = jnp.dot(cl_ref[0, 0], e, preferred_element_type=F32)
    inv = 1.0 / jnp.maximum(both[HEAD_DIM:HEAD_DIM + 1], 1e-30)
    o_cmp = both[:HEAD_DIM] * inv

    imp4 = both[HEAD_DIM + 2 * SUBLANES:] * inv
    sums = []
    for n in range(NSA_TILES):
        two = imp4[:, n * ROWS:n * ROWS + LANES] + imp4[:, n * ROWS + LANES:(n + 1) * ROWS]
        sums.append(two + pltpu.roll(two, Q_BLOCK, 1))
    low = lax.broadcasted_iota(jnp.int32, sums[0].shape, 1) < Q_BLOCK
    imp = jnp.concatenate([jnp.where(low, sums[n], sums[n + 1]) for n in range(0, NSA_TILES, 2)], axis=1)
    blk = lax.broadcasted_iota(jnp.int32, imp.shape, 0)
    cur = q0 + jnp.right_shift(lax.broadcasted_iota(jnp.int32, imp.shape, 1), Q_BLOCK.bit_length() - 1)
    blkf = blk.astype(F32)
    forced = (blk == 0) | (blk == cur) | (blk == cur - 1)
    score = jnp.where((blk <= cur) & ~forced, imp, -jnp.inf)
    notsel = jnp.where(forced, 0.0, 1.0)
    for _ in range(n_sel - N_FORCED):
        best = jnp.max(score, axis=0, keepdims=True)
        first = jnp.min(jnp.where(score == best, blkf, float(n_blk)), axis=0, keepdims=True)
        pick = blkf == first
        notsel = jnp.where(pick, 0.0, notsel)
        score = jnp.where(pick, -jnp.inf, score)
    halves = []
    for n in range(0, NSA_TILES, 2):
        pair = notsel[:, (n // 2) * LANES:(n // 2 + 1) * LANES]
        swapped = pltpu.roll(pair, Q_BLOCK, 1)
        halves += [jnp.where(low, pair, swapped), jnp.where(low, swapped, pair)]
    notsel = jnp.concatenate([h for h in halves for _ in (0, 1)], axis=1).astype(BF16)
    qwin = jnp.concatenate([qt, jnp.zeros_like(qt)], axis=0)
    tail = kaug_ref.shape[2] - 2 * HEAD_DIM - n_blk
    qaug = jnp.concatenate([qwin, notsel] + ([jnp.zeros((tail, COLS), notsel.dtype)] if tail else []), axis=0)

    def update(carry, sc, vt, top=None):
        m, acc = carry
        m_new = jnp.maximum(m, jnp.max(sc, axis=0, keepdims=True) if top is None else top)
        p = jnp.exp2((sc - m_new).astype(BF16))
        return m_new, jnp.exp2(m - m_new) * acc + jnp.dot(vt, p, preferred_element_type=F32)

    def chunk_keys(c):
        return pl.ds(pl.multiple_of(c * FAR_KEYS, FAR_KEYS), FAR_KEYS)

    def far_scores(c, buf_ref):
        sc = jnp.dot(kaug_ref[0, chunk_keys(c), :], qaug, preferred_element_type=F32)
        buf_ref[...] = sc
        return jnp.max(sc, axis=0, keepdims=True)

    def far_pair(i, carry):
        m, acc, top_a = carry
        top_b = far_scores(2 * i + 1, fb_ref)
        m, acc = update((m, acc), fa_ref[...], vst_ref[0, 0, :, chunk_keys(2 * i)], top_a)
        top_a = far_scores(2 * i + 2, fa_ref)
        m, acc = update((m, acc), fb_ref[...], vst_ref[0, 0, :, chunk_keys(2 * i + 1)], top_b)
        return m, acc, top_a

    def far_last(_, carry, n_far):
        m, acc, top_a = carry
        return update((m, acc), fa_ref[...], vst_ref[0, 0, :, chunk_keys(n_far - 1)], top_a) + (top_a,)

    def near_chunk(k_ref, vt_ref, qmat, blk0, tile_of, carry, buf_ref):
        n_keys = buf_ref.shape[0]
        keys = pl.ds(pl.multiple_of(blk0 * SEL_LEN, NSA_TILES * SEL_LEN), n_keys)
        buf_ref[...] = jnp.dot(k_ref[0, keys, :], qmat, preferred_element_type=F32)
        for o in range(n_keys // SEL_LEN):
            for n in range(NSA_TILES):
                buf_ref[o * SEL_LEN:(o + 1) * SEL_LEN, col_tile(n)] += nb_ref[0, tile_of(blk0 + o, q0 + n)]
        return update(carry, buf_ref[...], vt_ref[0, 0, :, keys])

    def sel_tile(j, qi):
        d = qi - j
        return jnp.where(d < 0, 4, jnp.minimum(d, 3))

    def win_tile(j, qi):
        d = qi - j
        return jnp.where((d < 0) | (d > WIN_BLOCKS), 4, jnp.where(d == WIN_BLOCKS, 5, jnp.minimum(d, 3)))

    init = (jnp.full((1, COLS), NEG, F32), jnp.zeros((vst_ref.shape[2], COLS), F32))
    _, acc = near_chunk(kw_ref, vwt_ref, qwin, jnp.maximum(q0 - WIN_BLOCKS, 0), win_tile, init, w_ref)
    o_win = acc[:HEAD_DIM] / acc[HEAD_DIM:HEAD_DIM + 1]
    n_far = jnp.maximum(q0 - 2, 0) // FAR_BLOCKS
    carry = lax.fori_loop(0, n_far // 2, far_pair, init + (far_scores(0, fa_ref),))
    carry = lax.fori_loop(0, n_far % 2, functools.partial(far_last, n_far=n_far), carry)
    _, acc = near_chunk(kaug_ref, vst_ref, qaug, n_far * FAR_BLOCKS, sel_tile, carry[:2], c_ref)
    o_slc = acc[:HEAD_DIM] / acc[HEAD_DIM:HEAD_DIM + 1]
    g = g_ref[0, 0, 0]
    out = g[0:1] * o_cmp + g[1:2] * o_slc + g[2:3] * o_win
    o_ref[0] = jnp.concatenate([_swap_heads_tokens(out[:, col_tile(n)]) for n in range(NSA_TILES)],
                               axis=0).astype(BF16)


def _nsa(act, kaug, vst, kw, vwt, kcp, cmp_lhs, vis, nb, cb, gt):
    bsz, nkv, nsteps, _, cols = gt.shape
    dh = HEAD_DIM
    seq = kw.shape[1]
    step_tokens = NSA_TILES * Q_BLOCK
    q_blk = pl.BlockSpec((1, step_tokens, GQA * dh), lambda b, g, i: (b, i, g))
    n_blk = seq // SEL_LEN
    ncp = kcp.shape[2]
    assert n_blk % CHUNK_BLOCKS == 0 and n_blk >= WIN_BLOCKS + NSA_TILES and CHUNK_BLOCKS == FAR_BLOCKS + NSA_TILES
    per_bg = lambda shape: pl.BlockSpec((1, 1) + shape, lambda b, g, i: (b, g, 0, 0))
    per_step = lambda shape: pl.BlockSpec((1, 1, 1) + shape, lambda b, g, i: (b, g, i, 0, 0))
    head_cols = lambda a: pl.BlockSpec((1, seq, a.shape[2] // nkv), lambda b, g, i: (b, 0, g))
    kern = functools.partial(_nsa_kernel, n_blk=n_blk, n_sel=min(SEL_TOP, n_blk))
    return pl.pallas_call(
        kern,
        grid=(bsz, nkv, nsteps),
        in_specs=[q_blk,
                  head_cols(kaug), per_bg((vst.shape[2], seq)), head_cols(kw), per_bg((vwt.shape[2], seq)),
                  per_bg((ncp, dh)), per_bg((cmp_lhs.shape[2], ncp)),
                  pl.BlockSpec(vis.shape, lambda b, g, i: (0, 0)),
                  pl.BlockSpec((1,) + nb.shape[1:], lambda b, g, i: (g, 0, 0, 0)),
                  pl.BlockSpec((1,) + cb.shape[1:], lambda b, g, i: (g, 0, 0, 0)),
                  per_step((SUBLANES, cols))],
        out_specs=q_blk,
        out_shape=jax.ShapeDtypeStruct((bsz, seq, nkv * GQA * dh), BF16),
        scratch_shapes=[pltpu.VMEM((ncp, cols), F32), pltpu.VMEM((CHUNK_KEYS, cols), F32),
                        pltpu.VMEM(((WIN_BLOCKS + NSA_TILES) * SEL_LEN, cols), F32),
                        pltpu.VMEM((FAR_KEYS, cols), F32), pltpu.VMEM((FAR_KEYS, cols), F32)],
        compiler_params=_cparams(("arbitrary", "arbitrary", "arbitrary")),
        name="nsa_attention",
    )(act, kaug, vst, kw, vwt, kcp, cmp_lhs, vis, nb, cb, gt)


def _s5_kernel(u_ref, wb_ref, a_ref, wc_ref, d_ref, z_ref, xs_ref, st_ref, *, bsz, n_state):
    @pl.when(pl.program_id(0) == 0)
    def _():
        st_ref[...] = jnp.zeros_like(st_ref)

    u = u_ref[...]
    ub = u.astype(BF16)
    n_diag = wb_ref.shape[0]
    c_in, c_st = u.shape[1] // n_diag, n_state // n_diag
    for k in range(n_diag):
        drv = jnp.dot(ub[:, k * c_in:(k + 1) * c_in], wb_ref[k], preferred_element_type=F32)
        xs_ref[:, k * c_st:(k + 1) * c_st] = drv[:, :c_st]
        xs_ref[:, n_state + k * c_st:n_state + (k + 1) * c_st] = drv[:, c_st:]
    steps = u.shape[0] // bsz
    for c0 in range(0, n_state, SCAN_LANES):
        re = pl.ds(c0, SCAN_LANES)
        im = pl.ds(n_state + c0, SCAN_LANES)
        ar = jnp.broadcast_to(a_ref[0:1, re], (bsz, SCAN_LANES))
        ai = jnp.broadcast_to(a_ref[0:1, im], (bsz, SCAN_LANES))

        def step(t, carry):
            xr, xi = carry
            rows = pl.ds(pl.multiple_of(t * bsz, bsz), bsz)
            nr = ar * xr - ai * xi + xs_ref[rows, re]
            ni = ar * xi + ai * xr + xs_ref[rows, im]
            xs_ref[rows, re] = nr
            xs_ref[rows, im] = ni
            return nr, ni

        xr, xi = lax.fori_loop(0, steps, step, (st_ref[:, re], st_ref[:, im]), unroll=8)
        st_ref[:, re] = xr
        st_ref[:, im] = xi
    ys = []
    for k in range(n_diag):
        st = jnp.concatenate([xs_ref[:, k * c_st:(k + 1) * c_st],
                              xs_ref[:, n_state + k * c_st:n_state + (k + 1) * c_st]], axis=1)
        ys.append(jnp.dot(st.astype(BF16), wc_ref[k], preferred_element_type=F32))
    y = jnp.concatenate(ys, axis=1) + d_ref[...] * u
    z_ref[...] = jax.nn.gelu(y).astype(BF16)


def _s5(u2, wb, a, wc, dsk, bsz):
    rows, width = u2.shape
    seq = rows // bsz
    chunk = min(SCAN_CHUNK, seq)
    n_state2 = a.shape[1]
    return pl.pallas_call(
        functools.partial(_s5_kernel, bsz=bsz, n_state=n_state2 // 2),
        grid=(seq // chunk,),
        in_specs=[pl.BlockSpec((chunk * bsz, width), lambda i: (i, 0)),
                  pl.BlockSpec(wb.shape, lambda i: (0, 0, 0)),
                  pl.BlockSpec(a.shape, lambda i: (0, 0)),
                  pl.BlockSpec(wc.shape, lambda i: (0, 0, 0)),
                  pl.BlockSpec(dsk.shape, lambda i: (0, 0))],
        out_specs=pl.BlockSpec((chunk * bsz, width), lambda i: (i, 0)),
        out_shape=jax.ShapeDtypeStruct((rows, width), BF16),
        scratch_shapes=[pltpu.VMEM((chunk * bsz, n_state2), F32), pltpu.VMEM((bsz, n_state2), F32)],
        compiler_params=_cparams(("arbitrary",)),
        name="s5_scan",
    )(u2, wb, a, wc, dsk)


def _mixout_kernel(z_ref, yn_ref, mg_ref, x_ref, g1_ref, sh2_ref, sc2_ref, g2_ref, lg_ref, lb_ref,
                   glu_ref, wo_ref, sg_ref, su_ref, sd_ref, base_ref, h_ref, *, alpha):
    dm = x_ref.shape[2]
    glu = jnp.dot(z_ref[...], glu_ref[...], preferred_element_type=F32)
    y_ssm = glu[:, :dm] * jax.nn.sigmoid(glu[:, dm:])
    mg = mg_ref[0].astype(F32)
    merged = mg[:, :dm] * y_ssm + mg[:, dm:] * yn_ref[0].astype(F32)
    y = jnp.dot(merged.astype(BF16), wo_ref[...], preferred_element_type=F32)
    x1 = _norm_rows(alpha * x_ref[0] + g1_ref[0] * y) * lg_ref[...] + lb_ref[...]
    hf = _norm_rows(x1) * (1.0 + sc2_ref[0]) + sh2_ref[0]
    h_ref[0] = _pack_rows(hf)
    h = hf.astype(BF16)
    hs =jax.nn.silu(jnp.dot(h, sg_ref[...], preferred_element_type=F32)) * jnp.dot(h, su_ref[...], preferred_element_type=F32)
    shared = jnp.dot(hs.astype(BF16), sd_ref[...], preferred_element_type=F32)
    base_ref[0] = alpha * x1 + g2_ref[0] * shared


def _mix_out(z2d, y_nsa, mg, x, g1, sh2, sc2, g2, ln_g, ln_b, glu_w, w_out, sg, su, sd, alpha):
    bsz, seq, dm = x.shape
    tm = min(ROW_TILE, seq)
    width = z2d.shape[1] // bsz
    vec = pl.BlockSpec((1, 1, dm), lambda b, i: (b, 0, 0))
    row = pl.BlockSpec((1, dm), lambda b, i: (0, 0))
    full = lambda w: pl.BlockSpec(w.shape, lambda b, i: (0, 0))
    tile = lambda n: pl.BlockSpec((1, tm, n), lambda b, i: (b, i, 0))
    return pl.pallas_call(
        functools.partial(_mixout_kernel, alpha=alpha),
        grid=(bsz, seq // tm),
        in_specs=[pl.BlockSpec((tm, width), lambda b, i: (i, b)), tile(dm), tile(2 * dm), tile(dm),
                  vec, vec, vec, vec, row, row, full(glu_w), full(w_out), full(sg), full(su), full(sd)],
        out_specs=[tile(dm), tile(dm // 2)],
        out_shape=[jax.ShapeDtypeStruct((bsz, seq, dm), F32), jax.ShapeDtypeStruct((bsz, seq, dm // 2), jnp.uint32)],
        compiler_params=_cparams(("arbitrary", "arbitrary")),
        name="merge_outproj_ln_shared",
    )(z2d, y_nsa, mg, x, g1, sh2, sc2, g2, ln_g, ln_b, glu_w, w_out, sg, su, sd)


def _router_kernel(h_ref, rwt_ref, rb_ref, tri_ref, e_ref, w_ref, p_ref, cnt_ref):
    @pl.when(pl.program_id(0) == 0)
    def _():
        cnt_ref[...] = jnp.zeros_like(cnt_ref)

    h = jnp.concatenate(_unpack_rows(h_ref[...]), axis=1).astype(BF16)
    logits = lax.dot_general(rwt_ref[...], h, (((1,), (1,)), ((), ())), preferred_element_type=F32)
    scores = jax.nn.sigmoid(logits)
    cur = scores + rb_ref[...]
    n_exp = scores.shape[0]
    eid = lax.broadcasted_iota(jnp.int32, scores.shape, 0).astype(F32)
    chosen = jnp.zeros(scores.shape, F32)
    ids, vals = [], []
    for _ in range(TOP_K):
        best = jnp.max(cur, axis=0, keepdims=True)
        first = jnp.min(jnp.where(cur == best, eid, float(n_exp)), axis=0, keepdims=True)
        pick = eid == first
        ids.append(first)
        vals.append(jnp.sum(jnp.where(pick, scores, 0.0), axis=0, keepdims=True))
        chosen = jnp.where(pick, 1.0, chosen)
        cur = jnp.where(pick, -jnp.inf, cur)
    top_s = jnp.concatenate(vals, axis=0)
    w_ref[...] = top_s / jnp.sum(top_s, axis=0, keepdims=True) * ROUTED_SCALE
    top_e = jnp.concatenate(ids, axis=0)
    e_ref[...] = top_e.astype(jnp.int32)
    before = jnp.dot(chosen.astype(BF16), tri_ref[...], preferred_element_type=F32) + cnt_ref[...]
    ranks = [jnp.sum(jnp.where(eid == ids[k], before, 0.0), axis=0, keepdims=True) for k in range(TOP_K)]
    p_ref[...] = jnp.concatenate(ranks, axis=0).astype(jnp.int32)
    cnt_ref[...] += jnp.sum(chosen, axis=1, keepdims=True)


def _router(h2, rwt, rb, tri):
    n_tok, words = h2.shape
    n_exp, dm = rwt.shape
    tm = tri.shape[0]
    kt = pl.BlockSpec((TOP_K, tm), lambda i: (0, i))
    return pl.pallas_call(
        _router_kernel,
        grid=(n_tok // tm,),
        in_specs=[pl.BlockSpec((tm, words), lambda i: (i, 0)),
                  pl.BlockSpec((n_exp, dm), lambda i: (0, 0)),
                  pl.BlockSpec((n_exp, 1), lambda i: (0, 0)),
                  pl.BlockSpec((tm, tm), lambda i: (0, 0))],
        out_specs=[kt, kt, kt, pl.BlockSpec((n_exp, 1), lambda i: (0, 0))],
        out_shape=[jax.ShapeDtypeStruct((TOP_K, n_tok), jnp.int32),
                   jax.ShapeDtypeStruct((TOP_K, n_tok), F32),
                   jax.ShapeDtypeStruct((TOP_K, n_tok), jnp.int32),
                   jax.ShapeDtypeStruct((n_exp, 1), F32)],
        compiler_params=_cparams(("arbitrary",)),
        name="router_topk_rank",
    )(h2, rwt, rb, tri)


def _dest_kernel(start_ref, e_ref, r_ref, o_ref):
    e = e_ref[...]
    start = lax.fori_loop(0, start_ref.shape[0], lambda j, acc: jnp.where(e == j, start_ref[j], acc),
                          jnp.zeros(e.shape, jnp.int32))
    o_ref[...] = start + r_ref[...]


def _dest_rows(pad_start, top_e, rank):
    n_tok = top_e.shape[1]
    tm = min(DEST_TILE, n_tok)
    blk = pl.BlockSpec((TOP_K, tm), lambda i, ps: (0, i))
    return pl.pallas_call(
        _dest_kernel,
        grid_spec=pltpu.PrefetchScalarGridSpec(num_scalar_prefetch=1, grid=(n_tok // tm,), in_specs=[blk, blk], out_specs=blk),
        out_shape=jax.ShapeDtypeStruct(top_e.shape, jnp.int32),
        compiler_params=_cparams(("arbitrary",)),
        name="moe_dest_rows",
    )(pad_start, top_e, rank)


def _row_copy(src_ref, src_row, dst_ref, dst_row, sem):
    return pltpu.make_async_copy(src_ref.at[pl.ds(src_row, 1), :], dst_ref.at[pl.ds(dst_row, 1), :], sem)


def _dispatch_kernel(pend_ref, padded_ref, nused_ref, dest_ref, h_ref, xs_ref, zero_ref, sem, zsem):
    tm = h_ref.shape[0]

    @pl.when(pl.program_id(0) == 0)
    def _():
        zero_ref[...] = jnp.zeros_like(zero_ref)
        n_tiles = xs_ref.shape[0] // EXPERT_TILE

        def zero_tile(row):
            return pltpu.make_async_copy(zero_ref, xs_ref.at[pl.ds(pl.multiple_of(row, EXPERT_TILE), EXPERT_TILE), :], zsem)

        def per_expert(act):
            def body(e, carry):
                @pl.when(padded_ref[e] > 0)
                def _():
                    act(zero_tile(pend_ref[e] - EXPERT_TILE))
                return carry
            lax.fori_loop(0, pend_ref.shape[0], body, 0)

        def per_unused(act):
            def body(i, carry):
                act(zero_tile(i * EXPERT_TILE))
                return carry
            lax.fori_loop(nused_ref[0], n_tiles, body, 0)

        for loop in (per_expert, per_unused):
            loop(lambda copy: copy.start())
        for loop in (per_expert, per_unused):
            loop(lambda copy: copy.wait())

    def issue(t8, carry):
        base = pl.multiple_of(t8 * SUBLANES, SUBLANES)
        for s in range(SUBLANES):
            for k in range(TOP_K):
                _row_copy(h_ref, base + s, xs_ref, dest_ref[0, k, base + s], sem).start(priority=k % 2)
        return carry

    lax.fori_loop(0, tm // SUBLANES, issue, 0)
    pltpu.make_async_copy(xs_ref.at[pl.ds(0, TOP_K * tm), :], xs_ref.at[pl.ds(0, TOP_K * tm), :], sem).wait()


def _dispatch(pad_end, padded, n_used, dest3, h2, n_rows):
    n_tok, words = h2.shape
    tm = dest3.shape[2]
    grid_spec = pltpu.PrefetchScalarGridSpec(
        num_scalar_prefetch=3,
        grid=(n_tok // tm,),
        in_specs=[pl.BlockSpec((1, TOP_K, tm), lambda i, *_: (i, 0, 0), memory_space=pltpu.SMEM),
                  pl.BlockSpec((tm, words), lambda i, *_: (i, 0))],
        out_specs=pl.BlockSpec(memory_space=pl.ANY),
        scratch_shapes=[pltpu.VMEM((EXPERT_TILE, words), jnp.uint32), pltpu.SemaphoreType.DMA(()),
                        pltpu.SemaphoreType.DMA(())],
    )
    return pl.pallas_call(
        _dispatch_kernel,
        grid_spec=grid_spec,
        out_shape=jax.ShapeDtypeStruct((n_rows, words), jnp.uint32),
        compiler_params=_cparams(("arbitrary",)),
        name="moe_dispatch",
    )(pad_end, padded, n_used, dest3, h2)


def _expert_kernel(te_ref, nu_ref, x_ref, wg_ref, wu_ref, wd_ref, y_ref):
    i = pl.program_id(0)

    @pl.when(i < nu_ref[0])
    def _():
        lo, hi = _unpack_rows(x_ref[...])
        lo, hi = lo.astype(BF16), hi.astype(BF16)
        half = lo.shape[1]

        def proj(w_ref):
            return (jnp.dot(lo, w_ref[0, :half, :].astype(BF16), preferred_element_type=F32)
                    + jnp.dot(hi, w_ref[0, half:, :].astype(BF16), preferred_element_type=F32))

        hmid = jax.nn.silu(proj(wg_ref)) * proj(wu_ref)
        y_ref[...] = _pack_rows(jnp.dot(hmid.astype(BF16), wd_ref[0].astype(BF16), preferred_element_type=F32))

    @pl.when(i >= nu_ref[0])
    def _():
        y_ref[...] = jnp.zeros_like(y_ref)


def _experts(tile_exp, n_used, xs, wg, wu, wd):
    n_rows, words = xs.shape
    dm, de = wg.shape[1], wg.shape[2]
    grid_spec = pltpu.PrefetchScalarGridSpec(
        num_scalar_prefetch=2,
        grid=(n_rows // EXPERT_TILE,),
        in_specs=[pl.BlockSpec((EXPERT_TILE, words), lambda i, te, nu: (i, 0)),
                  pl.BlockSpec((1, dm, de), lambda i, te, nu: (te[i], 0, 0)),
                  pl.BlockSpec((1, dm, de), lambda i, te, nu: (te[i], 0, 0)),
                  pl.BlockSpec((1, de, dm), lambda i, te, nu: (te[i], 0, 0))],
        out_specs=pl.BlockSpec((EXPERT_TILE, words), lambda i, te, nu: (i, 0)),
    )
    return pl.pallas_call(
        _expert_kernel,
        grid_spec=grid_spec,
        out_shape=jax.ShapeDtypeStruct((n_rows, words), jnp.uint32),
        compiler_params=_cparams(("arbitrary",)),
        name="expert_mlp",
    )(tile_exp, n_used, xs, wg, wu, wd)


def _final_kernel(dest_ref, next_ref, w_ref, base_ref, g2_ref, lg_ref, lb_ref, ys_ref, o_ref, buf_ref, sem):
    i = pl.program_id(0)
    tm = base_ref.shape[1]
    slot = i % 2

    def gather(rows_ref, to):
        def issue(t8, carry):
            base = pl.multiple_of(t8 * SUBLANES, SUBLANES)
            for s in range(SUBLANES):
                for k in range(TOP_K):
                    _row_copy(ys_ref, rows_ref[0, k, base + s], buf_ref.at[to, k], base + s,
                              sem.at[to]).start(priority=k % 2)
            return carry

        lax.fori_loop(0, tm // SUBLANES, issue, 0)

    @pl.when(i == 0)
    def _():
        gather(dest_ref, slot)

    @pl.when(i + 1 < pl.num_programs(0))
    def _():
        gather(next_ref, 1 - slot)

    pltpu.make_async_copy(buf_ref.at[slot], buf_ref.at[slot], sem.at[slot]).wait()
    w = w_ref[...]
    lo_sum, hi_sum = jnp.zeros((tm, buf_ref.shape[3]), F32), jnp.zeros((tm, buf_ref.shape[3]), F32)
    for k in range(TOP_K):
        lo, hi = _unpack_rows(buf_ref[slot, k])
        lo_sum += w[:, k:k + 1] * lo
        hi_sum += w[:, k:k + 1] * hi
    routed = jnp.concatenate([lo_sum, hi_sum], axis=1)
    o_ref[0] = _norm_rows(base_ref[0] + g2_ref[0] * routed) * lg_ref[...] + lb_ref[...]


def _final(dest3, w_tok, base, g2, ln_g, ln_b, ys):
    bsz, seq, dm = base.shape
    tm = dest3.shape[2]
    nt = seq // tm
    n_tiles = bsz * nt
    tile = pl.BlockSpec((1, tm, dm), lambda i: (i // nt, i % nt, 0))
    rows_of = lambda f: pl.BlockSpec((1, TOP_K, tm), lambda i: (f(i), 0, 0), memory_space=pltpu.SMEM)
    return pl.pallas_call(
        _final_kernel,
        grid=(n_tiles,),
        in_specs=[rows_of(lambda i: i), rows_of(lambda i: jnp.minimum(i + 1, n_tiles - 1)),
                  pl.BlockSpec((tm, TOP_K), lambda i: (i, 0)),
                  tile, pl.BlockSpec((1, 1, dm), lambda i: (i // nt, 0, 0)),
                  pl.BlockSpec((1, dm), lambda i: (0, 0)), pl.BlockSpec((1, dm), lambda i: (0, 0)),
                  pl.BlockSpec(memory_space=pl.ANY)],
        out_specs=tile,
        out_shape=jax.ShapeDtypeStruct((bsz, seq, dm), F32),
        scratch_shapes=[pltpu.VMEM((2, TOP_K, tm, ys.shape[1]), jnp.uint32), pltpu.SemaphoreType.DMA((2,))],
        compiler_params=_cparams(("arbitrary",)),
        name="combine_final_layernorm",
    )(dest3, dest3, w_tok, base, g2, ln_g, ln_b, ys)


def _rel_bucket(dist):
    dist = np.maximum(dist, 0)
    exact = REL_BUCKETS // 2
    log_ratio = np.log(np.maximum(dist, 1).astype(np.float32) / np.float32(exact)) / np.float32(math.log(REL_MAX_DIST / exact))
    large = np.minimum(exact + (log_ratio * (REL_BUCKETS - exact)).astype(np.int32), REL_BUCKETS - 1)
    return np.where(dist < exact, dist, large)


def _bias_tiles(rel_bias):
    n_d = 4 * SEL_LEN
    pick = (_rel_bucket(np.arange(n_d))[:, None] == np.arange(REL_BUCKETS)[None, :]).astype(np.float32)
    vec = jnp.dot(jnp.asarray(pick), rel_bias.astype(F32), precision=lax.Precision.HIGHEST).T
    far = rel_bias[REL_BUCKETS - 1]
    vec = ((vec - far[:, None]) * LOG2E).reshape(N_KV, GQA, n_d)
    tok = np.arange(Q_BLOCK)[None, :]
    key = np.arange(SEL_LEN)[:, None]
    padded = jnp.pad(vec, ((0, 0), (0, 0), (Q_BLOCK, 0)))

    def by_distance(d):
        first = np.clip(d[:, 0], -Q_BLOCK, n_d - Q_BLOCK) + Q_BLOCK
        rows = [lax.slice_in_dim(padded, int(f), int(f) + Q_BLOCK, axis=2) for f in first]
        return jnp.stack(rows, axis=2)

    def toeplitz(d):
        vals = jnp.where(jnp.asarray(d >= 0), by_distance(d), NEG)
        return jnp.transpose(vals, (0, 2, 1, 3)).reshape(N_KV, d.shape[0], ROWS)

    near = [toeplitz(delta + tok - key) for delta in (0, SEL_LEN, 2 * SEL_LEN)]
    zero = jnp.zeros((N_KV, SEL_LEN, ROWS), F32)
    edge = np.where(tok < key, 0.0, NEG).astype(np.float32)
    edge = jnp.broadcast_to(jnp.asarray(np.tile(edge, (1, GQA)))[None], (N_KV, SEL_LEN, ROWS))
    nb = jnp.stack(near + [zero, jnp.full_like(zero, NEG), edge], axis=1)
    w = np.arange(BAND_ROWS)[:, None]
    bands = []
    for ph in (0, 1):
        d = tok - CMP_STRIDE * (w - PAD_CMP - (Q_BLOCK // CMP_STRIDE) * ph) - (CMP_LEN - 1)
        vals = jnp.where(jnp.asarray(d >= 0), by_distance(d), 0.0)
        bands.append(jnp.transpose(vals, (0, 2, 1, 3)).reshape(N_KV, BAND_ROWS, ROWS))
    cb = jnp.stack(bands, axis=1)
    return nb.astype(F32), cb.astype(F32)


def _padded_cmp_rows(seq):
    return -(-(seq // CMP_STRIDE + 2 * PAD_CMP) // LANES) * LANES


def _overlap_t(seq):
    n_cmp = (seq - CMP_LEN) // CMP_STRIDE + 1
    n_blk = seq // SEL_LEN
    ncp = _padded_cmp_rows(seq)
    c_start = np.arange(n_cmp) * CMP_STRIDE
    c_end = c_start + CMP_LEN - 1
    blk = np.arange(n_blk)
    ov = ((c_start[:, None] < (blk[None, :] + 1) * SEL_LEN) & (c_end[:, None] >= blk[None, :] * SEL_LEN))
    out = np.zeros((2 * SUBLANES + n_blk, ncp), np.float32)
    out[0] = 1.0
    out[2 * SUBLANES:, PAD_CMP:PAD_CMP + n_cmp] = ov.T
    return jnp.asarray(out, BF16)


def _cmp_visibility(seq):
    n_cmp = (seq - CMP_LEN) // CMP_STRIDE + 1
    ncp = _padded_cmp_rows(seq)
    rho = np.arange(ncp)[:, None]
    col = np.arange(COLS)[None, :]
    thr = CMP_STRIDE * (rho - PAD_CMP) + (CMP_LEN - 1) - Q_BLOCK * (col // ROWS) - col % Q_BLOCK
    valid = (rho >= PAD_CMP) & (rho < PAD_CMP + n_cmp)
    return jnp.asarray(np.where(valid, thr, np.iinfo(np.int32).max), jnp.int32)


def _s5_params(lam_re, lam_im, log_step, b_re, b_im, c_re, c_im):
    lr, li = lam_re.astype(F32), lam_im.astype(F32)
    dt = jnp.exp(log_step.astype(F32))[:, None]
    mag = jnp.exp(lr * dt)
    ar, ai = mag * jnp.cos(li * dt), mag * jnp.sin(li * dt)
    den = lr * lr + li * li
    kr = ((ar - 1.0) * lr + ai * li) / den
    ki = (ai * lr - (ar - 1.0) * li) / den
    br, bi = b_re.astype(F32), b_im.astype(F32)
    bbr = kr[..., None] * br - ki[..., None] * bi
    bbi = kr[..., None] * bi + ki[..., None] * br
    n_g = lr.shape[0]
    eye = jnp.eye(n_g, dtype=F32)

    def drive(bb):
        return jnp.einsum('gpc,gh->gchp', bb, eye).reshape(n_g * SSM_GROUP, n_g * SSM_STATE)

    def readout(c):
        return jnp.einsum('gcp,gh->gphc', c, eye).reshape(n_g * SSM_STATE, n_g * SSM_GROUP)

    d_re, d_im = drive(bbr), drive(bbi)
    r_re, r_im = readout(c_re.astype(F32)), -readout(c_im.astype(F32))
    c_in, c_st = d_re.shape[0] // S5_DIAG_BLOCKS, d_re.shape[1] // S5_DIAG_BLOCKS
    blk = lambda m, k, rows, cols: m[k * rows:(k + 1) * rows, k * cols:(k + 1) * cols]
    wb = jnp.stack([jnp.concatenate([blk(d_re, k, c_in, c_st), blk(d_im, k, c_in, c_st)], axis=1)
                    for k in range(S5_DIAG_BLOCKS)]).astype(BF16)
    wc = jnp.stack([jnp.concatenate([blk(r_re, k, c_st, c_in), blk(r_im, k, c_st, c_in)], axis=0)
                    for k in range(S5_DIAG_BLOCKS)]).astype(BF16)
    a = jnp.concatenate([ar.reshape(1, -1), ai.reshape(1, -1)], axis=1)
    return wb, jnp.broadcast_to(a, (SUBLANES, a.shape[1])), wc


def _layer(x, mod, w_in, lam_re, lam_im, log_step, b_re, b_im, c_re, c_im, d_skip, glu_w, cmp_pos, cmp_w1,
           cmp_w2, rel_bias, w_out, ln1_g, ln1_b, router_w, router_bias, e_gate, e_up, e_down, sg, su, sd,
           ln2_g, ln2_b, alpha):
    bsz, seq, dm = x.shape
    n_tok = bsz * seq
    sh1, sc1, g1, sh2, sc2, g2 = [m[:, None, :] for m in jnp.split(mod, 6, axis=-1)]
    ssm_w = dm // 2
    attn_w = N_HEADS * HEAD_DIM
    kv_w = N_KV * HEAD_DIM
    n_gate = 3 * N_HEADS
    offs = np.cumsum([0, ssm_w, attn_w] + [kv_w] * 6 + [n_gate, 2 * dm])

    wu = w_in[:, offs[0]:offs[1]].astype(BF16)
    wq = w_in[:, offs[1]:offs[2]] * (HEAD_DIM ** -0.5 * LOG2E)
    wa = jnp.concatenate([wq, w_in[:, offs[2]:offs[4]]], axis=1).astype(BF16)
    wg = jnp.pad(w_in[:, offs[8]:offs[9]], ((0, 0), (0, LANES - n_gate))).astype(BF16)
    wm = w_in[:, offs[9]:offs[10]].astype(BF16)
    nq = seq // Q_BLOCK
    n_blk = seq // SEL_LEN
    nch = seq // CMP_STRIDE

    aug_w = 2 * HEAD_DIM + -(-n_blk // LANES) * LANES
    u2d, act, gates, mg, kaug, kw, vst, vwt = _input_projection(
        x, sh1, sc1, wu, wa, wg, wm, w_in[:, offs[4]:offs[5]].astype(BF16), w_in[:, offs[6]:offs[7]].astype(BF16),
        w_in[:, offs[5]:offs[6]].T.astype(BF16), w_in[:, offs[7]:offs[8]].T.astype(BF16), aug_w)

    def piece(i):
        return act[:, :, attn_w + i * kv_w: attn_w + (i + 1) * kv_w]

    def compress(raw, pos, w1, w2):
        xc = raw.reshape(bsz, nch, CMP_STRIDE, N_KV, HEAD_DIM).transpose(0, 3, 1, 2, 4).reshape(bsz, N_KV, nch, CMP_STRIDE * HEAD_DIM)
        half = CMP_STRIDE * HEAD_DIM
        w1cat = jnp.concatenate([w1[:half], w1[half:]], axis=1).astype(BF16)
        posb = jnp.dot(pos.reshape(1, -1), w1, precision=lax.Precision.HIGHEST)
        return _compress(xc, w1cat, posb, w2.astype(BF16))

    kc = compress(piece(0), cmp_pos[0], cmp_w1[0], cmp_w2[0])
    vc = compress(piece(1), cmp_pos[1], cmp_w1[1], cmp_w2[1])
    pad = ((0, 0), (0, 0), (PAD_CMP, _padded_cmp_rows(seq) - nch - PAD_CMP), (0, 0))
    kcp = jnp.pad(kc, pad).astype(BF16)
    vcpt = jnp.swapaxes(jnp.pad(vc, pad), 2, 3).astype(BF16)
    ones_ovt = _overlap_t(seq)
    cmp_lhs = jnp.concatenate([vcpt, jnp.broadcast_to(ones_ovt, (bsz, N_KV) + ones_ovt.shape)], axis=2)

    nst = nq // NSA_TILES
    gt = gates[:, :, :n_gate].reshape(bsz, nst, NSA_TILES, Q_BLOCK, N_KV, GQA, 3)
    gt = gt.transpose(0, 4, 1, 6, 2, 5, 3).reshape(bsz, N_KV, nst, 3, COLS)
    gt = jnp.pad(gt, ((0, 0), (0, 0), (0, 0), (0, SUBLANES - 3), (0, 0)))
    nb, cb = _bias_tiles(rel_bias)
    y_nsa = _nsa(act, kaug, vst, kw, vwt, kcp, cmp_lhs, _cmp_visibility(seq), nb, cb, gt)

    wb, a, wc = _s5_params(lam_re, lam_im, log_step, b_re, b_im, c_re, c_im)
    z2 = _s5(u2d.reshape(seq * bsz, ssm_w), wb, a, wc, d_skip.reshape(1, ssm_w).astype(F32), bsz)
    z2d = z2.reshape(seq, bsz * ssm_w)

    base, h2 = _mix_out(z2d, y_nsa, mg, x, g1, sh2, sc2, g2, ln1_g.reshape(1, dm), ln1_b.reshape(1, dm),
                        glu_w.astype(BF16), w_out.astype(BF16), sg.astype(BF16), su.astype(BF16), sd.astype(BF16), alpha)
    h2 = h2.reshape(n_tok, dm // 2)

    n_exp = router_w.shape[1]
    rt = min(ROUTER_TILE, n_tok)
    tri = jnp.asarray(np.triu(np.ones((rt, rt), np.float32), 1), BF16)
    top_e, top_w, rank, counts = _router(h2, router_w.T.astype(BF16), router_bias.reshape(n_exp, 1).astype(F32), tri)
    counts = counts[:, 0].astype(jnp.int32)
    padded = (counts + EXPERT_TILE - 1) // EXPERT_TILE * EXPERT_TILE
    pad_end = jnp.cumsum(padded)
    pad_start = pad_end - padded
    dest = _dest_rows(pad_start.astype(jnp.int32), top_e, rank)
    n_rows = n_tok * TOP_K + n_exp * EXPERT_TILE
    n_tiles = n_rows // EXPERT_TILE
    tile_exp = jnp.sum(pad_end[None, :] <= (jnp.arange(n_tiles) * EXPERT_TILE)[:, None], axis=1)
    tile_exp = jnp.minimum(tile_exp, n_exp - 1).astype(jnp.int32)
    n_used = (pad_end[-1] // EXPERT_TILE).astype(jnp.int32).reshape(1)
    mt = min(MOVE_TILE, seq)
    dest3 = dest.reshape(TOP_K, n_tok // mt, mt).transpose(1, 0, 2)
    xs = _dispatch(pad_end.astype(jnp.int32), padded.astype(jnp.int32), n_used, dest3, h2, n_rows)
    ys = _experts(tile_exp, n_used, xs, e_gate, e_up, e_down)
    return _final(dest3, top_w.T, base, g2, ln2_g.reshape(1, dm), ln2_b.reshape(1, dm), ys)


def kernel(x, c, ada_w, ada_b, w_in, ssm_lambda_re, ssm_lambda_im, ssm_log_step, ssm_b_re, ssm_b_im, ssm_c_re, ssm_c_im, ssm_d, ssm_glu_w, cmp_pos, cmp_w1, cmp_w2, rel_bias, w_out, ln1_g, ln1_b, router_w, router_bias, exp_w_gate, exp_w_up, exp_w_down, sh_w_gate, sh_w_up, sh_w_down, ln2_g, ln2_b):
    depth = ada_w.shape[0]
    alpha = (2 * depth) ** 0.25
    for l in range(depth):
        mod = _modulation(c, ada_w[l], ada_b[l])
        x = _layer(x, mod, w_in[l], ssm_lambda_re[l], ssm_lambda_im[l], ssm_log_step[l], ssm_b_re[l], ssm_b_im[l],
                   ssm_c_re[l], ssm_c_im[l], ssm_d[l], ssm_glu_w[l], cmp_pos[l], cmp_w1[l], cmp_w2[l], rel_bias,
                   w_out[l], ln1_g[l], ln1_b[l], router_w[l], router_bias[l], exp_w_gate[l], exp_w_up[l],
                   exp_w_down[l], sh_w_gate[l], sh_w_up[l], sh_w_down[l], ln2_g[l], ln2_b[l], alpha)
    return x
```

```python
import functools
import math

import numpy as np
import jax
import jax.numpy as jnp
from jax import lax
from jax.experimental import pallas as pl
from jax.experimental.pallas import tpu as pltpu

F32 = jnp.float32
BF16 = jnp.bfloat16

SSM_GROUP = 16
SSM_STATE = 64
N_HEADS = 16
HEAD_DIM = 64
N_KV = 4
GQA = N_HEADS // N_KV
CMP_LEN = 32
CMP_STRIDE = 16
SEL_LEN = 64
SEL_TOP = 16
WINDOW = 512
Q_BLOCK = 64
FORCE_BONUS = 1.0e4
N_FORCED = 3
assert GQA < FORCE_BONUS
REL_BUCKETS = 32
REL_MAX_DIST = 128
TOP_K = 8
ROUTED_SCALE = 2.5
LN_EPS = 1e-5

LANES = 128
SUBLANES = 8
VMEM_LIMIT_BYTES = 56 * 1024 * 1024
ROW_TILE = 512
ROUTER_TILE = 256
EXPERT_TILE = 512
MOVE_TILE = 512
DEST_TILE = 2048
SCAN_CHUNK = 128
SCAN_LANES = 512
S5_DIAG_BLOCKS = 2
MASK_BIG = 2.0 ** 100
NEG = -1.0e30
LOG2E = math.log2(math.e)
WIN_BLOCKS = WINDOW // SEL_LEN
ROWS = GQA * Q_BLOCK
NSA_TILES = 4
COLS = NSA_TILES * ROWS
CHUNK_BLOCKS = 8
CHUNK_KEYS = CHUNK_BLOCKS * SEL_LEN
FAR_BLOCKS = NSA_TILES
FAR_KEYS = FAR_BLOCKS * SEL_LEN
PAD_CMP = 8
BAND_ROWS = 24


def _cparams(sem):
    return pltpu.CompilerParams(dimension_semantics=sem, vmem_limit_bytes=VMEM_LIMIT_BYTES)


def _pack_rows(x):
    n = x.shape[1] // 2
    xb = x.astype(jnp.bfloat16).astype(F32)
    lo = lax.shift_right_logical(lax.bitcast_convert_type(xb[:, :n], jnp.uint32), jnp.uint32(16))
    return lax.bitcast_convert_type(xb[:, n:], jnp.uint32) | lo


def _unpack_rows(w):
    lo = lax.bitcast_convert_type(lax.shift_left(w, jnp.uint32(16)), F32)
    hi = lax.bitcast_convert_type(w & jnp.uint32(0xFFFF0000), F32)
    return lo, hi


def _norm_rows(x):
    mu = jnp.mean(x, axis=-1, keepdims=True)
    xc = x - mu
    var = jnp.mean(xc * xc, axis=-1, keepdims=True)
    return xc * lax.rsqrt(var + LN_EPS)


def _mod_kernel(c_ref, w_ref, b_ref, o_ref):
    cond = jax.nn.silu(c_ref[...])
    o_ref[...] = jnp.dot(cond.astype(BF16), w_ref[...].astype(BF16), preferred_element_type=F32) + b_ref[...]


def _modulation(c, ada_w, ada_b):
    bsz, dm = c.shape
    n = ada_w.shape[1]
    tn = dm
    return pl.pallas_call(
        _mod_kernel,
        grid=(n // tn,),
        in_specs=[pl.BlockSpec((bsz, dm), lambda j: (0, 0)),
                  pl.BlockSpec((dm, tn), lambda j: (0, j)),
                  pl.BlockSpec((1, tn), lambda j: (0, j))],
        out_specs=pl.BlockSpec((bsz, tn), lambda j: (0, j)),
        out_shape=jax.ShapeDtypeStruct((bsz, n), F32),
        compiler_params=_cparams(("arbitrary",)),
        name="ada_modulation",
    )(c, ada_w, ada_b.reshape(1, n))


def _inproj_kernel(x_ref, sh_ref, sc_ref, wu_ref, wa_ref, wg_ref, wm_ref, wks_ref, wkw_ref, wvs_ref, wvw_ref,
                   u_ref, a_ref, g_ref, m_ref, ks_ref, kw_ref, vs_ref, vw_ref, *, aug_w):
    h = _norm_rows(x_ref[0]) * (1.0 + sc_ref[0]) + sh_ref[0]
    hb = h.astype(BF16)
    tm = hb.shape[0]
    u_ref[...] = jnp.dot(hb, wu_ref[...], preferred_element_type=F32)
    a_ref[0] = jnp.dot(hb, wa_ref[...], preferred_element_type=F32).astype(BF16)
    g_ref[0] = jax.nn.sigmoid(jnp.dot(hb, wg_ref[...], preferred_element_type=F32))
    m_ref[0] = jax.nn.sigmoid(jnp.dot(hb, wm_ref[...], preferred_element_type=F32)).astype(BF16)
    low = lax.broadcasted_iota(jnp.int32, (tm, LANES), 1) < HEAD_DIM

    def spread_heads(k4):
        cols = [k4[:, (g // 2) * LANES:(g // 2 + 1) * LANES] for g in range(N_KV)]
        return [jnp.where(low, pltpu.roll(c, HEAD_DIM, 1) if g % 2 else c, 0.0) for g, c in enumerate(cols)]

    oh_shape = (tm, aug_w - 2 * HEAD_DIM)
    blk = jnp.right_shift(pl.program_id(1) * tm + lax.broadcasted_iota(jnp.int32, oh_shape, 0), SEL_LEN.bit_length() - 1)
    onehot = jnp.where(lax.broadcasted_iota(jnp.int32, oh_shape, 1) == blk, -MASK_BIG, 0.0)
    ks = spread_heads(jnp.dot(hb, wks_ref[...], preferred_element_type=F32))
    ks_ref[0] = jnp.concatenate([part for k in ks for part in (k, onehot)], axis=1).astype(BF16)
    kw_ref[0] = jnp.concatenate(spread_heads(jnp.dot(hb, wkw_ref[...], preferred_element_type=F32)), axis=1).astype(BF16)
    rows = lax.broadcasted_iota(jnp.int32, (2 * SUBLANES, tm), 0)
    ones_rows = jnp.where(rows == 0, 1.0, 0.0).astype(BF16)
    for w_ref, v_ref in ((wvs_ref, vs_ref), (wvw_ref, vw_ref)):
        vt = lax.dot_general(w_ref[...], hb, (((1,), (1,)), ((), ())), preferred_element_type=F32).astype(BF16)
        for g in range(N_KV):
            v_ref[0, g] = jnp.concatenate([vt[g * HEAD_DIM:(g + 1) * HEAD_DIM], ones_rows], axis=0)


def _input_projection(x, sh1, sc1, wu, wa, wg, wm, wks, wkw, wvs_t, wvw_t, aug_w):
    bsz, seq, dm = x.shape
    tm = min(ROW_TILE, seq)
    nu, na, ng, nm = wu.shape[1], wa.shape[1], wg.shape[1], wm.shape[1]
    assert 2 * HEAD_DIM == LANES
    n_ks, n_kw = N_KV * aug_w, N_KV * 2 * HEAD_DIM
    full = lambda w: pl.BlockSpec(w.shape, lambda b, i: (0, 0))
    vec = pl.BlockSpec((1, 1, dm), lambda b, i: (b, 0, 0))
    rows_out = lambda n: pl.BlockSpec((1, tm, n), lambda b, i: (b, i, 0))
    vt_rows = HEAD_DIM + 2 * SUBLANES
    vt_out = pl.BlockSpec((1, N_KV, vt_rows, tm), lambda b, i: (b, 0, 0, i))
    return pl.pallas_call(
        functools.partial(_inproj_kernel, aug_w=aug_w),
        grid=(bsz, seq // tm),
        in_specs=[pl.BlockSpec((1, tm, dm), lambda b, i: (b, i, 0)), vec, vec,
                  full(wu), full(wa), full(wg), full(wm), full(wks), full(wkw), full(wvs_t), full(wvw_t)],
        out_specs=[pl.BlockSpec((tm, nu), lambda b, i: (i, b)), rows_out(na), rows_out(ng), rows_out(nm),
                   rows_out(n_ks), rows_out(n_kw), vt_out, vt_out],
        out_shape=[jax.ShapeDtypeStruct((seq, bsz * nu), F32),
                   jax.ShapeDtypeStruct((bsz, seq, na), BF16),
                   jax.ShapeDtypeStruct((bsz, seq, ng), F32),
                   jax.ShapeDtypeStruct((bsz, seq, nm), BF16),
                   jax.ShapeDtypeStruct((bsz, seq, n_ks), BF16),
                   jax.ShapeDtypeStruct((bsz, seq, n_kw), BF16),
                   jax.ShapeDtypeStruct((bsz, N_KV, vt_rows, seq), BF16),
                   jax.ShapeDtypeStruct((bsz, N_KV, vt_rows, seq), BF16)],
        compiler_params=_cparams(("arbitrary", "arbitrary")),
        name="adaln_input_projection",
    )(x, sh1, sc1, wu, wa, wg, wm, wks, wkw, wvs_t, wvw_t)


def _compress_kernel(x_ref, w1_ref, pb_ref, w2_ref, o_ref, *, n_cmp):
    hid = w2_ref.shape[0]
    p = jnp.dot(x_ref[0, 0], w1_ref[...], preferred_element_type=F32)
    nrow = p.shape[0]
    nxt = pltpu.roll(p[:, hid:], nrow - 1, 0)
    hidv = jax.nn.gelu(p[:, :hid] + nxt + pb_ref[...])
    out = jnp.dot(hidv.astype(BF16), w2_ref[...], preferred_element_type=F32)
    rows = lax.broadcasted_iota(jnp.int32, out.shape, 0)
    o_ref[0, 0] = jnp.where(rows < n_cmp, out, 0.0)


def _compress(xc, w1cat, posb, w2):
    bsz, nkv, nch, kdim = xc.shape
    hid2 = w1cat.shape[1]
    return pl.pallas_call(
        functools.partial(_compress_kernel, n_cmp=nch - 1),
        grid=(bsz, nkv),
        in_specs=[pl.BlockSpec((1, 1, nch, kdim), lambda b, g: (b, g, 0, 0)),
                  pl.BlockSpec((kdim, hid2), lambda b, g: (0, 0)),
                  pl.BlockSpec((1, hid2 // 2), lambda b, g: (0, 0)),
                  pl.BlockSpec((hid2 // 2, HEAD_DIM), lambda b, g: (0, 0))],
        out_specs=pl.BlockSpec((1, 1, nch, HEAD_DIM), lambda b, g: (b, g, 0, 0)),
        out_shape=jax.ShapeDtypeStruct((bsz, nkv, nch, HEAD_DIM), F32),
        compiler_params=_cparams(("arbitrary", "arbitrary")),
        name="kv_compress",
    )(xc, w1cat, posb, w2)


def _swap_heads_tokens(a):
    t = jnp.concatenate([a, jnp.zeros_like(a)], axis=0).T
    p = [t[r * HEAD_DIM:(r + 1) * HEAD_DIM] for r in range(GQA)]
    return jnp.concatenate([p[r] + pltpu.roll(p[r + 1], Q_BLOCK, 1) for r in range(0, GQA, 2)], axis=1)


def _nsa_kernel(q_ref, kaug_ref, vst_ref, kw_ref, vwt_ref, kcp_ref, cl_ref, vis_ref, nb_ref, cb_ref,
                g_ref, o_ref, s_ref, c_ref, w_ref, fa_ref, fb_ref, *, n_blk, n_sel):
    q0 = pl.program_id(2) * NSA_TILES
    qf = q_ref[0].astype(F32)
    qt = jnp.concatenate([_swap_heads_tokens(qf[n * Q_BLOCK:(n + 1) * Q_BLOCK]) for n in range(NSA_TILES)],
                         axis=1).astype(BF16)
    col_tile = lambda n: slice(n * ROWS, (n + 1) * ROWS)

    s_ref[...] = jnp.dot(kcp_ref[0, 0], qt, preferred_element_type=F32)
    for n in range(NSA_TILES):
        band = pl.multiple_of(SUBLANES * ((q0 + n) // 2), SUBLANES)
        s_ref[pl.ds(band, BAND_ROWS), col_tile(n)] += cb_ref[0, n % 2]
    s = jnp.where(vis_ref[...] <= Q_BLOCK * q0, s_ref[...], -jnp.inf)
    mx = jnp.max(s, axis=0, keepdims=True)
    mx = jnp.where(mx == -jnp.inf, 0.0, mx)
    e = jnp.exp2((s - mx).astype(BF16))
    both = jnp.dot(cl_ref[0, 0], e, preferred_element_type=F32)
    inv = 1.0 / jnp.maximum(both[HEAD_DIM:HEAD_DIM + 1], 1e-30)
    o_cmp = both[:HEAD_DIM] * inv

    imp4 = both[HEAD_DIM + 2 * SUBLANES:] * inv
    sums = []
    for n in range(NSA_TILES):
        two = imp4[:, n * ROWS:n * ROWS + LANES] + imp4[:, n * ROWS + LANES:(n + 1) * ROWS]
        sums.append(two + pltpu.roll(two, Q_BLOCK, 1))
    low = lax.broadcasted_iota(jnp.int32, sums[0].shape, 1) < Q_BLOCK
    imp = jnp.concatenate([jnp.where(low, sums[n], sums[n + 1]) for n in range(0, NSA_TILES, 2)], axis=1)
    blk = lax.broadcasted_iota(jnp.int32, imp.shape, 0)
    cur = q0 + jnp.right_shift(lax.broadcasted_iota(jnp.int32, imp.shape, 1), Q_BLOCK.bit_length() - 1)
    blkf = blk.astype(F32)
    forced = (blk == 0) | (blk == cur) | (blk == cur - 1)
    score = jnp.where((blk <= cur) & ~forced, imp, -jnp.inf)
    notsel = jnp.where(forced, 0.0, 1.0)
    for _ in range(n_sel - N_FORCED):
        best = jnp.max(score, axis=0, keepdims=True)
        first = jnp.min(jnp.where(score == best, blkf, float(n_blk)), axis=0, keepdims=True)
        pick = blkf == first
        notsel = jnp.where(pick, 0.0, notsel)
        score = jnp.where(pick, -jnp.inf, score)
    halves = []
    for n in range(0, NSA_TILES, 2):
        pair = notsel[:, (n // 2) * LANES:(n // 2 + 1) * LANES]
        swapped = pltpu.roll(pair, Q_BLOCK, 1)
        halves += [jnp.where(low, pair, swapped), jnp.where(low, swapped, pair)]
    notsel = jnp.concatenate([h for h in halves for _ in (0, 1)], axis=1).astype(BF16)
    qwin = jnp.concatenate([qt, jnp.zeros_like(qt)], axis=0)
    tail = kaug_ref.shape[2] - 2 * HEAD_DIM - n_blk
    qaug = jnp.concatenate([qwin, notsel] + ([jnp.zeros((tail, COLS), notsel.dtype)] if tail else []), axis=0)

    def update(carry, sc, vt, top=None):
        m, acc = carry
        m_new = jnp.maximum(m, jnp.max(sc, axis=0, keepdims=True) if top is None else top)
        p = jnp.exp2((sc - m_new).astype(BF16))
        return m_new, jnp.exp2(m - m_new) * acc + jnp.dot(vt, p, preferred_element_type=F32)

    def chunk_keys(c):
        return pl.ds(pl.multiple_of(c * FAR_KEYS, FAR_KEYS), FAR_KEYS)

    def far_scores(c, buf_ref):
        sc = jnp.dot(kaug_ref[0, chunk_keys(c), :], qaug, preferred_element_type=F32)
        buf_ref[...] = sc
        return jnp.max(sc, axis=0, keepdims=True)

    def far_pair(i, carry):
        m, acc, top_a = carry
        top_b = far_scores(2 * i + 1, fb_ref)
        m, acc = update((m, acc), fa_ref[...], vst_ref[0, 0, :, chunk_keys(2 * i)], top_a)
        top_a = far_scores(2 * i + 2, fa_ref)
        m, acc = update((m, acc), fb_ref[...], vst_ref[0, 0, :, chunk_keys(2 * i + 1)], top_b)
        return m, acc, top_a

    def far_last(_, carry, n_far):
        m, acc, top_a = carry
        return update((m, acc), fa_ref[...], vst_ref[0, 0, :, chunk_keys(n_far - 1)], top_a) + (top_a,)

    def near_chunk(k_ref, vt_ref, qmat, blk0, tile_of, carry, buf_ref):
        n_keys = buf_ref.shape[0]
        keys = pl.ds(pl.multiple_of(blk0 * SEL_LEN, NSA_TILES * SEL_LEN), n_keys)
        buf_ref[...] = jnp.dot(k_ref[0, keys, :], qmat, preferred_element_type=F32)
        for o in range(n_keys // SEL_LEN):
            for n in range(NSA_TILES):
                buf_ref[o * SEL_LEN:(o + 1) * SEL_LEN, col_tile(n)] += nb_ref[0, tile_of(blk0 + o, q0 + n)]
        return update(carry, buf_ref[...], vt_ref[0, 0, :, keys])

    def sel_tile(j, qi):
        d = qi - j
        return jnp.where(d < 0, 4, jnp.minimum(d, 3))

    def win_tile(j, qi):
        d = qi - j
        return jnp.where((d < 0) | (d > WIN_BLOCKS), 4, jnp.where(d == WIN_BLOCKS, 5, jnp.minimum(d, 3)))

    init = (jnp.full((1, COLS), NEG, F32), jnp.zeros((vst_ref.shape[2], COLS), F32))
    _, acc = near_chunk(kw_ref, vwt_ref, qwin, jnp.maximum(q0 - WIN_BLOCKS, 0), win_tile, init, w_ref)
    o_win = acc[:HEAD_DIM] / acc[HEAD_DIM:HEAD_DIM + 1]
    n_far = jnp.maximum(q0 - 2, 0) // FAR_BLOCKS
    carry = lax.fori_loop(0, n_far // 2, far_pair, init + (far_scores(0, fa_ref),))
    carry = lax.fori_loop(0, n_far % 2, functools.partial(far_last, n_far=n_far), carry)
    _, acc = near_chunk(kaug_ref, vst_ref, qaug, n_far * FAR_BLOCKS, sel_tile, carry[:2], c_ref)
    o_slc = acc[:HEAD_DIM] / acc[HEAD_DIM:HEAD_DIM + 1]
    g = g_ref[0, 0, 0]
    out = g[0:1] * o_cmp + g[1:2] * o_slc + g[2:3] * o_win
    o_ref[0] = jnp.concatenate([_swap_heads_tokens(out[:, col_tile(n)]) for n in range(NSA_TILES)],
                               axis=0).astype(BF16)


def _nsa(act, kaug, vst, kw, vwt, kcp, cmp_lhs, vis, nb, cb, gt):
    bsz, nkv, nsteps, _, cols = gt.shape
    dh = HEAD_DIM
    seq = kw.shape[1]
    step_tokens = NSA_TILES * Q_BLOCK
    q_blk = pl.BlockSpec((1, step_tokens, GQA * dh), lambda b, g, i: (b, i, g))
    n_blk = seq // SEL_LEN
    ncp = kcp.shape[2]
    assert n_blk % CHUNK_BLOCKS == 0 and n_blk >= WIN_BLOCKS + NSA_TILES and CHUNK_BLOCKS == FAR_BLOCKS + NSA_TILES
    per_bg = lambda shape: pl.BlockSpec((1, 1) + shape, lambda b, g, i: (b, g, 0, 0))
    per_step = lambda shape: pl.BlockSpec((1, 1, 1) + shape, lambda b, g, i: (b, g, i, 0, 0))
    head_cols = lambda a: pl.BlockSpec((1, seq, a.shape[2] // nkv), lambda b, g, i: (b, 0, g))
    kern = functools.partial(_nsa_kernel, n_blk=n_blk, n_sel=min(SEL_TOP, n_blk))
    return pl.pallas_call(
        kern,
        grid=(bsz, nkv, nsteps),
        in_specs=[q_blk,
                  head_cols(kaug), per_bg((vst.shape[2], seq)), head_cols(kw), per_bg((vwt.shape[2], seq)),
                  per_bg((ncp, dh)), per_bg((cmp_lhs.shape[2], ncp)),
                  pl.BlockSpec(vis.shape, lambda b, g, i: (0, 0)),
                  pl.BlockSpec((1,) + nb.shape[1:], lambda b, g, i: (g, 0, 0, 0)),
                  pl.BlockSpec((1,) + cb.shape[1:], lambda b, g, i: (g, 0, 0, 0)),
                  per_step((SUBLANES, cols))],
        out_specs=q_blk,
        out_shape=jax.ShapeDtypeStruct((bsz, seq, nkv * GQA * dh), BF16),
        scratch_shapes=[pltpu.VMEM((ncp, cols), F32), pltpu.VMEM((CHUNK_KEYS, cols), F32),
                        pltpu.VMEM(((WIN_BLOCKS + NSA_TILES) * SEL_LEN, cols), F32),
                        pltpu.VMEM((FAR_KEYS, cols), F32), pltpu.VMEM((FAR_KEYS, cols), F32)],
        compiler_params=_cparams(("arbitrary", "arbitrary", "arbitrary")),
        name="nsa_attention",
    )(act, kaug, vst, kw, vwt, kcp, cmp_lhs, vis, nb, cb, gt)


def _s5_kernel(u_ref, wb_ref, a_ref, wc_ref, d_ref, z_ref, xs_ref, st_ref, *, bsz, n_state):
    @pl.when(pl.program_id(0) == 0)
    def _():
        st_ref[...] = jnp.zeros_like(st_ref)

    u = u_ref[...]
    ub = u.astype(BF16)
    n_diag = wb_ref.shape[0]
    c_in, c_st = u.shape[1] // n_diag, n_state // n_diag
    for k in range(n_diag):
        drv = jnp.dot(ub[:, k * c_in:(k + 1) * c_in], wb_ref[k], preferred_element_type=F32)
        xs_ref[:, k * c_st:(k + 1) * c_st] = drv[:, :c_st]
        xs_ref[:, n_state + k * c_st:n_state + (k + 1) * c_st] = drv[:, c_st:]
    steps = u.shape[0] // bsz
    for c0 in range(0, n_state, SCAN_LANES):
        re = pl.ds(c0, SCAN_LANES)
        im = pl.ds(n_state + c0, SCAN_LANES)
        ar = jnp.broadcast_to(a_ref[0:1, re], (bsz, SCAN_LANES))
        ai = jnp.broadcast_to(a_ref[0:1, im], (bsz, SCAN_LANES))

        def step(t, carry):
            xr, xi = carry
            rows = pl.ds(pl.multiple_of(t * bsz, bsz), bsz)
            nr = ar * xr - ai * xi + xs_ref[rows, re]
            ni = ar * xi + ai * xr + xs_ref[rows, im]
            xs_ref[rows, re] = nr
            xs_ref[rows, im] = ni
            return nr, ni

        xr, xi = lax.fori_loop(0, steps, step, (st_ref[:, re], st_ref[:, im]), unroll=8)
        st_ref[:, re] = xr
        st_ref[:, im] = xi
    ys = []
    for k in range(n_diag):
        st = jnp.concatenate([xs_ref[:, k * c_st:(k + 1) * c_st],
                              xs_ref[:, n_state + k * c_st:n_state + (k + 1) * c_st]], axis=1)
        ys.append(jnp.dot(st.astype(BF16), wc_ref[k], preferred_element_type=F32))
    y = jnp.concatenate(ys, axis=1) + d_ref[...] * u
    z_ref[...] = jax.nn.gelu(y).astype(BF16)


def _s5(u2, wb, a, wc, dsk, bsz):
    rows, width = u2.shape
    seq = rows // bsz
    chunk = min(SCAN_CHUNK, seq)
    n_state2 = a.shape[1]
    return pl.pallas_call(
        functools.partial(_s5_kernel, bsz=bsz, n_state=n_state2 // 2),
        grid=(seq // chunk,),
        in_specs=[pl.BlockSpec((chunk * bsz, width), lambda i: (i, 0)),
                  pl.BlockSpec(wb.shape, lambda i: (0, 0, 0)),
                  pl.BlockSpec(a.shape, lambda i: (0, 0)),
                  pl.BlockSpec(wc.shape, lambda i: (0, 0, 0)),
                  pl.BlockSpec(dsk.shape, lambda i: (0, 0))],
        out_specs=pl.BlockSpec((chunk * bsz, width), lambda i: (i, 0)),
        out_shape=jax.ShapeDtypeStruct((rows, width), BF16),
        scratch_shapes=[pltpu.VMEM((chunk * bsz, n_state2), F32), pltpu.VMEM((bsz, n_state2), F32)],
        compiler_params=_cparams(("arbitrary",)),
        name="s5_scan",
    )(u2, wb, a, wc, dsk)


def _mixout_kernel(z_ref, yn_ref, mg_ref, x_ref, g1_ref, sh2_ref, sc2_ref, g2_ref, lg_ref, lb_ref,
                   glu_ref, wo_ref, sg_ref, su_ref, sd_ref, base_ref, h_ref, *, alpha):
    dm = x_ref.shape[2]
    glu = jnp.dot(z_ref[...], glu_ref[...], preferred_element_type=F32)
    y_ssm = glu[:, :dm] * jax.nn.sigmoid(glu[:, dm:])
    mg = mg_ref[0].astype(F32)
    merged = mg[:, :dm] * y_ssm + mg[:, dm:] * yn_ref[0].astype(F32)
    y = jnp.dot(merged.astype(BF16), wo_ref[...], preferred_element_type=F32)
    x1 = _norm_rows(alpha * x_ref[0] + g1_ref[0] * y) * lg_ref[...] + lb_ref[...]
    hf = _norm_rows(x1) * (1.0 + sc2_ref[0]) + sh2_ref[0]
    h_ref[0] = _pack_rows(hf)
    h = hf.astype(BF16)
    hs =jax.nn.silu(jnp.dot(h, sg_ref[...], preferred_element_type=F32)) * jnp.dot(h, su_ref[...], preferred_element_type=F32)
    shared = jnp.dot(hs.astype(BF16), sd_ref[...], preferred_element_type=F32)
    base_ref[0] = alpha * x1 + g2_ref[0] * shared


def _mix_out(z2d, y_nsa, mg, x, g1, sh2, sc2, g2, ln_g, ln_b, glu_w, w_out, sg, su, sd, alpha):
    bsz, seq, dm = x.shape
    tm = min(ROW_TILE, seq)
    width = z2d.shape[1] // bsz
    vec = pl.BlockSpec((1, 1, dm), lambda b, i: (b, 0, 0))
    row = pl.BlockSpec((1, dm), lambda b, i: (0, 0))
    full = lambda w: pl.BlockSpec(w.shape, lambda b, i: (0, 0))
    tile = lambda n: pl.BlockSpec((1, tm, n), lambda b, i: (b, i, 0))
    return pl.pallas_call(
        functools.partial(_mixout_kernel, alpha=alpha),
        grid=(bsz, seq // tm),
        in_specs=[pl.BlockSpec((tm, width), lambda b, i: (i, b)), tile(dm), tile(2 * dm), tile(dm),
                  vec, vec, vec, vec, row, row, full(glu_w), full(w_out), full(sg), full(su), full(sd)],
        out_specs=[tile(dm), tile(dm // 2)],
        out_shape=[jax.ShapeDtypeStruct((bsz, seq, dm), F32), jax.ShapeDtypeStruct((bsz, seq, dm // 2), jnp.uint32)],
        compiler_params=_cparams(("arbitrary", "arbitrary")),
        name="merge_outproj_ln_shared",
    )(z2d, y_nsa, mg, x, g1, sh2, sc2, g2, ln_g, ln_b, glu_w, w_out, sg, su, sd)


def _router_kernel(h_ref, rwt_ref, rb_ref, tri_ref, e_ref, w_ref, p_ref, cnt_ref):
    @pl.when(pl.program_id(0) == 0)
    def _():
        cnt_ref[...] = jnp.zeros_like(cnt_ref)

    h = jnp.concatenate(_unpack_rows(h_ref[...]), axis=1).astype(BF16)
    logits = lax.dot_general(rwt_ref[...], h, (((1,), (1,)), ((), ())), preferred_element_type=F32)
    scores = jax.nn.sigmoid(logits)
    cur = scores + rb_ref[...]
    n_exp = scores.shape[0]
    eid = lax.broadcasted_iota(jnp.int32, scores.shape, 0).astype(F32)
    chosen = jnp.zeros(scores.shape, F32)
    ids, vals = [], []
    for _ in range(TOP_K):
        best = jnp.max(cur, axis=0, keepdims=True)
        first = jnp.min(jnp.where(cur == best, eid, float(n_exp)), axis=0, keepdims=True)
        pick = eid == first
        ids.append(first)
        vals.append(jnp.sum(jnp.where(pick, scores, 0.0), axis=0, keepdims=True))
        chosen = jnp.where(pick, 1.0, chosen)
        cur = jnp.where(pick, -jnp.inf, cur)
    top_s = jnp.concatenate(vals, axis=0)
    w_ref[...] = top_s / jnp.sum(top_s, axis=0, keepdims=True) * ROUTED_SCALE
    top_e = jnp.concatenate(ids, axis=0)
    e_ref[...] = top_e.astype(jnp.int32)
    before = jnp.dot(chosen.astype(BF16), tri_ref[...], preferred_element_type=F32) + cnt_ref[...]
    ranks = [jnp.sum(jnp.where(eid == ids[k], before, 0.0), axis=0, keepdims=True) for k in range(TOP_K)]
    p_ref[...] = jnp.concatenate(ranks, axis=0).astype(jnp.int32)
    cnt_ref[...] += jnp.sum(chosen, axis=1, keepdims=True)


def _router(h2, rwt, rb, tri):
    n_tok, words = h2.shape
    n_exp, dm = rwt.shape
    tm = tri.shape[0]
    kt = pl.BlockSpec((TOP_K, tm), lambda i: (0, i))
    return pl.pallas_call(
        _router_kernel,
        grid=(n_tok // tm,),
        in_specs=[pl.BlockSpec((tm, words), lambda i: (i, 0)),
                  pl.BlockSpec((n_exp, dm), lambda i: (0, 0)),
                  pl.BlockSpec((n_exp, 1), lambda i: (0, 0)),
                  pl.BlockSpec((tm, tm), lambda i: (0, 0))],
        out_specs=[kt, kt, kt, pl.BlockSpec((n_exp, 1), lambda i: (0, 0))],
        out_shape=[jax.ShapeDtypeStruct((TOP_K, n_tok), jnp.int32),
                   jax.ShapeDtypeStruct((TOP_K, n_tok), F32),
                   jax.ShapeDtypeStruct((TOP_K, n_tok), jnp.int32),
                   jax.ShapeDtypeStruct((n_exp, 1), F32)],
        compiler_params=_cparams(("arbitrary",)),
        name="router_topk_rank",
    )(h2, rwt, rb, tri)


def _dest_kernel(start_ref, e_ref, r_ref, o_ref):
    e = e_ref[...]
    start = lax.fori_loop(0, start_ref.shape[0], lambda j, acc: jnp.where(e == j, start_ref[j], acc),
                          jnp.zeros(e.shape, jnp.int32))
    o_ref[...] = start + r_ref[...]


def _dest_rows(pad_start, top_e, rank):
    n_tok = top_e.shape[1]
    tm = min(DEST_TILE, n_tok)
    blk = pl.BlockSpec((TOP_K, tm), lambda i, ps: (0, i))
    return pl.pallas_call(
        _dest_kernel,
        grid_spec=pltpu.PrefetchScalarGridSpec(num_scalar_prefetch=1, grid=(n_tok // tm,), in_specs=[blk, blk], out_specs=blk),
        out_shape=jax.ShapeDtypeStruct(top_e.shape, jnp.int32),
        compiler_params=_cparams(("arbitrary",)),
        name="moe_dest_rows",
    )(pad_start, top_e, rank)


def _row_copy(src_ref, src_row, dst_ref, dst_row, sem):
    return pltpu.make_async_copy(src_ref.at[pl.ds(src_row, 1), :], dst_ref.at[pl.ds(dst_row, 1), :], sem)


def _dispatch_kernel(pend_ref, padded_ref, nused_ref, dest_ref, h_ref, xs_ref, zero_ref, sem, zsem):
    tm = h_ref.shape[0]

    @pl.when(pl.program_id(0) == 0)
    def _():
        zero_ref[...] = jnp.zeros_like(zero_ref)
        n_tiles = xs_ref.shape[0] // EXPERT_TILE

        def zero_tile(row):
            return pltpu.make_async_copy(zero_ref, xs_ref.at[pl.ds(pl.multiple_of(row, EXPERT_TILE), EXPERT_TILE), :], zsem)

        def per_expert(act):
            def body(e, carry):
                @pl.when(padded_ref[e] > 0)
                def _():
                    act(zero_tile(pend_ref[e] - EXPERT_TILE))
                return carry
            lax.fori_loop(0, pend_ref.shape[0], body, 0)

        def per_unused(act):
            def body(i, carry):
                act(zero_tile(i * EXPERT_TILE))
                return carry
            lax.fori_loop(nused_ref[0], n_tiles, body, 0)

        for loop in (per_expert, per_unused):
            loop(lambda copy: copy.start())
        for loop in (per_expert, per_unused):
            loop(lambda copy: copy.wait())

    def issue(t8, carry):
        base = pl.multiple_of(t8 * SUBLANES, SUBLANES)
        for s in range(SUBLANES):
            for k in range(TOP_K):
                _row_copy(h_ref, base + s, xs_ref, dest_ref[0, k, base + s], sem).start(priority=k % 2)
        return carry

    lax.fori_loop(0, tm // SUBLANES, issue, 0)
    pltpu.make_async_copy(xs_ref.at[pl.ds(0, TOP_K * tm), :], xs_ref.at[pl.ds(0, TOP_K * tm), :], sem).wait()


def _dispatch(pad_end, padded, n_used, dest3, h2, n_rows):
    n_tok, words = h2.shape
    tm = dest3.shape[2]
    grid_spec = pltpu.PrefetchScalarGridSpec(
        num_scalar_prefetch=3,
        grid=(n_tok // tm,),
        in_specs=[pl.BlockSpec((1, TOP_K, tm), lambda i, *_: (i, 0, 0), memory_space=pltpu.SMEM),
                  pl.BlockSpec((tm, words), lambda i, *_: (i, 0))],
        out_specs=pl.BlockSpec(memory_space=pl.ANY),
        scratch_shapes=[pltpu.VMEM((EXPERT_TILE, words), jnp.uint32), pltpu.SemaphoreType.DMA(()),
                        pltpu.SemaphoreType.DMA(())],
    )
    return pl.pallas_call(
        _dispatch_kernel,
        grid_spec=grid_spec,
        out_shape=jax.ShapeDtypeStruct((n_rows, words), jnp.uint32),
        compiler_params=_cparams(("arbitrary",)),
        name="moe_dispatch",
    )(pad_end, padded, n_used, dest3, h2)


def _expert_kernel(te_ref, nu_ref, x_ref, wg_ref, wu_ref, wd_ref, y_ref):
    i = pl.program_id(0)

    @pl.when(i < nu_ref[0])
    def _():
        lo, hi = _unpack_rows(x_ref[...])
        lo, hi = lo.astype(BF16), hi.astype(BF16)
        half = lo.shape[1]

        def proj(w_ref):
            return (jnp.dot(lo, w_ref[0, :half, :].astype(BF16), preferred_element_type=F32)
                    + jnp.dot(hi, w_ref[0, half:, :].astype(BF16), preferred_element_type=F32))

        hmid = jax.nn.silu(proj(wg_ref)) * proj(wu_ref)
        y_ref[...] = _pack_rows(jnp.dot(hmid.astype(BF16), wd_ref[0].astype(BF16), preferred_element_type=F32))

    @pl.when(i >= nu_ref[0])
    def _():
        y_ref[...] = jnp.zeros_like(y_ref)


def _experts(tile_exp, n_used, xs, wg, wu, wd):
    n_rows, words = xs.shape
    dm, de = wg.shape[1], wg.shape[2]
    grid_spec = pltpu.PrefetchScalarGridSpec(
        num_scalar_prefetch=2,
        grid=(n_rows // EXPERT_TILE,),
        in_specs=[pl.BlockSpec((EXPERT_TILE, words), lambda i, te, nu: (i, 0)),
                  pl.BlockSpec((1, dm, de), lambda i, te, nu: (te[i], 0, 0)),
                  pl.BlockSpec((1, dm, de), lambda i, te, nu: (te[i], 0, 0)),
                  pl.BlockSpec((1, de, dm), lambda i, te, nu: (te[i], 0, 0))],
        out_specs=pl.BlockSpec((EXPERT_TILE, words), lambda i, te, nu: (i, 0)),
    )
    return pl.pallas_call(
        _expert_kernel,
        grid_spec=grid_spec,
        out_shape=jax.ShapeDtypeStruct((n_rows, words), jnp.uint32),
        compiler_params=_cparams(("arbitrary",)),
        name="expert_mlp",
    )(tile_exp, n_used, xs, wg, wu, wd)


def _final_kernel(dest_ref, next_ref, w_ref, base_ref, g2_ref, lg_ref, lb_ref, ys_ref, o_ref, buf_ref, sem):
    i = pl.program_id(0)
    tm = base_ref.shape[1]
    slot = i % 2
    words = buf_ref.shape[3]

    def issue_group(rows_ref, to, base):
        for s in range(SUBLANES):
            for k in range(TOP_K):
                _row_copy(ys_ref, rows_ref[0, k, base + s], buf_ref.at[to, k], base + s,
                          sem.at[to]).start(priority=k % 2)

    def slot_wait(which):
        pltpu.make_async_copy(buf_ref.at[which], buf_ref.at[which], sem.at[which]).wait()

    @pl.when(i == 0)
    def _():
        def first(t8, carry):
            issue_group(dest_ref, 0, pl.multiple_of(t8 * SUBLANES, SUBLANES))
            return carry
        lax.fori_loop(0, tm // SUBLANES, first, 0)

    def combine_from(cur):
        slot_wait(cur)

        def group(t8, carry):
            base = pl.multiple_of(t8 * SUBLANES, SUBLANES)
            issue_group(next_ref, 1 - cur, base)
            rows = pl.ds(base, SUBLANES)
            w = w_ref[rows, :]
            lo_sum, hi_sum = jnp.zeros((SUBLANES, words), F32), jnp.zeros((SUBLANES, words), F32)
            for k in range(TOP_K):
                lo, hi = _unpack_rows(buf_ref[cur, k, rows, :])
                lo_sum += w[:, k:k + 1] * lo
                hi_sum += w[:, k:k + 1] * hi
            routed = jnp.concatenate([lo_sum, hi_sum], axis=1)
            o_ref[0, rows, :] = base_ref[0, rows, :] + g2_ref[0] * routed
            return carry

        lax.fori_loop(0, tm // SUBLANES, group, 0)

        @pl.when(i + 1 == pl.num_programs(0))
        def _():
            slot_wait(1 - cur)

    for cur in (0, 1):
        pl.when(slot == cur)(functools.partial(combine_from, cur))
    o_ref[0] = _norm_rows(o_ref[0]) * lg_ref[...] + lb_ref[...]


def _final(dest3, w_tok, base, g2, ln_g, ln_b, ys):
    bsz, seq, dm = base.shape
    tm = dest3.shape[2]
    nt = seq // tm
    n_tiles = bsz * nt
    tile = pl.BlockSpec((1, tm, dm), lambda i: (i // nt, i % nt, 0))
    rows_of = lambda f: pl.BlockSpec((1, TOP_K, tm), lambda i: (f(i), 0, 0), memory_space=pltpu.SMEM)
    return pl.pallas_call(
        _final_kernel,
        grid=(n_tiles,),
        in_specs=[rows_of(lambda i: i), rows_of(lambda i: jnp.minimum(i + 1, n_tiles - 1)),
                  pl.BlockSpec((tm, TOP_K), lambda i: (i, 0)),
                  tile, pl.BlockSpec((1, 1, dm), lambda i: (i // nt, 0, 0)),
                  pl.BlockSpec((1, dm), lambda i: (0, 0)), pl.BlockSpec((1, dm), lambda i: (0, 0)),
                  pl.BlockSpec(memory_space=pl.ANY)],
        out_specs=tile,
        out_shape=jax.ShapeDtypeStruct((bsz, seq, dm), F32),
        scratch_shapes=[pltpu.VMEM((2, TOP_K, tm, ys.shape[1]), jnp.uint32), pltpu.SemaphoreType.DMA((2,))],
        compiler_params=_cparams(("arbitrary",)),
        name="combine_final_layernorm",
    )(dest3, dest3, w_tok, base, g2, ln_g, ln_b, ys)


def _rel_bucket(dist):
    dist = np.maximum(dist, 0)
    exact = REL_BUCKETS // 2
    log_ratio = np.log(np.maximum(dist, 1).astype(np.float32) / np.float32(exact)) / np.float32(math.log(REL_MAX_DIST / exact))
    large = np.minimum(exact + (log_ratio * (REL_BUCKETS - exact)).astype(np.int32), REL_BUCKETS - 1)
    return np.where(dist < exact, dist, large)


def _bias_tiles(rel_bias):
    n_d = 4 * SEL_LEN
    pick = (_rel_bucket(np.arange(n_d))[:, None] == np.arange(REL_BUCKETS)[None, :]).astype(np.float32)
    vec = jnp.dot(jnp.asarray(pick), rel_bias.astype(F32), precision=lax.Precision.HIGHEST).T
    far = rel_bias[REL_BUCKETS - 1]
    vec = ((vec - far[:, None]) * LOG2E).reshape(N_KV, GQA, n_d)
    tok = np.arange(Q_BLOCK)[None, :]
    key = np.arange(SEL_LEN)[:, None]
    padded = jnp.pad(vec, ((0, 0), (0, 0), (Q_BLOCK, 0)))

    def by_distance(d):
        first = np.clip(d[:, 0], -Q_BLOCK, n_d - Q_BLOCK) + Q_BLOCK
        rows = [lax.slice_in_dim(padded, int(f), int(f) + Q_BLOCK, axis=2) for f in first]
        return jnp.stack(rows, axis=2)

    def toeplitz(d):
        vals = jnp.where(jnp.asarray(d >= 0), by_distance(d), NEG)
        return jnp.transpose(vals, (0, 2, 1, 3)).reshape(N_KV, d.shape[0], ROWS)

    near = [toeplitz(delta + tok - key) for delta in (0, SEL_LEN, 2 * SEL_LEN)]
    zero = jnp.zeros((N_KV, SEL_LEN, ROWS), F32)
    edge = np.where(tok < key, 0.0, NEG).astype(np.float32)
    edge = jnp.broadcast_to(jnp.asarray(np.tile(edge, (1, GQA)))[None], (N_KV, SEL_LEN, ROWS))
    nb = jnp.stack(near + [zero, jnp.full_like(zero, NEG), edge], axis=1)
    w = np.arange(BAND_ROWS)[:, None]
    bands = []
    for ph in (0, 1):
        d = tok - CMP_STRIDE * (w - PAD_CMP - (Q_BLOCK // CMP_STRIDE) * ph) - (CMP_LEN - 1)
        vals = jnp.where(jnp.asarray(d >= 0), by_distance(d), 0.0)
        bands.append(jnp.transpose(vals, (0, 2, 1, 3)).reshape(N_KV, BAND_ROWS, ROWS))
    cb = jnp.stack(bands, axis=1)
    return nb.astype(F32), cb.astype(F32)


def _padded_cmp_rows(seq):
    return -(-(seq // CMP_STRIDE + 2 * PAD_CMP) // LANES) * LANES


def _overlap_t(seq):
    n_cmp = (seq - CMP_LEN) // CMP_STRIDE + 1
    n_blk = seq // SEL_LEN
    ncp = _padded_cmp_rows(seq)
    c_start = np.arange(n_cmp) * CMP_STRIDE
    c_end = c_start + CMP_LEN - 1
    blk = np.arange(n_blk)
    ov = ((c_start[:, None] < (blk[None, :] + 1) * SEL_LEN) & (c_end[:, None] >= blk[None, :] * SEL_LEN))
    out = np.zeros((2 * SUBLANES + n_blk, ncp), np.float32)
    out[0] = 1.0
    out[2 * SUBLANES:, PAD_CMP:PAD_CMP + n_cmp] = ov.T
    return jnp.asarray(out, BF16)


def _cmp_visibility(seq):
    n_cmp = (seq - CMP_LEN) // CMP_STRIDE + 1
    ncp = _padded_cmp_rows(seq)
    rho = np.arange(ncp)[:, None]
    col = np.arange(COLS)[None, :]
    thr = CMP_STRIDE * (rho - PAD_CMP) + (CMP_LEN - 1) - Q_BLOCK * (col // ROWS) - col % Q_BLOCK
    valid = (rho >= PAD_CMP) & (rho < PAD_CMP + n_cmp)
    return jnp.asarray(np.where(valid, thr, np.iinfo(np.int32).max), jnp.int32)


def _s5_params(lam_re, lam_im, log_step, b_re, b_im, c_re, c_im):
    lr, li = lam_re.astype(F32), lam_im.astype(F32)
    dt = jnp.exp(log_step.astype(F32))[:, None]
    mag = jnp.exp(lr * dt)
    ar, ai = mag * jnp.cos(li * dt), mag * jnp.sin(li * dt)
    den = lr * lr + li * li
    kr = ((ar - 1.0) * lr + ai * li) / den
    ki = (ai * lr - (ar - 1.0) * li) / den
    br, bi = b_re.astype(F32), b_im.astype(F32)
    bbr = kr[..., None] * br - ki[..., None] * bi
    bbi = kr[..., None] * bi + ki[..., None] * br
    n_g = lr.shape[0]
    eye = jnp.eye(n_g, dtype=F32)

    def drive(bb):
        return jnp.einsum('gpc,gh->gchp', bb, eye).reshape(n_g * SSM_GROUP, n_g * SSM_STATE)

    def readout(c):
        return jnp.einsum('gcp,gh->gphc', c, eye).reshape(n_g * SSM_STATE, n_g * SSM_GROUP)

    d_re, d_im = drive(bbr), drive(bbi)
    r_re, r_im = readout(c_re.astype(F32)), -readout(c_im.astype(F32))
    c_in, c_st = d_re.shape[0] // S5_DIAG_BLOCKS, d_re.shape[1] // S5_DIAG_BLOCKS
    blk = lambda m, k, rows, cols: m[k * rows:(k + 1) * rows, k * cols:(k + 1) * cols]
    wb = jnp.stack([jnp.concatenate([blk(d_re, k, c_in, c_st), blk(d_im, k, c_in, c_st)], axis=1)
                    for k in range(S5_DIAG_BLOCKS)]).astype(BF16)
    wc = jnp.stack([jnp.concatenate([blk(r_re, k, c_st, c_in), blk(r_im, k, c_st, c_in)], axis=0)
                    for k in range(S5_DIAG_BLOCKS)]).astype(BF16)
    a = jnp.concatenate([ar.reshape(1, -1), ai.reshape(1, -1)], axis=1)
    return wb, jnp.broadcast_to(a, (SUBLANES, a.shape[1])), wc


def _layer(x, mod, w_in, lam_re, lam_im, log_step, b_re, b_im, c_re, c_im, d_skip, glu_w, cmp_pos, cmp_w1,
           cmp_w2, rel_bias, w_out, ln1_g, ln1_b, router_w, router_bias, e_gate, e_up, e_down, sg, su, sd,
           ln2_g, ln2_b, alpha):
    bsz, seq, dm = x.shape
    n_tok = bsz * seq
    sh1, sc1, g1, sh2, sc2, g2 = [m[:, None, :] for m in jnp.split(mod, 6, axis=-1)]
    ssm_w = dm // 2
    attn_w = N_HEADS * HEAD_DIM
    kv_w = N_KV * HEAD_DIM
    n_gate = 3 * N_HEADS
    offs = np.cumsum([0, ssm_w, attn_w] + [kv_w] * 6 + [n_gate, 2 * dm])

    wu = w_in[:, offs[0]:offs[1]].astype(BF16)
    wq = w_in[:, offs[1]:offs[2]] * (HEAD_DIM ** -0.5 * LOG2E)
    wa = jnp.concatenate([wq, w_in[:, offs[2]:offs[4]]], axis=1).astype(BF16)
    wg = jnp.pad(w_in[:, offs[8]:offs[9]], ((0, 0), (0, LANES - n_gate))).astype(BF16)
    wm = w_in[:, offs[9]:offs[10]].astype(BF16)
    nq = seq // Q_BLOCK
    n_blk = seq // SEL_LEN
    nch = seq // CMP_STRIDE

    aug_w = 2 * HEAD_DIM + -(-n_blk // LANES) * LANES
    u2d, act, gates, mg, kaug, kw, vst, vwt = _input_projection(
        x, sh1, sc1, wu, wa, wg, wm, w_in[:, offs[4]:offs[5]].astype(BF16), w_in[:, offs[6]:offs[7]].astype(BF16),
        w_in[:, offs[5]:offs[6]].T.astype(BF16), w_in[:, offs[7]:offs[8]].T.astype(BF16), aug_w)

    def piece(i):
        return act[:, :, attn_w + i * kv_w: attn_w + (i + 1) * kv_w]

    def compress(raw, pos, w1, w2):
        xc = raw.reshape(bsz, nch, CMP_STRIDE, N_KV, HEAD_DIM).transpose(0, 3, 1, 2, 4).reshape(bsz, N_KV, nch, CMP_STRIDE * HEAD_DIM)
        half = CMP_STRIDE * HEAD_DIM
        w1cat = jnp.concatenate([w1[:half], w1[half:]], axis=1).astype(BF16)
        posb = jnp.dot(pos.reshape(1, -1), w1, precision=lax.Precision.HIGHEST)
        return _compress(xc, w1cat, posb, w2.astype(BF16))

    kc = compress(piece(0), cmp_pos[0], cmp_w1[0], cmp_w2[0])
    vc = compress(piece(1), cmp_pos[1], cmp_w1[1], cmp_w2[1])
    pad = ((0, 0), (0, 0), (PAD_CMP, _padded_cmp_rows(seq) - nch - PAD_CMP), (0, 0))
    kcp = jnp.pad(kc, pad).astype(BF16)
    vcpt = jnp.swapaxes(jnp.pad(vc, pad), 2, 3).astype(BF16)
    ones_ovt = _overlap_t(seq)
    cmp_lhs = jnp.concatenate([vcpt, jnp.broadcast_to(ones_ovt, (bsz, N_KV) + ones_ovt.shape)], axis=2)

    nst = nq // NSA_TILES
    gt = gates[:, :, :n_gate].reshape(bsz, nst, NSA_TILES, Q_BLOCK, N_KV, GQA, 3)
    gt = gt.transpose(0, 4, 1, 6, 2, 5, 3).reshape(bsz, N_KV, nst, 3, COLS)
    gt = jnp.pad(gt, ((0, 0), (0, 0), (0, 0), (0, SUBLANES - 3), (0, 0)))
    nb, cb = _bias_tiles(rel_bias)
    y_nsa = _nsa(act, kaug, vst, kw, vwt, kcp, cmp_lhs, _cmp_visibility(seq), nb, cb, gt)

    wb, a, wc = _s5_params(lam_re, lam_im, log_step, b_re, b_im, c_re, c_im)
    z2 = _s5(u2d.reshape(seq * bsz, ssm_w), wb, a, wc, d_skip.reshape(1, ssm_w).astype(F32), bsz)
    z2d = z2.reshape(seq, bsz * ssm_w)

    base, h2 = _mix_out(z2d, y_nsa, mg, x, g1, sh2, sc2, g2, ln1_g.reshape(1, dm), ln1_b.reshape(1, dm),
                        glu_w.astype(BF16), w_out.astype(BF16), sg.astype(BF16), su.astype(BF16), sd.astype(BF16), alpha)
    h2 = h2.reshape(n_tok, dm // 2)

    n_exp = router_w.shape[1]
    rt = min(ROUTER_TILE, n_tok)
    tri = jnp.asarray(np.triu(np.ones((rt, rt), np.float32), 1), BF16)
    top_e, top_w, rank, counts = _router(h2, router_w.T.astype(BF16), router_bias.reshape(n_exp, 1).astype(F32), tri)
    counts = counts[:, 0].astype(jnp.int32)
    padded = (counts + EXPERT_TILE - 1) // EXPERT_TILE * EXPERT_TILE
    pad_end = jnp.cumsum(padded)
    pad_start = pad_end - padded
    dest = _dest_rows(pad_start.astype(jnp.int32), top_e, rank)
    n_rows = n_tok * TOP_K + n_exp * EXPERT_TILE
    n_tiles = n_rows // EXPERT_TILE
    tile_exp = jnp.sum(pad_end[None, :] <= (jnp.arange(n_tiles) * EXPERT_TILE)[:, None], axis=1)
    tile_exp = jnp.minimum(tile_exp, n_exp - 1).astype(jnp.int32)
    n_used = (pad_end[-1] // EXPERT_TILE).astype(jnp.int32).reshape(1)
    mt = min(MOVE_TILE, seq)
    dest3 = dest.reshape(TOP_K, n_tok // mt, mt).transpose(1, 0, 2)
    xs = _dispatch(pad_end.astype(jnp.int32), padded.astype(jnp.int32), n_used, dest3, h2, n_rows)
    ys = _experts(tile_exp, n_used, xs, e_gate, e_up, e_down)
    return _final(dest3, top_w.T, base, g2, ln2_g.reshape(1, dm), ln2_b.reshape(1, dm), ys)


def kernel(x, c, ada_w, ada_b, w_in, ssm_lambda_re, ssm_lambda_im, ssm_log_step, ssm_b_re, ssm_b_im, ssm_c_re, ssm_c_im, ssm_d, ssm_glu_w, cmp_pos, cmp_w1, cmp_w2, rel_bias, w_out, ln1_g, ln1_b, router_w, router_bias, exp_w_gate, exp_w_up, exp_w_down, sh_w_gate, sh_w_up, sh_w_down, ln2_g, ln2_b):
    depth = ada_w.shape[0]
    alpha = (2 * depth) ** 0.25
    for l in range(depth):
        mod = _modulation(c, ada_w[l], ada_b[l])
        x = _layer(x, mod, w_in[l], ssm_lambda_re[l], ssm_lambda_im[l], ssm_log_step[l], ssm_b_re[l], ssm_b_im[l],
                   ssm_c_re[l], ssm_c_im[l], ssm_d[l], ssm_glu_w[l], cmp_pos[l], cmp_w1[l], cmp_w2[l], rel_bias,
                   w_out[l], ln1_g[l], ln1_b[l], router_w[l], router_bias[l], exp_w_gate[l], exp_w_up[l],
                   exp_w_down[l], sh_w_gate[l], sh_w_up[l], sh_w_down[l], ln2_g[l], ln2_b[l], alpha)
    return x
```
